```python
import jax, jax.numpy as jnp
from jax import lax
import numpy as np

D_MODEL = 1024
BATCH = 4
SEQ = 4096
DEPTH = 1

HEAD_DIM = 64
N_META = 16
BLOCK = 128
PAD = BLOCK - N_META
FOX_HEADS = 8
SWA_HEADS = 8
SWA_KV_HEADS = 2
WINDOW = 128
N_EXPERTS = 32
TOP_K = 4
D_FF = 1024
SWIGLU_LIMIT = 7.0
SWIGLU_ALPHA = 1.702
RMS_EPS = 1e-6
NEG_INF = -1e30
MOE_BLOCK = 128

FOX_W = FOX_HEADS * HEAD_DIM
SWA_QW = SWA_HEADS * HEAD_DIM
SWA_KW = SWA_KV_HEADS * HEAD_DIM
SPLITS = [FOX_W, FOX_W, FOX_W, FOX_HEADS, SWA_QW, SWA_KW, SWA_KW, D_MODEL, D_MODEL]
IN_WIDTH = 3 * FOX_W + FOX_HEADS + SWA_QW + 2 * SWA_KW + 2 * D_MODEL

kernel_name = "hybrid_fox_swa_sink_moe_block"


def _rmsnorm(t, gain):
    tf = t.astype(jnp.float32)
    y = tf * lax.rsqrt(jnp.mean(tf * tf, axis=-1, keepdims=True) + RMS_EPS)
    return (y * gain.astype(jnp.float32)).astype(t.dtype)


def _pad_front(t):
    return jnp.pad(t, ((0, 0), (PAD, 0)) + ((0, 0),) * (t.ndim - 2))


def _fox_attention(q, k, v, log_f):
    B, Lp, H, d = q.shape
    nb = Lp // BLOCK
    F = jnp.cumsum(log_f, axis=1)
    FT = F.transpose(0, 2, 1)
    pos = jnp.arange(Lp)
    key_ok = pos >= PAD
    qb = q.reshape(B, nb, BLOCK, H, d).transpose(1, 0, 2, 3, 4)
    Fb = F.reshape(B, nb, BLOCK, H).transpose(1, 0, 3, 2)
    scale = HEAD_DIM ** -0.5

    def one_block(args):
        qi, Fi, i = args
        s = jnp.einsum('bqhd,bkhd->bhqk', qi, k, preferred_element_type=jnp.float32) * scale
        s = s + Fi[..., :, None] - FT[:, :, None, :]
        qpos = i * BLOCK + jnp.arange(BLOCK)
        mask = (pos[None, :] <= qpos[:, None]) & key_ok[None, :]
        s = jnp.where(mask[None, None], s, NEG_INF)
        p = jax.nn.softmax(s, axis=-1)
        return jnp.einsum('bhqk,bkhd->bqhd', p.astype(v.dtype), v)

    out = lax.map(one_block, (qb, Fb, jnp.arange(nb)))
    return out.transpose(1, 0, 2, 3, 4).reshape(B, Lp, H, d)


def _swa_attention(q, k, v, sinks, slopes):
    B, Lp, H, d = q.shape
    G = k.shape[2]
    R = H // G
    nb = Lp // BLOCK
    qb = q.reshape(B, nb, BLOCK, G, R, d)

    def with_prev(t):
        tb = t.reshape(B, nb, BLOCK, G, d)
        prev = jnp.pad(tb[:, :-1], ((0, 0), (1, 0), (0, 0), (0, 0), (0, 0)))
        return jnp.concatenate([prev, tb], axis=2)

    kk, vv = with_prev(k), with_prev(v)
    s = jnp.einsum('bnqgrd,bnkgd->bngrqk', qb, kk, preferred_element_type=jnp.float32) * (HEAD_DIM ** -0.5)
    blk = jnp.arange(nb)[:, None]
    qpos = blk * BLOCK + jnp.arange(BLOCK)[None, :]
    kpos = blk * BLOCK - BLOCK + jnp.arange(2 * BLOCK)[None, :]
    dist = qpos[:, :, None] - kpos[:, None, :]
    mask = (dist >= 0) & (dist < WINDOW) & (kpos[:, None, :] >= PAD)
    s = s - slopes.reshape(G, R)[None, None, :, :, None, None] * dist.astype(jnp.float32)[None, :, None, None]
    s = jnp.where(mask[None, :, None, None], s, NEG_INF)
    sink = jnp.broadcast_to(sinks.astype(jnp.float32).reshape(G, R)[None, None, :, :, None, None], s.shape[:-1] + (1,))
    p = jax.nn.softmax(jnp.concatenate([s, sink], axis=-1), axis=-1)[..., :-1]
    o = jnp.einsum('bngrqk,bnkgd->bnqgrd', p.astype(v.dtype), vv)
    return o.reshape(B, Lp, H, d)


def _moe(h, w_router, b_router, w_up, b_up, w_down, b_down):
    N, D = h.shape
    logits = jnp.matmul(h, w_router).astype(jnp.float32) + b_router.astype(jnp.float32)
    top_val, top_idx = lax.top_k(logits, TOP_K)
    gates = jax.nn.softmax(top_val, axis=-1)
    n_assign = N * TOP_K
    flat_e = top_idx.reshape(-1)
    flat_tok = jnp.arange(n_assign) // TOP_K
    order = jnp.argsort(flat_e)
    se, stok, sw = flat_e[order], flat_tok[order], gates.reshape(-1)[order]
    counts = jnp.bincount(flat_e, length=N_EXPERTS)
    padded = (counts + MOE_BLOCK - 1) // MOE_BLOCK * MOE_BLOCK
    pend = jnp.cumsum(padded)
    pstart = pend - padded
    start = jnp.cumsum(counts) - counts
    dest = pstart[se] + jnp.arange(n_assign) - start[se]
    n_blocks = -(-n_assign // MOE_BLOCK) + N_EXPERTS
    n_rows = n_blocks * MOE_BLOCK
    row_tok = jnp.zeros((n_rows,), jnp.int32).at[dest].set(stok)
    row_w = jnp.zeros((n_rows,), jnp.float32).at[dest].set(sw)
    block_e = jnp.minimum(jnp.searchsorted(pend, jnp.arange(n_blocks) * MOE_BLOCK, side='right'), N_EXPERTS - 1)
    xs = h[row_tok].reshape(n_blocks, MOE_BLOCK, D)

    def expert_block(args):
        xb, e = args
        gu = jnp.matmul(xb, w_up[e]) + b_up[e]
        g, u = gu[:, :D_FF], gu[:, D_FF:]
        g = jnp.minimum(g, SWIGLU_LIMIT)
        u = jnp.clip(u, -SWIGLU_LIMIT, SWIGLU_LIMIT)
        act = (u + 1.0) * (g * jax.nn.sigmoid(SWIGLU_ALPHA * g))
        return jnp.matmul(act, w_down[e]) + b_down[e]

    ys = lax.map(expert_block, (xs, block_e)).reshape(n_rows, D)
    return jnp.zeros_like(h).at[row_tok].add(ys * row_w[:, None].astype(ys.dtype))


def setup_inputs(seed: int = 0) -> dict:
    key = jax.random.key(seed)
    ks = jax.random.split(key, 20)
    f32 = jnp.float32
    nrm = lambda k, shape, scale: jax.random.normal(k, shape, f32) * scale
    return {
        "x": nrm(ks[0], (BATCH, SEQ, D_MODEL), 1.0),
        "meta_tokens": nrm(ks[1], (N_META, D_MODEL), 1.0),
        "norm1_gain": 1.0 + nrm(ks[2], (DEPTH, D_MODEL), 0.02),
        "w_in": nrm(ks[3], (DEPTH, D_MODEL, IN_WIDTH), D_MODEL ** -0.5),
        "b_forget": jax.random.uniform(ks[4], (DEPTH, FOX_HEADS), f32, 1.0, 4.0),
        "fox_q_gain": 1.0 + nrm(ks[5], (DEPTH, HEAD_DIM), 0.02),
        "fox_k_gain": 1.0 + nrm(ks[6], (DEPTH, HEAD_DIM), 0.02),
        "swa_q_gain": 1.0 + nrm(ks[7], (DEPTH, HEAD_DIM), 0.02),
        "swa_k_gain": 1.0 + nrm(ks[8], (DEPTH, HEAD_DIM), 0.02),
        "swa_sinks": nrm(ks[9], (DEPTH, SWA_HEADS), 0.5),
        "w_fox_out": nrm(ks[10], (DEPTH, FOX_W, D_MODEL), FOX_W ** -0.5),
        "w_swa_out": nrm(ks[11], (DEPTH, SWA_QW, D_MODEL), SWA_QW ** -0.5),
        "w_out": nrm(ks[12], (DEPTH, D_MODEL, D_MODEL), D_MODEL ** -0.5),
        "norm2_gain": 1.0 + nrm(ks[13], (DEPTH, D_MODEL), 0.02),
        "w_router": nrm(ks[14], (DEPTH, D_MODEL, N_EXPERTS), D_MODEL ** -0.5),
        "b_router": nrm(ks[15], (DEPTH, N_EXPERTS), 0.01),
        "w_up": nrm(ks[16], (DEPTH, N_EXPERTS, D_MODEL, 2 * D_FF), D_MODEL ** -0.5),
        "b_up": nrm(ks[17], (DEPTH, N_EXPERTS, 2 * D_FF), 0.01),
        "w_down": nrm(ks[18], (DEPTH, N_EXPERTS, D_FF, D_MODEL), D_FF ** -0.5),
        "b_down": nrm(ks[19], (DEPTH, N_EXPERTS, D_MODEL), 0.01),
    }


def reference(x, meta_tokens, norm1_gain, w_in, b_forget, fox_q_gain, fox_k_gain, swa_q_gain, swa_k_gain, swa_sinks, w_fox_out, w_swa_out, w_out, norm2_gain, w_router, b_router, w_up, b_up, w_down, b_down):
    B = x.shape[0]
    meta = jnp.broadcast_to(meta_tokens[None].astype(x.dtype), (B, N_META, D_MODEL))
    hres = jnp.concatenate([meta, x], axis=1)
    L = hres.shape[1]
    slopes = jnp.exp2(-8.0 * (jnp.arange(SWA_HEADS, dtype=jnp.float32) + 1.0) / SWA_HEADS)
    cuts = np.cumsum(SPLITS)[:-1].tolist()

    def head_rms(t, n_heads, gain):
        return _rmsnorm(t.reshape(B, L, n_heads, HEAD_DIM), gain)

    for l in range(DEPTH):
        h = _rmsnorm(hres, norm1_gain[l])
        proj = jnp.matmul(h, w_in[l])
        fq, fk, fv, flog, sq, sk, sv, ga, gb = jnp.split(proj, cuts, axis=-1)
        fq = head_rms(fq, FOX_HEADS, fox_q_gain[l])
        fk = head_rms(fk, FOX_HEADS, fox_k_gain[l])
        fv = fv.reshape(B, L, FOX_HEADS, HEAD_DIM)
        log_f = jax.nn.log_sigmoid(flog.astype(jnp.float32) + b_forget[l].astype(jnp.float32))
        sq = head_rms(sq, SWA_HEADS, swa_q_gain[l])
        sk = head_rms(sk, SWA_KV_HEADS, swa_k_gain[l])
        sv = sv.reshape(B, L, SWA_KV_HEADS, HEAD_DIM)

        o_a = _fox_attention(_pad_front(fq), _pad_front(fk), _pad_front(fv), _pad_front(log_f))
        o_a = o_a[:, PAD:].reshape(B, L, FOX_W)
        o_b = _swa_attention(_pad_front(sq), _pad_front(sk), _pad_front(sv), swa_sinks[l], slopes)
        o_b = o_b[:, PAD:].reshape(B, L, SWA_QW)

        mix = jax.nn.sigmoid(ga) * jnp.matmul(o_a, w_fox_out[l]) + jax.nn.sigmoid(gb) * jnp.matmul(o_b, w_swa_out[l])
        hres = hres + jnp.matmul(mix, w_out[l])

        h2 = _rmsnorm(hres, norm2_gain[l]).reshape(B * L, D_MODEL)
        hres = hres + _moe(h2, w_router[l], b_router[l], w_up[l], b_up[l], w_down[l], b_down[l]).reshape(B, L, D_MODEL)

    return hres[:, N_META:]
```

```python
import functools

import jax
import jax.numpy as jnp
import numpy as np
from jax import lax
from jax.experimental import pallas as pl
from jax.experimental.pallas import tpu as pltpu

HEAD_DIM = 64
N_META = 16
FOX_HEADS = 8
SWA_HEADS = 8
SWA_KV_HEADS = 2
WINDOW = 128
N_EXPERTS = 32
TOP_K = 4
D_FF = 1024
SWIGLU_LIMIT = 7.0
SWIGLU_ALPHA = 1.702
RMS_EPS = 1e-6
NEG_INF = -1e30

LANES = 128
META_BLOCK = 128
META_PAD = META_BLOCK - N_META
RUN_ALIGN = 8
VMEM_LIMIT = 56 * 1024 * 1024

FOX_W = FOX_HEADS * HEAD_DIM
SWA_QW = SWA_HEADS * HEAD_DIM
SWA_KW = SWA_KV_HEADS * HEAD_DIM

_BF = jnp.bfloat16
_F32 = jnp.float32


def _cparams(sem):
    return pltpu.CompilerParams(dimension_semantics=sem, vmem_limit_bytes=VMEM_LIMIT)


def _rms(t, gain):
    return t * lax.rsqrt(jnp.mean(t * t, axis=-1, keepdims=True) + RMS_EPS) * gain


def _head_rms(z, gain_row):
    lane = lax.broadcasted_iota(jnp.int32, (z.shape[0], LANES), 1)
    lo_mask = lane < HEAD_DIM
    outs = []
    for b in range(z.shape[1] // LANES):
        v = z[:, b * LANES:(b + 1) * LANES]
        v2 = v * v
        tot = jnp.sum(v2, axis=-1, keepdims=True)
        lo = jnp.sum(jnp.where(lo_mask, v2, 0.0), axis=-1, keepdims=True)
        hi = tot - lo
        r_lo = lax.rsqrt(lo * (1.0 / HEAD_DIM) + RMS_EPS)
        r_hi = lax.rsqrt(hi * (1.0 / HEAD_DIM) + RMS_EPS)
        outs.append(v * jnp.where(lo_mask, r_lo, r_hi))
    return jnp.concatenate(outs, axis=-1) * gain_row


def _lane_cumsum(x):
    n = x.shape[-1]
    lane = lax.broadcasted_iota(jnp.int32, x.shape, x.ndim - 1)
    s = 1
    while s < n:
        x = x + jnp.where(lane >= s, pltpu.roll(x, s, x.ndim - 1), 0.0)
        s *= 2
    return x


def _qkv_kernel(x_ref, g1_ref, wf_ref, wfl_ref, ws_ref, bf_ref, fqg_ref, fkg_ref, sqg_ref, skg_ref, f0_ref,
                fq_ref, fk_ref, fv_ref, sq_ref, sk_ref, sv_ref, nf_ref, carry_ref, *, n_pad):
    j = pl.program_id(1)

    @pl.when(j == 0)
    def _():
        carry_ref[...] = f0_ref[...]

    x = x_ref[...]
    h = _rms(x, g1_ref[...]).astype(_BF)
    f = jnp.dot(h, wf_ref[...], preferred_element_type=_F32)
    s = jnp.dot(h, ws_ref[...], preferred_element_type=_F32)
    flog = lax.dot_general(wfl_ref[...], h, (((1,), (1,)), ((), ())), preferred_element_type=_F32)

    fq_ref[...] = _head_rms(f[:, :FOX_W], fqg_ref[...]).astype(_BF)
    fk_ref[...] = _head_rms(f[:, FOX_W:2 * FOX_W], fkg_ref[...]).astype(_BF)
    fv_ref[...] = f[:, 2 * FOX_W:].astype(_BF)
    sq_ref[...] = _head_rms(s[:, :SWA_QW], sqg_ref[...]).astype(_BF)
    sk_ref[...] = _head_rms(s[:, SWA_QW:SWA_QW + SWA_KW], skg_ref[...]).astype(_BF)
    sv_ref[...] = s[:, SWA_QW + SWA_KW:].astype(_BF)

    z = flog + bf_ref[...]
    log_f = jnp.minimum(z, 0.0) - jnp.log1p(jnp.exp(-jnp.abs(z)))
    if n_pad:
        lane = lax.broadcasted_iota(jnp.int32, log_f.shape, 1)
        log_f = jnp.where(lane >= n_pad, log_f, 0.0)
    cum = _lane_cumsum(log_f) + carry_ref[...]
    nf_ref[...] = -cum
    carry_ref[...] = cum[:, -1:]


def _qkv_call(x3, g1, wf, wfl, ws, bfg, fqg, fkg, sqg, skg, f0, *, tm, n_pad):
    B, S, D = x3.shape
    row = lambda w: pl.BlockSpec((None, tm, w), lambda b, j: (b, j, 0))
    full = lambda a: pl.BlockSpec(a.shape, lambda b, j: (0,) * a.ndim)
    out_shape = [jax.ShapeDtypeStruct((B, S, FOX_W), _BF)] * 3 + [
        jax.ShapeDtypeStruct((B, S, SWA_QW), _BF), jax.ShapeDtypeStruct((B, S, SWA_KW), _BF),
        jax.ShapeDtypeStruct((B, S, SWA_KW), _BF), jax.ShapeDtypeStruct((B, FOX_HEADS, S), _F32)]
    out_specs = [row(FOX_W)] * 3 + [row(SWA_QW), row(SWA_KW), row(SWA_KW),
                                    pl.BlockSpec((None, FOX_HEADS, tm), lambda b, j: (b, 0, j))]
    return pl.pallas_call(
        functools.partial(_qkv_kernel, n_pad=n_pad),
        grid=(B, S // tm),
        in_specs=[row(D)] + [full(a) for a in (g1, wf, wfl, ws, bfg, fqg, fkg, sqg, skg, f0)],
        out_specs=out_specs, out_shape=out_shape,
        scratch_shapes=[pltpu.VMEM((FOX_HEADS, 1), _F32)],
        compiler_params=_cparams(("arbitrary", "arbitrary")),
        name="qkv_proj",
    )(x3, g1, wf, wfl, ws, bfg, fqg, fkg, sqg, skg, f0)


def _fox_kernel(qi_tab, ki_tab, q_ref, k_ref, v_ref, nf_ref, km_ref, vm_ref, nfm_ref, o_ref, m_ref, l_ref, acc_ref):
    p = pl.program_id(1)
    t = pl.program_id(2)
    qi = qi_tab[t]
    ki = ki_tab[t]
    tq = q_ref.shape[0]
    tk = k_ref.shape[0]

    q = q_ref[...]
    lane_q = lax.broadcasted_iota(jnp.int32, q.shape, 1)
    qs = (jnp.where(lane_q < HEAD_DIM, q, jnp.zeros_like(q)), jnp.where(lane_q >= HEAD_DIM, q, jnp.zeros_like(q)))

    def update(a, s, v, first):
        if first:
            m_new = jnp.max(s, axis=-1, keepdims=True)
            pr = jnp.exp(s - m_new)
            l_ref[a] = jnp.sum(pr, axis=-1, keepdims=True)
            acc_ref[a] = jnp.dot(pr.astype(_BF), v, preferred_element_type=_F32)
        else:
            m_old = m_ref[a]
            m_new = jnp.maximum(m_old, jnp.max(s, axis=-1, keepdims=True))
            alpha = jnp.exp(m_old - m_new)
            pr = jnp.exp(s - m_new)
            l_ref[a] = alpha * l_ref[a] + jnp.sum(pr, axis=-1, keepdims=True)
            acc_ref[a] = alpha * acc_ref[a] + jnp.dot(pr.astype(_BF), v, preferred_element_type=_F32)
        m_ref[a] = m_new

    def scores(a, k, nf_row):
        s = lax.dot_general(qs[a], k, (((1,), (1,)), ((), ())), preferred_element_type=_F32)
        return s + nf_row

    @pl.when(ki == 0)
    def _():
        km = km_ref[...]
        vm = vm_ref[...]
        col = lax.broadcasted_iota(jnp.int32, (tq, META_BLOCK), 1)
        for a in range(2):
            s = scores(a, km, nfm_ref[pl.ds(2 * p + a, 1), :])
            s = jnp.where(col >= META_PAD, s, NEG_INF)
            update(a, s, vm, True)

    @pl.when(ki < qi)
    def _():
        k = k_ref[...]
        v = v_ref[...]
        for a in range(2):
            update(a, scores(a, k, nf_ref[pl.ds(2 * p + a, 1), :]), v, False)

    @pl.when(ki == qi)
    def _():
        k = k_ref[...]
        v = v_ref[...]
        row = lax.broadcasted_iota(jnp.int32, (tq, tk), 0)
        col = lax.broadcasted_iota(jnp.int32, (tq, tk), 1)
        for a in range(2):
            s = scores(a, k, nf_ref[pl.ds(2 * p + a, 1), :])
            update(a, jnp.where(col <= row, s, NEG_INF), v, False)
        o0 = acc_ref[0] / l_ref[0]
        o1 = acc_ref[1] / l_ref[1]
        lane = lax.broadcasted_iota(jnp.int32, o0.shape, 1)
        o_ref[...] = jnp.where(lane < HEAD_DIM, o0, o1).astype(_BF)


def _fox_call(fq, fk, fv, nf, km, vm, nfm, *, tq):
    B, S, _ = fq.shape
    nq = S // tq
    qi_tab = np.concatenate([np.full(i + 1, i, np.int32) for i in range(nq)])
    ki_tab = np.concatenate([np.arange(i + 1, dtype=np.int32) for i in range(nq)])
    qspec = pl.BlockSpec((None, tq, LANES), lambda b, p, t, qt, kt: (b, qt[t], p))
    kspec = pl.BlockSpec((None, tq, LANES), lambda b, p, t, qt, kt: (b, kt[t], p))
    grid_spec = pltpu.PrefetchScalarGridSpec(
        num_scalar_prefetch=2, grid=(B, FOX_HEADS // 2, len(qi_tab)),
        in_specs=[qspec, kspec, kspec,
                  pl.BlockSpec((None, FOX_HEADS, tq), lambda b, p, t, qt, kt: (b, 0, kt[t])),
                  pl.BlockSpec((META_BLOCK, LANES), lambda b, p, t, qt, kt: (0, p)),
                  pl.BlockSpec((META_BLOCK, LANES), lambda b, p, t, qt, kt: (0, p)),
                  pl.BlockSpec((FOX_HEADS, META_BLOCK), lambda b, p, t, qt, kt: (0, 0))],
        out_specs=qspec,
        scratch_shapes=[pltpu.VMEM((2, tq, 1), _F32), pltpu.VMEM((2, tq, 1), _F32), pltpu.VMEM((2, tq, LANES), _F32)])
    return pl.pallas_call(
        _fox_kernel, grid_spec=grid_spec, out_shape=jax.ShapeDtypeStruct((B, S, FOX_W), _BF),
        compiler_params=_cparams(("arbitrary", "arbitrary", "arbitrary")), name="fox_attn",
    )(jnp.asarray(qi_tab), jnp.asarray(ki_tab), fq, fk, fv, nf, km, vm, nfm)


def _swa_kernel(slope_ref, sink_ref, q_ref, kc_ref, kp_ref, vc_ref, vp_ref, km_ref, vm_ref, o_ref):
    p = pl.program_id(1)
    qi = pl.program_id(2)
    tq = q_ref.shape[0]
    q = q_ref[...]
    lane_q = lax.broadcasted_iota(jnp.int32, q.shape, 1)
    qs = (jnp.where(lane_q < HEAD_DIM, q, jnp.zeros_like(q)), jnp.where(lane_q >= HEAD_DIM, q, jnp.zeros_like(q)))
    first = qi == 0
    kp = jnp.where(first, km_ref[...], kp_ref[...])
    vp = jnp.where(first, vm_ref[...], vp_ref[...])
    kc = kc_ref[...]
    vc = vc_ref[...]
    cmin = jnp.where(first, META_PAD, 0)

    rp = lax.broadcasted_iota(jnp.int32, (tq, WINDOW), 0)
    cp = lax.broadcasted_iota(jnp.int32, (tq, WINDOW), 1)
    dist_p = (rp - cp + WINDOW).astype(_F32)
    ok_p = (rp < cp) & (cp >= cmin)
    rc = lax.broadcasted_iota(jnp.int32, (tq, tq), 0)
    cc = lax.broadcasted_iota(jnp.int32, (tq, tq), 1)
    dc = rc - cc
    dist_c = dc.astype(_F32)
    ok_c = (dc >= 0) & (dc < WINDOW)

    outs = []
    for a in range(2):
        h = p + a * (SWA_HEADS // 2)
        slope = slope_ref[h]
        sink = sink_ref[h]
        sp = lax.dot_general(qs[a], kp, (((1,), (1,)), ((), ())), preferred_element_type=_F32)
        sc = lax.dot_general(qs[a], kc, (((1,), (1,)), ((), ())), preferred_element_type=_F32)
        sp = jnp.where(ok_p, sp - slope * dist_p, NEG_INF)
        sc = jnp.where(ok_c, sc - slope * dist_c, NEG_INF)
        m = jnp.maximum(jnp.maximum(jnp.max(sp, axis=-1, keepdims=True), jnp.max(sc, axis=-1, keepdims=True)), sink)
        pp = jnp.exp(sp - m)
        pc = jnp.exp(sc - m)
        den = jnp.sum(pp, axis=-1, keepdims=True) + jnp.sum(pc, axis=-1, keepdims=True) + jnp.exp(sink - m)
        o = jnp.dot(pp.astype(_BF), vp, preferred_element_type=_F32) + jnp.dot(pc.astype(_BF), vc, preferred_element_type=_F32)
        outs.append(o / den)
    lane = lax.broadcasted_iota(jnp.int32, outs[0].shape, 1)
    o_ref[...] = jnp.where(lane < HEAD_DIM, outs[0], outs[1]).astype(_BF)


def _swa_call(slopes, sinks, sq, sk, sv, km, vm, *, tq):
    B, S, _ = sq.shape
    r = tq // WINDOW
    smem = pl.BlockSpec(memory_space=pltpu.SMEM)
    cur = pl.BlockSpec((None, tq, LANES), lambda b, p, i: (b, i, 0))
    prev = pl.BlockSpec((None, WINDOW, LANES), lambda b, p, i: (b, jnp.maximum(i * r - 1, 0), 0))
    meta = pl.BlockSpec((META_BLOCK, LANES), lambda b, p, i: (0, 0))
    qspec = pl.BlockSpec((None, tq, LANES), lambda b, p, i: (b, i, p))
    return pl.pallas_call(
        _swa_kernel, grid=(B, SWA_HEADS // 2, S // tq),
        in_specs=[smem, smem, qspec, cur, prev, cur, prev, meta, meta],
        out_specs=qspec, out_shape=jax.ShapeDtypeStruct((B, S, SWA_QW), _BF),
        compiler_params=_cparams(("arbitrary", "arbitrary", "arbitrary")), name="swa_attn",
    )(slopes, sinks, sq, sk, sk, sv, sv, km, vm)


def _split3(v):
    a = v.astype(_BF)
    r = v - a.astype(_F32)
    b = r.astype(_BF)
    c = (r - b.astype(_F32)).astype(_BF)
    return a, b, c


def _mix_kernel(x_ref, oa_ref, ob_ref, g1_ref, wga_ref, wgb_ref, wfo_ref, wso_ref, wo_ref, g2_ref, wr_ref, br_ref,
                hres_ref, h2_ref, lg_ref):
    x = x_ref[...]
    h = _rms(x, g1_ref[...]).astype(_BF)
    ga = jax.nn.sigmoid(jnp.dot(h, wga_ref[...], preferred_element_type=_F32))
    gb = jax.nn.sigmoid(jnp.dot(h, wgb_ref[...], preferred_element_type=_F32))
    mix = ga * jnp.dot(oa_ref[...], wfo_ref[...], preferred_element_type=_F32) \
        + gb * jnp.dot(ob_ref[...], wso_ref[...], preferred_element_type=_F32)
    hres = x + jnp.dot(mix.astype(_BF), wo_ref[...], preferred_element_type=_F32)
    hres_ref[...] = hres
    h2 = _rms(hres, g2_ref[...])
    h2_ref[...] = h2.astype(_BF)
    wr = wr_ref[...]
    w_hi = wr.astype(_BF)
    w_lo = (wr - w_hi.astype(_F32)).astype(_BF)
    h_hi = h2.astype(_BF)
    h_lo = (h2 - h_hi.astype(_F32)).astype(_BF)
    lg = jnp.dot(h_hi, w_hi, preferred_element_type=_F32) + jnp.dot(h_hi, w_lo, preferred_element_type=_F32) \
        + jnp.dot(h_lo, w_hi, preferred_element_type=_F32)
    lg_ref[...] = lg + br_ref[...]


def _mix_call(x2, oa, ob, g1, wga, wgb, wfo, wso, wo, g2, wr, br, *, tm):
    N, D = x2.shape
    row = lambda w: pl.BlockSpec((tm, w), lambda i: (i, 0))
    full = lambda a: pl.BlockSpec(a.shape, lambda i: (0,) * a.ndim)
    return pl.pallas_call(
        _mix_kernel, grid=(N // tm,),
        in_specs=[row(D), row(FOX_W), row(SWA_QW)] + [full(a) for a in (g1, wga, wgb, wfo, wso, wo, g2, wr, br)],
        out_specs=[row(D), row(D), row(LANES)],
        out_shape=[jax.ShapeDtypeStruct((N, D), _F32), jax.ShapeDtypeStruct((N, D), _BF),
                   jax.ShapeDtypeStruct((N, LANES), _F32)],
        compiler_params=_cparams(("arbitrary",)), name="mix_proj",
    )(x2, oa, ob, g1, wga, wgb, wfo, wso, wo, g2, wr, br)


def _route_kernel(lg_ref, a_ref, at_ref, cnt_ref):
    lg = lg_ref[...]
    T = lg.shape[0]
    lane = lax.broadcasted_iota(jnp.int32, lg.shape, 1)
    cur = lg
    vals, idxs = [], []
    for _ in range(TOP_K):
        m = jnp.max(cur, axis=-1, keepdims=True)
        idx = jnp.min(jnp.where(cur == m, lane, LANES), axis=-1, keepdims=True)
        vals.append(m)
        idxs.append(idx)
        cur = jnp.where(lane == idx, -jnp.inf, cur)
    es = [jnp.exp(v - vals[0]) for v in vals]
    den = es[0] + es[1] + es[2] + es[3]
    gates = [e / den for e in es]

    onehot = jnp.zeros(lg.shape, _F32)
    for idx in idxs:
        onehot = onehot + jnp.where(lane == idx, 1.0, 0.0)
    r_i = lax.broadcasted_iota(jnp.int32, (T, T), 0)
    c_i = lax.broadcasted_iota(jnp.int32, (T, T), 1)
    ltri = jnp.where(c_i < r_i, 1.0, 0.0).astype(_BF)
    prefix = jnp.dot(ltri, onehot.astype(_BF), preferred_element_type=_F32)
    cnt = jnp.sum(onehot, axis=0, keepdims=True)
    chunks = jnp.floor((cnt + (RUN_ALIGN - 1)) * (1.0 / RUN_ALIGN))
    u_r = lax.broadcasted_iota(jnp.int32, (LANES, LANES), 0)
    u_c = lax.broadcasted_iota(jnp.int32, (LANES, LANES), 1)
    utri = jnp.where(u_r < u_c, 1.0, 0.0).astype(_BF)
    lbase = RUN_ALIGN * jnp.dot(jnp.broadcast_to(chunks, (8, LANES)).astype(_BF), utri,
                                preferred_element_type=_F32)[0:1]
    slot_all = prefix + lbase
    a = jnp.zeros(lg.shape, _F32)
    for k in range(TOP_K):
        slot_k = jnp.sum(jnp.where(lane == idxs[k], slot_all, 0.0), axis=-1, keepdims=True)
        a = a + jnp.where(lane == k, slot_k, 0.0) + jnp.where(lane == TOP_K + k, gates[k], 0.0)
    a_ref[...] = a
    at_ref[...] = a.T[0:8]
    cnt_ref[...] = jnp.broadcast_to(cnt, (8, LANES)).astype(jnp.int32)


def _route_call(lg, *, T):
    N = lg.shape[0]
    nt = N // T
    return pl.pallas_call(
        _route_kernel, grid=(nt,),
        in_specs=[pl.BlockSpec((T, LANES), lambda i: (i, 0))],
        out_specs=[pl.BlockSpec((T, LANES), lambda i: (i, 0)), pl.BlockSpec((8, T), lambda i: (0, i)),
                   pl.BlockSpec((None, 8, LANES), lambda i: (i, 0, 0))],
        out_shape=[jax.ShapeDtypeStruct((N, LANES), _F32), jax.ShapeDtypeStruct((8, N), _F32),
                   jax.ShapeDtypeStruct((nt, 8, LANES), jnp.int32)],
        compiler_params=_cparams(("arbitrary",)), name="route",
    )(lg)


def _plan_kernel(cnt_ref, goff_ref, lbase_ref, nch_ref, blk_ref, gap_ref, info_ref, *, nt, nb, block_rows):
    def per_expert(e, base):
        def per_tile(i, run):
            c = cnt_ref[i * N_EXPERTS + e]
            n = (c + (RUN_ALIGN - 1)) // RUN_ALIGN
            goff_ref[i * N_EXPERTS + e] = run
            nch_ref[i * N_EXPERTS + e] = n
            return run + n * RUN_ALIGN
        end = lax.fori_loop(0, nt, per_tile, base)
        nxt = ((end + block_rows - 1) // block_rows) * block_rows
        gap_ref[2 * e] = end
        gap_ref[2 * e + 1] = (nxt - end) // RUN_ALIGN

        def mark(b, c):
            blk_ref[b] = e
            return c
        lax.fori_loop(base // block_rows, nxt // block_rows, mark, 0)
        return nxt
    total = lax.fori_loop(0, N_EXPERTS, per_expert, 0)
    used = total // block_rows
    info_ref[0] = used

    def tail(b, c):
        blk_ref[b] = N_EXPERTS - 1
        return c
    lax.fori_loop(used, nb, tail, 0)

    def per_tile2(i, c):
        def per_e(e, run):
            lbase_ref[i * N_EXPERTS + e] = run
            return run + nch_ref[i * N_EXPERTS + e] * RUN_ALIGN
        lax.fori_loop(0, N_EXPERTS, per_e, 0)
        return c
    lax.fori_loop(0, nt, per_tile2, 0)


def _plan_call(cnt_flat, *, nt, nb, block_rows):
    smem = pl.BlockSpec(memory_space=pltpu.SMEM)
    i32 = lambda n: jax.ShapeDtypeStruct((n,), jnp.int32)
    return pl.pallas_call(
        functools.partial(_plan_kernel, nt=nt, nb=nb, block_rows=block_rows),
        in_specs=[smem], out_specs=[smem] * 6,
        out_shape=[i32(nt * N_EXPERTS), i32(nt * N_EXPERTS), i32(nt * N_EXPERTS), i32(nb), i32(2 * N_EXPERTS), i32(1)],
        name="plan",
    )(cnt_flat)


def _pack_pairs(v):
    w = v.shape[1] // 2
    return pltpu.pack_elementwise([v[:, :w], v[:, w:]], packed_dtype=_BF)


def _unpack_pairs(wd):
    lo = pltpu.unpack_elementwise(wd, index=0, packed_dtype=_BF, unpacked_dtype=_F32)
    hi = pltpu.unpack_elementwise(wd, index=1, packed_dtype=_BF, unpacked_dtype=_F32)
    return lo.astype(_BF), hi.astype(_BF)


def _dispatch_kernel(goff, lbase, nch, gap, info, at_ref, h2_ref, xs_ref, xl_ref, z_ref, sem, zsem, tsem):
    i = pl.program_id(0)
    R = xl_ref.shape[0]
    T = h2_ref.shape[0]
    block_rows = z_ref.shape[0]
    nb = xs_ref.shape[0] // block_rows

    @pl.when(i == 0)
    def _():
        z_ref[...] = jnp.zeros(z_ref.shape, z_ref.dtype)
        z8 = z_ref.at[pl.ds(0, RUN_ALIGN)]

        def fill(e, c):
            start = gap[2 * e]
            n = gap[2 * e + 1]

            def one(j, c2):
                pltpu.make_async_copy(z8, xs_ref.at[pl.ds(pl.multiple_of(start + j * RUN_ALIGN, RUN_ALIGN), RUN_ALIGN)],
                                      zsem).start()
                return c2
            lax.fori_loop(0, n, one, 0)

            def wait(j, c2):
                pltpu.make_async_copy(z8, xs_ref.at[pl.ds(0, RUN_ALIGN)], zsem).wait()
                return c2
            lax.fori_loop(0, n, wait, 0)
            return c
        lax.fori_loop(0, N_EXPERTS, fill, 0)

        def tail(b, c):
            pltpu.make_async_copy(z_ref, xs_ref.at[pl.ds(pl.multiple_of(b * block_rows, block_rows), block_rows)],
                                  tsem).start()
            return c
        lax.fori_loop(info[0], nb, tail, 0)

        def tail_wait(b, c):
            pltpu.make_async_copy(z_ref, xs_ref.at[pl.ds(0, block_rows)], tsem).wait()
            return c
        lax.fori_loop(info[0], nb, tail_wait, 0)

    slot_iota = lax.broadcasted_iota(jnp.int32, (R, T), 0).astype(_F32)
    pm = jnp.zeros((R, T), _F32)
    for k in range(TOP_K):
        pm = pm + jnp.where(slot_iota == at_ref[k:k + 1, :], 1.0, 0.0)
    xl = jnp.dot(pm.astype(_BF), h2_ref[...], preferred_element_type=_F32)
    xl_ref[...] = _pack_pairs(xl)

    def issue(e, total):
        n = nch[i * N_EXPERTS + e]
        src0 = lbase[i * N_EXPERTS + e]
        dst0 = goff[i * N_EXPERTS + e]

        def one(j, c):
            s = pl.multiple_of(src0 + j * RUN_ALIGN, RUN_ALIGN)
            d = pl.multiple_of(dst0 + j * RUN_ALIGN, RUN_ALIGN)
            pltpu.make_async_copy(xl_ref.at[pl.ds(s, RUN_ALIGN)], xs_ref.at[pl.ds(d, RUN_ALIGN)], sem).start()
            return c
        lax.fori_loop(0, n, one, 0)
        return total + n
    total = lax.fori_loop(0, N_EXPERTS, issue, 0)

    def wait(j, c):
        pltpu.make_async_copy(xl_ref.at[pl.ds(0, RUN_ALIGN)], xs_ref.at[pl.ds(0, RUN_ALIGN)], sem).wait()
        return c
    lax.fori_loop(0, total, wait, 0)


def _dispatch_call(goff, lbase, nch, gap, info, at, h2, *, T, R, rows, block_rows):
    N, D = h2.shape
    grid_spec = pltpu.PrefetchScalarGridSpec(
        num_scalar_prefetch=5, grid=(N // T,),
        in_specs=[pl.BlockSpec((8, T), lambda i, *_: (0, i)), pl.BlockSpec((T, D), lambda i, *_: (i, 0))],
        out_specs=pl.BlockSpec(memory_space=pl.ANY),
        scratch_shapes=[pltpu.VMEM((R, D // 2), jnp.uint32), pltpu.VMEM((block_rows, D // 2), jnp.uint32),
                        pltpu.SemaphoreType.DMA, pltpu.SemaphoreType.DMA, pltpu.SemaphoreType.DMA])
    return pl.pallas_call(
        _dispatch_kernel, grid_spec=grid_spec, out_shape=jax.ShapeDtypeStruct((rows, D // 2), jnp.uint32),
        compiler_params=_cparams(("arbitrary",)), name="dispatch",
    )(goff, lbase, nch, gap, info, at, h2)


def _expert_kernel(blk, info, xs_ref, wu_ref, bu_ref, wd_ref, bd_ref, ys_ref):
    b = pl.program_id(0)

    @pl.when(b < info[0])
    def _():
        half = wu_ref.shape[0] // 2
        x_lo, x_hi = _unpack_pairs(xs_ref[...])
        gu = jnp.dot(x_lo, wu_ref[:half, :], preferred_element_type=_F32) \
            + jnp.dot(x_hi, wu_ref[half:, :], preferred_element_type=_F32) + bu_ref[...]
        g = jnp.minimum(gu[:, :D_FF], SWIGLU_LIMIT)
        u = jnp.clip(gu[:, D_FF:], -SWIGLU_LIMIT, SWIGLU_LIMIT)
        act = (u + 1.0) * (g * jax.nn.sigmoid(SWIGLU_ALPHA * g))
        y = jnp.dot(act.astype(_BF), wd_ref[...], preferred_element_type=_F32) + bd_ref[...]
        ys_ref[...] = _pack_pairs(y)

    @pl.when(b >= info[0])
    def _():
        ys_ref[...] = _pack_pairs(jnp.zeros((ys_ref.shape[0], 2 * ys_ref.shape[1]), _F32))


def _expert_call(blk, info, xs, wu, bu, wd, bd, *, block_rows):
    rows, half = xs.shape
    nb = rows // block_rows
    D = 2 * half
    rb = lambda b, blk, info: (jnp.minimum(b, info[0] - 1), 0)
    ex = lambda b, blk, info: (blk[b], 0, 0)
    grid_spec = pltpu.PrefetchScalarGridSpec(
        num_scalar_prefetch=2, grid=(nb,),
        in_specs=[pl.BlockSpec((block_rows, half), rb),
                  pl.BlockSpec((None, D, 2 * D_FF), ex), pl.BlockSpec((None, 1, 2 * D_FF), ex),
                  pl.BlockSpec((None, D_FF, D), ex), pl.BlockSpec((None, 1, D), ex)],
        out_specs=pl.BlockSpec((block_rows, half), lambda b, blk, info: (b, 0)))
    return pl.pallas_call(
        _expert_kernel, grid_spec=grid_spec, out_shape=jax.ShapeDtypeStruct((rows, half), jnp.uint32),
        compiler_params=_cparams(("arbitrary",)), name="experts",
    )(blk, info, xs, wu, bu, wd, bd)


def _combine_kernel(goff, lbase, nch, a_ref, hres_ref, ys_ref, o_ref, yl_ref, sem):
    i = pl.program_id(0)
    R = yl_ref.shape[0]
    T = a_ref.shape[0]

    @pl.when(i == 0)
    def _():
        yl_ref[...] = jnp.zeros(yl_ref.shape, yl_ref.dtype)

    def issue(e, total):
        n = nch[i * N_EXPERTS + e]
        dst0 = lbase[i * N_EXPERTS + e]
        src0 = goff[i * N_EXPERTS + e]

        def one(j, c):
            s = pl.multiple_of(src0 + j * RUN_ALIGN, RUN_ALIGN)
            d = pl.multiple_of(dst0 + j * RUN_ALIGN, RUN_ALIGN)
            pltpu.make_async_copy(ys_ref.at[pl.ds(s, RUN_ALIGN)], yl_ref.at[pl.ds(d, RUN_ALIGN)], sem).start()
            return c
        lax.fori_loop(0, n, one, 0)
        return total + n
    total = lax.fori_loop(0, N_EXPERTS, issue, 0)

    def wait(j, c):
        pltpu.make_async_copy(ys_ref.at[pl.ds(0, RUN_ALIGN)], yl_ref.at[pl.ds(0, RUN_ALIGN)], sem).wait()
        return c
    lax.fori_loop(0, total, wait, 0)

    a = a_ref[...]
    slot_iota = lax.broadcasted_iota(jnp.int32, (T, R), 1).astype(_F32)
    pt = jnp.zeros((T, R), _F32)
    for k in range(TOP_K):
        pt = pt + jnp.where(slot_iota == a[:, k:k + 1], a[:, TOP_K + k:TOP_K + k + 1], 0.0)
    pt = pt.astype(_BF)
    y_lo, y_hi = _unpack_pairs(yl_ref[...])
    half = y_lo.shape[1]
    o_ref[:, :half] = hres_ref[:, :half] + jnp.dot(pt, y_lo, preferred_element_type=_F32)
    o_ref[:, half:] = hres_ref[:, half:] + jnp.dot(pt, y_hi, preferred_element_type=_F32)


def _combine_call(goff, lbase, nch, a, hres, ys, *, T, R):
    N, D = hres.shape
    grid_spec = pltpu.PrefetchScalarGridSpec(
        num_scalar_prefetch=3, grid=(N // T,),
        in_specs=[pl.BlockSpec((T, LANES), lambda i, *_: (i, 0)), pl.BlockSpec((T, D), lambda i, *_: (i, 0)),
                  pl.BlockSpec(memory_space=pl.ANY)],
        out_specs=pl.BlockSpec((T, D), lambda i, *_: (i, 0)),
        scratch_shapes=[pltpu.VMEM((R, D // 2), jnp.uint32), pltpu.SemaphoreType.DMA])
    return pl.pallas_call(
        _combine_kernel, grid_spec=grid_spec, out_shape=jax.ShapeDtypeStruct((N, D), _F32),
        compiler_params=_cparams(("arbitrary",)), name="combine",
    )(goff, lbase, nch, a, hres, ys)


def _tiles(S):
    pick = lambda pref: next(t for t in pref if S % t == 0)
    return dict(tm=pick((512, 256, 128)), tq_fox=pick((512, 256, 128)), tq_swa=pick((256, 128)),
                tm_mix=pick((512, 256, 128)), t_route=pick((256, 128)), block_rows=256)


def kernel(x, meta_tokens, norm1_gain, w_in, b_forget, fox_q_gain, fox_k_gain, swa_q_gain, swa_k_gain, swa_sinks,
           w_fox_out, w_swa_out, w_out, norm2_gain, w_router, b_router, w_up, b_up, w_down, b_down):
    B, S, D = x.shape
    assert norm1_gain.shape[0] == 1 and S % LANES == 0
    tl = _tiles(S)
    N = B * S
    scale = HEAD_DIM ** -0.5

    w = w_in[0]
    c0 = 3 * FOX_W
    c1 = c0 + FOX_HEADS
    c2 = c1 + SWA_QW + 2 * SWA_KW
    wf = w[:, :c0].astype(_BF)
    wfl = w[:, c0:c1].T.astype(_BF)
    perm_heads = np.array([h for p in range(SWA_HEADS // 2) for h in (p, p + SWA_HEADS // 2)])
    perm_cols = (perm_heads[:, None] * HEAD_DIM + np.arange(HEAD_DIM)[None, :]).reshape(-1)
    ws_all = w[:, c1:c2]
    ws = jnp.concatenate([ws_all[:, :SWA_QW][:, perm_cols], ws_all[:, SWA_QW:]], axis=1).astype(_BF)
    wga = w[:, c2:c2 + D].astype(_BF)
    wgb = w[:, c2 + D:].astype(_BF)
    g1 = norm1_gain[0][None, :]
    g2 = norm2_gain[0][None, :]
    bfg = b_forget[0][:, None]
    fqg = jnp.tile(fox_q_gain[0], FOX_HEADS)[None, :] * scale
    fkg = jnp.tile(fox_k_gain[0], FOX_HEADS)[None, :]
    sqg = jnp.tile(swa_q_gain[0], SWA_HEADS)[None, :] * scale
    skg = jnp.tile(swa_k_gain[0], SWA_KV_HEADS)[None, :]
    wfo = w_fox_out[0].astype(_BF)
    wso = w_swa_out[0][perm_cols, :].astype(_BF)
    wo = w_out[0].astype(_BF)
    wr = jnp.pad(w_router[0], ((0, 0), (0, LANES - N_EXPERTS)))
    br = jnp.pad(b_router[0], (0, LANES - N_EXPERTS), constant_values=NEG_INF)[None, :]
    slopes = jnp.exp2(-8.0 * (jnp.arange(SWA_HEADS, dtype=_F32) + 1.0) / SWA_HEADS)
    sinks = swa_sinks[0].astype(_F32)
    wu = w_up[0].astype(_BF)
    wd = w_down[0].astype(_BF)
    bu = b_up[0][:, None, :]
    bd = b_down[0][:, None, :]

    meta_blk = jnp.pad(meta_tokens.astype(x.dtype), ((META_PAD, 0), (0, 0)))[None]
    zero_f = jnp.zeros((FOX_HEADS, 1), _F32)
    _, fk_m, fv_m, _, sk_m, sv_m, nf_m = _qkv_call(meta_blk, g1, wf, wfl, ws, bfg, fqg, fkg, sqg, skg, zero_f,
                                                    tm=META_BLOCK, n_pad=META_PAD)
    f0 = -nf_m[0, :, META_BLOCK - 1:]

    fq, fk, fv, sq, sk, sv, nf = _qkv_call(x, g1, wf, wfl, ws, bfg, fqg, fkg, sqg, skg, f0, tm=tl["tm"], n_pad=0)
    o_a = _fox_call(fq, fk, fv, nf, fk_m[0], fv_m[0], nf_m[0], tq=tl["tq_fox"])
    o_b = _swa_call(slopes, sinks, sq, sk, sv, sk_m[0], sv_m[0], tq=tl["tq_swa"])

    hres, h2, lg = _mix_call(x.reshape(N, D), o_a.reshape(N, FOX_W), o_b.reshape(N, SWA_QW),
                             g1, wga, wgb, wfo, wso, wo, g2, wr, br, tm=tl["tm_mix"])

    T = tl["t_route"]
    nt = N // T
    block_rows = tl["block_rows"]
    R = -(-(TOP_K * T + N_EXPERTS * (RUN_ALIGN - 1)) // LANES) * LANES
    max_rows = N * TOP_K + nt * N_EXPERTS * (RUN_ALIGN - 1) + N_EXPERTS * (block_rows - RUN_ALIGN)
    nb = -(-max_rows // block_rows)
    a, at, cnt = _route_call(lg, T=T)
    goff, lbase, nch, blk, gap, info = _plan_call(cnt[:, 0, :N_EXPERTS].reshape(-1), nt=nt, nb=nb, block_rows=block_rows)
    xs = _dispatch_call(goff, lbase, nch, gap, info, at, h2, T=T, R=R, rows=nb * block_rows, block_rows=block_rows)
    ys = _expert_call(blk, info, xs, wu, bu, wd, bd, block_rows=block_rows)
    out = _combine_call(goff, lbase, nch, a, hres, ys, T=T, R=R)
    return out.reshape(B, S, D)
```

```python
import functools

import jax
import jax.numpy as jnp
import numpy as np
from jax import lax
from jax.experimental import pallas as pl
from jax.experimental.pallas import tpu as pltpu

HEAD_DIM = 64
N_META = 16
FOX_HEADS = 8
SWA_HEADS = 8
SWA_KV_HEADS = 2
WINDOW = 128
N_EXPERTS = 32
TOP_K = 4
D_FF = 1024
SWIGLU_LIMIT = 7.0
SWIGLU_ALPHA = 1.702
RMS_EPS = 1e-6
NEG_INF = -1e30
LOG2E = 1.4426950408889634

LANES = 128
META_BLOCK = 128
META_PAD = META_BLOCK - N_META
RUN_ALIGN = 8
VMEM_LIMIT = 56 * 1024 * 1024

FOX_W = FOX_HEADS * HEAD_DIM
SWA_QW = SWA_HEADS * HEAD_DIM
SWA_KW = SWA_KV_HEADS * HEAD_DIM

_BF = jnp.bfloat16
_F32 = jnp.float32


def _cparams(sem):
    return pltpu.CompilerParams(dimension_semantics=sem, vmem_limit_bytes=VMEM_LIMIT)


def _rms(t, gain):
    return t * lax.rsqrt(jnp.mean(t * t, axis=-1, keepdims=True) + RMS_EPS) * gain


def _head_rms(z, gain_row):
    lane = lax.broadcasted_iota(jnp.int32, (z.shape[0], LANES), 1)
    lo_mask = lane < HEAD_DIM
    outs = []
    for b in range(z.shape[1] // LANES):
        v = z[:, b * LANES:(b + 1) * LANES]
        v2 = v * v
        tot = jnp.sum(v2, axis=-1, keepdims=True)
        lo = jnp.sum(jnp.where(lo_mask, v2, 0.0), axis=-1, keepdims=True)
        hi = tot - lo
        r_lo = lax.rsqrt(lo * (1.0 / HEAD_DIM) + RMS_EPS)
        r_hi = lax.rsqrt(hi * (1.0 / HEAD_DIM) + RMS_EPS)
        outs.append(v * jnp.where(lo_mask, r_lo, r_hi))
    return jnp.concatenate(outs, axis=-1) * gain_row


def _lane_cumsum(x):
    n = x.shape[-1]
    lane = lax.broadcasted_iota(jnp.int32, x.shape, x.ndim - 1)
    s = 1
    while s < n:
        x = x + jnp.where(lane >= s, pltpu.roll(x, s, x.ndim - 1), 0.0)
        s *= 2
    return x


def _qkv_kernel(x_ref, g1_ref, wqk_ref, wvt_ref, wfl_ref, ws_ref, wsvt_ref, bf_ref, fqg_ref, fkg_ref, sqg_ref, skg_ref,
                f0_ref, fq_ref, fk_ref, aug_ref, fvt_ref, sq_ref, sk_ref, svt_ref, fend_ref, carry_ref, *, n_pad):
    j = pl.program_id(1)

    @pl.when(j == 0)
    def _():
        carry_ref[...] = f0_ref[...]

    x = x_ref[...]
    tm = x.shape[0]
    h = _rms(x, g1_ref[...]).astype(_BF)
    nt = (((1,), (1,)), ((), ()))
    f = jnp.dot(h, wqk_ref[...], preferred_element_type=_F32)
    s = jnp.dot(h, ws_ref[...], preferred_element_type=_F32)
    svt = lax.dot_general(wsvt_ref[...], h, nt, preferred_element_type=_F32)
    vt = lax.dot_general(wvt_ref[...], h, nt, preferred_element_type=_F32)
    flog = lax.dot_general(wfl_ref[...], h, nt, preferred_element_type=_F32)

    fq_ref[...] = _head_rms(f[:, :FOX_W], fqg_ref[...]).astype(_BF)
    fk_ref[...] = _head_rms(f[:, FOX_W:], fkg_ref[...]).astype(_BF)
    sq_ref[...] = _head_rms(s[:, :SWA_QW], sqg_ref[...]).astype(_BF)
    sk_ref[...] = _head_rms(s[:, SWA_QW:], skg_ref[...]).astype(_BF)

    ones_row = jnp.where(lax.broadcasted_iota(jnp.int32, (HEAD_DIM, tm), 0) == 0, 1.0, 0.0).astype(_BF)
    for hd in range(FOX_HEADS):
        fvt_ref[hd] = jnp.concatenate([vt[hd * HEAD_DIM:(hd + 1) * HEAD_DIM].astype(_BF), ones_row], axis=0)
    for g in range(SWA_KV_HEADS):
        blk = jnp.concatenate([svt[g * HEAD_DIM:(g + 1) * HEAD_DIM].astype(_BF), ones_row], axis=0)
        for c in range(tm // WINDOW):
            svt_ref[g, c] = blk[:, c * WINDOW:(c + 1) * WINDOW]

    z = flog + bf_ref[...]
    log_f = jnp.minimum(z, 0.0) - jnp.log1p(jnp.exp(-jnp.abs(z)))
    if n_pad:
        lane = lax.broadcasted_iota(jnp.int32, log_f.shape, 1)
        log_f = jnp.where(lane >= n_pad, log_f, 0.0)
    cum = _lane_cumsum(log_f) + carry_ref[...]
    carry_ref[...] = cum[:, -1:]
    fend_ref[...] = cum[:, -1:]
    nfl = cum * (-LOG2E)
    p1 = nfl.astype(_BF).astype(_F32)
    p2 = (nfl - p1).astype(_BF).astype(_F32)
    p3 = (nfl - p1 - p2).astype(_BF).astype(_F32)
    g = jnp.concatenate([p1, p2, p3, jnp.zeros((LANES - 3 * FOX_HEADS, tm), _F32)], axis=0)
    aug_ref[...] = g.T.astype(_BF)


def _qkv_call(x3, g1, wqk, wvt, wfl, ws, wsvt, bfg, fqg, fkg, sqg, skg, f0, *, tm, n_pad):
    B, S, D = x3.shape
    nj = S // tm
    row = lambda w: pl.BlockSpec((None, tm, w), lambda b, j: (b, j, 0))
    full = lambda a: pl.BlockSpec(a.shape, lambda b, j: (0,) * a.ndim)
    bsd = lambda w: jax.ShapeDtypeStruct((B, S, w), _BF)
    out_shape = [bsd(FOX_W), bsd(FOX_W), bsd(LANES), jax.ShapeDtypeStruct((B, FOX_HEADS, nj, 2 * HEAD_DIM, tm), _BF),
                 bsd(SWA_QW), bsd(SWA_KW),
                 jax.ShapeDtypeStruct((B, SWA_KV_HEADS, S // WINDOW, 2 * HEAD_DIM, WINDOW), _BF),
                 jax.ShapeDtypeStruct((B, FOX_HEADS, 1), _F32)]
    out_specs = [row(FOX_W), row(FOX_W), row(LANES),
                 pl.BlockSpec((None, FOX_HEADS, None, 2 * HEAD_DIM, tm), lambda b, j: (b, 0, j, 0, 0)),
                 row(SWA_QW), row(SWA_KW),
                 pl.BlockSpec((None, SWA_KV_HEADS, tm // WINDOW, 2 * HEAD_DIM, WINDOW), lambda b, j: (b, 0, j, 0, 0)),
                 pl.BlockSpec((None, FOX_HEADS, 1), lambda b, j: (b, 0, 0))]
    return pl.pallas_call(
        functools.partial(_qkv_kernel, n_pad=n_pad),
        grid=(B, nj),
        in_specs=[row(D)] + [full(a) for a in (g1, wqk, wvt, wfl, ws, wsvt, bfg, fqg, fkg, sqg, skg, f0)],
        out_specs=out_specs, out_shape=out_shape,
        scratch_shapes=[pltpu.VMEM((FOX_HEADS, 1), _F32)],
        compiler_params=_cparams(("arbitrary", "arbitrary")),
        name="qkv_proj",
    )(x3, g1, wqk, wvt, wfl, ws, wsvt, bfg, fqg, fkg, sqg, skg, f0)


def _fox_kernel(q_ref, k_ref, aug_ref, vt_ref, km_ref, augm_ref, vtm_ref, o_ref, m_ref, acc_ref, s_ref):
    p = pl.program_id(1)
    qi = pl.program_id(2)
    tq = q_ref.shape[0]
    tk = vt_ref.shape[-1]
    nt = (((1,), (1,)), ((), ()))

    q = q_ref[...]
    lane = lax.broadcasted_iota(jnp.int32, q.shape, 1)
    qaug = []
    for a in range(2):
        keep = (lane < HEAD_DIM) if a == 0 else (lane >= HEAD_DIM)
        ones = jnp.where((lane < 3 * FOX_HEADS) & ((lane & (FOX_HEADS - 1)) == 2 * p + a), 1.0, 0.0).astype(_BF)
        qaug.append(jnp.concatenate([jnp.where(keep, q, jnp.zeros_like(q)), ones], axis=1))
    qaug = jnp.concatenate(qaug, axis=0)

    def scores(kaug):
        return lax.dot_general(kaug, qaug, nt, preferred_element_type=_F32)

    def tile_scores(ki):
        off = pl.multiple_of(ki * tk, tk)
        return scores(jnp.concatenate([k_ref[pl.ds(off, tk), :], aug_ref[pl.ds(off, tk), :]], axis=1))

    def process(st, vts, mask, first):
        if mask is not None:
            st = jnp.where(mask, st, NEG_INF)
        mx = jnp.max(st, axis=0, keepdims=True)
        m_new = mx if first else jnp.maximum(m_ref[...], mx)
        pt = jnp.exp2(st - m_new).astype(_BF)
        if not first:
            alpha = jnp.exp2(m_ref[...] - m_new)
        for a in range(2):
            pv = jnp.dot(vts[a], pt[:, a * tq:(a + 1) * tq], preferred_element_type=_F32)
            acc_ref[a] = pv if first else alpha[:, a * tq:(a + 1) * tq] * acc_ref[a] + pv
        m_ref[...] = m_new

    s_ref[...] = tile_scores(0)
    mask_m = lax.broadcasted_iota(jnp.int32, (META_BLOCK, 2 * tq), 0) >= META_PAD
    process(scores(jnp.concatenate([km_ref[...], augm_ref[...]], axis=1)), (vtm_ref[0], vtm_ref[1]), mask_m, True)

    def body(ki, c):
        nxt = tile_scores(ki + 1)
        process(s_ref[...], (vt_ref[0, ki], vt_ref[1, ki]), None, False)
        s_ref[...] = nxt
        return c
    lax.fori_loop(0, qi, body, 0)
    key = lax.broadcasted_iota(jnp.int32, (tk, 2 * tq), 0)
    qry = lax.broadcasted_iota(jnp.int32, (tk, 2 * tq), 1) & (tq - 1)
    process(s_ref[...], (vt_ref[0, qi], vt_ref[1, qi]), key <= qry, False)

    outs = []
    for a in range(2):
        acc = acc_ref[a]
        outs.append(acc[:HEAD_DIM] / acc[HEAD_DIM:HEAD_DIM + 1])
    o_ref[...] = jnp.concatenate(outs, axis=0).T.astype(_BF)


def _fox_call(fq, fk, aug, fvt, km, augm, vtm, *, tq):
    B, S, _ = fq.shape
    nk = fvt.shape[2]
    assert fvt.shape[-1] == tq
    qspec = pl.BlockSpec((None, tq, LANES), lambda b, p, i: (b, i, p))
    return pl.pallas_call(
        _fox_kernel, grid=(B, FOX_HEADS // 2, S // tq),
        in_specs=[qspec,
                  pl.BlockSpec((None, S, LANES), lambda b, p, i: (b, 0, p)),
                  pl.BlockSpec((None, S, LANES), lambda b, p, i: (b, 0, 0)),
                  pl.BlockSpec((None, 2, nk, 2 * HEAD_DIM, tq), lambda b, p, i: (b, p, 0, 0, 0)),
                  pl.BlockSpec((META_BLOCK, LANES), lambda b, p, i: (0, p)),
                  pl.BlockSpec((META_BLOCK, LANES), lambda b, p, i: (0, 0)),
                  pl.BlockSpec((2, 2 * HEAD_DIM, META_BLOCK), lambda b, p, i: (p, 0, 0))],
        out_specs=qspec, out_shape=jax.ShapeDtypeStruct((B, S, FOX_W), _BF),
        scratch_shapes=[pltpu.VMEM((1, 2 * tq), _F32), pltpu.VMEM((2, 2 * HEAD_DIM, tq), _F32),
                        pltpu.VMEM((tq, 2 * tq), _F32)],
        compiler_params=_cparams(("arbitrary", "arbitrary", "arbitrary")), name="fox_attn",
    )(fq, fk, aug, fvt, km, augm, vtm)


def _swa_bias(tq):
    j = np.arange(WINDOW + tq)[:, None] - WINDOW
    i = np.arange(tq)[None, :]
    dist = i - j
    ok = (dist >= 0) & (dist < WINDOW)
    slopes = np.exp2(-8.0 * (np.arange(SWA_HEADS, dtype=np.float32) + 1.0) / SWA_HEADS).astype(np.float32)
    per_head = np.where(ok[None], -slopes[:, None, None] * dist[None].astype(np.float32), np.float32(NEG_INF))
    rep = SWA_HEADS // SWA_KV_HEADS
    return np.stack([np.concatenate(list(per_head[g * rep:(g + 1) * rep]), axis=1) for g in range(SWA_KV_HEADS)])


def _swa_kernel(q_ref, kc_ref, kp_ref, km_ref, vc_ref, vp_ref, vm_ref, bias_ref, sink_ref, o_ref):
    qi = pl.program_id(1)
    tq = q_ref.shape[0]
    rep = SWA_HEADS // SWA_KV_HEADS
    nt = (((1,), (1,)), ((), ()))
    first = qi == 0
    kcat = jnp.concatenate([jnp.where(first, km_ref[...], kp_ref[...]), kc_ref[...]], axis=0)
    row = lax.broadcasted_iota(jnp.int32, (WINDOW + tq, rep * tq), 0)
    row_ok = row >= jnp.where(first, META_PAD, 0)
    lane = lax.broadcasted_iota(jnp.int32, (tq, LANES), 1)
    outs = [None] * SWA_HEADS
    for g in range(SWA_KV_HEADS):
        keep = (lane < HEAD_DIM) if g == 0 else (lane >= HEAD_DIM)
        qg = jnp.concatenate([jnp.where(keep, q_ref[:, p * LANES:(p + 1) * LANES], jnp.zeros((tq, LANES), _BF))
                              for p in range(rep)], axis=0)
        vcat = jnp.concatenate([jnp.where(first, vm_ref[g], vp_ref[g, 0])] + [vc_ref[g, c] for c in range(tq // WINDOW)],
                               axis=1)
        st = lax.dot_general(kcat, qg, nt, preferred_element_type=_F32) + bias_ref[g]
        st = jnp.where(row_ok, st, NEG_INF)
        sink = sink_ref[g]
        m = jnp.maximum(jnp.max(st, axis=0, keepdims=True), sink)
        pt = jnp.exp(st - m).astype(_BF)
        acc = jnp.dot(vcat, pt, preferred_element_type=_F32)
        o = acc[:HEAD_DIM] / (acc[HEAD_DIM:HEAD_DIM + 1] + jnp.exp(sink - m))
        for p in range(rep):
            outs[2 * p + g] = o[:, p * tq:(p + 1) * tq]
    o_ref[...] = jnp.concatenate(outs, axis=0).T.astype(_BF)


def _swa_call(sinks, sq, sk, svt, km, vtm, *, tq):
    B, S, _ = sq.shape
    r = tq // WINDOW
    rep = SWA_HEADS // SWA_KV_HEADS
    bias = jnp.asarray(_swa_bias(tq))
    sink_rows = jnp.repeat(sinks.reshape(SWA_KV_HEADS, rep), tq, axis=1)[:, None, :]
    prev_blk = lambda i: jnp.maximum(i * r - 1, 0)
    return pl.pallas_call(
        _swa_kernel, grid=(B, S // tq),
        in_specs=[pl.BlockSpec((None, tq, SWA_QW), lambda b, i: (b, i, 0)),
                  pl.BlockSpec((None, tq, LANES), lambda b, i: (b, i, 0)),
                  pl.BlockSpec((None, WINDOW, LANES), lambda b, i: (b, prev_blk(i), 0)),
                  pl.BlockSpec((META_BLOCK, LANES), lambda b, i: (0, 0)),
                  pl.BlockSpec((None, SWA_KV_HEADS, r, 2 * HEAD_DIM, WINDOW), lambda b, i: (b, 0, i, 0, 0)),
                  pl.BlockSpec((None, SWA_KV_HEADS, 1, 2 * HEAD_DIM, WINDOW), lambda b, i: (b, 0, prev_blk(i), 0, 0)),
                  pl.BlockSpec((SWA_KV_HEADS, 2 * HEAD_DIM, META_BLOCK), lambda b, i: (0, 0, 0)),
                  pl.BlockSpec(bias.shape, lambda b, i: (0, 0, 0)),
                  pl.BlockSpec(sink_rows.shape, lambda b, i: (0, 0, 0))],
        out_specs=pl.BlockSpec((None, tq, SWA_QW), lambda b, i: (b, i, 0)),
        out_shape=jax.ShapeDtypeStruct((B, S, SWA_QW), _BF),
        compiler_params=_cparams(("arbitrary", "arbitrary")), name="swa_attn",
    )(sq, sk, sk, km, svt, svt, vtm, bias, sink_rows)


def _split3(v):
    a = v.astype(_BF)
    r = v - a.astype(_F32)
    b = r.astype(_BF)
    c = (r - b.astype(_F32)).astype(_BF)
    return a, b, c


def _mix_kernel(x_ref, oa_ref, ob_ref, g1_ref, wga_ref, wgb_ref, wfo_ref, wso_ref, wo_ref, g2_ref, wr_ref, br_ref,
                hres_ref, h2_ref, lg_ref):
    x = x_ref[...]
    h = _rms(x, g1_ref[...]).astype(_BF)
    ga = jax.nn.sigmoid(jnp.dot(h, wga_ref[...], preferred_element_type=_F32))
    gb = jax.nn.sigmoid(jnp.dot(h, wgb_ref[...], preferred_element_type=_F32))
    mix = ga * jnp.dot(oa_ref[...], wfo_ref[...], preferred_element_type=_F32) \
        + gb * jnp.dot(ob_ref[...], wso_ref[...], preferred_element_type=_F32)
    hres = x + jnp.dot(mix.astype(_BF), wo_ref[...], preferred_element_type=_F32)
    hres_ref[...] = hres
    h2 = _rms(hres, g2_ref[...])
    h2_ref[...] = h2.astype(_BF)
    wr = wr_ref[...]
    w_hi = wr.astype(_BF)
    w_lo = (wr - w_hi.astype(_F32)).astype(_BF)
    h_hi = h2.astype(_BF)
    h_lo = (h2 - h_hi.astype(_F32)).astype(_BF)
    lg = jnp.dot(h_hi, w_hi, preferred_element_type=_F32) + jnp.dot(h_hi, w_lo, preferred_element_type=_F32) \
        + jnp.dot(h_lo, w_hi, preferred_element_type=_F32)
    lg_ref[...] = lg + br_ref[...]


def _mix_call(x2, oa, ob, g1, wga, wgb, wfo, wso, wo, g2, wr, br, *, tm):
    N, D = x2.shape
    row = lambda w: pl.BlockSpec((tm, w), lambda i: (i, 0))
    full = lambda a: pl.BlockSpec(a.shape, lambda i: (0,) * a.ndim)
    return pl.pallas_call(
        _mix_kernel, grid=(N // tm,),
        in_specs=[row(D), row(FOX_W), row(SWA_QW)] + [full(a) for a in (g1, wga, wgb, wfo, wso, wo, g2, wr, br)],
        out_specs=[row(D), row(D), row(LANES)],
        out_shape=[jax.ShapeDtypeStruct((N, D), _F32), jax.ShapeDtypeStruct((N, D), _BF),
                   jax.ShapeDtypeStruct((N, LANES), _F32)],
        compiler_params=_cparams(("arbitrary",)), name="mix_proj",
    )(x2, oa, ob, g1, wga, wgb, wfo, wso, wo, g2, wr, br)


def _route_kernel(lg_ref, a_ref, at_ref, cnt_ref):
    lg = lg_ref[...]
    T = lg.shape[0]
    lane = lax.broadcasted_iota(jnp.int32, lg.shape, 1)
    cur = lg
    vals, idxs = [], []
    for _ in range(TOP_K):
        m = jnp.max(cur, axis=-1, keepdims=True)
        idx = jnp.min(jnp.where(cur == m, lane, LANES), axis=-1, keepdims=True)
        vals.append(m)
        idxs.append(idx)
        cur = jnp.where(lane == idx, -jnp.inf, cur)
    es = [jnp.exp(v - vals[0]) for v in vals]
    den = es[0] + es[1] + es[2] + es[3]
    gates = [e / den for e in es]

    onehot = jnp.zeros(lg.shape, _F32)
    for idx in idxs:
        onehot = onehot + jnp.where(lane == idx, 1.0, 0.0)
    r_i = lax.broadcasted_iota(jnp.int32, (T, T), 0)
    c_i = lax.broadcasted_iota(jnp.int32, (T, T), 1)
    ltri = jnp.where(c_i < r_i, 1.0, 0.0).astype(_BF)
    prefix = jnp.dot(ltri, onehot.astype(_BF), preferred_element_type=_F32)
    cnt = jnp.sum(onehot, axis=0, keepdims=True)
    chunks = jnp.floor((cnt + (RUN_ALIGN - 1)) * (1.0 / RUN_ALIGN))
    u_r = lax.broadcasted_iota(jnp.int32, (LANES, LANES), 0)
    u_c = lax.broadcasted_iota(jnp.int32, (LANES, LANES), 1)
    utri = jnp.where(u_r < u_c, 1.0, 0.0).astype(_BF)
    lbase = RUN_ALIGN * jnp.dot(jnp.broadcast_to(chunks, (8, LANES)).astype(_BF), utri,
                                preferred_element_type=_F32)[0:1]
    slot_all = prefix + lbase
    a = jnp.zeros(lg.shape, _F32)
    for k in range(TOP_K):
        slot_k = jnp.sum(jnp.where(lane == idxs[k], slot_all, 0.0), axis=-1, keepdims=True)
        a = a + jnp.where(lane == k, slot_k, 0.0) + jnp.where(lane == TOP_K + k, gates[k], 0.0)
    a_ref[...] = a
    at_ref[...] = a.T[0:8]
    cnt_ref[...] = jnp.broadcast_to(cnt, (8, LANES)).astype(jnp.int32)


def _route_call(lg, *, T):
    N = lg.shape[0]
    nt = N // T
    return pl.pallas_call(
        _route_kernel, grid=(nt,),
        in_specs=[pl.BlockSpec((T, LANES), lambda i: (i, 0))],
        out_specs=[pl.BlockSpec((T, LANES), lambda i: (i, 0)), pl.BlockSpec((8, T), lambda i: (0, i)),
                   pl.BlockSpec((None, 8, LANES), lambda i: (i, 0, 0))],
        out_shape=[jax.ShapeDtypeStruct((N, LANES), _F32), jax.ShapeDtypeStruct((8, N), _F32),
                   jax.ShapeDtypeStruct((nt, 8, LANES), jnp.int32)],
        compiler_params=_cparams(("arbitrary",)), name="route",
    )(lg)


def _plan_kernel(cnt_ref, goff_ref, lbase_ref, nch_ref, blk_ref, gap_ref, info_ref, *, nt, nb, block_rows):
    def per_expert(e, base):
        def per_tile(i, run):
            c = cnt_ref[i * N_EXPERTS + e]
            n = (c + (RUN_ALIGN - 1)) // RUN_ALIGN
            goff_ref[i * N_EXPERTS + e] = run
            nch_ref[i * N_EXPERTS + e] = n
            return run + n * RUN_ALIGN
        end = lax.fori_loop(0, nt, per_tile, base)
        nxt = ((end + block_rows - 1) // block_rows) * block_rows
        gap_ref[2 * e] = end
        gap_ref[2 * e + 1] = (nxt - end) // RUN_ALIGN

        def mark(b, c):
            blk_ref[b] = e
            return c
        lax.fori_loop(base // block_rows, nxt // block_rows, mark, 0)
        return nxt
    total = lax.fori_loop(0, N_EXPERTS, per_expert, 0)
    used = total // block_rows
    info_ref[0] = used

    def tail(b, c):
        blk_ref[b] = N_EXPERTS - 1
        return c
    lax.fori_loop(used, nb, tail, 0)

    def per_tile2(i, c):
        def per_e(e, run):
            lbase_ref[i * N_EXPERTS + e] = run
            return run + nch_ref[i * N_EXPERTS + e] * RUN_ALIGN
        lax.fori_loop(0, N_EXPERTS, per_e, 0)
        return c
    lax.fori_loop(0, nt, per_tile2, 0)


def _plan_call(cnt_flat, *, nt, nb, block_rows):
    smem = pl.BlockSpec(memory_space=pltpu.SMEM)
    i32 = lambda n: jax.ShapeDtypeStruct((n,), jnp.int32)
    return pl.pallas_call(
        functools.partial(_plan_kernel, nt=nt, nb=nb, block_rows=block_rows),
        in_specs=[smem], out_specs=[smem] * 6,
        out_shape=[i32(nt * N_EXPERTS), i32(nt * N_EXPERTS), i32(nt * N_EXPERTS), i32(nb), i32(2 * N_EXPERTS), i32(1)],
        name="plan",
    )(cnt_flat)


def _pack_pairs(v):
    w = v.shape[1] // 2
    return pltpu.pack_elementwise([v[:, :w], v[:, w:]], packed_dtype=_BF)


def _unpack_pairs(wd):
    lo = pltpu.unpack_elementwise(wd, index=0, packed_dtype=_BF, unpacked_dtype=_F32)
    hi = pltpu.unpack_elementwise(wd, index=1, packed_dtype=_BF, unpacked_dtype=_F32)
    return lo.astype(_BF), hi.astype(_BF)


def _dispatch_kernel(goff, lbase, nch, gap, info, at_ref, h2_ref, xs_ref, xl_ref, z_ref, sem, zsem, tsem):
    i = pl.program_id(0)
    R = xl_ref.shape[0]
    T = h2_ref.shape[0]
    block_rows = z_ref.shape[0]
    nb = xs_ref.shape[0] // block_rows

    @pl.when(i == 0)
    def _():
        z_ref[...] = jnp.zeros(z_ref.shape, z_ref.dtype)
        z8 = z_ref.at[pl.ds(0, RUN_ALIGN)]

        def fill(e, c):
            start = gap[2 * e]
            n = gap[2 * e + 1]

            def one(j, c2):
                pltpu.make_async_copy(z8, xs_ref.at[pl.ds(pl.multiple_of(start + j * RUN_ALIGN, RUN_ALIGN), RUN_ALIGN)],
                                      zsem).start()
                return c2
            lax.fori_loop(0, n, one, 0)

            def wait(j, c2):
                pltpu.make_async_copy(z8, xs_ref.at[pl.ds(0, RUN_ALIGN)], zsem).wait()
                return c2
            lax.fori_loop(0, n, wait, 0)
            return c
        lax.fori_loop(0, N_EXPERTS, fill, 0)

        def tail(b, c):
            pltpu.make_async_copy(z_ref, xs_ref.at[pl.ds(pl.multiple_of(b * block_rows, block_rows), block_rows)],
                                  tsem).start()
            return c
        lax.fori_loop(info[0], nb, tail, 0)

        def tail_wait(b, c):
            pltpu.make_async_copy(z_ref, xs_ref.at[pl.ds(0, block_rows)], tsem).wait()
            return c
        lax.fori_loop(info[0], nb, tail_wait, 0)

    slot_iota = lax.broadcasted_iota(jnp.int32, (R, T), 0).astype(_F32)
    pm = jnp.zeros((R, T), _F32)
    for k in range(TOP_K):
        pm = pm + jnp.where(slot_iota == at_ref[k:k + 1, :], 1.0, 0.0)
    xl = jnp.dot(pm.astype(_BF), h2_ref[...], preferred_element_type=_F32)
    xl_ref[...] = _pack_pairs(xl)

    def issue(e, total):
        n = nch[i * N_EXPERTS + e]
        src0 = lbase[i * N_EXPERTS + e]
        dst0 = goff[i * N_EXPERTS + e]

        def one(j, c):
            s = pl.multiple_of(src0 + j * RUN_ALIGN, RUN_ALIGN)
            d = pl.multiple_of(dst0 + j * RUN_ALIGN, RUN_ALIGN)
            pltpu.make_async_copy(xl_ref.at[pl.ds(s, RUN_ALIGN)], xs_ref.at[pl.ds(d, RUN_ALIGN)], sem).start()
            return c
        lax.fori_loop(0, n, one, 0)
        return total + n
    total = lax.fori_loop(0, N_EXPERTS, issue, 0)

    def wait(j, c):
        pltpu.make_async_copy(xl_ref.at[pl.ds(0, RUN_ALIGN)], xs_ref.at[pl.ds(0, RUN_ALIGN)], sem).wait()
        return c
    lax.fori_loop(0, total, wait, 0)


def _dispatch_call(goff, lbase, nch, gap, info, at, h2, *, T, R, rows, block_rows):
    N, D = h2.shape
    grid_spec = pltpu.PrefetchScalarGridSpec(
        num_scalar_prefetch=5, grid=(N // T,),
        in_specs=[pl.BlockSpec((8, T), lambda i, *_: (0, i)), pl.BlockSpec((T, D), lambda i, *_: (i, 0))],
        out_specs=pl.BlockSpec(memory_space=pl.ANY),
        scratch_shapes=[pltpu.VMEM((R, D // 2), jnp.uint32), pltpu.VMEM((block_rows, D // 2), jnp.uint32),
                        pltpu.SemaphoreType.DMA, pltpu.SemaphoreType.DMA, pltpu.SemaphoreType.DMA])
    return pl.pallas_call(
        _dispatch_kernel, grid_spec=grid_spec, out_shape=jax.ShapeDtypeStruct((rows, D // 2), jnp.uint32),
        compiler_params=_cparams(("arbitrary",)), name="dispatch",
    )(goff, lbase, nch, gap, info, at, h2)


def _expert_kernel(blk, info, xs_ref, wu_ref, bu_ref, wd_ref, bd_ref, ys_ref):
    b = pl.program_id(0)

    @pl.when(b < info[0])
    def _():
        half = wu_ref.shape[0] // 2
        x_lo, x_hi = _unpack_pairs(xs_ref[...])
        gu = jnp.dot(x_lo, wu_ref[:half, :], preferred_element_type=_F32) \
            + jnp.dot(x_hi, wu_ref[half:, :], preferred_element_type=_F32) + bu_ref[...]
        g = jnp.minimum(gu[:, :D_FF], SWIGLU_LIMIT)
        u = jnp.clip(gu[:, D_FF:], -SWIGLU_LIMIT, SWIGLU_LIMIT)
        act = (u + 1.0) * (g * jax.nn.sigmoid(SWIGLU_ALPHA * g))
        y = jnp.dot(act.astype(_BF), wd_ref[...], preferred_element_type=_F32) + bd_ref[...]
        ys_ref[...] = _pack_pairs(y)

    @pl.when(b >= info[0])
    def _():
        ys_ref[...] = _pack_pairs(jnp.zeros((ys_ref.shape[0], 2 * ys_ref.shape[1]), _F32))


def _expert_call(blk, info, xs, wu, bu, wd, bd, *, block_rows):
    rows, half = xs.shape
    nb = rows // block_rows
    D = 2 * half
    rb = lambda b, blk, info: (jnp.minimum(b, info[0] - 1), 0)
    ex = lambda b, blk, info: (blk[b], 0, 0)
    grid_spec = pltpu.PrefetchScalarGridSpec(
        num_scalar_prefetch=2, grid=(nb,),
        in_specs=[pl.BlockSpec((block_rows, half), rb),
                  pl.BlockSpec((None, D, 2 * D_FF), ex), pl.BlockSpec((None, 1, 2 * D_FF), ex),
                  pl.BlockSpec((None, D_FF, D), ex), pl.BlockSpec((None, 1, D), ex)],
        out_specs=pl.BlockSpec((block_rows, half), lambda b, blk, info: (b, 0)))
    return pl.pallas_call(
        _expert_kernel, grid_spec=grid_spec, out_shape=jax.ShapeDtypeStruct((rows, half), jnp.uint32),
        compiler_params=_cparams(("arbitrary",)), name="experts",
    )(blk, info, xs, wu, bu, wd, bd)


def _combine_kernel(goff, lbase, nch, a_ref, hres_ref, ys_ref, o_ref, yl_ref, sem):
    i = pl.program_id(0)
    R = yl_ref.shape[0]
    T = a_ref.shape[0]

    @pl.when(i == 0)
    def _():
        yl_ref[...] = jnp.zeros(yl_ref.shape, yl_ref.dtype)

    def issue(e, total):
        n = nch[i * N_EXPERTS + e]
        dst0 = lbase[i * N_EXPERTS + e]
        src0 = goff[i * N_EXPERTS + e]

        def one(j, c):
            s = pl.multiple_of(src0 + j * RUN_ALIGN, RUN_ALIGN)
            d = pl.multiple_of(dst0 + j * RUN_ALIGN, RUN_ALIGN)
            pltpu.make_async_copy(ys_ref.at[pl.ds(s, RUN_ALIGN)], yl_ref.at[pl.ds(d, RUN_ALIGN)], sem).start()
            return c
        lax.fori_loop(0, n, one, 0)
        return total + n
    total = lax.fori_loop(0, N_EXPERTS, issue, 0)

    def wait(j, c):
        pltpu.make_async_copy(ys_ref.at[pl.ds(0, RUN_ALIGN)], yl_ref.at[pl.ds(0, RUN_ALIGN)], sem).wait()
        return c
    lax.fori_loop(0, total, wait, 0)

    a = a_ref[...]
    slot_iota = lax.broadcasted_iota(jnp.int32, (T, R), 1).astype(_F32)
    pt = jnp.zeros((T, R), _F32)
    for k in range(TOP_K):
        pt = pt + jnp.where(slot_iota == a[:, k:k + 1], a[:, TOP_K + k:TOP_K + k + 1], 0.0)
    pt = pt.astype(_BF)
    y_lo, y_hi = _unpack_pairs(yl_ref[...])
    half = y_lo.shape[1]
    o_ref[:, :half] = hres_ref[:, :half] + jnp.dot(pt, y_lo, preferred_element_type=_F32)
    o_ref[:, half:] = hres_ref[:, half:] + jnp.dot(pt, y_hi, preferred_element_type=_F32)


def _combine_call(goff, lbase, nch, a, hres, ys, *, T, R):
    N, D = hres.shape
    grid_spec = pltpu.PrefetchScalarGridSpec(
        num_scalar_prefetch=3, grid=(N // T,),
        in_specs=[pl.BlockSpec((T, LANES), lambda i, *_: (i, 0)), pl.BlockSpec((T, D), lambda i, *_: (i, 0)),
                  pl.BlockSpec(memory_space=pl.ANY)],
        out_specs=pl.BlockSpec((T, D), lambda i, *_: (i, 0)),
        scratch_shapes=[pltpu.VMEM((R, D // 2), jnp.uint32), pltpu.SemaphoreType.DMA])
    return pl.pallas_call(
        _combine_kernel, grid_spec=grid_spec, out_shape=jax.ShapeDtypeStruct((N, D), _F32),
        compiler_params=_cparams(("arbitrary",)), name="combine",
    )(goff, lbase, nch, a, hres, ys)


def _tiles(S):
    pick = lambda pref: next(t for t in pref if S % t == 0)
    tm = pick((512, 256, 128))
    return dict(tm=tm, tq_fox=tm, tq_swa=pick((256, 128)),
                tm_mix=pick((512, 256, 128)), t_route=pick((256, 128)), block_rows=256)


def kernel(x, meta_tokens, norm1_gain, w_in, b_forget, fox_q_gain, fox_k_gain, swa_q_gain, swa_k_gain, swa_sinks,
           w_fox_out, w_swa_out, w_out, norm2_gain, w_router, b_router, w_up, b_up, w_down, b_down):
    B, S, D = x.shape
    assert norm1_gain.shape[0] == 1 and S % LANES == 0
    tl = _tiles(S)
    N = B * S
    scale = HEAD_DIM ** -0.5

    w = w_in[0]
    c0 = 3 * FOX_W
    c1 = c0 + FOX_HEADS
    c2 = c1 + SWA_QW + 2 * SWA_KW
    wqk = w[:, :2 * FOX_W].astype(_BF)
    wvt = w[:, 2 * FOX_W:c0].T.astype(_BF)
    wfl = w[:, c0:c1].T.astype(_BF)
    perm_heads = np.array([h for p in range(SWA_HEADS // 2) for h in (p, p + SWA_HEADS // 2)])
    perm_cols = (perm_heads[:, None] * HEAD_DIM + np.arange(HEAD_DIM)[None, :]).reshape(-1)
    ws_all = w[:, c1:c2]
    ws = jnp.concatenate([ws_all[:, :SWA_QW][:, perm_cols], ws_all[:, SWA_QW:SWA_QW + SWA_KW]], axis=1).astype(_BF)
    wsvt = ws_all[:, SWA_QW + SWA_KW:].T.astype(_BF)
    wga = w[:, c2:c2 + D].astype(_BF)
    wgb = w[:, c2 + D:].astype(_BF)
    g1 = norm1_gain[0][None, :]
    g2 = norm2_gain[0][None, :]
    bfg = b_forget[0][:, None]
    fqg = jnp.tile(fox_q_gain[0], FOX_HEADS)[None, :] * (scale * LOG2E)
    fkg = jnp.tile(fox_k_gain[0], FOX_HEADS)[None, :]
    sqg = jnp.tile(swa_q_gain[0], SWA_HEADS)[None, :] * scale
    skg = jnp.tile(swa_k_gain[0], SWA_KV_HEADS)[None, :]
    wfo = w_fox_out[0].astype(_BF)
    wso = w_swa_out[0][perm_cols, :].astype(_BF)
    wo = w_out[0].astype(_BF)
    wr = jnp.pad(w_router[0], ((0, 0), (0, LANES - N_EXPERTS)))
    br = jnp.pad(b_router[0], (0, LANES - N_EXPERTS), constant_values=NEG_INF)[None, :]
    sinks = swa_sinks[0].astype(_F32)
    wu = w_up[0].astype(_BF)
    wd = w_down[0].astype(_BF)
    bu = b_up[0][:, None, :]
    bd = b_down[0][:, None, :]

    meta_blk = jnp.pad(meta_tokens.astype(x.dtype), ((META_PAD, 0), (0, 0)))[None]
    zero_f = jnp.zeros((FOX_HEADS, 1), _F32)
    _, fk_m, aug_m, fvt_m, _, sk_m, svt_m, fend_m = _qkv_call(meta_blk, g1, wqk, wvt, wfl, ws, wsvt, bfg, fqg, fkg, sqg,
                                                              skg, zero_f, tm=META_BLOCK, n_pad=META_PAD)
    fq, fk, aug, fvt, sq, sk, svt, _ = _qkv_call(x, g1, wqk, wvt, wfl, ws, wsvt, bfg, fqg, fkg, sqg, skg, fend_m[0],
                                                 tm=tl["tm"], n_pad=0)
    o_a = _fox_call(fq, fk, aug, fvt, fk_m[0], aug_m[0], fvt_m[0, :, 0], tq=tl["tq_fox"])
    o_b = _swa_call(sinks, sq, sk, svt, sk_m[0], svt_m[0, :, 0], tq=tl["tq_swa"])

    hres, h2, lg = _mix_call(x.reshape(N, D), o_a.reshape(N, FOX_W), o_b.reshape(N, SWA_QW),
                             g1, wga, wgb, wfo, wso, wo, g2, wr, br, tm=tl["tm_mix"])

    T = tl["t_route"]
    nt = N // T
    block_rows = tl["block_rows"]
    R = -(-(TOP_K * T + N_EXPERTS * (RUN_ALIGN - 1)) // LANES) * LANES
    max_rows = N * TOP_K + nt * N_EXPERTS * (RUN_ALIGN - 1) + N_EXPERTS * (block_rows - RUN_ALIGN)
    nb = -(-max_rows // block_rows)
    a, at, cnt = _route_call(lg, T=T)
    goff, lbase, nch, blk, gap, info = _plan_call(cnt[:, 0, :N_EXPERTS].reshape(-1), nt=nt, nb=nb, block_rows=block_rows)
    xs = _dispatch_call(goff, lbase, nch, gap, info, at, h2, T=T, R=R, rows=nb * block_rows, block_rows=block_rows)
    ys = _expert_call(blk, info, xs, wu, bu, wd, bd, block_rows=block_rows)
    out = _combine_call(goff, lbase, nch, a, hres, ys, T=T, R=R)
    return out.reshape(B, S, D)
```

```python
import functools

import jax
import jax.numpy as jnp
import numpy as np
from jax import lax
from jax.experimental import pallas as pl
from jax.experimental.pallas import tpu as pltpu

HEAD_DIM = 64
N_META = 16
FOX_HEADS = 8
SWA_HEADS = 8
SWA_KV_HEADS = 2
WINDOW = 128
N_EXPERTS = 32
TOP_K = 4
D_FF = 1024
SWIGLU_LIMIT = 7.0
SWIGLU_ALPHA = 1.702
RMS_EPS = 1e-6
NEG_INF = -1e30
LOG2E = 1.4426950408889634

LANES = 128
META_BLOCK = 128
META_PAD = META_BLOCK - N_META
RUN_ALIGN = 8
VMEM_LIMIT = 56 * 1024 * 1024

FOX_W = FOX_HEADS * HEAD_DIM
SWA_QW = SWA_HEADS * HEAD_DIM
SWA_KW = SWA_KV_HEADS * HEAD_DIM

_BF = jnp.bfloat16
_F32 = jnp.float32


def _cparams(sem):
    return pltpu.CompilerParams(dimension_semantics=sem, vmem_limit_bytes=VMEM_LIMIT)


def _rms(t, gain):
    return t * lax.rsqrt(jnp.mean(t * t, axis=-1, keepdims=True) + RMS_EPS) * gain


def _head_rms(z, gain_row):
    lane = lax.broadcasted_iota(jnp.int32, (z.shape[0], LANES), 1)
    lo_mask = lane < HEAD_DIM
    outs = []
    for b in range(z.shape[1] // LANES):
        v = z[:, b * LANES:(b + 1) * LANES]
        v2 = v * v
        tot = jnp.sum(v2, axis=-1, keepdims=True)
        lo = jnp.sum(jnp.where(lo_mask, v2, 0.0), axis=-1, keepdims=True)
        hi = tot - lo
        r_lo = lax.rsqrt(lo * (1.0 / HEAD_DIM) + RMS_EPS)
        r_hi = lax.rsqrt(hi * (1.0 / HEAD_DIM) + RMS_EPS)
        outs.append(v * jnp.where(lo_mask, r_lo, r_hi))
    return jnp.concatenate(outs, axis=-1) * gain_row


def _lane_cumsum(x):
    n = x.shape[-1]
    lane = lax.broadcasted_iota(jnp.int32, x.shape, x.ndim - 1)
    s = 1
    while s < n:
        x = x + jnp.where(lane >= s, pltpu.roll(x, s, x.ndim - 1), 0.0)
        s *= 2
    return x


def _qkv_kernel(x_ref, g1_ref, wqk_ref, wvt_ref, wfl_ref, ws_ref, wsvt_ref, bf_ref, fqg_ref, fkg_ref, sqg_ref, skg_ref,
                f0_ref, fq_ref, fk_ref, aug_ref, fvt_ref, sq_ref, sk_ref, svt_ref, fend_ref, carry_ref, *, n_pad):
    j = pl.program_id(1)

    @pl.when(j == 0)
    def _():
        carry_ref[...] = f0_ref[...]

    x = x_ref[...]
    tm = x.shape[0]
    h = _rms(x, g1_ref[...]).astype(_BF)
    nt = (((1,), (1,)), ((), ()))
    f = jnp.dot(h, wqk_ref[...], preferred_element_type=_F32)
    s = jnp.dot(h, ws_ref[...], preferred_element_type=_F32)
    svt = lax.dot_general(wsvt_ref[...], h, nt, preferred_element_type=_F32)
    vt = lax.dot_general(wvt_ref[...], h, nt, preferred_element_type=_F32)
    flog = lax.dot_general(wfl_ref[...], h, nt, preferred_element_type=_F32)

    fq_ref[...] = _head_rms(f[:, :FOX_W], fqg_ref[...]).astype(_BF)
    fk_ref[...] = _head_rms(f[:, FOX_W:], fkg_ref[...]).astype(_BF)
    sq_ref[...] = _head_rms(s[:, :SWA_QW], sqg_ref[...]).astype(_BF)
    sk_ref[...] = _head_rms(s[:, SWA_QW:], skg_ref[...]).astype(_BF)

    ones_row = jnp.where(lax.broadcasted_iota(jnp.int32, (HEAD_DIM, tm), 0) == 0, 1.0, 0.0).astype(_BF)
    for hd in range(FOX_HEADS):
        fvt_ref[hd] = jnp.concatenate([vt[hd * HEAD_DIM:(hd + 1) * HEAD_DIM].astype(_BF), ones_row], axis=0)
    for g in range(SWA_KV_HEADS):
        blk = jnp.concatenate([svt[g * HEAD_DIM:(g + 1) * HEAD_DIM].astype(_BF), ones_row], axis=0)
        for c in range(tm // WINDOW):
            svt_ref[g, c] = blk[:, c * WINDOW:(c + 1) * WINDOW]

    z = flog + bf_ref[...]
    log_f = jnp.minimum(z, 0.0) - jnp.log1p(jnp.exp(-jnp.abs(z)))
    if n_pad:
        lane = lax.broadcasted_iota(jnp.int32, log_f.shape, 1)
        log_f = jnp.where(lane >= n_pad, log_f, 0.0)
    cum = _lane_cumsum(log_f) + carry_ref[...]
    carry_ref[...] = cum[:, -1:]
    fend_ref[...] = cum[:, -1:]
    nfl = cum * (-LOG2E)
    p1 = nfl.astype(_BF).astype(_F32)
    p2 = (nfl - p1).astype(_BF).astype(_F32)
    p3 = (nfl - p1 - p2).astype(_BF).astype(_F32)
    g = jnp.concatenate([p1, p2, p3, jnp.zeros((LANES - 3 * FOX_HEADS, tm), _F32)], axis=0)
    aug_ref[...] = g.T.astype(_BF)


def _qkv_call(x3, g1, wqk, wvt, wfl, ws, wsvt, bfg, fqg, fkg, sqg, skg, f0, *, tm, n_pad):
    B, S, D = x3.shape
    nj = S // tm
    row = lambda w: pl.BlockSpec((None, tm, w), lambda b, j: (b, j, 0))
    full = lambda a: pl.BlockSpec(a.shape, lambda b, j: (0,) * a.ndim)
    bsd = lambda w: jax.ShapeDtypeStruct((B, S, w), _BF)
    out_shape = [bsd(FOX_W), bsd(FOX_W), bsd(LANES), jax.ShapeDtypeStruct((B, FOX_HEADS, nj, 2 * HEAD_DIM, tm), _BF),
                 bsd(SWA_QW), bsd(SWA_KW),
                 jax.ShapeDtypeStruct((B, SWA_KV_HEADS, S // WINDOW, 2 * HEAD_DIM, WINDOW), _BF),
                 jax.ShapeDtypeStruct((B, FOX_HEADS, 1), _F32)]
    out_specs = [row(FOX_W), row(FOX_W), row(LANES),
                 pl.BlockSpec((None, FOX_HEADS, None, 2 * HEAD_DIM, tm), lambda b, j: (b, 0, j, 0, 0)),
                 row(SWA_QW), row(SWA_KW),
                 pl.BlockSpec((None, SWA_KV_HEADS, tm // WINDOW, 2 * HEAD_DIM, WINDOW), lambda b, j: (b, 0, j, 0, 0)),
                 pl.BlockSpec((None, FOX_HEADS, 1), lambda b, j: (b, 0, 0))]
    return pl.pallas_call(
        functools.partial(_qkv_kernel, n_pad=n_pad),
        grid=(B, nj),
        in_specs=[row(D)] + [full(a) for a in (g1, wqk, wvt, wfl, ws, wsvt, bfg, fqg, fkg, sqg, skg, f0)],
        out_specs=out_specs, out_shape=out_shape,
        scratch_shapes=[pltpu.VMEM((FOX_HEADS, 1), _F32)],
        compiler_params=_cparams(("arbitrary", "arbitrary")),
        name="qkv_proj",
    )(x3, g1, wqk, wvt, wfl, ws, wsvt, bfg, fqg, fkg, sqg, skg, f0)


def _fox_kernel(q_ref, k_ref, aug_ref, vt_ref, km_ref, augm_ref, vtm_ref, o_ref, m_ref, acc_ref, s_ref):
    p = pl.program_id(1)
    qi = pl.program_id(2)
    tq = q_ref.shape[0]
    tk = vt_ref.shape[-1]
    nt = (((1,), (1,)), ((), ()))

    q = q_ref[...]
    lane = lax.broadcasted_iota(jnp.int32, q.shape, 1)
    qaug = []
    for a in range(2):
        keep = (lane < HEAD_DIM) if a == 0 else (lane >= HEAD_DIM)
        ones = jnp.where((lane < 3 * FOX_HEADS) & ((lane & (FOX_HEADS - 1)) == 2 * p + a), 1.0, 0.0).astype(_BF)
        qaug.append(jnp.concatenate([jnp.where(keep, q, jnp.zeros_like(q)), ones], axis=1))
    qaug = jnp.concatenate(qaug, axis=0)

    def scores(kaug):
        return lax.dot_general(kaug, qaug, nt, preferred_element_type=_F32)

    def tile_scores(ki):
        off = pl.multiple_of(ki * tk, tk)
        return scores(jnp.concatenate([k_ref[pl.ds(off, tk), :], aug_ref[pl.ds(off, tk), :]], axis=1))

    def process(st, vts, mask, first):
        if mask is not None:
            st = jnp.where(mask, st, NEG_INF)
        mx = jnp.max(st, axis=0, keepdims=True)
        m_new = mx if first else jnp.maximum(m_ref[...], mx)
        pt = jnp.exp2(st - m_new).astype(_BF)
        if not first:
            alpha = jnp.exp2(m_ref[...] - m_new)
        for a in range(2):
            pv = jnp.dot(vts[a], pt[:, a * tq:(a + 1) * tq], preferred_element_type=_F32)
            acc_ref[a] = pv if first else alpha[:, a * tq:(a + 1) * tq] * acc_ref[a] + pv
        m_ref[...] = m_new

    s_ref[...] = tile_scores(0)
    mask_m = lax.broadcasted_iota(jnp.int32, (META_BLOCK, 2 * tq), 0) >= META_PAD
    process(scores(jnp.concatenate([km_ref[...], augm_ref[...]], axis=1)), (vtm_ref[0], vtm_ref[1]), mask_m, True)

    def body(ki, c):
        nxt = tile_scores(ki + 1)
        process(s_ref[...], (vt_ref[0, ki], vt_ref[1, ki]), None, False)
        s_ref[...] = nxt
        return c
    lax.fori_loop(0, qi, body, 0)
    key = lax.broadcasted_iota(jnp.int32, (tk, 2 * tq), 0)
    qry = lax.broadcasted_iota(jnp.int32, (tk, 2 * tq), 1) & (tq - 1)
    process(s_ref[...], (vt_ref[0, qi], vt_ref[1, qi]), key <= qry, False)

    outs = []
    for a in range(2):
        acc = acc_ref[a]
        outs.append(acc[:HEAD_DIM] / acc[HEAD_DIM:HEAD_DIM + 1])
    o_ref[...] = jnp.concatenate(outs, axis=0).T.astype(_BF)


def _fox_call(fq, fk, aug, fvt, km, augm, vtm, *, tq):
    B, S, _ = fq.shape
    nk = fvt.shape[2]
    assert fvt.shape[-1] == tq
    qspec = pl.BlockSpec((None, tq, LANES), lambda b, p, i: (b, i, p))
    return pl.pallas_call(
        _fox_kernel, grid=(B, FOX_HEADS // 2, S // tq),
        in_specs=[qspec,
                  pl.BlockSpec((None, S, LANES), lambda b, p, i: (b, 0, p)),
                  pl.BlockSpec((None, S, LANES), lambda b, p, i: (b, 0, 0)),
                  pl.BlockSpec((None, 2, nk, 2 * HEAD_DIM, tq), lambda b, p, i: (b, p, 0, 0, 0)),
                  pl.BlockSpec((META_BLOCK, LANES), lambda b, p, i: (0, p)),
                  pl.BlockSpec((META_BLOCK, LANES), lambda b, p, i: (0, 0)),
                  pl.BlockSpec((2, 2 * HEAD_DIM, META_BLOCK), lambda b, p, i: (p, 0, 0))],
        out_specs=qspec, out_shape=jax.ShapeDtypeStruct((B, S, FOX_W), _BF),
        scratch_shapes=[pltpu.VMEM((1, 2 * tq), _F32), pltpu.VMEM((2, 2 * HEAD_DIM, tq), _F32),
                        pltpu.VMEM((tq, 2 * tq), _F32)],
        compiler_params=_cparams(("arbitrary", "arbitrary", "arbitrary")), name="fox_attn",
    )(fq, fk, aug, fvt, km, augm, vtm)


def _swa_bias(tq):
    j = np.arange(WINDOW + tq)[:, None] - WINDOW
    i = np.arange(tq)[None, :]
    dist = i - j
    ok = (dist >= 0) & (dist < WINDOW)
    slopes = np.exp2(-8.0 * (np.arange(SWA_HEADS, dtype=np.float32) + 1.0) / SWA_HEADS).astype(np.float32)
    per_head = np.where(ok[None], -slopes[:, None, None] * dist[None].astype(np.float32), np.float32(NEG_INF))
    rep = SWA_HEADS // SWA_KV_HEADS
    return np.stack([np.concatenate(list(per_head[g * rep:(g + 1) * rep]), axis=1) for g in range(SWA_KV_HEADS)])


def _swa_kernel(q_ref, kc_ref, kp_ref, km_ref, vc_ref, vp_ref, vm_ref, bias_ref, sink_ref, o_ref):
    qi = pl.program_id(1)
    tq = q_ref.shape[0]
    rep = SWA_HEADS // SWA_KV_HEADS
    nt = (((1,), (1,)), ((), ()))
    first = qi == 0
    kcat = jnp.concatenate([jnp.where(first, km_ref[...], kp_ref[...]), kc_ref[...]], axis=0)
    row = lax.broadcasted_iota(jnp.int32, (WINDOW + tq, rep * tq), 0)
    row_ok = row >= jnp.where(first, META_PAD, 0)
    lane = lax.broadcasted_iota(jnp.int32, (tq, LANES), 1)
    outs = [None] * SWA_HEADS
    for g in range(SWA_KV_HEADS):
        keep = (lane < HEAD_DIM) if g == 0 else (lane >= HEAD_DIM)
        qg = jnp.concatenate([jnp.where(keep, q_ref[:, p * LANES:(p + 1) * LANES], jnp.zeros((tq, LANES), _BF))
                              for p in range(rep)], axis=0)
        vcat = jnp.concatenate([jnp.where(first, vm_ref[g], vp_ref[g, 0])] + [vc_ref[g, c] for c in range(tq // WINDOW)],
                               axis=1)
        st = lax.dot_general(kcat, qg, nt, preferred_element_type=_F32) + bias_ref[g]
        st = jnp.where(row_ok, st, NEG_INF)
        sink = sink_ref[g]
        m = jnp.maximum(jnp.max(st, axis=0, keepdims=True), sink)
        pt = jnp.exp(st - m).astype(_BF)
        acc = jnp.dot(vcat, pt, preferred_element_type=_F32)
        o = acc[:HEAD_DIM] / (acc[HEAD_DIM:HEAD_DIM + 1] + jnp.exp(sink - m))
        for p in range(rep):
            outs[2 * p + g] = o[:, p * tq:(p + 1) * tq]
    o_ref[...] = jnp.concatenate(outs, axis=0).T.astype(_BF)


def _swa_call(sinks, sq, sk, svt, km, vtm, *, tq):
    B, S, _ = sq.shape
    r = tq // WINDOW
    rep = SWA_HEADS // SWA_KV_HEADS
    bias = jnp.asarray(_swa_bias(tq))
    sink_rows = jnp.repeat(sinks.reshape(SWA_KV_HEADS, rep), tq, axis=1)[:, None, :]
    prev_blk = lambda i: jnp.maximum(i * r - 1, 0)
    return pl.pallas_call(
        _swa_kernel, grid=(B, S // tq),
        in_specs=[pl.BlockSpec((None, tq, SWA_QW), lambda b, i: (b, i, 0)),
                  pl.BlockSpec((None, tq, LANES), lambda b, i: (b, i, 0)),
                  pl.BlockSpec((None, WINDOW, LANES), lambda b, i: (b, prev_blk(i), 0)),
                  pl.BlockSpec((META_BLOCK, LANES), lambda b, i: (0, 0)),
                  pl.BlockSpec((None, SWA_KV_HEADS, r, 2 * HEAD_DIM, WINDOW), lambda b, i: (b, 0, i, 0, 0)),
                  pl.BlockSpec((None, SWA_KV_HEADS, 1, 2 * HEAD_DIM, WINDOW), lambda b, i: (b, 0, prev_blk(i), 0, 0)),
                  pl.BlockSpec((SWA_KV_HEADS, 2 * HEAD_DIM, META_BLOCK), lambda b, i: (0, 0, 0)),
                  pl.BlockSpec(bias.shape, lambda b, i: (0, 0, 0)),
                  pl.BlockSpec(sink_rows.shape, lambda b, i: (0, 0, 0))],
        out_specs=pl.BlockSpec((None, tq, SWA_QW), lambda b, i: (b, i, 0)),
        out_shape=jax.ShapeDtypeStruct((B, S, SWA_QW), _BF),
        compiler_params=_cparams(("arbitrary", "arbitrary")), name="swa_attn",
    )(sq, sk, sk, km, svt, svt, vtm, bias, sink_rows)


def _split3(v):
    a = v.astype(_BF)
    r = v - a.astype(_F32)
    b = r.astype(_BF)
    c = (r - b.astype(_F32)).astype(_BF)
    return a, b, c


def _mix_kernel(x_ref, oa_ref, ob_ref, g1_ref, wga_ref, wgb_ref, wfo_ref, wso_ref, wo_ref, g2_ref, wr_ref, br_ref,
                hres_ref, h2_ref, lg_ref):
    x = x_ref[...]
    h = _rms(x, g1_ref[...]).astype(_BF)
    ga = jax.nn.sigmoid(jnp.dot(h, wga_ref[...], preferred_element_type=_F32))
    gb = jax.nn.sigmoid(jnp.dot(h, wgb_ref[...], preferred_element_type=_F32))
    mix = ga * jnp.dot(oa_ref[...], wfo_ref[...], preferred_element_type=_F32) \
        + gb * jnp.dot(ob_ref[...], wso_ref[...], preferred_element_type=_F32)
    hres = x + jnp.dot(mix.astype(_BF), wo_ref[...], preferred_element_type=_F32)
    hres_ref[...] = hres
    h2 = _rms(hres, g2_ref[...])
    h2_ref[...] = h2.astype(_BF)
    wr = wr_ref[...]
    w_hi = wr.astype(_BF)
    w_lo = (wr - w_hi.astype(_F32)).astype(_BF)
    h_hi = h2.astype(_BF)
    h_lo = (h2 - h_hi.astype(_F32)).astype(_BF)
    lg = jnp.dot(h_hi, w_hi, preferred_element_type=_F32) + jnp.dot(h_hi, w_lo, preferred_element_type=_F32) \
        + jnp.dot(h_lo, w_hi, preferred_element_type=_F32)
    lg_ref[...] = lg + br_ref[...]


def _mix_call(x2, oa, ob, g1, wga, wgb, wfo, wso, wo, g2, wr, br, *, tm):
    N, D = x2.shape
    row = lambda w: pl.BlockSpec((tm, w), lambda i: (i, 0))
    full = lambda a: pl.BlockSpec(a.shape, lambda i: (0,) * a.ndim)
    return pl.pallas_call(
        _mix_kernel, grid=(N // tm,),
        in_specs=[row(D), row(FOX_W), row(SWA_QW)] + [full(a) for a in (g1, wga, wgb, wfo, wso, wo, g2, wr, br)],
        out_specs=[row(D), row(D), row(LANES)],
        out_shape=[jax.ShapeDtypeStruct((N, D), _F32), jax.ShapeDtypeStruct((N, D), _BF),
                   jax.ShapeDtypeStruct((N, LANES), _F32)],
        compiler_params=_cparams(("arbitrary",)), name="mix_proj",
    )(x2, oa, ob, g1, wga, wgb, wfo, wso, wo, g2, wr, br)


def _route_kernel(lg_ref, a_ref, at_ref, cnt_ref):
    lg = lg_ref[...]
    T = lg.shape[0]
    lane = lax.broadcasted_iota(jnp.int32, lg.shape, 1)
    cur = lg
    vals, idxs = [], []
    for _ in range(TOP_K):
        m = jnp.max(cur, axis=-1, keepdims=True)
        idx = jnp.min(jnp.where(cur == m, lane, LANES), axis=-1, keepdims=True)
        vals.append(m)
        idxs.append(idx)
        cur = jnp.where(lane == idx, -jnp.inf, cur)
    es = [jnp.exp(v - vals[0]) for v in vals]
    den = es[0] + es[1] + es[2] + es[3]
    gates = [e / den for e in es]

    onehot = jnp.zeros(lg.shape, _F32)
    for idx in idxs:
        onehot = onehot + jnp.where(lane == idx, 1.0, 0.0)
    r_i = lax.broadcasted_iota(jnp.int32, (T, T), 0)
    c_i = lax.broadcasted_iota(jnp.int32, (T, T), 1)
    ltri = jnp.where(c_i < r_i, 1.0, 0.0).astype(_BF)
    prefix = jnp.dot(ltri, onehot.astype(_BF), preferred_element_type=_F32)
    cnt = jnp.sum(onehot, axis=0, keepdims=True)
    chunks = jnp.floor((cnt + (RUN_ALIGN - 1)) * (1.0 / RUN_ALIGN))
    u_r = lax.broadcasted_iota(jnp.int32, (LANES, LANES), 0)
    u_c = lax.broadcasted_iota(jnp.int32, (LANES, LANES), 1)
    utri = jnp.where(u_r < u_c, 1.0, 0.0).astype(_BF)
    lbase = RUN_ALIGN * jnp.dot(jnp.broadcast_to(chunks, (8, LANES)).astype(_BF), utri,
                                preferred_element_type=_F32)[0:1]
    slot_all = prefix + lbase
    a = jnp.zeros(lg.shape, _F32)
    for k in range(TOP_K):
        slot_k = jnp.sum(jnp.where(lane == idxs[k], slot_all, 0.0), axis=-1, keepdims=True)
        a = a + jnp.where(lane == k, slot_k, 0.0) + jnp.where(lane == TOP_K + k, gates[k], 0.0)
    a_ref[...] = a
    at_ref[...] = a.T[0:8]
    cnt_ref[...] = jnp.broadcast_to(cnt, (8, LANES)).astype(jnp.int32)


def _route_call(lg, *, T):
    N = lg.shape[0]
    nt = N // T
    return pl.pallas_call(
        _route_kernel, grid=(nt,),
        in_specs=[pl.BlockSpec((T, LANES), lambda i: (i, 0))],
        out_specs=[pl.BlockSpec((T, LANES), lambda i: (i, 0)), pl.BlockSpec((8, T), lambda i: (0, i)),
                   pl.BlockSpec((None, 8, LANES), lambda i: (i, 0, 0))],
        out_shape=[jax.ShapeDtypeStruct((N, LANES), _F32), jax.ShapeDtypeStruct((8, N), _F32),
                   jax.ShapeDtypeStruct((nt, 8, LANES), jnp.int32)],
        compiler_params=_cparams(("arbitrary",)), name="route",
    )(lg)


def _div_pow2(v, d):
    assert d & (d - 1) == 0
    return lax.shift_right_logical(v, d.bit_length() - 1)


def _plan_kernel(cnt_ref, goff_ref, lbase_ref, nch_ref, eblk_ref, gap_ref, info_ref, *, nt, block_rows):
    def per_expert(e, base):
        def per_tile(i, run):
            n = _div_pow2(cnt_ref[i * N_EXPERTS + e] + (RUN_ALIGN - 1), RUN_ALIGN)
            goff_ref[i * N_EXPERTS + e] = run
            nch_ref[i * N_EXPERTS + e] = n
            return run + n * RUN_ALIGN
        end = lax.fori_loop(0, nt, per_tile, base)
        nblk = _div_pow2(end - base + (block_rows - 1), block_rows)
        nxt = base + nblk * block_rows
        gap_ref[2 * e] = end
        gap_ref[2 * e + 1] = _div_pow2(nxt - end, RUN_ALIGN)
        eblk_ref[2 * e] = _div_pow2(base, block_rows)
        eblk_ref[2 * e + 1] = nblk
        return nxt
    total = lax.fori_loop(0, N_EXPERTS, per_expert, 0)
    info_ref[0] = _div_pow2(total, block_rows)

    def per_tile2(i, c):
        def per_e(e, run):
            lbase_ref[i * N_EXPERTS + e] = run
            return run + nch_ref[i * N_EXPERTS + e] * RUN_ALIGN
        lax.fori_loop(0, N_EXPERTS, per_e, 0)
        return c
    lax.fori_loop(0, nt, per_tile2, 0)


def _plan_call(cnt_flat, *, nt, block_rows):
    smem = pl.BlockSpec(memory_space=pltpu.SMEM)
    i32 = lambda n: jax.ShapeDtypeStruct((n,), jnp.int32)
    return pl.pallas_call(
        functools.partial(_plan_kernel, nt=nt, block_rows=block_rows),
        in_specs=[smem], out_specs=[smem] * 6,
        out_shape=[i32(nt * N_EXPERTS), i32(nt * N_EXPERTS), i32(nt * N_EXPERTS), i32(2 * N_EXPERTS), i32(2 * N_EXPERTS),
                   i32(1)],
        name="plan",
    )(cnt_flat)


def _pack_pairs(v):
    w = v.shape[1] // 2
    return pltpu.pack_elementwise([v[:, :w], v[:, w:]], packed_dtype=_BF)


def _unpack_pairs(wd):
    lo = pltpu.unpack_elementwise(wd, index=0, packed_dtype=_BF, unpacked_dtype=_F32)
    hi = pltpu.unpack_elementwise(wd, index=1, packed_dtype=_BF, unpacked_dtype=_F32)
    return lo.astype(_BF), hi.astype(_BF)


def _dispatch_kernel(goff, lbase, nch, gap, info, at_ref, h2_ref, xs_ref, xl_ref, z_ref, sem, zsem, tsem):
    i = pl.program_id(0)
    R = xl_ref.shape[0]
    T = h2_ref.shape[0]
    block_rows = z_ref.shape[0]
    nb = xs_ref.shape[0] // block_rows

    @pl.when(i == 0)
    def _():
        z_ref[...] = jnp.zeros(z_ref.shape, z_ref.dtype)

        def gap_copy(e):
            rows = pl.multiple_of(gap[2 * e + 1] * RUN_ALIGN, RUN_ALIGN)
            return pltpu.make_async_copy(z_ref.at[pl.ds(0, rows)],
                                         xs_ref.at[pl.ds(pl.multiple_of(gap[2 * e], RUN_ALIGN), rows)], zsem)

        def fill(e, c):
            @pl.when(gap[2 * e + 1] > 0)
            def _():
                gap_copy(e).start()
            return c
        lax.fori_loop(0, N_EXPERTS, fill, 0)

        def fill_wait(e, c):
            @pl.when(gap[2 * e + 1] > 0)
            def _():
                gap_copy(e).wait()
            return c
        lax.fori_loop(0, N_EXPERTS, fill_wait, 0)

        def tail(b, c):
            pltpu.make_async_copy(z_ref, xs_ref.at[pl.ds(pl.multiple_of(b * block_rows, block_rows), block_rows)],
                                  tsem).start()
            return c
        lax.fori_loop(info[0], nb, tail, 0)

        def tail_wait(b, c):
            pltpu.make_async_copy(z_ref, xs_ref.at[pl.ds(0, block_rows)], tsem).wait()
            return c
        lax.fori_loop(info[0], nb, tail_wait, 0)

    slot_iota = lax.broadcasted_iota(jnp.int32, (R, T), 0).astype(_F32)
    pm = jnp.zeros((R, T), _F32)
    for k in range(TOP_K):
        pm = pm + jnp.where(slot_iota == at_ref[k:k + 1, :], 1.0, 0.0)
    xl = jnp.dot(pm.astype(_BF), h2_ref[...], preferred_element_type=_F32)
    xl_ref[...] = _pack_pairs(xl)

    def issue(e, total):
        n = nch[i * N_EXPERTS + e]
        src0 = lbase[i * N_EXPERTS + e]
        dst0 = goff[i * N_EXPERTS + e]

        @pl.when(n > 0)
        def _():
            rows = pl.multiple_of(n * RUN_ALIGN, RUN_ALIGN)
            pltpu.make_async_copy(xl_ref.at[pl.ds(pl.multiple_of(src0, RUN_ALIGN), rows)],
                                  xs_ref.at[pl.ds(pl.multiple_of(dst0, RUN_ALIGN), rows)], sem).start()
        return total + n
    total = lax.fori_loop(0, N_EXPERTS, issue, 0)

    @pl.when(total > 0)
    def _():
        rows = pl.multiple_of(total * RUN_ALIGN, RUN_ALIGN)
        pltpu.make_async_copy(xl_ref.at[pl.ds(0, rows)], xs_ref.at[pl.ds(0, rows)], sem).wait()


def _dispatch_call(goff, lbase, nch, gap, info, at, h2, *, T, R, rows, block_rows):
    N, D = h2.shape
    grid_spec = pltpu.PrefetchScalarGridSpec(
        num_scalar_prefetch=5, grid=(N // T,),
        in_specs=[pl.BlockSpec((8, T), lambda i, *_: (0, i)), pl.BlockSpec((T, D), lambda i, *_: (i, 0))],
        out_specs=pl.BlockSpec(memory_space=pl.ANY),
        scratch_shapes=[pltpu.VMEM((R, D // 2), jnp.uint32), pltpu.VMEM((block_rows, D // 2), jnp.uint32),
                        pltpu.SemaphoreType.DMA, pltpu.SemaphoreType.DMA, pltpu.SemaphoreType.DMA])
    return pl.pallas_call(
        _dispatch_kernel, grid_spec=grid_spec, out_shape=jax.ShapeDtypeStruct((rows, D // 2), jnp.uint32),
        compiler_params=_cparams(("arbitrary",)), name="dispatch",
    )(goff, lbase, nch, gap, info, at, h2)


def _expert_kernel(eblk, info, xs_ref, wu_ref, bu_ref, wd_ref, bd_ref, ys_ref, wub_ref, wdb_ref, xb_ref, yb_ref, xsem, ysem):
    e = pl.program_id(0)
    block_rows = xb_ref.shape[1]
    nb = ys_ref.shape[0] // block_rows
    b0 = eblk[2 * e]
    n = eblk[2 * e + 1]
    half = wu_ref.shape[0] // 2
    wub_ref[...] = wu_ref[...].astype(_BF)
    wdb_ref[...] = wd_ref[...].astype(_BF)

    def rows_of(j):
        return pl.ds(pl.multiple_of((b0 + j) * block_rows, block_rows), block_rows)

    def x_copy(j, slot):
        return pltpu.make_async_copy(xs_ref.at[rows_of(j)], xb_ref.at[slot], xsem.at[slot])

    def y_copy(j, slot):
        return pltpu.make_async_copy(yb_ref.at[slot], ys_ref.at[rows_of(j)], ysem.at[slot])

    @pl.when(n > 0)
    def _():
        x_copy(0, 0).start()

    def body(j, c):
        slot = j & 1
        x_copy(j, slot).wait()

        @pl.when(j + 1 < n)
        def _():
            x_copy(j + 1, 1 - slot).start()

        @pl.when(j >= 2)
        def _():
            y_copy(j - 2, slot).wait()

        x_lo, x_hi = _unpack_pairs(xb_ref[slot])
        gu = jnp.dot(x_lo, wub_ref[:half, :], preferred_element_type=_F32) \
            + jnp.dot(x_hi, wub_ref[half:, :], preferred_element_type=_F32) + bu_ref[...]
        g = jnp.minimum(gu[:, :D_FF], SWIGLU_LIMIT)
        u = jnp.clip(gu[:, D_FF:], -SWIGLU_LIMIT, SWIGLU_LIMIT)
        act = (u + 1.0) * (g * jax.nn.sigmoid(SWIGLU_ALPHA * g))
        y = jnp.dot(act.astype(_BF), wdb_ref[...], preferred_element_type=_F32) + bd_ref[...]
        yb_ref[slot] = _pack_pairs(y)
        y_copy(j, slot).start()
        return c
    lax.fori_loop(0, n, body, 0)

    @pl.when(n >= 2)
    def _():
        y_copy(n - 2, n & 1).wait()

    @pl.when(n >= 1)
    def _():
        y_copy(n - 1, (n - 1) & 1).wait()

    @pl.when(e == pl.num_programs(0) - 1)
    def _():
        yb_ref[0] = _pack_pairs(jnp.zeros((block_rows, 2 * yb_ref.shape[2]), _F32))

        def tail_copy(b):
            return pltpu.make_async_copy(yb_ref.at[0], ys_ref.at[pl.ds(pl.multiple_of(b * block_rows, block_rows), block_rows)],
                                         ysem.at[0])

        def tail(b, c):
            tail_copy(b).start()
            return c
        lax.fori_loop(info[0], nb, tail, 0)

        def tail_wait(b, c):
            tail_copy(b).wait()
            return c
        lax.fori_loop(info[0], nb, tail_wait, 0)


def _expert_call(eblk, info, xs, wu, bu, wd, bd, *, block_rows):
    rows, half = xs.shape
    D = 2 * half
    ex = lambda e, *_: (e, 0, 0)
    grid_spec = pltpu.PrefetchScalarGridSpec(
        num_scalar_prefetch=2, grid=(N_EXPERTS,),
        in_specs=[pl.BlockSpec(memory_space=pl.ANY),
                  pl.BlockSpec((None, D, 2 * D_FF), ex), pl.BlockSpec((None, 1, 2 * D_FF), ex),
                  pl.BlockSpec((None, D_FF, D), ex), pl.BlockSpec((None, 1, D), ex)],
        out_specs=pl.BlockSpec(memory_space=pl.ANY),
        scratch_shapes=[pltpu.VMEM((D, 2 * D_FF), _BF), pltpu.VMEM((D_FF, D), _BF),
                        pltpu.VMEM((2, block_rows, half), jnp.uint32), pltpu.VMEM((2, block_rows, half), jnp.uint32),
                        pltpu.SemaphoreType.DMA((2,)), pltpu.SemaphoreType.DMA((2,))])
    return pl.pallas_call(
        _expert_kernel, grid_spec=grid_spec, out_shape=jax.ShapeDtypeStruct((rows, half), jnp.uint32),
        compiler_params=_cparams(("arbitrary",)), name="experts",
    )(eblk, info, xs, wu, bu, wd, bd)


def _combine_kernel(goff, lbase, nch, a_ref, hres_ref, ys_ref, o_ref, yl_ref, sem):
    i = pl.program_id(0)
    R = yl_ref.shape[0]
    T = a_ref.shape[0]

    @pl.when(i == 0)
    def _():
        yl_ref[...] = jnp.zeros(yl_ref.shape, yl_ref.dtype)

    def issue(e, total):
        n = nch[i * N_EXPERTS + e]
        dst0 = lbase[i * N_EXPERTS + e]
        src0 = goff[i * N_EXPERTS + e]

        @pl.when(n > 0)
        def _():
            rows = pl.multiple_of(n * RUN_ALIGN, RUN_ALIGN)
            pltpu.make_async_copy(ys_ref.at[pl.ds(pl.multiple_of(src0, RUN_ALIGN), rows)],
                                  yl_ref.at[pl.ds(pl.multiple_of(dst0, RUN_ALIGN), rows)], sem).start()
        return total + n
    total = lax.fori_loop(0, N_EXPERTS, issue, 0)

    @pl.when(total > 0)
    def _():
        rows = pl.multiple_of(total * RUN_ALIGN, RUN_ALIGN)
        pltpu.make_async_copy(ys_ref.at[pl.ds(0, rows)], yl_ref.at[pl.ds(0, rows)], sem).wait()

    a = a_ref[...]
    slot_iota = lax.broadcasted_iota(jnp.int32, (T, R), 1).astype(_F32)
    pt = jnp.zeros((T, R), _F32)
    for k in range(TOP_K):
        pt = pt + jnp.where(slot_iota == a[:, k:k + 1], a[:, TOP_K + k:TOP_K + k + 1], 0.0)
    pt = pt.astype(_BF)
    y_lo, y_hi = _unpack_pairs(yl_ref[...])
    half = y_lo.shape[1]
    o_ref[:, :half] = hres_ref[:, :half] + jnp.dot(pt, y_lo, preferred_element_type=_F32)
    o_ref[:, half:] = hres_ref[:, half:] + jnp.dot(pt, y_hi, preferred_element_type=_F32)


def _combine_call(goff, lbase, nch, a, hres, ys, *, T, R):
    N, D = hres.shape
    grid_spec = pltpu.PrefetchScalarGridSpec(
        num_scalar_prefetch=3, grid=(N // T,),
        in_specs=[pl.BlockSpec((T, LANES), lambda i, *_: (i, 0)), pl.BlockSpec((T, D), lambda i, *_: (i, 0)),
                  pl.BlockSpec(memory_space=pl.ANY)],
        out_specs=pl.BlockSpec((T, D), lambda i, *_: (i, 0)),
        scratch_shapes=[pltpu.VMEM((R, D // 2), jnp.uint32), pltpu.SemaphoreType.DMA])
    return pl.pallas_call(
        _combine_kernel, grid_spec=grid_spec, out_shape=jax.ShapeDtypeStruct((N, D), _F32),
        compiler_params=_cparams(("arbitrary",)), name="combine",
    )(goff, lbase, nch, a, hres, ys)


def _tiles(S):
    pick = lambda pref: next(t for t in pref if S % t == 0)
    tm = pick((512, 256, 128))
    return dict(tm=tm, tq_fox=tm, tq_swa=pick((256, 128)),
                tm_mix=pick((512, 256, 128)), t_route=pick((256, 128)), block_rows=256)


def kernel(x, meta_tokens, norm1_gain, w_in, b_forget, fox_q_gain, fox_k_gain, swa_q_gain, swa_k_gain, swa_sinks,
           w_fox_out, w_swa_out, w_out, norm2_gain, w_router, b_router, w_up, b_up, w_down, b_down):
    B, S, D = x.shape
    assert norm1_gain.shape[0] == 1 and S % LANES == 0
    tl = _tiles(S)
    N = B * S
    scale = HEAD_DIM ** -0.5

    w = w_in[0]
    c0 = 3 * FOX_W
    c1 = c0 + FOX_HEADS
    c2 = c1 + SWA_QW + 2 * SWA_KW
    wqk = w[:, :2 * FOX_W].astype(_BF)
    wvt = w[:, 2 * FOX_W:c0].T.astype(_BF)
    wfl = w[:, c0:c1].T.astype(_BF)
    perm_heads = np.array([h for p in range(SWA_HEADS // 2) for h in (p, p + SWA_HEADS // 2)])
    perm_cols = (perm_heads[:, None] * HEAD_DIM + np.arange(HEAD_DIM)[None, :]).reshape(-1)
    ws_all = w[:, c1:c2]
    ws = jnp.concatenate([ws_all[:, :SWA_QW][:, perm_cols], ws_all[:, SWA_QW:SWA_QW + SWA_KW]], axis=1).astype(_BF)
    wsvt = ws_all[:, SWA_QW + SWA_KW:].T.astype(_BF)
    wga = w[:, c2:c2 + D].astype(_BF)
    wgb = w[:, c2 + D:].astype(_BF)
    g1 = norm1_gain[0][None, :]
    g2 = norm2_gain[0][None, :]
    bfg = b_forget[0][:, None]
    fqg = jnp.tile(fox_q_gain[0], FOX_HEADS)[None, :] * (scale * LOG2E)
    fkg = jnp.tile(fox_k_gain[0], FOX_HEADS)[None, :]
    sqg = jnp.tile(swa_q_gain[0], SWA_HEADS)[None, :] * scale
    skg = jnp.tile(swa_k_gain[0], SWA_KV_HEADS)[None, :]
    wfo = w_fox_out[0].astype(_BF)
    wso = w_swa_out[0][perm_cols, :].astype(_BF)
    wo = w_out[0].astype(_BF)
    wr = jnp.pad(w_router[0], ((0, 0), (0, LANES - N_EXPERTS)))
    br = jnp.pad(b_router[0], (0, LANES - N_EXPERTS), constant_values=NEG_INF)[None, :]
    sinks = swa_sinks[0].astype(_F32)
    wu = w_up[0]
    wd = w_down[0]
    bu = b_up[0][:, None, :]
    bd = b_down[0][:, None, :]

    meta_blk = jnp.pad(meta_tokens.astype(x.dtype), ((META_PAD, 0), (0, 0)))[None]
    zero_f = jnp.zeros((FOX_HEADS, 1), _F32)
    _, fk_m, aug_m, fvt_m, _, sk_m, svt_m, fend_m = _qkv_call(meta_blk, g1, wqk, wvt, wfl, ws, wsvt, bfg, fqg, fkg, sqg,
                                                              skg, zero_f, tm=META_BLOCK, n_pad=META_PAD)
    fq, fk, aug, fvt, sq, sk, svt, _ = _qkv_call(x, g1, wqk, wvt, wfl, ws, wsvt, bfg, fqg, fkg, sqg, skg, fend_m[0],
                                                 tm=tl["tm"], n_pad=0)
    o_a = _fox_call(fq, fk, aug, fvt, fk_m[0], aug_m[0], fvt_m[0, :, 0], tq=tl["tq_fox"])
    o_b = _swa_call(sinks, sq, sk, svt, sk_m[0], svt_m[0, :, 0], tq=tl["tq_swa"])

    hres, h2, lg = _mix_call(x.reshape(N, D), o_a.reshape(N, FOX_W), o_b.reshape(N, SWA_QW),
                             g1, wga, wgb, wfo, wso, wo, g2, wr, br, tm=tl["tm_mix"])

    T = tl["t_route"]
    nt = N // T
    block_rows = tl["block_rows"]
    R = -(-(TOP_K * T + N_EXPERTS * (RUN_ALIGN - 1)) // LANES) * LANES
    max_rows = N * TOP_K + nt * N_EXPERTS * (RUN_ALIGN - 1) + N_EXPERTS * (block_rows - RUN_ALIGN)
    nb = -(-max_rows // block_rows)
    a, at, cnt = _route_call(lg, T=T)
    goff, lbase, nch, eblk, gap, info = _plan_call(cnt[:, 0, :N_EXPERTS].reshape(-1), nt=nt, block_rows=block_rows)
    xs = _dispatch_call(goff, lbase, nch, gap, info, at, h2, T=T, R=R, rows=nb * block_rows, block_rows=block_rows)
    ys = _expert_call(eblk, info, xs, wu, bu, wd, bd, block_rows=block_rows)
    out = _combine_call(goff, lbase, nch, a, hres, ys, T=T, R=R)
    return out.reshape(B, S, D)
```

```python
import functools

import jax
import jax.numpy as jnp
import numpy as np
from jax import lax
from jax.experimental import pallas as pl
from jax.experimental.pallas import tpu as pltpu

HEAD_DIM = 64
N_META = 16
FOX_HEADS = 8
SWA_HEADS = 8
SWA_KV_HEADS = 2
WINDOW = 128
N_EXPERTS = 32
TOP_K = 4
D_FF = 1024
SWIGLU_LIMIT = 7.0
SWIGLU_ALPHA = 1.702
RMS_EPS = 1e-6
NEG_INF = -1e30
LOG2E = 1.4426950408889634

LANES = 128
META_BLOCK = 128
META_PAD = META_BLOCK - N_META
RUN_ALIGN = 8
VMEM_LIMIT = 56 * 1024 * 1024

FOX_W = FOX_HEADS * HEAD_DIM
SWA_QW = SWA_HEADS * HEAD_DIM
SWA_KW = SWA_KV_HEADS * HEAD_DIM

_BF = jnp.bfloat16
_F32 = jnp.float32


def _cparams(sem):
    return pltpu.CompilerParams(dimension_semantics=sem, vmem_limit_bytes=VMEM_LIMIT)


def _rms(t, gain):
    return t * lax.rsqrt(jnp.mean(t * t, axis=-1, keepdims=True) + RMS_EPS) * gain


def _head_rms(z, gain_row):
    lane = lax.broadcasted_iota(jnp.int32, (z.shape[0], LANES), 1)
    lo_mask = lane < HEAD_DIM
    outs = []
    for b in range(z.shape[1] // LANES):
        v = z[:, b * LANES:(b + 1) * LANES]
        v2 = v * v
        tot = jnp.sum(v2, axis=-1, keepdims=True)
        lo = jnp.sum(jnp.where(lo_mask, v2, 0.0), axis=-1, keepdims=True)
        hi = tot - lo
        r_lo = lax.rsqrt(lo * (1.0 / HEAD_DIM) + RMS_EPS)
        r_hi = lax.rsqrt(hi * (1.0 / HEAD_DIM) + RMS_EPS)
        outs.append(v * jnp.where(lo_mask, r_lo, r_hi))
    return jnp.concatenate(outs, axis=-1) * gain_row


def _lane_cumsum(x):
    n = x.shape[-1]
    lane = lax.broadcasted_iota(jnp.int32, x.shape, x.ndim - 1)
    s = 1
    while s < n:
        x = x + jnp.where(lane >= s, pltpu.roll(x, s, x.ndim - 1), 0.0)
        s *= 2
    return x


def _qkv_kernel(x_ref, g1_ref, wqk_ref, wvt_ref, wfl_ref, ws_ref, wsvt_ref, bf_ref, fqg_ref, fkg_ref, sqg_ref, skg_ref,
                f0_ref, fq_ref, fk_ref, aug_ref, fvt_ref, sq_ref, sk_ref, svt_ref, fend_ref, carry_ref, *, n_pad):
    j = pl.program_id(1)

    @pl.when(j == 0)
    def _():
        carry_ref[...] = f0_ref[...]

    x = x_ref[...]
    tm = x.shape[0]
    h = _rms(x, g1_ref[...]).astype(_BF)
    nt = (((1,), (1,)), ((), ()))
    f = jnp.dot(h, wqk_ref[...], preferred_element_type=_F32)
    s = jnp.dot(h, ws_ref[...], preferred_element_type=_F32)
    svt = lax.dot_general(wsvt_ref[...], h, nt, preferred_element_type=_F32)
    vt = lax.dot_general(wvt_ref[...], h, nt, preferred_element_type=_F32)
    flog = lax.dot_general(wfl_ref[...], h, nt, preferred_element_type=_F32)

    fq_ref[...] = _head_rms(f[:, :FOX_W], fqg_ref[...]).astype(_BF)
    fk_ref[...] = _head_rms(f[:, FOX_W:], fkg_ref[...]).astype(_BF)
    sq_ref[...] = _head_rms(s[:, :SWA_QW], sqg_ref[...]).astype(_BF)
    sk_ref[...] = _head_rms(s[:, SWA_QW:], skg_ref[...]).astype(_BF)

    ones_row = jnp.where(lax.broadcasted_iota(jnp.int32, (HEAD_DIM, tm), 0) == 0, 1.0, 0.0).astype(_BF)
    for hd in range(FOX_HEADS):
        fvt_ref[hd] = jnp.concatenate([vt[hd * HEAD_DIM:(hd + 1) * HEAD_DIM].astype(_BF), ones_row], axis=0)
    for g in range(SWA_KV_HEADS):
        blk = jnp.concatenate([svt[g * HEAD_DIM:(g + 1) * HEAD_DIM].astype(_BF), ones_row], axis=0)
        for c in range(tm // WINDOW):
            svt_ref[g, c] = blk[:, c * WINDOW:(c + 1) * WINDOW]

    z = flog + bf_ref[...]
    log_f = jnp.minimum(z, 0.0) - jnp.log1p(jnp.exp(-jnp.abs(z)))
    if n_pad:
        lane = lax.broadcasted_iota(jnp.int32, log_f.shape, 1)
        log_f = jnp.where(lane >= n_pad, log_f, 0.0)
    cum = _lane_cumsum(log_f) + carry_ref[...]
    carry_ref[...] = cum[:, -1:]
    fend_ref[...] = cum[:, -1:]
    nfl = cum * (-LOG2E)
    p1 = nfl.astype(_BF).astype(_F32)
    p2 = (nfl - p1).astype(_BF).astype(_F32)
    p3 = (nfl - p1 - p2).astype(_BF).astype(_F32)
    g = jnp.concatenate([p1, p2, p3, jnp.zeros((LANES - 3 * FOX_HEADS, tm), _F32)], axis=0)
    aug_ref[...] = g.T.astype(_BF)


def _qkv_call(x3, g1, wqk, wvt, wfl, ws, wsvt, bfg, fqg, fkg, sqg, skg, f0, *, tm, n_pad):
    B, S, D = x3.shape
    nj = S // tm
    row = lambda w: pl.BlockSpec((None, tm, w), lambda b, j: (b, j, 0))
    full = lambda a: pl.BlockSpec(a.shape, lambda b, j: (0,) * a.ndim)
    bsd = lambda w: jax.ShapeDtypeStruct((B, S, w), _BF)
    out_shape = [bsd(FOX_W), bsd(FOX_W), bsd(LANES), jax.ShapeDtypeStruct((B, FOX_HEADS, nj, 2 * HEAD_DIM, tm), _BF),
                 bsd(SWA_QW), bsd(SWA_KW),
                 jax.ShapeDtypeStruct((B, SWA_KV_HEADS, S // WINDOW, 2 * HEAD_DIM, WINDOW), _BF),
                 jax.ShapeDtypeStruct((B, FOX_HEADS, 1), _F32)]
    out_specs = [row(FOX_W), row(FOX_W), row(LANES),
                 pl.BlockSpec((None, FOX_HEADS, None, 2 * HEAD_DIM, tm), lambda b, j: (b, 0, j, 0, 0)),
                 row(SWA_QW), row(SWA_KW),
                 pl.BlockSpec((None, SWA_KV_HEADS, tm // WINDOW, 2 * HEAD_DIM, WINDOW), lambda b, j: (b, 0, j, 0, 0)),
                 pl.BlockSpec((None, FOX_HEADS, 1), lambda b, j: (b, 0, 0))]
    return pl.pallas_call(
        functools.partial(_qkv_kernel, n_pad=n_pad),
        grid=(B, nj),
        in_specs=[row(D)] + [full(a) for a in (g1, wqk, wvt, wfl, ws, wsvt, bfg, fqg, fkg, sqg, skg, f0)],
        out_specs=out_specs, out_shape=out_shape,
        scratch_shapes=[pltpu.VMEM((FOX_HEADS, 1), _F32)],
        compiler_params=_cparams(("arbitrary", "arbitrary")),
        name="qkv_proj",
    )(x3, g1, wqk, wvt, wfl, ws, wsvt, bfg, fqg, fkg, sqg, skg, f0)


def _fox_kernel(q_ref, k_ref, aug_ref, vt_ref, km_ref, augm_ref, vtm_ref, o_ref, m_ref, acc_ref, s_ref):
    p = pl.program_id(1)
    qi = pl.program_id(2)
    tq = q_ref.shape[0]
    tk = vt_ref.shape[-1]
    nt = (((1,), (1,)), ((), ()))

    q = q_ref[...]
    lane = lax.broadcasted_iota(jnp.int32, q.shape, 1)
    qaug = []
    for a in range(2):
        keep = (lane < HEAD_DIM) if a == 0 else (lane >= HEAD_DIM)
        ones = jnp.where((lane < 3 * FOX_HEADS) & ((lane & (FOX_HEADS - 1)) == 2 * p + a), 1.0, 0.0).astype(_BF)
        qaug.append(jnp.concatenate([jnp.where(keep, q, jnp.zeros_like(q)), ones], axis=1))
    qaug = jnp.concatenate(qaug, axis=0)

    def scores(kaug):
        return lax.dot_general(kaug, qaug, nt, preferred_element_type=_F32)

    def tile_scores(ki):
        off = pl.multiple_of(ki * tk, tk)
        return scores(jnp.concatenate([k_ref[pl.ds(off, tk), :], aug_ref[pl.ds(off, tk), :]], axis=1))

    def process(st, vts, mask, first):
        if mask is not None:
            st = jnp.where(mask, st, NEG_INF)
        mx = jnp.max(st, axis=0, keepdims=True)
        m_new = mx if first else jnp.maximum(m_ref[...], mx)
        pt = jnp.exp2(st - m_new).astype(_BF)
        if not first:
            alpha = jnp.exp2(m_ref[...] - m_new)
        for a in range(2):
            pv = jnp.dot(vts[a], pt[:, a * tq:(a + 1) * tq], preferred_element_type=_F32)
            acc_ref[a] = pv if first else alpha[:, a * tq:(a + 1) * tq] * acc_ref[a] + pv
        m_ref[...] = m_new

    s_ref[...] = tile_scores(0)
    mask_m = lax.broadcasted_iota(jnp.int32, (META_BLOCK, 2 * tq), 0) >= META_PAD
    process(scores(jnp.concatenate([km_ref[...], augm_ref[...]], axis=1)), (vtm_ref[0], vtm_ref[1]), mask_m, True)

    def body(ki, c):
        nxt = tile_scores(ki + 1)
        process(s_ref[...], (vt_ref[0, ki], vt_ref[1, ki]), None, False)
        s_ref[...] = nxt
        return c
    lax.fori_loop(0, qi, body, 0)
    key = lax.broadcasted_iota(jnp.int32, (tk, 2 * tq), 0)
    qry = lax.broadcasted_iota(jnp.int32, (tk, 2 * tq), 1) & (tq - 1)
    process(s_ref[...], (vt_ref[0, qi], vt_ref[1, qi]), key <= qry, False)

    outs = []
    for a in range(2):
        acc = acc_ref[a]
        outs.append(acc[:HEAD_DIM] / acc[HEAD_DIM:HEAD_DIM + 1])
    o_ref[...] = jnp.concatenate(outs, axis=0).T.astype(_BF)


def _fox_call(fq, fk, aug, fvt, km, augm, vtm, *, tq):
    B, S, _ = fq.shape
    nk = fvt.shape[2]
    assert fvt.shape[-1] == tq
    qspec = pl.BlockSpec((None, tq, LANES), lambda b, p, i: (b, i, p))
    return pl.pallas_call(
        _fox_kernel, grid=(B, FOX_HEADS // 2, S // tq),
        in_specs=[qspec,
                  pl.BlockSpec((None, S, LANES), lambda b, p, i: (b, 0, p)),
                  pl.BlockSpec((None, S, LANES), lambda b, p, i: (b, 0, 0)),
                  pl.BlockSpec((None, 2, nk, 2 * HEAD_DIM, tq), lambda b, p, i: (b, p, 0, 0, 0)),
                  pl.BlockSpec((META_BLOCK, LANES), lambda b, p, i: (0, p)),
                  pl.BlockSpec((META_BLOCK, LANES), lambda b, p, i: (0, 0)),
                  pl.BlockSpec((2, 2 * HEAD_DIM, META_BLOCK), lambda b, p, i: (p, 0, 0))],
        out_specs=qspec, out_shape=jax.ShapeDtypeStruct((B, S, FOX_W), _BF),
        scratch_shapes=[pltpu.VMEM((1, 2 * tq), _F32), pltpu.VMEM((2, 2 * HEAD_DIM, tq), _F32),
                        pltpu.VMEM((tq, 2 * tq), _F32)],
        compiler_params=_cparams(("arbitrary", "arbitrary", "arbitrary")), name="fox_attn",
    )(fq, fk, aug, fvt, km, augm, vtm)


def _swa_bias(tq):
    j = np.arange(WINDOW + tq)[:, None] - WINDOW
    i = np.arange(tq)[None, :]
    dist = i - j
    ok = (dist >= 0) & (dist < WINDOW)
    slopes = np.exp2(-8.0 * (np.arange(SWA_HEADS, dtype=np.float32) + 1.0) / SWA_HEADS).astype(np.float32)
    per_head = np.where(ok[None], -slopes[:, None, None] * dist[None].astype(np.float32), np.float32(NEG_INF))
    rep = SWA_HEADS // SWA_KV_HEADS
    return np.stack([np.concatenate(list(per_head[g * rep:(g + 1) * rep]), axis=1) for g in range(SWA_KV_HEADS)])


def _swa_kernel(q_ref, kc_ref, kp_ref, km_ref, vc_ref, vp_ref, vm_ref, bias_ref, sink_ref, o_ref):
    qi = pl.program_id(1)
    tq = q_ref.shape[0]
    rep = SWA_HEADS // SWA_KV_HEADS
    nt = (((1,), (1,)), ((), ()))
    first = qi == 0
    kcat = jnp.concatenate([jnp.where(first, km_ref[...], kp_ref[...]), kc_ref[...]], axis=0)
    row = lax.broadcasted_iota(jnp.int32, (WINDOW + tq, rep * tq), 0)
    row_ok = row >= jnp.where(first, META_PAD, 0)
    lane = lax.broadcasted_iota(jnp.int32, (tq, LANES), 1)
    outs = [None] * SWA_HEADS
    for g in range(SWA_KV_HEADS):
        keep = (lane < HEAD_DIM) if g == 0 else (lane >= HEAD_DIM)
        qg = jnp.concatenate([jnp.where(keep, q_ref[:, p * LANES:(p + 1) * LANES], jnp.zeros((tq, LANES), _BF))
                              for p in range(rep)], axis=0)
        vcat = jnp.concatenate([jnp.where(first, vm_ref[g], vp_ref[g, 0])] + [vc_ref[g, c] for c in range(tq // WINDOW)],
                               axis=1)
        st = lax.dot_general(kcat, qg, nt, preferred_element_type=_F32) + bias_ref[g]
        st = jnp.where(row_ok, st, NEG_INF)
        sink = sink_ref[g]
        m = jnp.maximum(jnp.max(st, axis=0, keepdims=True), sink)
        pt = jnp.exp(st - m).astype(_BF)
        acc = jnp.dot(vcat, pt, preferred_element_type=_F32)
        o = acc[:HEAD_DIM] / (acc[HEAD_DIM:HEAD_DIM + 1] + jnp.exp(sink - m))
        for p in range(rep):
            outs[2 * p + g] = o[:, p * tq:(p + 1) * tq]
    o_ref[...] = jnp.concatenate(outs, axis=0).T.astype(_BF)


def _swa_call(sinks, sq, sk, svt, km, vtm, *, tq):
    B, S, _ = sq.shape
    r = tq // WINDOW
    rep = SWA_HEADS // SWA_KV_HEADS
    bias = jnp.asarray(_swa_bias(tq))
    sink_rows = jnp.repeat(sinks.reshape(SWA_KV_HEADS, rep), tq, axis=1)[:, None, :]
    prev_blk = lambda i: jnp.maximum(i * r - 1, 0)
    return pl.pallas_call(
        _swa_kernel, grid=(B, S // tq),
        in_specs=[pl.BlockSpec((None, tq, SWA_QW), lambda b, i: (b, i, 0)),
                  pl.BlockSpec((None, tq, LANES), lambda b, i: (b, i, 0)),
                  pl.BlockSpec((None, WINDOW, LANES), lambda b, i: (b, prev_blk(i), 0)),
                  pl.BlockSpec((META_BLOCK, LANES), lambda b, i: (0, 0)),
                  pl.BlockSpec((None, SWA_KV_HEADS, r, 2 * HEAD_DIM, WINDOW), lambda b, i: (b, 0, i, 0, 0)),
                  pl.BlockSpec((None, SWA_KV_HEADS, 1, 2 * HEAD_DIM, WINDOW), lambda b, i: (b, 0, prev_blk(i), 0, 0)),
                  pl.BlockSpec((SWA_KV_HEADS, 2 * HEAD_DIM, META_BLOCK), lambda b, i: (0, 0, 0)),
                  pl.BlockSpec(bias.shape, lambda b, i: (0, 0, 0)),
                  pl.BlockSpec(sink_rows.shape, lambda b, i: (0, 0, 0))],
        out_specs=pl.BlockSpec((None, tq, SWA_QW), lambda b, i: (b, i, 0)),
        out_shape=jax.ShapeDtypeStruct((B, S, SWA_QW), _BF),
        compiler_params=_cparams(("arbitrary", "arbitrary")), name="swa_attn",
    )(sq, sk, sk, km, svt, svt, vtm, bias, sink_rows)


def _split3(v):
    a = v.astype(_BF)
    r = v - a.astype(_F32)
    b = r.astype(_BF)
    c = (r - b.astype(_F32)).astype(_BF)
    return a, b, c


def _mix_kernel(x_ref, oa_ref, ob_ref, g1_ref, wga_ref, wgb_ref, wfo_ref, wso_ref, wo_ref, g2_ref, wr_ref, br_ref,
                hres_ref, h2_ref, lg_ref):
    x = x_ref[...]
    h = _rms(x, g1_ref[...]).astype(_BF)
    ga = jax.nn.sigmoid(jnp.dot(h, wga_ref[...], preferred_element_type=_F32))
    gb = jax.nn.sigmoid(jnp.dot(h, wgb_ref[...], preferred_element_type=_F32))
    mix = ga * jnp.dot(oa_ref[...], wfo_ref[...], preferred_element_type=_F32) \
        + gb * jnp.dot(ob_ref[...], wso_ref[...], preferred_element_type=_F32)
    hres = x + jnp.dot(mix.astype(_BF), wo_ref[...], preferred_element_type=_F32)
    hres_ref[...] = hres
    h2 = _rms(hres, g2_ref[...])
    h2_ref[...] = h2.astype(_BF)
    wr = wr_ref[...]
    w_hi = wr.astype(_BF)
    w_lo = (wr - w_hi.astype(_F32)).astype(_BF)
    h_hi = h2.astype(_BF)
    h_lo = (h2 - h_hi.astype(_F32)).astype(_BF)
    lg = jnp.dot(h_hi, w_hi, preferred_element_type=_F32) + jnp.dot(h_hi, w_lo, preferred_element_type=_F32) \
        + jnp.dot(h_lo, w_hi, preferred_element_type=_F32)
    lg_ref[...] = lg + br_ref[...]


def _mix_call(x2, oa, ob, g1, wga, wgb, wfo, wso, wo, g2, wr, br, *, tm):
    N, D = x2.shape
    row = lambda w: pl.BlockSpec((tm, w), lambda i: (i, 0))
    full = lambda a: pl.BlockSpec(a.shape, lambda i: (0,) * a.ndim)
    return pl.pallas_call(
        _mix_kernel, grid=(N // tm,),
        in_specs=[row(D), row(FOX_W), row(SWA_QW)] + [full(a) for a in (g1, wga, wgb, wfo, wso, wo, g2, wr, br)],
        out_specs=[row(D), row(D), row(LANES)],
        out_shape=[jax.ShapeDtypeStruct((N, D), _F32), jax.ShapeDtypeStruct((N, D), _BF),
                   jax.ShapeDtypeStruct((N, LANES), _F32)],
        compiler_params=_cparams(("arbitrary",)), name="mix_proj",
    )(x2, oa, ob, g1, wga, wgb, wfo, wso, wo, g2, wr, br)


def _route_kernel(lg_ref, a_ref, at_ref, cnt_ref):
    lg = lg_ref[...]
    T = lg.shape[0]
    lane = lax.broadcasted_iota(jnp.int32, lg.shape, 1)
    cur = lg
    vals, idxs = [], []
    for _ in range(TOP_K):
        m = jnp.max(cur, axis=-1, keepdims=True)
        idx = jnp.min(jnp.where(cur == m, lane, LANES), axis=-1, keepdims=True)
        vals.append(m)
        idxs.append(idx)
        cur = jnp.where(lane == idx, -jnp.inf, cur)
    es = [jnp.exp(v - vals[0]) for v in vals]
    den = es[0] + es[1] + es[2] + es[3]
    gates = [e / den for e in es]

    onehot = jnp.zeros(lg.shape, _F32)
    for idx in idxs:
        onehot = onehot + jnp.where(lane == idx, 1.0, 0.0)
    r_i = lax.broadcasted_iota(jnp.int32, (T, T), 0)
    c_i = lax.broadcasted_iota(jnp.int32, (T, T), 1)
    ltri = jnp.where(c_i < r_i, 1.0, 0.0).astype(_BF)
    prefix = jnp.dot(ltri, onehot.astype(_BF), preferred_element_type=_F32)
    cnt = jnp.sum(onehot, axis=0, keepdims=True)
    chunks = jnp.floor((cnt + (RUN_ALIGN - 1)) * (1.0 / RUN_ALIGN))
    u_r = lax.broadcasted_iota(jnp.int32, (LANES, LANES), 0)
    u_c = lax.broadcasted_iota(jnp.int32, (LANES, LANES), 1)
    utri = jnp.where(u_r < u_c, 1.0, 0.0).astype(_BF)
    lbase = RUN_ALIGN * jnp.dot(jnp.broadcast_to(chunks, (8, LANES)).astype(_BF), utri,
                                preferred_element_type=_F32)[0:1]
    slot_all = prefix + lbase
    a = jnp.zeros(lg.shape, _F32)
    for k in range(TOP_K):
        slot_k = jnp.sum(jnp.where(lane == idxs[k], slot_all, 0.0), axis=-1, keepdims=True)
        a = a + jnp.where(lane == k, slot_k, 0.0) + jnp.where(lane == TOP_K + k, gates[k], 0.0)
    a_ref[...] = a
    at_ref[...] = a.T[0:8]
    cnt_ref[...] = jnp.broadcast_to(cnt, (8, LANES)).astype(jnp.int32)


def _route_call(lg, *, T):
    N = lg.shape[0]
    nt = N // T
    return pl.pallas_call(
        _route_kernel, grid=(nt,),
        in_specs=[pl.BlockSpec((T, LANES), lambda i: (i, 0))],
        out_specs=[pl.BlockSpec((T, LANES), lambda i: (i, 0)), pl.BlockSpec((8, T), lambda i: (0, i)),
                   pl.BlockSpec((None, 8, LANES), lambda i: (i, 0, 0))],
        out_shape=[jax.ShapeDtypeStruct((N, LANES), _F32), jax.ShapeDtypeStruct((8, N), _F32),
                   jax.ShapeDtypeStruct((nt, 8, LANES), jnp.int32)],
        compiler_params=_cparams(("arbitrary",)), name="route",
    )(lg)


def _div_pow2(v, d):
    assert d & (d - 1) == 0
    return lax.shift_right_logical(v, d.bit_length() - 1)


def _plan_kernel(cnt_ref, goff_ref, lbase_ref, nch_ref, eblk_ref, gap_ref, info_ref, *, nt, block_rows):
    def per_expert(e, base):
        def per_tile(i, run):
            n = _div_pow2(cnt_ref[i * N_EXPERTS + e] + (RUN_ALIGN - 1), RUN_ALIGN)
            goff_ref[i * N_EXPERTS + e] = run
            nch_ref[i * N_EXPERTS + e] = n
            return run + n * RUN_ALIGN
        end = lax.fori_loop(0, nt, per_tile, base)
        nblk = _div_pow2(end - base + (block_rows - 1), block_rows)
        nxt = base + nblk * block_rows
        gap_ref[2 * e] = end
        gap_ref[2 * e + 1] = _div_pow2(nxt - end, RUN_ALIGN)
        eblk_ref[2 * e] = _div_pow2(base, block_rows)
        eblk_ref[2 * e + 1] = nblk
        return nxt
    total = lax.fori_loop(0, N_EXPERTS, per_expert, 0)
    info_ref[0] = _div_pow2(total, block_rows)

    def per_tile2(i, c):
        def per_e(e, run):
            lbase_ref[i * N_EXPERTS + e] = run
            return run + nch_ref[i * N_EXPERTS + e] * RUN_ALIGN
        lax.fori_loop(0, N_EXPERTS, per_e, 0)
        return c
    lax.fori_loop(0, nt, per_tile2, 0)


def _plan_call(cnt_flat, *, nt, block_rows):
    smem = pl.BlockSpec(memory_space=pltpu.SMEM)
    i32 = lambda n: jax.ShapeDtypeStruct((n,), jnp.int32)
    return pl.pallas_call(
        functools.partial(_plan_kernel, nt=nt, block_rows=block_rows),
        in_specs=[smem], out_specs=[smem] * 6,
        out_shape=[i32(nt * N_EXPERTS), i32(nt * N_EXPERTS), i32(nt * N_EXPERTS), i32(2 * N_EXPERTS), i32(2 * N_EXPERTS),
                   i32(1)],
        name="plan",
    )(cnt_flat)


def _pack_pairs(v):
    w = v.shape[1] // 2
    return pltpu.pack_elementwise([v[:, :w], v[:, w:]], packed_dtype=_BF)


def _unpack_pairs(wd):
    lo = pltpu.unpack_elementwise(wd, index=0, packed_dtype=_BF, unpacked_dtype=_F32)
    hi = pltpu.unpack_elementwise(wd, index=1, packed_dtype=_BF, unpacked_dtype=_F32)
    return lo.astype(_BF), hi.astype(_BF)


def _dispatch_kernel(goff, lbase, nch, gap, info, at_ref, h2_ref, xs_ref, xl_ref, z_ref, sem, zsem, tsem):
    i = pl.program_id(0)
    R = xl_ref.shape[1]
    T = h2_ref.shape[0]
    block_rows = z_ref.shape[0]
    nb = xs_ref.shape[0] // block_rows

    @pl.when(i == 0)
    def _():
        z_ref[...] = jnp.zeros(z_ref.shape, z_ref.dtype)

        def gap_copy(e):
            rows = pl.multiple_of(gap[2 * e + 1] * RUN_ALIGN, RUN_ALIGN)
            return pltpu.make_async_copy(z_ref.at[pl.ds(0, rows)],
                                         xs_ref.at[pl.ds(pl.multiple_of(gap[2 * e], RUN_ALIGN), rows)], zsem)

        def fill(e, c):
            @pl.when(gap[2 * e + 1] > 0)
            def _():
                gap_copy(e).start()
            return c
        lax.fori_loop(0, N_EXPERTS, fill, 0)

        def fill_wait(e, c):
            @pl.when(gap[2 * e + 1] > 0)
            def _():
                gap_copy(e).wait()
            return c
        lax.fori_loop(0, N_EXPERTS, fill_wait, 0)

        def tail(b, c):
            pltpu.make_async_copy(z_ref, xs_ref.at[pl.ds(pl.multiple_of(b * block_rows, block_rows), block_rows)],
                                  tsem).start()
            return c
        lax.fori_loop(info[0], nb, tail, 0)

        def tail_wait(b, c):
            pltpu.make_async_copy(z_ref, xs_ref.at[pl.ds(0, block_rows)], tsem).wait()
            return c
        lax.fori_loop(info[0], nb, tail_wait, 0)

    slot_iota = lax.broadcasted_iota(jnp.int32, (R, T), 0).astype(_F32)
    pm = jnp.zeros((R, T), _F32)
    for k in range(TOP_K):
        pm = pm + jnp.where(slot_iota == at_ref[k:k + 1, :], 1.0, 0.0)
    xl = jnp.dot(pm.astype(_BF), h2_ref[...], preferred_element_type=_F32)
    slot = i & 1

    def wait_tile(t, s):
        total = lax.fori_loop(0, N_EXPERTS, lambda e, c: c + nch[t * N_EXPERTS + e], 0)

        @pl.when(total > 0)
        def _():
            rows = pl.multiple_of(total * RUN_ALIGN, RUN_ALIGN)
            pltpu.make_async_copy(xl_ref.at[s, pl.ds(0, rows)], xs_ref.at[pl.ds(0, rows)], sem.at[s]).wait()

    @pl.when(i >= 2)
    def _():
        wait_tile(i - 2, slot)
    xl_ref[slot] = _pack_pairs(xl)

    def issue(e, c):
        n = nch[i * N_EXPERTS + e]

        @pl.when(n > 0)
        def _():
            rows = pl.multiple_of(n * RUN_ALIGN, RUN_ALIGN)
            pltpu.make_async_copy(xl_ref.at[slot, pl.ds(pl.multiple_of(lbase[i * N_EXPERTS + e], RUN_ALIGN), rows)],
                                  xs_ref.at[pl.ds(pl.multiple_of(goff[i * N_EXPERTS + e], RUN_ALIGN), rows)],
                                  sem.at[slot]).start()
        return c
    lax.fori_loop(0, N_EXPERTS, issue, 0)

    @pl.when(i == pl.num_programs(0) - 1)
    def _():
        @pl.when(i >= 1)
        def _():
            wait_tile(i - 1, 1 - slot)
        wait_tile(i, slot)


def _dispatch_call(goff, lbase, nch, gap, info, at, h2, *, T, R, rows, block_rows):
    N, D = h2.shape
    grid_spec = pltpu.PrefetchScalarGridSpec(
        num_scalar_prefetch=5, grid=(N // T,),
        in_specs=[pl.BlockSpec((8, T), lambda i, *_: (0, i)), pl.BlockSpec((T, D), lambda i, *_: (i, 0))],
        out_specs=pl.BlockSpec(memory_space=pl.ANY),
        scratch_shapes=[pltpu.VMEM((2, R, D // 2), jnp.uint32), pltpu.VMEM((block_rows, D // 2), jnp.uint32),
                        pltpu.SemaphoreType.DMA((2,)), pltpu.SemaphoreType.DMA, pltpu.SemaphoreType.DMA])
    return pl.pallas_call(
        _dispatch_kernel, grid_spec=grid_spec, out_shape=jax.ShapeDtypeStruct((rows, D // 2), jnp.uint32),
        compiler_params=_cparams(("arbitrary",)), name="dispatch",
    )(goff, lbase, nch, gap, info, at, h2)


def _expert_kernel(eblk, info, xs_ref, wu_ref, bu_ref, wd_ref, bd_ref, ys_ref, wub_ref, wdb_ref, xb_ref, yb_ref, xsem, ysem):
    e = pl.program_id(0)
    block_rows = xb_ref.shape[1]
    nb = ys_ref.shape[0] // block_rows
    b0 = eblk[2 * e]
    n = eblk[2 * e + 1]
    half = wu_ref.shape[0] // 2
    wub_ref[...] = wu_ref[...].astype(_BF)
    wdb_ref[...] = wd_ref[...].astype(_BF)

    def rows_of(blk):
        return pl.ds(pl.multiple_of(blk * block_rows, block_rows), block_rows)

    def x_copy(j, slot):
        return pltpu.make_async_copy(xs_ref.at[rows_of(b0 + j)], xb_ref.at[slot], xsem.at[slot])

    def y_copy(j, slot):
        return pltpu.make_async_copy(yb_ref.at[slot], ys_ref.at[rows_of(b0 + j)], ysem.at[slot])

    @pl.when((e == 0) & (n > 0))
    def _():
        x_copy(0, 0).start(priority=1)

    def body(j, c):
        slot = j & 1
        x_copy(j, slot).wait()

        @pl.when(j + 1 < n)
        def _():
            x_copy(j + 1, 1 - slot).start(priority=1)

        @pl.when(j >= 2)
        def _():
            y_copy(j - 2, slot).wait()

        x_lo, x_hi = _unpack_pairs(xb_ref[slot])
        gu = jnp.dot(x_lo, wub_ref[:half, :], preferred_element_type=_F32) \
            + jnp.dot(x_hi, wub_ref[half:, :], preferred_element_type=_F32) + bu_ref[...]
        g = jnp.minimum(gu[:, :D_FF], SWIGLU_LIMIT)
        u = jnp.clip(gu[:, D_FF:], -SWIGLU_LIMIT, SWIGLU_LIMIT)
        act = (u + 1.0) * (g * jax.nn.sigmoid(SWIGLU_ALPHA * g))
        y = jnp.dot(act.astype(_BF), wdb_ref[...], preferred_element_type=_F32) + bd_ref[...]
        yb_ref[slot] = _pack_pairs(y)
        y_copy(j, slot).start()
        return c
    lax.fori_loop(0, n, body, 0)

    @pl.when(n >= 2)
    def _():
        y_copy(n - 2, n & 1).wait()

    @pl.when(n >= 1)
    def _():
        y_copy(n - 1, (n - 1) & 1).wait()

    last = pl.num_programs(0) - 1
    e_next = jnp.minimum(e + 1, last)

    @pl.when((e < last) & (eblk[2 * e_next + 1] > 0))
    def _():
        pltpu.make_async_copy(xs_ref.at[rows_of(eblk[2 * e_next])], xb_ref.at[0], xsem.at[0]).start(priority=1)

    @pl.when(e == last)
    def _():
        yb_ref[0] = _pack_pairs(jnp.zeros((block_rows, 2 * yb_ref.shape[2]), _F32))

        def tail_copy(b):
            return pltpu.make_async_copy(yb_ref.at[0], ys_ref.at[pl.ds(pl.multiple_of(b * block_rows, block_rows), block_rows)],
                                         ysem.at[0])

        def tail(b, c):
            tail_copy(b).start()
            return c
        lax.fori_loop(info[0], nb, tail, 0)

        def tail_wait(b, c):
            tail_copy(b).wait()
            return c
        lax.fori_loop(info[0], nb, tail_wait, 0)


def _expert_call(eblk, info, xs, wu, bu, wd, bd, *, block_rows):
    rows, half = xs.shape
    D = 2 * half
    ex = lambda e, *_: (e, 0, 0)
    grid_spec = pltpu.PrefetchScalarGridSpec(
        num_scalar_prefetch=2, grid=(N_EXPERTS,),
        in_specs=[pl.BlockSpec(memory_space=pl.ANY),
                  pl.BlockSpec((None, D, 2 * D_FF), ex), pl.BlockSpec((None, 1, 2 * D_FF), ex),
                  pl.BlockSpec((None, D_FF, D), ex), pl.BlockSpec((None, 1, D), ex)],
        out_specs=pl.BlockSpec(memory_space=pl.ANY),
        scratch_shapes=[pltpu.VMEM((D, 2 * D_FF), _BF), pltpu.VMEM((D_FF, D), _BF),
                        pltpu.VMEM((2, block_rows, half), jnp.uint32), pltpu.VMEM((2, block_rows, half), jnp.uint32),
                        pltpu.SemaphoreType.DMA((2,)), pltpu.SemaphoreType.DMA((2,))])
    return pl.pallas_call(
        _expert_kernel, grid_spec=grid_spec, out_shape=jax.ShapeDtypeStruct((rows, half), jnp.uint32),
        compiler_params=_cparams(("arbitrary",)), name="experts",
    )(eblk, info, xs, wu, bu, wd, bd)


def _combine_kernel(goff, lbase, nch, a_ref, hres_ref, ys_ref, o_ref, yl_ref, sem):
    i = pl.program_id(0)
    nt = pl.num_programs(0)
    R = yl_ref.shape[1]
    T = a_ref.shape[0]

    def fetch(t, slot):
        def issue(e, c):
            n = nch[t * N_EXPERTS + e]

            @pl.when(n > 0)
            def _():
                rows = pl.multiple_of(n * RUN_ALIGN, RUN_ALIGN)
                pltpu.make_async_copy(
                    ys_ref.at[pl.ds(pl.multiple_of(goff[t * N_EXPERTS + e], RUN_ALIGN), rows)],
                    yl_ref.at[slot, pl.ds(pl.multiple_of(lbase[t * N_EXPERTS + e], RUN_ALIGN), rows)], sem.at[slot]).start()
            return c
        lax.fori_loop(0, N_EXPERTS, issue, 0)

    @pl.when(i == 0)
    def _():
        yl_ref[...] = jnp.zeros(yl_ref.shape, yl_ref.dtype)
        fetch(0, 0)

    slot = i & 1

    @pl.when(i + 1 < nt)
    def _():
        fetch(i + 1, 1 - slot)

    total = lax.fori_loop(0, N_EXPERTS, lambda e, c: c + nch[i * N_EXPERTS + e], 0)

    @pl.when(total > 0)
    def _():
        rows = pl.multiple_of(total * RUN_ALIGN, RUN_ALIGN)
        pltpu.make_async_copy(ys_ref.at[pl.ds(0, rows)], yl_ref.at[slot, pl.ds(0, rows)], sem.at[slot]).wait()

    a = a_ref[...]
    slot_iota = lax.broadcasted_iota(jnp.int32, (T, R), 1).astype(_F32)
    pt = jnp.zeros((T, R), _F32)
    for k in range(TOP_K):
        pt = pt + jnp.where(slot_iota == a[:, k:k + 1], a[:, TOP_K + k:TOP_K + k + 1], 0.0)
    pt = pt.astype(_BF)
    y_lo, y_hi = _unpack_pairs(yl_ref[slot])
    half = y_lo.shape[1]
    o_ref[:, :half] = hres_ref[:, :half] + jnp.dot(pt, y_lo, preferred_element_type=_F32)
    o_ref[:, half:] = hres_ref[:, half:] + jnp.dot(pt, y_hi, preferred_element_type=_F32)


def _combine_call(goff, lbase, nch, a, hres, ys, *, T, R):
    N, D = hres.shape
    grid_spec = pltpu.PrefetchScalarGridSpec(
        num_scalar_prefetch=3, grid=(N // T,),
        in_specs=[pl.BlockSpec((T, LANES), lambda i, *_: (i, 0)), pl.BlockSpec((T, D), lambda i, *_: (i, 0)),
                  pl.BlockSpec(memory_space=pl.ANY)],
        out_specs=pl.BlockSpec((T, D), lambda i, *_: (i, 0)),
        scratch_shapes=[pltpu.VMEM((2, R, D // 2), jnp.uint32), pltpu.SemaphoreType.DMA((2,))])
    return pl.pallas_call(
        _combine_kernel, grid_spec=grid_spec, out_shape=jax.ShapeDtypeStruct((N, D), _F32),
        compiler_params=_cparams(("arbitrary",)), name="combine",
    )(goff, lbase, nch, a, hres, ys)


def _tiles(S):
    pick = lambda pref: next(t for t in pref if S % t == 0)
    tm = pick((512, 256, 128))
    return dict(tm=tm, tq_fox=tm, tq_swa=pick((256, 128)),
                tm_mix=pick((512, 256, 128)), t_route=pick((256, 128)), block_rows=256)


def kernel(x, meta_tokens, norm1_gain, w_in, b_forget, fox_q_gain, fox_k_gain, swa_q_gain, swa_k_gain, swa_sinks,
           w_fox_out, w_swa_out, w_out, norm2_gain, w_router, b_router, w_up, b_up, w_down, b_down):
    B, S, D = x.shape
    assert norm1_gain.shape[0] == 1 and S % LANES == 0
    tl = _tiles(S)
    N = B * S
    scale = HEAD_DIM ** -0.5

    w = w_in[0]
    c0 = 3 * FOX_W
    c1 = c0 + FOX_HEADS
    c2 = c1 + SWA_QW + 2 * SWA_KW
    wqk = w[:, :2 * FOX_W].astype(_BF)
    wvt = w[:, 2 * FOX_W:c0].T.astype(_BF)
    wfl = w[:, c0:c1].T.astype(_BF)
    perm_heads = np.array([h for p in range(SWA_HEADS // 2) for h in (p, p + SWA_HEADS // 2)])
    perm_cols = (perm_heads[:, None] * HEAD_DIM + np.arange(HEAD_DIM)[None, :]).reshape(-1)
    ws_all = w[:, c1:c2]
    ws = jnp.concatenate([ws_all[:, :SWA_QW][:, perm_cols], ws_all[:, SWA_QW:SWA_QW + SWA_KW]], axis=1).astype(_BF)
    wsvt = ws_all[:, SWA_QW + SWA_KW:].T.astype(_BF)
    wga = w[:, c2:c2 + D].astype(_BF)
    wgb = w[:, c2 + D:].astype(_BF)
    g1 = norm1_gain[0][None, :]
    g2 = norm2_gain[0][None, :]
    bfg = b_forget[0][:, None]
    fqg = jnp.tile(fox_q_gain[0], FOX_HEADS)[None, :] * (scale * LOG2E)
    fkg = jnp.tile(fox_k_gain[0], FOX_HEADS)[None, :]
    sqg = jnp.tile(swa_q_gain[0], SWA_HEADS)[None, :] * scale
    skg = jnp.tile(swa_k_gain[0], SWA_KV_HEADS)[None, :]
    wfo = w_fox_out[0].astype(_BF)
    wso = w_swa_out[0][perm_cols, :].astype(_BF)
    wo = w_out[0].astype(_BF)
    wr = jnp.pad(w_router[0], ((0, 0), (0, LANES - N_EXPERTS)))
    br = jnp.pad(b_router[0], (0, LANES - N_EXPERTS), constant_values=NEG_INF)[None, :]
    sinks = swa_sinks[0].astype(_F32)
    wu = w_up[0]
    wd = w_down[0]
    bu = b_up[0][:, None, :]
    bd = b_down[0][:, None, :]

    meta_blk = jnp.pad(meta_tokens.astype(x.dtype), ((META_PAD, 0), (0, 0)))[None]
    zero_f = jnp.zeros((FOX_HEADS, 1), _F32)
    _, fk_m, aug_m, fvt_m, _, sk_m, svt_m, fend_m = _qkv_call(meta_blk, g1, wqk, wvt, wfl, ws, wsvt, bfg, fqg, fkg, sqg,
                                                              skg, zero_f, tm=META_BLOCK, n_pad=META_PAD)
    fq, fk, aug, fvt, sq, sk, svt, _ = _qkv_call(x, g1, wqk, wvt, wfl, ws, wsvt, bfg, fqg, fkg, sqg, skg, fend_m[0],
                                                 tm=tl["tm"], n_pad=0)
    o_a = _fox_call(fq, fk, aug, fvt, fk_m[0], aug_m[0], fvt_m[0, :, 0], tq=tl["tq_fox"])
    o_b = _swa_call(sinks, sq, sk, svt, sk_m[0], svt_m[0, :, 0], tq=tl["tq_swa"])

    hres, h2, lg = _mix_call(x.reshape(N, D), o_a.reshape(N, FOX_W), o_b.reshape(N, SWA_QW),
                             g1, wga, wgb, wfo, wso, wo, g2, wr, br, tm=tl["tm_mix"])

    T = tl["t_route"]
    nt = N // T
    block_rows = tl["block_rows"]
    R = -(-(TOP_K * T + N_EXPERTS * (RUN_ALIGN - 1)) // LANES) * LANES
    max_rows = N * TOP_K + nt * N_EXPERTS * (RUN_ALIGN - 1) + N_EXPERTS * (block_rows - RUN_ALIGN)
    nb = -(-max_rows // block_rows)
    a, at, cnt = _route_call(lg, T=T)
    goff, lbase, nch, eblk, gap, info = _plan_call(cnt[:, 0, :N_EXPERTS].reshape(-1), nt=nt, block_rows=block_rows)
    xs = _dispatch_call(goff, lbase, nch, gap, info, at, h2, T=T, R=R, rows=nb * block_rows, block_rows=block_rows)
    ys = _expert_call(eblk, info, xs, wu, bu, wd, bd, block_rows=block_rows)
    out = _combine_call(goff, lbase, nch, a, hres, ys, T=T, R=R)
    return out.reshape(B, S, D)
```

```python
import functools

import jax
import jax.numpy as jnp
import numpy as np
from jax import lax
from jax.experimental import pallas as pl
from jax.experimental.pallas import tpu as pltpu

HEAD_DIM = 64
N_META = 16
FOX_HEADS = 8
SWA_HEADS = 8
SWA_KV_HEADS = 2
WINDOW = 128
N_EXPERTS = 32
TOP_K = 4
D_FF = 1024
SWIGLU_LIMIT = 7.0
SWIGLU_ALPHA = 1.702
RMS_EPS = 1e-6
NEG_INF = -1e30
LOG2E = 1.4426950408889634

LANES = 128
META_BLOCK = 128
META_PAD = META_BLOCK - N_META
FOX_PAIRS = 2
RUN_ALIGN = 8
VMEM_LIMIT = 56 * 1024 * 1024

FOX_W = FOX_HEADS * HEAD_DIM
SWA_QW = SWA_HEADS * HEAD_DIM
SWA_KW = SWA_KV_HEADS * HEAD_DIM

_BF = jnp.bfloat16
_F32 = jnp.float32


def _cparams(sem):
    return pltpu.CompilerParams(dimension_semantics=sem, vmem_limit_bytes=VMEM_LIMIT)


def _rms(t, gain):
    return t * lax.rsqrt(jnp.mean(t * t, axis=-1, keepdims=True) + RMS_EPS) * gain


def _head_rms(z, gain_row):
    lane = lax.broadcasted_iota(jnp.int32, (z.shape[0], LANES), 1)
    lo_mask = lane < HEAD_DIM
    outs = []
    for b in range(z.shape[1] // LANES):
        v = z[:, b * LANES:(b + 1) * LANES]
        v2 = v * v
        tot = jnp.sum(v2, axis=-1, keepdims=True)
        lo = jnp.sum(jnp.where(lo_mask, v2, 0.0), axis=-1, keepdims=True)
        hi = tot - lo
        r_lo = lax.rsqrt(lo * (1.0 / HEAD_DIM) + RMS_EPS)
        r_hi = lax.rsqrt(hi * (1.0 / HEAD_DIM) + RMS_EPS)
        outs.append(v * jnp.where(lo_mask, r_lo, r_hi))
    return jnp.concatenate(outs, axis=-1) * gain_row


def _lane_cumsum(x):
    n = x.shape[-1]
    lane = lax.broadcasted_iota(jnp.int32, x.shape, x.ndim - 1)
    s = 1
    while s < n:
        x = x + jnp.where(lane >= s, pltpu.roll(x, s, x.ndim - 1), 0.0)
        s *= 2
    return x


def _qkv_kernel(x_ref, g1_ref, wqk_ref, wvt_ref, wfl_ref, ws_ref, wsvt_ref, bf_ref, fqg_ref, fkg_ref, sqg_ref, skg_ref,
                f0_ref, fq_ref, fk_ref, aug_ref, fvt_ref, sq_ref, sk_ref, svt_ref, fend_ref, carry_ref, *, n_pad):
    j = pl.program_id(1)

    @pl.when(j == 0)
    def _():
        carry_ref[...] = f0_ref[...]

    x = x_ref[...]
    tm = x.shape[0]
    h = _rms(x, g1_ref[...]).astype(_BF)
    nt = (((1,), (1,)), ((), ()))
    f = jnp.dot(h, wqk_ref[...], preferred_element_type=_F32)
    s = jnp.dot(h, ws_ref[...], preferred_element_type=_F32)
    svt = lax.dot_general(wsvt_ref[...], h, nt, preferred_element_type=_F32)
    vt = lax.dot_general(wvt_ref[...], h, nt, preferred_element_type=_F32)
    flog = lax.dot_general(wfl_ref[...], h, nt, preferred_element_type=_F32)

    fq_ref[...] = _head_rms(f[:, :FOX_W], fqg_ref[...]).astype(_BF)
    fk_ref[...] = _head_rms(f[:, FOX_W:], fkg_ref[...]).astype(_BF)
    sq_ref[...] = _head_rms(s[:, :SWA_QW], sqg_ref[...]).astype(_BF)
    sk_ref[...] = _head_rms(s[:, SWA_QW:], skg_ref[...]).astype(_BF)

    ones_row = jnp.where(lax.broadcasted_iota(jnp.int32, (HEAD_DIM, tm), 0) == 0, 1.0, 0.0).astype(_BF)
    for hd in range(FOX_HEADS):
        fvt_ref[hd] = jnp.concatenate([vt[hd * HEAD_DIM:(hd + 1) * HEAD_DIM].astype(_BF), ones_row], axis=0)
    for g in range(SWA_KV_HEADS):
        blk = jnp.concatenate([svt[g * HEAD_DIM:(g + 1) * HEAD_DIM].astype(_BF), ones_row], axis=0)
        for c in range(tm // WINDOW):
            svt_ref[g, c] = blk[:, c * WINDOW:(c + 1) * WINDOW]

    z = flog + bf_ref[...]
    log_f = jnp.minimum(z, 0.0) - jnp.log1p(jnp.exp(-jnp.abs(z)))
    if n_pad:
        lane = lax.broadcasted_iota(jnp.int32, log_f.shape, 1)
        log_f = jnp.where(lane >= n_pad, log_f, 0.0)
    cum = _lane_cumsum(log_f) + carry_ref[...]
    carry_ref[...] = cum[:, -1:]
    fend_ref[...] = cum[:, -1:]
    nfl = cum * (-LOG2E)
    p1 = nfl.astype(_BF).astype(_F32)
    p2 = (nfl - p1).astype(_BF).astype(_F32)
    p3 = (nfl - p1 - p2).astype(_BF).astype(_F32)
    g = jnp.concatenate([p1, p2, p3, jnp.zeros((LANES - 3 * FOX_HEADS, tm), _F32)], axis=0)
    aug_ref[...] = g.T.astype(_BF)


def _qkv_call(x3, g1, wqk, wvt, wfl, ws, wsvt, bfg, fqg, fkg, sqg, skg, f0, *, tm, n_pad):
    B, S, D = x3.shape
    nj = S // tm
    row = lambda w: pl.BlockSpec((None, tm, w), lambda b, j: (b, j, 0))
    full = lambda a: pl.BlockSpec(a.shape, lambda b, j: (0,) * a.ndim)
    bsd = lambda w: jax.ShapeDtypeStruct((B, S, w), _BF)
    out_shape = [bsd(FOX_W), bsd(FOX_W), bsd(LANES), jax.ShapeDtypeStruct((B, FOX_HEADS, nj, 2 * HEAD_DIM, tm), _BF),
                 bsd(SWA_QW), bsd(SWA_KW),
                 jax.ShapeDtypeStruct((B, SWA_KV_HEADS, S // WINDOW, 2 * HEAD_DIM, WINDOW), _BF),
                 jax.ShapeDtypeStruct((B, FOX_HEADS, 1), _F32)]
    out_specs = [row(FOX_W), row(FOX_W), row(LANES),
                 pl.BlockSpec((None, FOX_HEADS, None, 2 * HEAD_DIM, tm), lambda b, j: (b, 0, j, 0, 0)),
                 row(SWA_QW), row(SWA_KW),
                 pl.BlockSpec((None, SWA_KV_HEADS, tm // WINDOW, 2 * HEAD_DIM, WINDOW), lambda b, j: (b, 0, j, 0, 0)),
                 pl.BlockSpec((None, FOX_HEADS, 1), lambda b, j: (b, 0, 0))]
    return pl.pallas_call(
        functools.partial(_qkv_kernel, n_pad=n_pad),
        grid=(B, nj),
        in_specs=[row(D)] + [full(a) for a in (g1, wqk, wvt, wfl, ws, wsvt, bfg, fqg, fkg, sqg, skg, f0)],
        out_specs=out_specs, out_shape=out_shape,
        scratch_shapes=[pltpu.VMEM((FOX_HEADS, 1), _F32)],
        compiler_params=_cparams(("arbitrary", "arbitrary")),
        name="qkv_proj",
    )(x3, g1, wqk, wvt, wfl, ws, wsvt, bfg, fqg, fkg, sqg, skg, f0)


def _fox_kernel(q_ref, k_ref, aug_ref, vt_ref, km_ref, augm_ref, vtm_ref, o_ref, m_ref, acc_ref, s_ref, mx_ref):
    g = pl.program_id(1)
    qi = pl.program_id(2)
    tq = q_ref.shape[0]
    tk = vt_ref.shape[-1]
    nt = (((1,), (1,)), ((), ()))
    lane = lax.broadcasted_iota(jnp.int32, (tq, LANES), 1)

    qaugs = []
    for pp in range(FOX_PAIRS):
        q = q_ref[:, pp * LANES:(pp + 1) * LANES]
        parts = []
        for a in range(2):
            head = 2 * (FOX_PAIRS * g + pp) + a
            keep = (lane < HEAD_DIM) if a == 0 else (lane >= HEAD_DIM)
            ones = jnp.where((lane < 3 * FOX_HEADS) & ((lane & (FOX_HEADS - 1)) == head), 1.0, 0.0).astype(_BF)
            parts.append(jnp.concatenate([jnp.where(keep, q, jnp.zeros_like(q)), ones], axis=1))
        qaugs.append(jnp.concatenate(parts, axis=0))

    def scores(pp, k, aug, mask):
        st = lax.dot_general(jnp.concatenate([k, aug], axis=1), qaugs[pp], nt, preferred_element_type=_F32)
        if mask is not None:
            st = jnp.where(mask, st, NEG_INF)
        return st, jnp.max(st, axis=0, keepdims=True)

    def tile_scores(pp, ki, mask=None):
        off = pl.multiple_of(ki * tk, tk)
        return scores(pp, k_ref[pl.ds(off, tk), pp * LANES:(pp + 1) * LANES], aug_ref[pl.ds(off, tk), :], mask)

    def process(pp, st, mx, vts, first):
        m_new = mx if first else jnp.maximum(m_ref[pp], mx)
        pt = jnp.exp2(st - m_new).astype(_BF)
        if not first:
            alpha = jnp.exp2(m_ref[pp] - m_new)
        for a in range(2):
            pv = jnp.dot(vts[a], pt[:, a * tq:(a + 1) * tq], preferred_element_type=_F32)
            acc_ref[2 * pp + a] = pv if first else alpha[:, a * tq:(a + 1) * tq] * acc_ref[2 * pp + a] + pv
        m_ref[pp] = m_new

    def stash(buf, pp, st_mx):
        s_ref[buf, pp] = st_mx[0]
        mx_ref[buf, pp] = st_mx[1]

    def vts(pp, ki):
        return vt_ref[2 * pp, ki], vt_ref[2 * pp + 1, ki]

    causal = lax.broadcasted_iota(jnp.int32, (tk, 2 * tq), 0) <= (lax.broadcasted_iota(jnp.int32, (tk, 2 * tq), 1) & (tq - 1))
    mask_m = lax.broadcasted_iota(jnp.int32, (META_BLOCK, 2 * tq), 0) >= META_PAD
    for pp in range(FOX_PAIRS):
        stash(0, pp, tile_scores(pp, qi, causal))
        st, mx = scores(pp, km_ref[:, pp * LANES:(pp + 1) * LANES], augm_ref[...], mask_m)
        process(pp, st, mx, (vtm_ref[2 * pp], vtm_ref[2 * pp + 1]), True)

    def step(j, buf):
        nxt = [tile_scores(pp, j) for pp in range(FOX_PAIRS)]
        cur = jnp.where(j == 0, qi, j - 1)
        for pp in range(FOX_PAIRS):
            process(pp, s_ref[buf, pp], mx_ref[buf, pp], vts(pp, cur), False)
        for pp in range(FOX_PAIRS):
            stash(1 - buf, pp, nxt[pp])

    def body(jj, c):
        step(2 * jj, 0)
        step(2 * jj + 1, 1)
        return c
    lax.fori_loop(0, qi // 2, body, 0)

    def finish(buf):
        last = jnp.where(qi == 0, qi, qi - 1)
        outs = []
        for pp in range(FOX_PAIRS):
            process(pp, s_ref[buf, pp], mx_ref[buf, pp], vts(pp, last), False)
            for a in range(2):
                acc = acc_ref[2 * pp + a]
                outs.append(acc[:HEAD_DIM] / acc[HEAD_DIM:HEAD_DIM + 1])
        o_ref[...] = jnp.concatenate(outs, axis=0).T.astype(_BF)

    @pl.when(qi % 2 == 1)
    def _():
        step(qi - 1, 0)
        finish(1)

    @pl.when(qi % 2 == 0)
    def _():
        finish(0)


def _fox_call(fq, fk, aug, fvt, km, augm, vtm, *, tq):
    B, S, _ = fq.shape
    nk = fvt.shape[2]
    assert fvt.shape[-1] == tq
    w = FOX_PAIRS * LANES
    qspec = pl.BlockSpec((None, tq, w), lambda b, g, i: (b, i, g))
    return pl.pallas_call(
        _fox_kernel, grid=(B, FOX_HEADS // (2 * FOX_PAIRS), S // tq),
        in_specs=[qspec,
                  pl.BlockSpec((None, S, w), lambda b, g, i: (b, 0, g)),
                  pl.BlockSpec((None, S, LANES), lambda b, g, i: (b, 0, 0)),
                  pl.BlockSpec((None, 2 * FOX_PAIRS, nk, 2 * HEAD_DIM, tq), lambda b, g, i: (b, g, 0, 0, 0)),
                  pl.BlockSpec((META_BLOCK, w), lambda b, g, i: (0, g)),
                  pl.BlockSpec((META_BLOCK, LANES), lambda b, g, i: (0, 0)),
                  pl.BlockSpec((2 * FOX_PAIRS, 2 * HEAD_DIM, META_BLOCK), lambda b, g, i: (g, 0, 0))],
        out_specs=qspec, out_shape=jax.ShapeDtypeStruct((B, S, FOX_W), _BF),
        scratch_shapes=[pltpu.VMEM((FOX_PAIRS, 1, 2 * tq), _F32), pltpu.VMEM((2 * FOX_PAIRS, 2 * HEAD_DIM, tq), _F32),
                        pltpu.VMEM((2, FOX_PAIRS, tq, 2 * tq), _F32), pltpu.VMEM((2, FOX_PAIRS, 1, 2 * tq), _F32)],
        compiler_params=_cparams(("arbitrary", "arbitrary", "arbitrary")), name="fox_attn",
    )(fq, fk, aug, fvt, km, augm, vtm)


def _swa_bias(tq):
    j = np.arange(WINDOW + tq)[:, None] - WINDOW
    i = np.arange(tq)[None, :]
    dist = i - j
    ok = (dist >= 0) & (dist < WINDOW)
    slopes = np.exp2(-8.0 * (np.arange(SWA_HEADS, dtype=np.float32) + 1.0) / SWA_HEADS).astype(np.float32)
    per_head = np.where(ok[None], -slopes[:, None, None] * dist[None].astype(np.float32), np.float32(NEG_INF))
    rep = SWA_HEADS // SWA_KV_HEADS
    return np.stack([np.concatenate(list(per_head[g * rep:(g + 1) * rep]), axis=1) for g in range(SWA_KV_HEADS)])


def _swa_kernel(q_ref, kc_ref, kp_ref, km_ref, vc_ref, vp_ref, vm_ref, bias_ref, sink_ref, o_ref):
    qi = pl.program_id(1)
    tq = q_ref.shape[0]
    rep = SWA_HEADS // SWA_KV_HEADS
    nt = (((1,), (1,)), ((), ()))
    first = qi == 0
    kcat = jnp.concatenate([jnp.where(first, km_ref[...], kp_ref[...]), kc_ref[...]], axis=0)
    row = lax.broadcasted_iota(jnp.int32, (WINDOW + tq, rep * tq), 0)
    row_ok = row >= jnp.where(first, META_PAD, 0)
    lane = lax.broadcasted_iota(jnp.int32, (tq, LANES), 1)
    outs = [None] * SWA_HEADS
    for g in range(SWA_KV_HEADS):
        keep = (lane < HEAD_DIM) if g == 0 else (lane >= HEAD_DIM)
        qg = jnp.concatenate([jnp.where(keep, q_ref[:, p * LANES:(p + 1) * LANES], jnp.zeros((tq, LANES), _BF))
                              for p in range(rep)], axis=0)
        vcat = jnp.concatenate([jnp.where(first, vm_ref[g], vp_ref[g, 0])] + [vc_ref[g, c] for c in range(tq // WINDOW)],
                               axis=1)
        st = lax.dot_general(kcat, qg, nt, preferred_element_type=_F32) + bias_ref[g]
        st = jnp.where(row_ok, st, NEG_INF)
        sink = sink_ref[g]
        m = jnp.maximum(jnp.max(st, axis=0, keepdims=True), sink)
        pt = jnp.exp(st - m).astype(_BF)
        acc = jnp.dot(vcat, pt, preferred_element_type=_F32)
        o = acc[:HEAD_DIM] / (acc[HEAD_DIM:HEAD_DIM + 1] + jnp.exp(sink - m))
        for p in range(rep):
            outs[2 * p + g] = o[:, p * tq:(p + 1) * tq]
    o_ref[...] = jnp.concatenate(outs, axis=0).T.astype(_BF)


def _swa_call(sinks, sq, sk, svt, km, vtm, *, tq):
    B, S, _ = sq.shape
    r = tq // WINDOW
    rep = SWA_HEADS // SWA_KV_HEADS
    bias = jnp.asarray(_swa_bias(tq))
    sink_rows = jnp.repeat(sinks.reshape(SWA_KV_HEADS, rep), tq, axis=1)[:, None, :]
    prev_blk = lambda i: jnp.maximum(i * r - 1, 0)
    return pl.pallas_call(
        _swa_kernel, grid=(B, S // tq),
        in_specs=[pl.BlockSpec((None, tq, SWA_QW), lambda b, i: (b, i, 0)),
                  pl.BlockSpec((None, tq, LANES), lambda b, i: (b, i, 0)),
                  pl.BlockSpec((None, WINDOW, LANES), lambda b, i: (b, prev_blk(i), 0)),
                  pl.BlockSpec((META_BLOCK, LANES), lambda b, i: (0, 0)),
                  pl.BlockSpec((None, SWA_KV_HEADS, r, 2 * HEAD_DIM, WINDOW), lambda b, i: (b, 0, i, 0, 0)),
                  pl.BlockSpec((None, SWA_KV_HEADS, 1, 2 * HEAD_DIM, WINDOW), lambda b, i: (b, 0, prev_blk(i), 0, 0)),
                  pl.BlockSpec((SWA_KV_HEADS, 2 * HEAD_DIM, META_BLOCK), lambda b, i: (0, 0, 0)),
                  pl.BlockSpec(bias.shape, lambda b, i: (0, 0, 0)),
                  pl.BlockSpec(sink_rows.shape, lambda b, i: (0, 0, 0))],
        out_specs=pl.BlockSpec((None, tq, SWA_QW), lambda b, i: (b, i, 0)),
        out_shape=jax.ShapeDtypeStruct((B, S, SWA_QW), _BF),
        compiler_params=_cparams(("arbitrary", "arbitrary")), name="swa_attn",
    )(sq, sk, sk, km, svt, svt, vtm, bias, sink_rows)


def _split3(v):
    a = v.astype(_BF)
    r = v - a.astype(_F32)
    b = r.astype(_BF)
    c = (r - b.astype(_F32)).astype(_BF)
    return a, b, c


def _mix_kernel(x_ref, oa_ref, ob_ref, g1_ref, wga_ref, wgb_ref, wfo_ref, wso_ref, wo_ref, g2_ref, wr_ref, br_ref,
                hres_ref, h2_ref, lg_ref):
    x = x_ref[...]
    h = _rms(x, g1_ref[...]).astype(_BF)
    ga = jax.nn.sigmoid(jnp.dot(h, wga_ref[...], preferred_element_type=_F32))
    gb = jax.nn.sigmoid(jnp.dot(h, wgb_ref[...], preferred_element_type=_F32))
    mix = ga * jnp.dot(oa_ref[...], wfo_ref[...], preferred_element_type=_F32) \
        + gb * jnp.dot(ob_ref[...], wso_ref[...], preferred_element_type=_F32)
    hres = x + jnp.dot(mix.astype(_BF), wo_ref[...], preferred_element_type=_F32)
    hres_ref[...] = hres
    h2 = _rms(hres, g2_ref[...])
    h2_ref[...] = h2.astype(_BF)
    wr = wr_ref[...]
    w_hi = wr.astype(_BF)
    w_lo = (wr - w_hi.astype(_F32)).astype(_BF)
    h_hi = h2.astype(_BF)
    h_lo = (h2 - h_hi.astype(_F32)).astype(_BF)
    lg = jnp.dot(h_hi, w_hi, preferred_element_type=_F32) + jnp.dot(h_hi, w_lo, preferred_element_type=_F32) \
        + jnp.dot(h_lo, w_hi, preferred_element_type=_F32)
    lg_ref[...] = lg + br_ref[...]


def _mix_call(x2, oa, ob, g1, wga, wgb, wfo, wso, wo, g2, wr, br, *, tm):
    N, D = x2.shape
    row = lambda w: pl.BlockSpec((tm, w), lambda i: (i, 0))
    full = lambda a: pl.BlockSpec(a.shape, lambda i: (0,) * a.ndim)
    return pl.pallas_call(
        _mix_kernel, grid=(N // tm,),
        in_specs=[row(D), row(FOX_W), row(SWA_QW)] + [full(a) for a in (g1, wga, wgb, wfo, wso, wo, g2, wr, br)],
        out_specs=[row(D), row(D), row(LANES)],
        out_shape=[jax.ShapeDtypeStruct((N, D), _F32), jax.ShapeDtypeStruct((N, D), _BF),
                   jax.ShapeDtypeStruct((N, LANES), _F32)],
        compiler_params=_cparams(("arbitrary",)), name="mix_proj",
    )(x2, oa, ob, g1, wga, wgb, wfo, wso, wo, g2, wr, br)


def _route_kernel(lg_ref, a_ref, at_ref, cnt_ref):
    T = lg_ref.shape[0]
    lt = lg_ref[...].T[:N_EXPERTS]
    e_iota = lax.broadcasted_iota(jnp.int32, lt.shape, 0)
    cur = lt
    vals, idxs = [], []
    for _ in range(TOP_K):
        m = jnp.max(cur, axis=0, keepdims=True)
        idx = jnp.min(jnp.where(cur == m, e_iota, N_EXPERTS), axis=0, keepdims=True)
        vals.append(m)
        idxs.append(idx)
        cur = jnp.where(e_iota == idx, -jnp.inf, cur)
    es = [jnp.exp(v - vals[0]) for v in vals]
    den = es[0] + es[1] + es[2] + es[3]
    gates = [e / den for e in es]

    onehot = jnp.zeros(lt.shape, _F32)
    for idx in idxs:
        onehot = onehot + jnp.where(e_iota == idx, 1.0, 0.0)
    r_i = lax.broadcasted_iota(jnp.int32, (T, T), 0)
    c_i = lax.broadcasted_iota(jnp.int32, (T, T), 1)
    utri = jnp.where(r_i < c_i, 1.0, 0.0).astype(_BF)
    prefix = jnp.dot(onehot.astype(_BF), utri, preferred_element_type=_F32)
    cnt = jnp.sum(onehot, axis=1, keepdims=True)
    chunks = jnp.floor((cnt + (RUN_ALIGN - 1)) * (1.0 / RUN_ALIGN))
    l_r = lax.broadcasted_iota(jnp.int32, (N_EXPERTS, N_EXPERTS), 0)
    l_c = lax.broadcasted_iota(jnp.int32, (N_EXPERTS, N_EXPERTS), 1)
    ltri = jnp.where(l_c < l_r, 1.0, 0.0).astype(_BF)
    lbase = RUN_ALIGN * jnp.dot(ltri, jnp.broadcast_to(chunks, (N_EXPERTS, LANES)).astype(_BF),
                                preferred_element_type=_F32)[:, 0:1]
    slot_all = prefix + lbase
    rows = [jnp.sum(jnp.where(e_iota == idx, slot_all, 0.0), axis=0, keepdims=True) for idx in idxs] + gates
    at = jnp.concatenate(rows, axis=0)
    at_ref[...] = at
    a_ref[...] = jnp.concatenate([at, jnp.zeros((LANES - 2 * TOP_K, T), _F32)], axis=0).T
    cnt_ref[...] = jnp.broadcast_to(cnt, (N_EXPERTS, LANES)).astype(jnp.int32)


def _route_call(lg, *, T):
    N = lg.shape[0]
    nt = N // T
    return pl.pallas_call(
        _route_kernel, grid=(nt,),
        in_specs=[pl.BlockSpec((T, LANES), lambda i: (i, 0))],
        out_specs=[pl.BlockSpec((T, LANES), lambda i: (i, 0)), pl.BlockSpec((8, T), lambda i: (0, i)),
                   pl.BlockSpec((None, N_EXPERTS, LANES), lambda i: (i, 0, 0))],
        out_shape=[jax.ShapeDtypeStruct((N, LANES), _F32), jax.ShapeDtypeStruct((8, N), _F32),
                   jax.ShapeDtypeStruct((nt, N_EXPERTS, LANES), jnp.int32)],
        compiler_params=_cparams(("arbitrary",)), name="route",
    )(lg)


def _div_pow2(v, d):
    assert d & (d - 1) == 0
    return lax.shift_right_logical(v, d.bit_length() - 1)


def _plan_kernel(cnt_ref, goff_ref, lbase_ref, nch_ref, eblk_ref, gap_ref, info_ref, *, nt, block_rows):
    def per_expert(e, base):
        def per_tile(i, run):
            n = _div_pow2(cnt_ref[i * N_EXPERTS + e] + (RUN_ALIGN - 1), RUN_ALIGN)
            goff_ref[i * N_EXPERTS + e] = run
            nch_ref[i * N_EXPERTS + e] = n
            return run + n * RUN_ALIGN
        end = lax.fori_loop(0, nt, per_tile, base)
        nblk = _div_pow2(end - base + (block_rows - 1), block_rows)
        nxt = base + nblk * block_rows
        gap_ref[2 * e] = end
        gap_ref[2 * e + 1] = _div_pow2(nxt - end, RUN_ALIGN)
        eblk_ref[2 * e] = _div_pow2(base, block_rows)
        eblk_ref[2 * e + 1] = nblk
        return nxt
    total = lax.fori_loop(0, N_EXPERTS, per_expert, 0)
    info_ref[0] = _div_pow2(total, block_rows)

    def per_tile2(i, c):
        def per_e(e, run):
            lbase_ref[i * N_EXPERTS + e] = run
            return run + nch_ref[i * N_EXPERTS + e] * RUN_ALIGN
        lax.fori_loop(0, N_EXPERTS, per_e, 0)
        return c
    lax.fori_loop(0, nt, per_tile2, 0)


def _plan_call(cnt_flat, *, nt, block_rows):
    smem = pl.BlockSpec(memory_space=pltpu.SMEM)
    i32 = lambda n: jax.ShapeDtypeStruct((n,), jnp.int32)
    return pl.pallas_call(
        functools.partial(_plan_kernel, nt=nt, block_rows=block_rows),
        in_specs=[smem], out_specs=[smem] * 6,
        out_shape=[i32(nt * N_EXPERTS), i32(nt * N_EXPERTS), i32(nt * N_EXPERTS), i32(2 * N_EXPERTS), i32(2 * N_EXPERTS),
                   i32(1)],
        name="plan",
    )(cnt_flat)


def _pack_pairs(v):
    w = v.shape[1] // 2
    return pltpu.pack_elementwise([v[:, :w], v[:, w:]], packed_dtype=_BF)


def _unpack_pairs(wd):
    lo = pltpu.unpack_elementwise(wd, index=0, packed_dtype=_BF, unpacked_dtype=_F32)
    hi = pltpu.unpack_elementwise(wd, index=1, packed_dtype=_BF, unpacked_dtype=_F32)
    return lo.astype(_BF), hi.astype(_BF)


def _dispatch_kernel(goff, lbase, nch, gap, info, at_ref, h2_ref, xs_ref, xl_ref, z_ref, sem, zsem, tsem):
    i = pl.program_id(0)
    R = xl_ref.shape[1]
    T = h2_ref.shape[0]
    block_rows = z_ref.shape[0]
    nb = xs_ref.shape[0] // block_rows

    @pl.when(i == 0)
    def _():
        z_ref[...] = jnp.zeros(z_ref.shape, z_ref.dtype)

        def gap_copy(e):
            rows = pl.multiple_of(gap[2 * e + 1] * RUN_ALIGN, RUN_ALIGN)
            return pltpu.make_async_copy(z_ref.at[pl.ds(0, rows)],
                                         xs_ref.at[pl.ds(pl.multiple_of(gap[2 * e], RUN_ALIGN), rows)], zsem)

        def fill(e, c):
            @pl.when(gap[2 * e + 1] > 0)
            def _():
                gap_copy(e).start()
            return c
        lax.fori_loop(0, N_EXPERTS, fill, 0)

        def fill_wait(e, c):
            @pl.when(gap[2 * e + 1] > 0)
            def _():
                gap_copy(e).wait()
            return c
        lax.fori_loop(0, N_EXPERTS, fill_wait, 0)

        def tail(b, c):
            pltpu.make_async_copy(z_ref, xs_ref.at[pl.ds(pl.multiple_of(b * block_rows, block_rows), block_rows)],
                                  tsem).start()
            return c
        lax.fori_loop(info[0], nb, tail, 0)

        def tail_wait(b, c):
            pltpu.make_async_copy(z_ref, xs_ref.at[pl.ds(0, block_rows)], tsem).wait()
            return c
        lax.fori_loop(info[0], nb, tail_wait, 0)

    slot_iota = lax.broadcasted_iota(jnp.int32, (R, T), 0).astype(_F32)
    pm = jnp.zeros((R, T), _F32)
    for k in range(TOP_K):
        pm = pm + jnp.where(slot_iota == at_ref[k:k + 1, :], 1.0, 0.0)
    xl = jnp.dot(pm.astype(_BF), h2_ref[...], preferred_element_type=_F32)
    slot = i & 1

    def wait_tile(t, s):
        total = lax.fori_loop(0, N_EXPERTS, lambda e, c: c + nch[t * N_EXPERTS + e], 0)

        @pl.when(total > 0)
        def _():
            rows = pl.multiple_of(total * RUN_ALIGN, RUN_ALIGN)
            pltpu.make_async_copy(xl_ref.at[s, pl.ds(0, rows)], xs_ref.at[pl.ds(0, rows)], sem.at[s]).wait()

    @pl.when(i >= 2)
    def _():
        wait_tile(i - 2, slot)
    xl_ref[slot] = _pack_pairs(xl)

    def issue(e, c):
        n = nch[i * N_EXPERTS + e]

        @pl.when(n > 0)
        def _():
            rows = pl.multiple_of(n * RUN_ALIGN, RUN_ALIGN)
            pltpu.make_async_copy(xl_ref.at[slot, pl.ds(pl.multiple_of(lbase[i * N_EXPERTS + e], RUN_ALIGN), rows)],
                                  xs_ref.at[pl.ds(pl.multiple_of(goff[i * N_EXPERTS + e], RUN_ALIGN), rows)],
                                  sem.at[slot]).start()
        return c
    lax.fori_loop(0, N_EXPERTS, issue, 0)

    @pl.when(i == pl.num_programs(0) - 1)
    def _():
        @pl.when(i >= 1)
        def _():
            wait_tile(i - 1, 1 - slot)
        wait_tile(i, slot)


def _dispatch_call(goff, lbase, nch, gap, info, at, h2, *, T, R, rows, block_rows):
    N, D = h2.shape
    grid_spec = pltpu.PrefetchScalarGridSpec(
        num_scalar_prefetch=5, grid=(N // T,),
        in_specs=[pl.BlockSpec((8, T), lambda i, *_: (0, i)), pl.BlockSpec((T, D), lambda i, *_: (i, 0))],
        out_specs=pl.BlockSpec(memory_space=pl.ANY),
        scratch_shapes=[pltpu.VMEM((2, R, D // 2), jnp.uint32), pltpu.VMEM((block_rows, D // 2), jnp.uint32),
                        pltpu.SemaphoreType.DMA((2,)), pltpu.SemaphoreType.DMA, pltpu.SemaphoreType.DMA])
    return pl.pallas_call(
        _dispatch_kernel, grid_spec=grid_spec, out_shape=jax.ShapeDtypeStruct((rows, D // 2), jnp.uint32),
        compiler_params=_cparams(("arbitrary",)), name="dispatch",
    )(goff, lbase, nch, gap, info, at, h2)


def _expert_kernel(eblk, info, xs_ref, wu_ref, bu_ref, wd_ref, bd_ref, ys_ref, wub_ref, wdb_ref, xb_ref, yb_ref, xsem, ysem):
    e = pl.program_id(0)
    block_rows = xb_ref.shape[1]
    nb = ys_ref.shape[0] // block_rows
    b0 = eblk[2 * e]
    n = eblk[2 * e + 1]
    half = wu_ref.shape[0] // 2
    wub_ref[...] = wu_ref[...].astype(_BF)
    wdb_ref[...] = wd_ref[...].astype(_BF)

    def rows_of(blk):
        return pl.ds(pl.multiple_of(blk * block_rows, block_rows), block_rows)

    def x_copy(j, slot):
        return pltpu.make_async_copy(xs_ref.at[rows_of(b0 + j)], xb_ref.at[slot], xsem.at[slot])

    def y_copy(j, slot):
        return pltpu.make_async_copy(yb_ref.at[slot], ys_ref.at[rows_of(b0 + j)], ysem.at[slot])

    @pl.when((e == 0) & (n > 0))
    def _():
        x_copy(0, 0).start(priority=1)

    def body(j, c):
        slot = j & 1
        x_copy(j, slot).wait()

        @pl.when(j + 1 < n)
        def _():
            x_copy(j + 1, 1 - slot).start(priority=1)

        @pl.when(j >= 2)
        def _():
            y_copy(j - 2, slot).wait()

        x_lo, x_hi = _unpack_pairs(xb_ref[slot])
        gu = jnp.dot(x_lo, wub_ref[:half, :], preferred_element_type=_F32) \
            + jnp.dot(x_hi, wub_ref[half:, :], preferred_element_type=_F32) + bu_ref[...]
        g = jnp.minimum(gu[:, :D_FF], SWIGLU_LIMIT)
        u = jnp.clip(gu[:, D_FF:], -SWIGLU_LIMIT, SWIGLU_LIMIT)
        act = (u + 1.0) * (g * jax.nn.sigmoid(SWIGLU_ALPHA * g))
        y = jnp.dot(act.astype(_BF), wdb_ref[...], preferred_element_type=_F32) + bd_ref[...]
        yb_ref[slot] = _pack_pairs(y)
        y_copy(j, slot).start()
        return c
    lax.fori_loop(0, n, body, 0)

    @pl.when(n >= 2)
    def _():
        y_copy(n - 2, n & 1).wait()

    @pl.when(n >= 1)
    def _():
        y_copy(n - 1, (n - 1) & 1).wait()

    last = pl.num_programs(0) - 1
    e_next = jnp.minimum(e + 1, last)

    @pl.when((e < last) & (eblk[2 * e_next + 1] > 0))
    def _():
        pltpu.make_async_copy(xs_ref.at[rows_of(eblk[2 * e_next])], xb_ref.at[0], xsem.at[0]).start(priority=1)

    @pl.when(e == last)
    def _():
        yb_ref[0] = _pack_pairs(jnp.zeros((block_rows, 2 * yb_ref.shape[2]), _F32))

        def tail_copy(b):
            return pltpu.make_async_copy(yb_ref.at[0], ys_ref.at[pl.ds(pl.multiple_of(b * block_rows, block_rows), block_rows)],
                                         ysem.at[0])

        def tail(b, c):
            tail_copy(b).start()
            return c
        lax.fori_loop(info[0], nb, tail, 0)

        def tail_wait(b, c):
            tail_copy(b).wait()
            return c
        lax.fori_loop(info[0], nb, tail_wait, 0)


def _expert_call(eblk, info, xs, wu, bu, wd, bd, *, block_rows):
    rows, half = xs.shape
    D = 2 * half
    ex = lambda e, *_: (e, 0, 0)
    grid_spec = pltpu.PrefetchScalarGridSpec(
        num_scalar_prefetch=2, grid=(N_EXPERTS,),
        in_specs=[pl.BlockSpec(memory_space=pl.ANY),
                  pl.BlockSpec((None, D, 2 * D_FF), ex), pl.BlockSpec((None, 1, 2 * D_FF), ex),
                  pl.BlockSpec((None, D_FF, D), ex), pl.BlockSpec((None, 1, D), ex)],
        out_specs=pl.BlockSpec(memory_space=pl.ANY),
        scratch_shapes=[pltpu.VMEM((D, 2 * D_FF), _BF), pltpu.VMEM((D_FF, D), _BF),
                        pltpu.VMEM((2, block_rows, half), jnp.uint32), pltpu.VMEM((2, block_rows, half), jnp.uint32),
                        pltpu.SemaphoreType.DMA((2,)), pltpu.SemaphoreType.DMA((2,))])
    return pl.pallas_call(
        _expert_kernel, grid_spec=grid_spec, out_shape=jax.ShapeDtypeStruct((rows, half), jnp.uint32),
        compiler_params=_cparams(("arbitrary",)), name="experts",
    )(eblk, info, xs, wu, bu, wd, bd)


def _combine_kernel(goff, lbase, nch, a_ref, hres_ref, ys_ref, o_ref, yl_ref, sem):
    i = pl.program_id(0)
    nt = pl.num_programs(0)
    R = yl_ref.shape[1]
    T = a_ref.shape[0]

    def fetch(t, slot):
        def issue(e, c):
            n = nch[t * N_EXPERTS + e]

            @pl.when(n > 0)
            def _():
                rows = pl.multiple_of(n * RUN_ALIGN, RUN_ALIGN)
                pltpu.make_async_copy(
                    ys_ref.at[pl.ds(pl.multiple_of(goff[t * N_EXPERTS + e], RUN_ALIGN), rows)],
                    yl_ref.at[slot, pl.ds(pl.multiple_of(lbase[t * N_EXPERTS + e], RUN_ALIGN), rows)], sem.at[slot]).start()
            return c
        lax.fori_loop(0, N_EXPERTS, issue, 0)

    @pl.when(i == 0)
    def _():
        yl_ref[...] = jnp.zeros(yl_ref.shape, yl_ref.dtype)
        fetch(0, 0)

    slot = i & 1

    @pl.when(i + 1 < nt)
    def _():
        fetch(i + 1, 1 - slot)

    total = lax.fori_loop(0, N_EXPERTS, lambda e, c: c + nch[i * N_EXPERTS + e], 0)

    @pl.when(total > 0)
    def _():
        rows = pl.multiple_of(total * RUN_ALIGN, RUN_ALIGN)
        pltpu.make_async_copy(ys_ref.at[pl.ds(0, rows)], yl_ref.at[slot, pl.ds(0, rows)], sem.at[slot]).wait()

    a = a_ref[...]
    slot_iota = lax.broadcasted_iota(jnp.int32, (T, R), 1).astype(_F32)
    pt = jnp.zeros((T, R), _F32)
    for k in range(TOP_K):
        pt = pt + jnp.where(slot_iota == a[:, k:k + 1], a[:, TOP_K + k:TOP_K + k + 1], 0.0)
    pt = pt.astype(_BF)
    y_lo, y_hi = _unpack_pairs(yl_ref[slot])
    half = y_lo.shape[1]
    o_ref[:, :half] = hres_ref[:, :half] + jnp.dot(pt, y_lo, preferred_element_type=_F32)
    o_ref[:, half:] = hres_ref[:, half:] + jnp.dot(pt, y_hi, preferred_element_type=_F32)


def _combine_call(goff, lbase, nch, a, hres, ys, *, T, R):
    N, D = hres.shape
    grid_spec = pltpu.PrefetchScalarGridSpec(
        num_scalar_prefetch=3, grid=(N // T,),
        in_specs=[pl.BlockSpec((T, LANES), lambda i, *_: (i, 0)), pl.BlockSpec((T, D), lambda i, *_: (i, 0)),
                  pl.BlockSpec(memory_space=pl.ANY)],
        out_specs=pl.BlockSpec((T, D), lambda i, *_: (i, 0)),
        scratch_shapes=[pltpu.VMEM((2, R, D // 2), jnp.uint32), pltpu.SemaphoreType.DMA((2,))])
    return pl.pallas_call(
        _combine_kernel, grid_spec=grid_spec, out_shape=jax.ShapeDtypeStruct((N, D), _F32),
        compiler_params=_cparams(("arbitrary",)), name="combine",
    )(goff, lbase, nch, a, hres, ys)


def _tiles(S):
    pick = lambda pref: next(t for t in pref if S % t == 0)
    tm = pick((512, 256, 128))
    return dict(tm=tm, tq_fox=tm, tq_swa=pick((256, 128)),
                tm_mix=pick((512, 256, 128)), t_route=pick((256, 128)), block_rows=512)


def kernel(x, meta_tokens, norm1_gain, w_in, b_forget, fox_q_gain, fox_k_gain, swa_q_gain, swa_k_gain, swa_sinks,
           w_fox_out, w_swa_out, w_out, norm2_gain, w_router, b_router, w_up, b_up, w_down, b_down):
    B, S, D = x.shape
    assert norm1_gain.shape[0] == 1 and S % LANES == 0
    tl = _tiles(S)
    N = B * S
    scale = HEAD_DIM ** -0.5

    w = w_in[0]
    c0 = 3 * FOX_W
    c1 = c0 + FOX_HEADS
    c2 = c1 + SWA_QW + 2 * SWA_KW
    wqk = w[:, :2 * FOX_W].astype(_BF)
    wvt = w[:, 2 * FOX_W:c0].T.astype(_BF)
    wfl = w[:, c0:c1].T.astype(_BF)
    perm_heads = np.array([h for p in range(SWA_HEADS // 2) for h in (p, p + SWA_HEADS // 2)])
    perm_cols = (perm_heads[:, None] * HEAD_DIM + np.arange(HEAD_DIM)[None, :]).reshape(-1)
    ws_all = w[:, c1:c2]
    ws = jnp.concatenate([ws_all[:, :SWA_QW][:, perm_cols], ws_all[:, SWA_QW:SWA_QW + SWA_KW]], axis=1).astype(_BF)
    wsvt = ws_all[:, SWA_QW + SWA_KW:].T.astype(_BF)
    wga = w[:, c2:c2 + D].astype(_BF)
    wgb = w[:, c2 + D:].astype(_BF)
    g1 = norm1_gain[0][None, :]
    g2 = norm2_gain[0][None, :]
    bfg = b_forget[0][:, None]
    fqg = jnp.tile(fox_q_gain[0], FOX_HEADS)[None, :] * (scale * LOG2E)
    fkg = jnp.tile(fox_k_gain[0], FOX_HEADS)[None, :]
    sqg = jnp.tile(swa_q_gain[0], SWA_HEADS)[None, :] * scale
    skg = jnp.tile(swa_k_gain[0], SWA_KV_HEADS)[None, :]
    wfo = w_fox_out[0].astype(_BF)
    wso = w_swa_out[0][perm_cols, :].astype(_BF)
    wo = w_out[0].astype(_BF)
    wr = jnp.pad(w_router[0], ((0, 0), (0, LANES - N_EXPERTS)))
    br = jnp.pad(b_router[0], (0, LANES - N_EXPERTS), constant_values=NEG_INF)[None, :]
    sinks = swa_sinks[0].astype(_F32)
    wu = w_up[0]
    wd = w_down[0]
    bu = b_up[0][:, None, :]
    bd = b_down[0][:, None, :]

    meta_blk = jnp.pad(meta_tokens.astype(x.dtype), ((META_PAD, 0), (0, 0)))[None]
    zero_f = jnp.zeros((FOX_HEADS, 1), _F32)
    _, fk_m, aug_m, fvt_m, _, sk_m, svt_m, fend_m = _qkv_call(meta_blk, g1, wqk, wvt, wfl, ws, wsvt, bfg, fqg, fkg, sqg,
                                                              skg, zero_f, tm=META_BLOCK, n_pad=META_PAD)
    fq, fk, aug, fvt, sq, sk, svt, _ = _qkv_call(x, g1, wqk, wvt, wfl, ws, wsvt, bfg, fqg, fkg, sqg, skg, fend_m[0],
                                                 tm=tl["tm"], n_pad=0)
    o_a = _fox_call(fq, fk, aug, fvt, fk_m[0], aug_m[0], fvt_m[0, :, 0], tq=tl["tq_fox"])
    o_b = _swa_call(sinks, sq, sk, svt, sk_m[0], svt_m[0, :, 0], tq=tl["tq_swa"])

    hres, h2, lg = _mix_call(x.reshape(N, D), o_a.reshape(N, FOX_W), o_b.reshape(N, SWA_QW),
                             g1, wga, wgb, wfo, wso, wo, g2, wr, br, tm=tl["tm_mix"])

    T = tl["t_route"]
    nt = N // T
    block_rows = tl["block_rows"]
    R = -(-(TOP_K * T + N_EXPERTS * (RUN_ALIGN - 1)) // LANES) * LANES
    max_rows = N * TOP_K + nt * N_EXPERTS * (RUN_ALIGN - 1) + N_EXPERTS * (block_rows - RUN_ALIGN)
    nb = -(-max_rows // block_rows)
    a, at, cnt = _route_call(lg, T=T)
    goff, lbase, nch, eblk, gap, info = _plan_call(cnt[:, :, 0].reshape(-1), nt=nt, block_rows=block_rows)
    xs = _dispatch_call(goff, lbase, nch, gap, info, at, h2, T=T, R=R, rows=nb * block_rows, block_rows=block_rows)
    ys = _expert_call(eblk, info, xs, wu, bu, wd, bd, block_rows=block_rows)
    out = _combine_call(goff, lbase, nch, a, hres, ys, T=T, R=R)
    return out.reshape(B, S, D)
```

```python
import functools

import jax
import jax.numpy as jnp
import numpy as np
from jax import lax
from jax.experimental import pallas as pl
from jax.experimental.pallas import tpu as pltpu

HEAD_DIM = 64
N_META = 16
FOX_HEADS = 8
SWA_HEADS = 8
SWA_KV_HEADS = 2
WINDOW = 128
N_EXPERTS = 32
TOP_K = 4
D_FF = 1024
SWIGLU_LIMIT = 7.0
SWIGLU_ALPHA = 1.702
RMS_EPS = 1e-6
NEG_INF = -1e30
LOG2E = 1.4426950408889634

LANES = 128
META_BLOCK = 128
META_PAD = META_BLOCK - N_META
FOX_PAIRS = 2
RUN_ALIGN = 8
VMEM_LIMIT = 56 * 1024 * 1024

FOX_W = FOX_HEADS * HEAD_DIM
SWA_QW = SWA_HEADS * HEAD_DIM
SWA_KW = SWA_KV_HEADS * HEAD_DIM

_BF = jnp.bfloat16
_F32 = jnp.float32


def _cparams(sem):
    return pltpu.CompilerParams(dimension_semantics=sem, vmem_limit_bytes=VMEM_LIMIT)


def _rms(t, gain):
    return t * lax.rsqrt(jnp.mean(t * t, axis=-1, keepdims=True) + RMS_EPS) * gain


def _head_rms(z, gain_row):
    lane = lax.broadcasted_iota(jnp.int32, (z.shape[0], LANES), 1)
    lo_mask = lane < HEAD_DIM
    outs = []
    for b in range(z.shape[1] // LANES):
        v = z[:, b * LANES:(b + 1) * LANES]
        v2 = v * v
        tot = jnp.sum(v2, axis=-1, keepdims=True)
        lo = jnp.sum(jnp.where(lo_mask, v2, 0.0), axis=-1, keepdims=True)
        hi = tot - lo
        r_lo = lax.rsqrt(lo * (1.0 / HEAD_DIM) + RMS_EPS)
        r_hi = lax.rsqrt(hi * (1.0 / HEAD_DIM) + RMS_EPS)
        outs.append(v * jnp.where(lo_mask, r_lo, r_hi))
    return jnp.concatenate(outs, axis=-1) * gain_row


def _lane_cumsum(x):
    n = x.shape[-1]
    lane = lax.broadcasted_iota(jnp.int32, x.shape, x.ndim - 1)
    s = 1
    while s < n:
        x = x + jnp.where(lane >= s, pltpu.roll(x, s, x.ndim - 1), 0.0)
        s *= 2
    return x


def _qkv_kernel(x_ref, g1_ref, wqk_ref, wvt_ref, wfl_ref, ws_ref, wsvt_ref, bf_ref, fqg_ref, fkg_ref, sqg_ref, skg_ref,
                f0_ref, fq_ref, fk_ref, aug_ref, fvt_ref, sq_ref, sk_ref, svt_ref, fend_ref, carry_ref, *, n_pad):
    j = pl.program_id(1)

    @pl.when(j == 0)
    def _():
        carry_ref[...] = f0_ref[...]

    x = x_ref[...]
    tm = x.shape[0]
    h = _rms(x, g1_ref[...]).astype(_BF)
    nt = (((1,), (1,)), ((), ()))
    f = jnp.dot(h, wqk_ref[...], preferred_element_type=_F32)
    s = jnp.dot(h, ws_ref[...], preferred_element_type=_F32)
    svt = lax.dot_general(wsvt_ref[...], h, nt, preferred_element_type=_F32)
    vt = lax.dot_general(wvt_ref[...], h, nt, preferred_element_type=_F32)
    flog = lax.dot_general(wfl_ref[...], h, nt, preferred_element_type=_F32)

    fq_ref[...] = _head_rms(f[:, :FOX_W], fqg_ref[...]).astype(_BF)
    fk_ref[...] = _head_rms(f[:, FOX_W:], fkg_ref[...]).astype(_BF)
    sq_ref[...] = _head_rms(s[:, :SWA_QW], sqg_ref[...]).astype(_BF)
    sk_ref[...] = _head_rms(s[:, SWA_QW:], skg_ref[...]).astype(_BF)

    ones_row = jnp.where(lax.broadcasted_iota(jnp.int32, (HEAD_DIM, tm), 0) == 0, 1.0, 0.0).astype(_BF)
    for hd in range(FOX_HEADS):
        fvt_ref[hd] = jnp.concatenate([vt[hd * HEAD_DIM:(hd + 1) * HEAD_DIM].astype(_BF), ones_row], axis=0)
    for g in range(SWA_KV_HEADS):
        blk = jnp.concatenate([svt[g * HEAD_DIM:(g + 1) * HEAD_DIM].astype(_BF), ones_row], axis=0)
        for c in range(tm // WINDOW):
            svt_ref[g, c] = blk[:, c * WINDOW:(c + 1) * WINDOW]

    z = flog + bf_ref[...]
    log_f = jnp.minimum(z, 0.0) - jnp.log1p(jnp.exp(-jnp.abs(z)))
    if n_pad:
        lane = lax.broadcasted_iota(jnp.int32, log_f.shape, 1)
        log_f = jnp.where(lane >= n_pad, log_f, 0.0)
    cum = _lane_cumsum(log_f) + carry_ref[...]
    carry_ref[...] = cum[:, -1:]
    fend_ref[...] = cum[:, -1:]
    nfl = cum * (-LOG2E)
    p1 = nfl.astype(_BF).astype(_F32)
    p2 = (nfl - p1).astype(_BF).astype(_F32)
    p3 = (nfl - p1 - p2).astype(_BF).astype(_F32)
    g = jnp.concatenate([p1, p2, p3, jnp.zeros((LANES - 3 * FOX_HEADS, tm), _F32)], axis=0)
    aug_ref[...] = g.T.astype(_BF)


def _qkv_call(x3, g1, wqk, wvt, wfl, ws, wsvt, bfg, fqg, fkg, sqg, skg, f0, *, tm, n_pad):
    B, S, D = x3.shape
    nj = S // tm
    row = lambda w: pl.BlockSpec((None, tm, w), lambda b, j: (b, j, 0))
    full = lambda a: pl.BlockSpec(a.shape, lambda b, j: (0,) * a.ndim)
    bsd = lambda w: jax.ShapeDtypeStruct((B, S, w), _BF)
    out_shape = [bsd(FOX_W), bsd(FOX_W), bsd(LANES), jax.ShapeDtypeStruct((B, FOX_HEADS, nj, 2 * HEAD_DIM, tm), _BF),
                 bsd(SWA_QW), bsd(SWA_KW),
                 jax.ShapeDtypeStruct((B, SWA_KV_HEADS, S // WINDOW, 2 * HEAD_DIM, WINDOW), _BF),
                 jax.ShapeDtypeStruct((B, FOX_HEADS, 1), _F32)]
    out_specs = [row(FOX_W), row(FOX_W), row(LANES),
                 pl.BlockSpec((None, FOX_HEADS, None, 2 * HEAD_DIM, tm), lambda b, j: (b, 0, j, 0, 0)),
                 row(SWA_QW), row(SWA_KW),
                 pl.BlockSpec((None, SWA_KV_HEADS, tm // WINDOW, 2 * HEAD_DIM, WINDOW), lambda b, j: (b, 0, j, 0, 0)),
                 pl.BlockSpec((None, FOX_HEADS, 1), lambda b, j: (b, 0, 0))]
    return pl.pallas_call(
        functools.partial(_qkv_kernel, n_pad=n_pad),
        grid=(B, nj),
        in_specs=[row(D)] + [full(a) for a in (g1, wqk, wvt, wfl, ws, wsvt, bfg, fqg, fkg, sqg, skg, f0)],
        out_specs=out_specs, out_shape=out_shape,
        scratch_shapes=[pltpu.VMEM((FOX_HEADS, 1), _F32)],
        compiler_params=_cparams(("arbitrary", "arbitrary")),
        name="qkv_proj",
    )(x3, g1, wqk, wvt, wfl, ws, wsvt, bfg, fqg, fkg, sqg, skg, f0)


def _fox_kernel(q_ref, k_ref, aug_ref, vt_ref, km_ref, augm_ref, vtm_ref, o_ref, m_ref, acc_ref, s_ref, mx_ref):
    g = pl.program_id(1)
    qi = pl.program_id(2)
    tq = q_ref.shape[0]
    tk = vt_ref.shape[-1]
    nt = (((1,), (1,)), ((), ()))
    lane = lax.broadcasted_iota(jnp.int32, (tq, LANES), 1)

    qaugs = []
    for pp in range(FOX_PAIRS):
        q = q_ref[:, pp * LANES:(pp + 1) * LANES]
        parts = []
        for a in range(2):
            head = 2 * (FOX_PAIRS * g + pp) + a
            keep = (lane < HEAD_DIM) if a == 0 else (lane >= HEAD_DIM)
            ones = jnp.where((lane < 3 * FOX_HEADS) & ((lane & (FOX_HEADS - 1)) == head), 1.0, 0.0).astype(_BF)
            parts.append(jnp.concatenate([jnp.where(keep, q, jnp.zeros_like(q)), ones], axis=1))
        qaugs.append(jnp.concatenate(parts, axis=0))

    def scores(pp, k, aug, mask):
        st = lax.dot_general(jnp.concatenate([k, aug], axis=1), qaugs[pp], nt, preferred_element_type=_F32)
        if mask is not None:
            st = jnp.where(mask, st, NEG_INF)
        return st, jnp.max(st, axis=0, keepdims=True)

    def tile_scores(pp, ki, mask=None):
        off = pl.multiple_of(ki * tk, tk)
        return scores(pp, k_ref[pl.ds(off, tk), pp * LANES:(pp + 1) * LANES], aug_ref[pl.ds(off, tk), :], mask)

    def process(pp, st, mx, vts, first):
        m_new = mx if first else jnp.maximum(m_ref[pp], mx)
        pt = jnp.exp2(st - m_new).astype(_BF)
        if not first:
            alpha = jnp.exp2(m_ref[pp] - m_new)
        for a in range(2):
            pv = jnp.dot(vts[a], pt[:, a * tq:(a + 1) * tq], preferred_element_type=_F32)
            acc_ref[2 * pp + a] = pv if first else alpha[:, a * tq:(a + 1) * tq] * acc_ref[2 * pp + a] + pv
        m_ref[pp] = m_new

    def stash(buf, pp, st_mx):
        s_ref[buf, pp] = st_mx[0]
        mx_ref[buf, pp] = st_mx[1]

    def vts(pp, ki):
        return vt_ref[2 * pp, ki], vt_ref[2 * pp + 1, ki]

    causal = lax.broadcasted_iota(jnp.int32, (tk, 2 * tq), 0) <= (lax.broadcasted_iota(jnp.int32, (tk, 2 * tq), 1) & (tq - 1))
    mask_m = lax.broadcasted_iota(jnp.int32, (META_BLOCK, 2 * tq), 0) >= META_PAD
    for pp in range(FOX_PAIRS):
        stash(0, pp, tile_scores(pp, qi, causal))
        st, mx = scores(pp, km_ref[:, pp * LANES:(pp + 1) * LANES], augm_ref[...], mask_m)
        process(pp, st, mx, (vtm_ref[2 * pp], vtm_ref[2 * pp + 1]), True)

    def step(j, buf):
        nxt = [tile_scores(pp, j) for pp in range(FOX_PAIRS)]
        cur = jnp.where(j == 0, qi, j - 1)
        for pp in range(FOX_PAIRS):
            process(pp, s_ref[buf, pp], mx_ref[buf, pp], vts(pp, cur), False)
        for pp in range(FOX_PAIRS):
            stash(1 - buf, pp, nxt[pp])

    def body(jj, c):
        step(2 * jj, 0)
        step(2 * jj + 1, 1)
        return c
    lax.fori_loop(0, qi // 2, body, 0)

    def finish(buf):
        last = jnp.where(qi == 0, qi, qi - 1)
        outs = []
        for pp in range(FOX_PAIRS):
            process(pp, s_ref[buf, pp], mx_ref[buf, pp], vts(pp, last), False)
            for a in range(2):
                acc = acc_ref[2 * pp + a]
                outs.append(acc[:HEAD_DIM] / acc[HEAD_DIM:HEAD_DIM + 1])
        o_ref[...] = jnp.concatenate(outs, axis=0).T.astype(_BF)

    @pl.when(qi % 2 == 1)
    def _():
        step(qi - 1, 0)
        finish(1)

    @pl.when(qi % 2 == 0)
    def _():
        finish(0)


def _fox_call(fq, fk, aug, fvt, km, augm, vtm, *, tq):
    B, S, _ = fq.shape
    nk = fvt.shape[2]
    assert fvt.shape[-1] == tq
    w = FOX_PAIRS * LANES
    qspec = pl.BlockSpec((None, tq, w), lambda b, g, i: (b, i, g))
    return pl.pallas_call(
        _fox_kernel, grid=(B, FOX_HEADS // (2 * FOX_PAIRS), S // tq),
        in_specs=[qspec,
                  pl.BlockSpec((None, S, w), lambda b, g, i: (b, 0, g)),
                  pl.BlockSpec((None, S, LANES), lambda b, g, i: (b, 0, 0)),
                  pl.BlockSpec((None, 2 * FOX_PAIRS, nk, 2 * HEAD_DIM, tq), lambda b, g, i: (b, g, 0, 0, 0)),
                  pl.BlockSpec((META_BLOCK, w), lambda b, g, i: (0, g)),
                  pl.BlockSpec((META_BLOCK, LANES), lambda b, g, i: (0, 0)),
                  pl.BlockSpec((2 * FOX_PAIRS, 2 * HEAD_DIM, META_BLOCK), lambda b, g, i: (g, 0, 0))],
        out_specs=qspec, out_shape=jax.ShapeDtypeStruct((B, S, FOX_W), _BF),
        scratch_shapes=[pltpu.VMEM((FOX_PAIRS, 1, 2 * tq), _F32), pltpu.VMEM((2 * FOX_PAIRS, 2 * HEAD_DIM, tq), _F32),
                        pltpu.VMEM((2, FOX_PAIRS, tq, 2 * tq), _F32), pltpu.VMEM((2, FOX_PAIRS, 1, 2 * tq), _F32)],
        compiler_params=_cparams(("arbitrary", "arbitrary", "arbitrary")), name="fox_attn",
    )(fq, fk, aug, fvt, km, augm, vtm)


def _swa_bias(tq):
    j = np.arange(WINDOW + tq)[:, None] - WINDOW
    i = np.arange(tq)[None, :]
    dist = i - j
    ok = (dist >= 0) & (dist < WINDOW)
    slopes = np.exp2(-8.0 * (np.arange(SWA_HEADS, dtype=np.float32) + 1.0) / SWA_HEADS).astype(np.float32)
    per_head = np.where(ok[None], -slopes[:, None, None] * dist[None].astype(np.float32), np.float32(NEG_INF))
    rep = SWA_HEADS // SWA_KV_HEADS
    return np.stack([np.concatenate(list(per_head[g * rep:(g + 1) * rep]), axis=1) for g in range(SWA_KV_HEADS)])


def _swa_kernel(q_ref, kc_ref, kp_ref, km_ref, vc_ref, vp_ref, vm_ref, bias_ref, sink_ref, o_ref):
    qi = pl.program_id(1)
    tq = q_ref.shape[0]
    rep = SWA_HEADS // SWA_KV_HEADS
    nt = (((1,), (1,)), ((), ()))
    first = qi == 0
    kcat = jnp.concatenate([jnp.where(first, km_ref[...], kp_ref[...]), kc_ref[...]], axis=0)
    row = lax.broadcasted_iota(jnp.int32, (WINDOW + tq, rep * tq), 0)
    row_ok = row >= jnp.where(first, META_PAD, 0)
    lane = lax.broadcasted_iota(jnp.int32, (tq, LANES), 1)
    outs = [None] * SWA_HEADS
    for g in range(SWA_KV_HEADS):
        keep = (lane < HEAD_DIM) if g == 0 else (lane >= HEAD_DIM)
        qg = jnp.concatenate([jnp.where(keep, q_ref[:, p * LANES:(p + 1) * LANES], jnp.zeros((tq, LANES), _BF))
                              for p in range(rep)], axis=0)
        vcat = jnp.concatenate([jnp.where(first, vm_ref[g], vp_ref[g, 0])] + [vc_ref[g, c] for c in range(tq // WINDOW)],
                               axis=1)
        st = lax.dot_general(kcat, qg, nt, preferred_element_type=_F32) + bias_ref[g]
        st = jnp.where(row_ok, st, NEG_INF)
        sink = sink_ref[g]
        m = jnp.maximum(jnp.max(st, axis=0, keepdims=True), sink)
        pt = jnp.exp(st - m).astype(_BF)
        acc = jnp.dot(vcat, pt, preferred_element_type=_F32)
        o = acc[:HEAD_DIM] / (acc[HEAD_DIM:HEAD_DIM + 1] + jnp.exp(sink - m))
        for p in range(rep):
            outs[2 * p + g] = o[:, p * tq:(p + 1) * tq]
    o_ref[...] = jnp.concatenate(outs, axis=0).T.astype(_BF)


def _swa_call(sinks, sq, sk, svt, km, vtm, *, tq):
    B, S, _ = sq.shape
    r = tq // WINDOW
    rep = SWA_HEADS // SWA_KV_HEADS
    bias = jnp.asarray(_swa_bias(tq))
    sink_rows = jnp.repeat(sinks.reshape(SWA_KV_HEADS, rep), tq, axis=1)[:, None, :]
    prev_blk = lambda i: jnp.maximum(i * r - 1, 0)
    return pl.pallas_call(
        _swa_kernel, grid=(B, S // tq),
        in_specs=[pl.BlockSpec((None, tq, SWA_QW), lambda b, i: (b, i, 0)),
                  pl.BlockSpec((None, tq, LANES), lambda b, i: (b, i, 0)),
                  pl.BlockSpec((None, WINDOW, LANES), lambda b, i: (b, prev_blk(i), 0)),
                  pl.BlockSpec((META_BLOCK, LANES), lambda b, i: (0, 0)),
                  pl.BlockSpec((None, SWA_KV_HEADS, r, 2 * HEAD_DIM, WINDOW), lambda b, i: (b, 0, i, 0, 0)),
                  pl.BlockSpec((None, SWA_KV_HEADS, 1, 2 * HEAD_DIM, WINDOW), lambda b, i: (b, 0, prev_blk(i), 0, 0)),
                  pl.BlockSpec((SWA_KV_HEADS, 2 * HEAD_DIM, META_BLOCK), lambda b, i: (0, 0, 0)),
                  pl.BlockSpec(bias.shape, lambda b, i: (0, 0, 0)),
                  pl.BlockSpec(sink_rows.shape, lambda b, i: (0, 0, 0))],
        out_specs=pl.BlockSpec((None, tq, SWA_QW), lambda b, i: (b, i, 0)),
        out_shape=jax.ShapeDtypeStruct((B, S, SWA_QW), _BF),
        compiler_params=_cparams(("arbitrary", "arbitrary")), name="swa_attn",
    )(sq, sk, sk, km, svt, svt, vtm, bias, sink_rows)


def _split3(v):
    a = v.astype(_BF)
    r = v - a.astype(_F32)
    b = r.astype(_BF)
    c = (r - b.astype(_F32)).astype(_BF)
    return a, b, c


def _mix_kernel(x_ref, oa_ref, ob_ref, g1_ref, wga_ref, wgb_ref, wfo_ref, wso_ref, wo_ref, g2_ref, wr_ref, br_ref,
                hres_ref, h2_ref, lg_ref):
    x = x_ref[...]
    h = _rms(x, g1_ref[...]).astype(_BF)
    ga = jax.nn.sigmoid(jnp.dot(h, wga_ref[...], preferred_element_type=_F32))
    gb = jax.nn.sigmoid(jnp.dot(h, wgb_ref[...], preferred_element_type=_F32))
    mix = ga * jnp.dot(oa_ref[...], wfo_ref[...], preferred_element_type=_F32) \
        + gb * jnp.dot(ob_ref[...], wso_ref[...], preferred_element_type=_F32)
    hres = x + jnp.dot(mix.astype(_BF), wo_ref[...], preferred_element_type=_F32)
    hres_ref[...] = hres
    h2 = _rms(hres, g2_ref[...])
    h2_ref[...] = h2.astype(_BF)
    wr = wr_ref[...]
    w_hi = wr.astype(_BF)
    w_lo = (wr - w_hi.astype(_F32)).astype(_BF)
    h_hi = h2.astype(_BF)
    h_lo = (h2 - h_hi.astype(_F32)).astype(_BF)
    lg = jnp.dot(h_hi, w_hi, preferred_element_type=_F32) + jnp.dot(h_hi, w_lo, preferred_element_type=_F32) \
        + jnp.dot(h_lo, w_hi, preferred_element_type=_F32)
    lg_ref[...] = lg + br_ref[...]


def _mix_call(x2, oa, ob, g1, wga, wgb, wfo, wso, wo, g2, wr, br, *, tm):
    N, D = x2.shape
    row = lambda w: pl.BlockSpec((tm, w), lambda i: (i, 0))
    full = lambda a: pl.BlockSpec(a.shape, lambda i: (0,) * a.ndim)
    return pl.pallas_call(
        _mix_kernel, grid=(N // tm,),
        in_specs=[row(D), row(FOX_W), row(SWA_QW)] + [full(a) for a in (g1, wga, wgb, wfo, wso, wo, g2, wr, br)],
        out_specs=[row(D), row(D), row(LANES)],
        out_shape=[jax.ShapeDtypeStruct((N, D), _F32), jax.ShapeDtypeStruct((N, D), _BF),
                   jax.ShapeDtypeStruct((N, LANES), _F32)],
        compiler_params=_cparams(("arbitrary",)), name="mix_proj",
    )(x2, oa, ob, g1, wga, wgb, wfo, wso, wo, g2, wr, br)


def _route_kernel(lg_ref, a_ref, at_ref, cnt_ref):
    T = lg_ref.shape[0]
    lt = lg_ref[...].T[:N_EXPERTS]
    e_iota = lax.broadcasted_iota(jnp.int32, lt.shape, 0)
    cur = lt
    vals, idxs = [], []
    for _ in range(TOP_K):
        m = jnp.max(cur, axis=0, keepdims=True)
        idx = jnp.min(jnp.where(cur == m, e_iota, N_EXPERTS), axis=0, keepdims=True)
        vals.append(m)
        idxs.append(idx)
        cur = jnp.where(e_iota == idx, -jnp.inf, cur)
    es = [jnp.exp(v - vals[0]) for v in vals]
    den = es[0] + es[1] + es[2] + es[3]
    gates = [e / den for e in es]

    onehot = jnp.zeros(lt.shape, _F32)
    for idx in idxs:
        onehot = onehot + jnp.where(e_iota == idx, 1.0, 0.0)
    r_i = lax.broadcasted_iota(jnp.int32, (T, T), 0)
    c_i = lax.broadcasted_iota(jnp.int32, (T, T), 1)
    utri = jnp.where(r_i < c_i, 1.0, 0.0).astype(_BF)
    prefix = jnp.dot(onehot.astype(_BF), utri, preferred_element_type=_F32)
    cnt = jnp.sum(onehot, axis=1, keepdims=True)
    chunks = jnp.floor((cnt + (RUN_ALIGN - 1)) * (1.0 / RUN_ALIGN))
    l_r = lax.broadcasted_iota(jnp.int32, (N_EXPERTS, N_EXPERTS), 0)
    l_c = lax.broadcasted_iota(jnp.int32, (N_EXPERTS, N_EXPERTS), 1)
    ltri = jnp.where(l_c < l_r, 1.0, 0.0).astype(_BF)
    lbase = RUN_ALIGN * jnp.dot(ltri, jnp.broadcast_to(chunks, (N_EXPERTS, LANES)).astype(_BF),
                                preferred_element_type=_F32)[:, 0:1]
    slot_all = prefix + lbase
    rows = [jnp.sum(jnp.where(e_iota == idx, slot_all, 0.0), axis=0, keepdims=True) for idx in idxs] + gates
    at = jnp.concatenate(rows, axis=0)
    at_ref[...] = at
    a_ref[...] = jnp.concatenate([at, jnp.zeros((LANES - 2 * TOP_K, T), _F32)], axis=0).T
    cnt_ref[...] = jnp.broadcast_to(cnt, (N_EXPERTS, LANES)).astype(jnp.int32)


def _route_call(lg, *, T):
    N = lg.shape[0]
    nt = N // T
    return pl.pallas_call(
        _route_kernel, grid=(nt,),
        in_specs=[pl.BlockSpec((T, LANES), lambda i: (i, 0))],
        out_specs=[pl.BlockSpec((T, LANES), lambda i: (i, 0)), pl.BlockSpec((8, T), lambda i: (0, i)),
                   pl.BlockSpec((None, N_EXPERTS, LANES), lambda i: (i, 0, 0))],
        out_shape=[jax.ShapeDtypeStruct((N, LANES), _F32), jax.ShapeDtypeStruct((8, N), _F32),
                   jax.ShapeDtypeStruct((nt, N_EXPERTS, LANES), jnp.int32)],
        compiler_params=_cparams(("arbitrary",)), name="route",
    )(lg)


def _div_pow2(v, d):
    assert d & (d - 1) == 0
    return lax.shift_right_logical(v, d.bit_length() - 1)


def _plan_kernel(cnt_ref, goff_ref, lbase_ref, nch_ref, eblk_ref, gap_ref, info_ref, *, nt, block_rows):
    def per_expert(e, base):
        def per_tile(i, run):
            n = _div_pow2(cnt_ref[i * N_EXPERTS + e] + (RUN_ALIGN - 1), RUN_ALIGN)
            goff_ref[i * N_EXPERTS + e] = run
            nch_ref[i * N_EXPERTS + e] = n
            return run + n * RUN_ALIGN
        end = lax.fori_loop(0, nt, per_tile, base)
        nblk = _div_pow2(end - base + (block_rows - 1), block_rows)
        nxt = base + nblk * block_rows
        gap_ref[2 * e] = end
        gap_ref[2 * e + 1] = _div_pow2(nxt - end, RUN_ALIGN)
        eblk_ref[2 * e] = _div_pow2(base, block_rows)
        eblk_ref[2 * e + 1] = nblk
        return nxt
    total = lax.fori_loop(0, N_EXPERTS, per_expert, 0)
    info_ref[0] = _div_pow2(total, block_rows)

    def per_tile2(i, c):
        def per_e(e, run):
            lbase_ref[i * N_EXPERTS + e] = run
            return run + nch_ref[i * N_EXPERTS + e] * RUN_ALIGN
        lax.fori_loop(0, N_EXPERTS, per_e, 0)
        return c
    lax.fori_loop(0, nt, per_tile2, 0)


def _plan_call(cnt_flat, *, nt, block_rows):
    smem = pl.BlockSpec(memory_space=pltpu.SMEM)
    i32 = lambda n: jax.ShapeDtypeStruct((n,), jnp.int32)
    return pl.pallas_call(
        functools.partial(_plan_kernel, nt=nt, block_rows=block_rows),
        in_specs=[smem], out_specs=[smem] * 6,
        out_shape=[i32(nt * N_EXPERTS), i32(nt * N_EXPERTS), i32(nt * N_EXPERTS), i32(2 * N_EXPERTS), i32(2 * N_EXPERTS),
                   i32(1)],
        name="plan",
    )(cnt_flat)


def _pack_pairs(v):
    w = v.shape[1] // 2
    return pltpu.pack_elementwise([v[:, :w], v[:, w:]], packed_dtype=_BF)


def _unpack_pairs(wd):
    lo = pltpu.unpack_elementwise(wd, index=0, packed_dtype=_BF, unpacked_dtype=_F32)
    hi = pltpu.unpack_elementwise(wd, index=1, packed_dtype=_BF, unpacked_dtype=_F32)
    return lo.astype(_BF), hi.astype(_BF)


def _dispatch_kernel(goff, lbase, nch, gap, info, at_ref, h2_ref, xs_ref, xl_ref, z_ref, sem, zsem, tsem):
    i = pl.program_id(0)
    R = xl_ref.shape[1]
    T = h2_ref.shape[0]
    block_rows = z_ref.shape[0]
    nb = xs_ref.shape[0] // block_rows

    @pl.when(i == 0)
    def _():
        z_ref[...] = jnp.zeros(z_ref.shape, z_ref.dtype)

        def gap_copy(e):
            rows = pl.multiple_of(gap[2 * e + 1] * RUN_ALIGN, RUN_ALIGN)
            return pltpu.make_async_copy(z_ref.at[pl.ds(0, rows)],
                                         xs_ref.at[pl.ds(pl.multiple_of(gap[2 * e], RUN_ALIGN), rows)], zsem)

        def fill(e, c):
            @pl.when(gap[2 * e + 1] > 0)
            def _():
                gap_copy(e).start()
            return c
        lax.fori_loop(0, N_EXPERTS, fill, 0)

        def fill_wait(e, c):
            @pl.when(gap[2 * e + 1] > 0)
            def _():
                gap_copy(e).wait()
            return c
        lax.fori_loop(0, N_EXPERTS, fill_wait, 0)

        def tail(b, c):
            pltpu.make_async_copy(z_ref, xs_ref.at[pl.ds(pl.multiple_of(b * block_rows, block_rows), block_rows)],
                                  tsem).start()
            return c
        lax.fori_loop(info[0], nb, tail, 0)

        def tail_wait(b, c):
            pltpu.make_async_copy(z_ref, xs_ref.at[pl.ds(0, block_rows)], tsem).wait()
            return c
        lax.fori_loop(info[0], nb, tail_wait, 0)

    slot_iota = lax.broadcasted_iota(jnp.int32, (R, T), 0).astype(_F32)
    pm = jnp.zeros((R, T), _F32)
    for k in range(TOP_K):
        pm = pm + jnp.where(slot_iota == at_ref[k:k + 1, :], 1.0, 0.0)
    xl = jnp.dot(pm.astype(_BF), h2_ref[...], preferred_element_type=_F32)
    slot = i & 1

    def wait_tile(t, s):
        total = lax.fori_loop(0, N_EXPERTS, lambda e, c: c + nch[t * N_EXPERTS + e], 0)

        @pl.when(total > 0)
        def _():
            rows = pl.multiple_of(total * RUN_ALIGN, RUN_ALIGN)
            pltpu.make_async_copy(xl_ref.at[s, pl.ds(0, rows)], xs_ref.at[pl.ds(0, rows)], sem.at[s]).wait()

    @pl.when(i >= 2)
    def _():
        wait_tile(i - 2, slot)
    xl_ref[slot] = _pack_pairs(xl)

    def issue(e, c):
        n = nch[i * N_EXPERTS + e]

        @pl.when(n > 0)
        def _():
            rows = pl.multiple_of(n * RUN_ALIGN, RUN_ALIGN)
            pltpu.make_async_copy(xl_ref.at[slot, pl.ds(pl.multiple_of(lbase[i * N_EXPERTS + e], RUN_ALIGN), rows)],
                                  xs_ref.at[pl.ds(pl.multiple_of(goff[i * N_EXPERTS + e], RUN_ALIGN), rows)],
                                  sem.at[slot]).start()
        return c
    lax.fori_loop(0, N_EXPERTS, issue, 0)

    @pl.when(i == pl.num_programs(0) - 1)
    def _():
        @pl.when(i >= 1)
        def _():
            wait_tile(i - 1, 1 - slot)
        wait_tile(i, slot)


def _dispatch_call(goff, lbase, nch, gap, info, at, h2, *, T, R, rows, block_rows):
    N, D = h2.shape
    grid_spec = pltpu.PrefetchScalarGridSpec(
        num_scalar_prefetch=5, grid=(N // T,),
        in_specs=[pl.BlockSpec((8, T), lambda i, *_: (0, i)), pl.BlockSpec((T, D), lambda i, *_: (i, 0))],
        out_specs=pl.BlockSpec(memory_space=pl.ANY),
        scratch_shapes=[pltpu.VMEM((2, R, D // 2), jnp.uint32), pltpu.VMEM((block_rows, D // 2), jnp.uint32),
                        pltpu.SemaphoreType.DMA((2,)), pltpu.SemaphoreType.DMA, pltpu.SemaphoreType.DMA])
    return pl.pallas_call(
        _dispatch_kernel, grid_spec=grid_spec, out_shape=jax.ShapeDtypeStruct((rows, D // 2), jnp.uint32),
        compiler_params=_cparams(("arbitrary",)), name="dispatch",
    )(goff, lbase, nch, gap, info, at, h2)


def _expert_kernel(eblk, info, xs_ref, wub_ref, bu_ref, wdb_ref, bd_ref, ys_ref, xb_ref, yb_ref, xsem, ysem):
    e = pl.program_id(0)
    block_rows = xb_ref.shape[1]
    nb = ys_ref.shape[0] // block_rows
    b0 = eblk[2 * e]
    n = eblk[2 * e + 1]
    half = wub_ref.shape[0] // 2

    def rows_of(blk):
        return pl.ds(pl.multiple_of(blk * block_rows, block_rows), block_rows)

    def x_copy(j, slot):
        return pltpu.make_async_copy(xs_ref.at[rows_of(b0 + j)], xb_ref.at[slot], xsem.at[slot])

    def y_copy(j, slot):
        return pltpu.make_async_copy(yb_ref.at[slot], ys_ref.at[rows_of(b0 + j)], ysem.at[slot])

    @pl.when((e == 0) & (n > 0))
    def _():
        x_copy(0, 0).start(priority=1)

    def body(j, c):
        slot = j & 1
        x_copy(j, slot).wait()

        @pl.when(j + 1 < n)
        def _():
            x_copy(j + 1, 1 - slot).start(priority=1)

        @pl.when(j >= 2)
        def _():
            y_copy(j - 2, slot).wait()

        x_lo, x_hi = _unpack_pairs(xb_ref[slot])
        gu = jnp.dot(x_lo, wub_ref[:half, :], preferred_element_type=_F32) \
            + jnp.dot(x_hi, wub_ref[half:, :], preferred_element_type=_F32) + bu_ref[...]
        g = jnp.minimum(gu[:, :D_FF], SWIGLU_LIMIT)
        u = jnp.clip(gu[:, D_FF:], -SWIGLU_LIMIT, SWIGLU_LIMIT)
        act = (u + 1.0) * (g * jax.nn.sigmoid(SWIGLU_ALPHA * g))
        y = jnp.dot(act.astype(_BF), wdb_ref[...], preferred_element_type=_F32) + bd_ref[...]
        yb_ref[slot] = _pack_pairs(y)
        y_copy(j, slot).start()
        return c
    lax.fori_loop(0, n, body, 0)

    @pl.when(n >= 2)
    def _():
        y_copy(n - 2, n & 1).wait()

    @pl.when(n >= 1)
    def _():
        y_copy(n - 1, (n - 1) & 1).wait()

    last = pl.num_programs(0) - 1
    e_next = jnp.minimum(e + 1, last)

    @pl.when((e < last) & (eblk[2 * e_next + 1] > 0))
    def _():
        pltpu.make_async_copy(xs_ref.at[rows_of(eblk[2 * e_next])], xb_ref.at[0], xsem.at[0]).start(priority=1)

    @pl.when(e == last)
    def _():
        yb_ref[0] = _pack_pairs(jnp.zeros((block_rows, 2 * yb_ref.shape[2]), _F32))

        def tail_copy(b):
            return pltpu.make_async_copy(yb_ref.at[0], ys_ref.at[pl.ds(pl.multiple_of(b * block_rows, block_rows), block_rows)],
                                         ysem.at[0])

        def tail(b, c):
            tail_copy(b).start()
            return c
        lax.fori_loop(info[0], nb, tail, 0)

        def tail_wait(b, c):
            tail_copy(b).wait()
            return c
        lax.fori_loop(info[0], nb, tail_wait, 0)


def _expert_call(eblk, info, xs, wu, bu, wd, bd, *, block_rows):
    rows, half = xs.shape
    D = 2 * half
    ex = lambda e, *_: (e, 0, 0)
    grid_spec = pltpu.PrefetchScalarGridSpec(
        num_scalar_prefetch=2, grid=(N_EXPERTS,),
        in_specs=[pl.BlockSpec(memory_space=pl.ANY),
                  pl.BlockSpec((None, D, 2 * D_FF), ex), pl.BlockSpec((None, 1, 2 * D_FF), ex),
                  pl.BlockSpec((None, D_FF, D), ex), pl.BlockSpec((None, 1, D), ex)],
        out_specs=pl.BlockSpec(memory_space=pl.ANY),
        scratch_shapes=[pltpu.VMEM((2, block_rows, half), jnp.uint32), pltpu.VMEM((2, block_rows, half), jnp.uint32),
                        pltpu.SemaphoreType.DMA((2,)), pltpu.SemaphoreType.DMA((2,))])
    return pl.pallas_call(
        _expert_kernel, grid_spec=grid_spec, out_shape=jax.ShapeDtypeStruct((rows, half), jnp.uint32),
        compiler_params=_cparams(("arbitrary",)), name="experts",
    )(eblk, info, xs, wu, bu, wd, bd)


def _combine_kernel(goff, lbase, nch, a_ref, hres_ref, ys_ref, o_ref, yl_ref, sem):
    i = pl.program_id(0)
    nt = pl.num_programs(0)
    R = yl_ref.shape[1]
    T = a_ref.shape[0]

    def fetch(t, slot):
        def issue(e, c):
            n = nch[t * N_EXPERTS + e]

            @pl.when(n > 0)
            def _():
                rows = pl.multiple_of(n * RUN_ALIGN, RUN_ALIGN)
                pltpu.make_async_copy(
                    ys_ref.at[pl.ds(pl.multiple_of(goff[t * N_EXPERTS + e], RUN_ALIGN), rows)],
                    yl_ref.at[slot, pl.ds(pl.multiple_of(lbase[t * N_EXPERTS + e], RUN_ALIGN), rows)], sem.at[slot]).start()
            return c
        lax.fori_loop(0, N_EXPERTS, issue, 0)

    @pl.when(i == 0)
    def _():
        yl_ref[...] = jnp.zeros(yl_ref.shape, yl_ref.dtype)
        fetch(0, 0)

    slot = i & 1

    @pl.when(i + 1 < nt)
    def _():
        fetch(i + 1, 1 - slot)

    total = lax.fori_loop(0, N_EXPERTS, lambda e, c: c + nch[i * N_EXPERTS + e], 0)

    @pl.when(total > 0)
    def _():
        rows = pl.multiple_of(total * RUN_ALIGN, RUN_ALIGN)
        pltpu.make_async_copy(ys_ref.at[pl.ds(0, rows)], yl_ref.at[slot, pl.ds(0, rows)], sem.at[slot]).wait()

    a = a_ref[...]
    slot_iota = lax.broadcasted_iota(jnp.int32, (T, R), 1).astype(_F32)
    pt = jnp.zeros((T, R), _F32)
    for k in range(TOP_K):
        pt = pt + jnp.where(slot_iota == a[:, k:k + 1], a[:, TOP_K + k:TOP_K + k + 1], 0.0)
    pt = pt.astype(_BF)
    y_lo, y_hi = _unpack_pairs(yl_ref[slot])
    half = y_lo.shape[1]
    o_ref[:, :half] = hres_ref[:, :half] + jnp.dot(pt, y_lo, preferred_element_type=_F32)
    o_ref[:, half:] = hres_ref[:, half:] + jnp.dot(pt, y_hi, preferred_element_type=_F32)


def _combine_call(goff, lbase, nch, a, hres, ys, *, T, R):
    N, D = hres.shape
    grid_spec = pltpu.PrefetchScalarGridSpec(
        num_scalar_prefetch=3, grid=(N // T,),
        in_specs=[pl.BlockSpec((T, LANES), lambda i, *_: (i, 0)), pl.BlockSpec((T, D), lambda i, *_: (i, 0)),
                  pl.BlockSpec(memory_space=pl.ANY)],
        out_specs=pl.BlockSpec((T, D), lambda i, *_: (i, 0)),
        scratch_shapes=[pltpu.VMEM((2, R, D // 2), jnp.uint32), pltpu.SemaphoreType.DMA((2,))])
    return pl.pallas_call(
        _combine_kernel, grid_spec=grid_spec, out_shape=jax.ShapeDtypeStruct((N, D), _F32),
        compiler_params=_cparams(("arbitrary",)), name="combine",
    )(goff, lbase, nch, a, hres, ys)


def _tiles(S):
    pick = lambda pref: next(t for t in pref if S % t == 0)
    tm = pick((512, 256, 128))
    return dict(tm=tm, tq_fox=tm, tq_swa=pick((256, 128)),
                tm_mix=pick((512, 256, 128)), t_route=pick((256, 128)), block_rows=256)


def kernel(x, meta_tokens, norm1_gain, w_in, b_forget, fox_q_gain, fox_k_gain, swa_q_gain, swa_k_gain, swa_sinks,
           w_fox_out, w_swa_out, w_out, norm2_gain, w_router, b_router, w_up, b_up, w_down, b_down):
    B, S, D = x.shape
    assert norm1_gain.shape[0] == 1 and S % LANES == 0
    tl = _tiles(S)
    N = B * S
    scale = HEAD_DIM ** -0.5

    w = w_in[0]
    c0 = 3 * FOX_W
    c1 = c0 + FOX_HEADS
    c2 = c1 + SWA_QW + 2 * SWA_KW
    wqk = w[:, :2 * FOX_W].astype(_BF)
    wvt = w[:, 2 * FOX_W:c0].T.astype(_BF)
    wfl = w[:, c0:c1].T.astype(_BF)
    perm_heads = np.array([h for p in range(SWA_HEADS // 2) for h in (p, p + SWA_HEADS // 2)])
    perm_cols = (perm_heads[:, None] * HEAD_DIM + np.arange(HEAD_DIM)[None, :]).reshape(-1)
    ws_all = w[:, c1:c2]
    ws = jnp.concatenate([ws_all[:, :SWA_QW][:, perm_cols], ws_all[:, SWA_QW:SWA_QW + SWA_KW]], axis=1).astype(_BF)
    wsvt = ws_all[:, SWA_QW + SWA_KW:].T.astype(_BF)
    wga = w[:, c2:c2 + D].astype(_BF)
    wgb = w[:, c2 + D:].astype(_BF)
    g1 = norm1_gain[0][None, :]
    g2 = norm2_gain[0][None, :]
    bfg = b_forget[0][:, None]
    fqg = jnp.tile(fox_q_gain[0], FOX_HEADS)[None, :] * (scale * LOG2E)
    fkg = jnp.tile(fox_k_gain[0], FOX_HEADS)[None, :]
    sqg = jnp.tile(swa_q_gain[0], SWA_HEADS)[None, :] * scale
    skg = jnp.tile(swa_k_gain[0], SWA_KV_HEADS)[None, :]
    wfo = w_fox_out[0].astype(_BF)
    wso = w_swa_out[0][perm_cols, :].astype(_BF)
    wo = w_out[0].astype(_BF)
    wr = jnp.pad(w_router[0], ((0, 0), (0, LANES - N_EXPERTS)))
    br = jnp.pad(b_router[0], (0, LANES - N_EXPERTS), constant_values=NEG_INF)[None, :]
    sinks = swa_sinks[0].astype(_F32)
    wu = w_up[0].astype(_BF)
    wd = w_down[0].astype(_BF)
    bu = b_up[0][:, None, :]
    bd = b_down[0][:, None, :]

    meta_blk = jnp.pad(meta_tokens.astype(x.dtype), ((META_PAD, 0), (0, 0)))[None]
    zero_f = jnp.zeros((FOX_HEADS, 1), _F32)
    _, fk_m, aug_m, fvt_m, _, sk_m, svt_m, fend_m = _qkv_call(meta_blk, g1, wqk, wvt, wfl, ws, wsvt, bfg, fqg, fkg, sqg,
                                                              skg, zero_f, tm=META_BLOCK, n_pad=META_PAD)
    fq, fk, aug, fvt, sq, sk, svt, _ = _qkv_call(x, g1, wqk, wvt, wfl, ws, wsvt, bfg, fqg, fkg, sqg, skg, fend_m[0],
                                                 tm=tl["tm"], n_pad=0)
    o_a = _fox_call(fq, fk, aug, fvt, fk_m[0], aug_m[0], fvt_m[0, :, 0], tq=tl["tq_fox"])
    o_b = _swa_call(sinks, sq, sk, svt, sk_m[0], svt_m[0, :, 0], tq=tl["tq_swa"])

    hres, h2, lg = _mix_call(x.reshape(N, D), o_a.reshape(N, FOX_W), o_b.reshape(N, SWA_QW),
                             g1, wga, wgb, wfo, wso, wo, g2, wr, br, tm=tl["tm_mix"])

    T = tl["t_route"]
    nt = N // T
    block_rows = tl["block_rows"]
    R = -(-(TOP_K * T + N_EXPERTS * (RUN_ALIGN - 1)) // LANES) * LANES
    max_rows = N * TOP_K + nt * N_EXPERTS * (RUN_ALIGN - 1) + N_EXPERTS * (block_rows - RUN_ALIGN)
    nb = -(-max_rows // block_rows)
    a, at, cnt = _route_call(lg, T=T)
    goff, lbase, nch, eblk, gap, info = _plan_call(cnt[:, :, 0].reshape(-1), nt=nt, block_rows=block_rows)
    xs = _dispatch_call(goff, lbase, nch, gap, info, at, h2, T=T, R=R, rows=nb * block_rows, block_rows=block_rows)
    ys = _expert_call(eblk, info, xs, wu, bu, wd, bd, block_rows=block_rows)
    out = _combine_call(goff, lbase, nch, a, hres, ys, T=T, R=R)
    return out.reshape(B, S, D)
```

```python
import functools

import jax
import jax.numpy as jnp
import numpy as np
from jax import lax
from jax.experimental import pallas as pl
from jax.experimental.pallas import tpu as pltpu

HEAD_DIM = 64
N_META = 16
FOX_HEADS = 8
SWA_HEADS = 8
SWA_KV_HEADS = 2
WINDOW = 128
N_EXPERTS = 32
TOP_K = 4
D_FF = 1024
SWIGLU_LIMIT = 7.0
SWIGLU_ALPHA = 1.702
RMS_EPS = 1e-6
NEG_INF = -1e30
LOG2E = 1.4426950408889634

LANES = 128
META_BLOCK = 128
META_PAD = META_BLOCK - N_META
FOX_PAIRS = 2
RUN_ALIGN = 8
VMEM_LIMIT = 56 * 1024 * 1024

FOX_W = FOX_HEADS * HEAD_DIM
SWA_QW = SWA_HEADS * HEAD_DIM
SWA_KW = SWA_KV_HEADS * HEAD_DIM

_BF = jnp.bfloat16
_F32 = jnp.float32


def _cparams(sem):
    return pltpu.CompilerParams(dimension_semantics=sem, vmem_limit_bytes=VMEM_LIMIT)


def _rms(t, gain):
    return t * lax.rsqrt(jnp.mean(t * t, axis=-1, keepdims=True) + RMS_EPS) * gain


def _head_rms(z, gain_row):
    lane = lax.broadcasted_iota(jnp.int32, (z.shape[0], LANES), 1)
    lo_mask = lane < HEAD_DIM
    outs = []
    for b in range(z.shape[1] // LANES):
        v = z[:, b * LANES:(b + 1) * LANES]
        v2 = v * v
        tot = jnp.sum(v2, axis=-1, keepdims=True)
        lo = jnp.sum(jnp.where(lo_mask, v2, 0.0), axis=-1, keepdims=True)
        hi = tot - lo
        r_lo = lax.rsqrt(lo * (1.0 / HEAD_DIM) + RMS_EPS)
        r_hi = lax.rsqrt(hi * (1.0 / HEAD_DIM) + RMS_EPS)
        outs.append(v * jnp.where(lo_mask, r_lo, r_hi))
    return jnp.concatenate(outs, axis=-1) * gain_row


def _lane_cumsum(x):
    n = x.shape[-1]
    lane = lax.broadcasted_iota(jnp.int32, x.shape, x.ndim - 1)
    s = 1
    while s < n:
        x = x + jnp.where(lane >= s, pltpu.roll(x, s, x.ndim - 1), 0.0)
        s *= 2
    return x


def _qkv_kernel(x_ref, g1_ref, wqk_ref, wvt_ref, wfl_ref, ws_ref, wsvt_ref, bf_ref, fqg_ref, fkg_ref, sqg_ref, skg_ref,
                f0_ref, fq_ref, fk_ref, aug_ref, fvt_ref, sq_ref, sk_ref, svt_ref, fend_ref, carry_ref, *, n_pad):
    j = pl.program_id(1)

    @pl.when(j == 0)
    def _():
        carry_ref[...] = f0_ref[...]

    x = x_ref[...]
    tm = x.shape[0]
    h = _rms(x, g1_ref[...]).astype(_BF)
    nt = (((1,), (1,)), ((), ()))
    f = jnp.dot(h, wqk_ref[...], preferred_element_type=_F32)
    s = jnp.dot(h, ws_ref[...], preferred_element_type=_F32)
    svt = lax.dot_general(wsvt_ref[...], h, nt, preferred_element_type=_F32)
    vt = lax.dot_general(wvt_ref[...], h, nt, preferred_element_type=_F32)
    flog = lax.dot_general(wfl_ref[...], h, nt, preferred_element_type=_F32)

    fq_ref[...] = _head_rms(f[:, :FOX_W], fqg_ref[...]).astype(_BF)
    fk_ref[...] = _head_rms(f[:, FOX_W:], fkg_ref[...]).astype(_BF)
    sq_ref[...] = _head_rms(s[:, :SWA_QW], sqg_ref[...]).astype(_BF)
    sk_ref[...] = _head_rms(s[:, SWA_QW:], skg_ref[...]).astype(_BF)

    ones_row = jnp.where(lax.broadcasted_iota(jnp.int32, (HEAD_DIM, tm), 0) == 0, 1.0, 0.0).astype(_BF)
    for hd in range(FOX_HEADS):
        fvt_ref[hd] = jnp.concatenate([vt[hd * HEAD_DIM:(hd + 1) * HEAD_DIM].astype(_BF), ones_row], axis=0)
    for g in range(SWA_KV_HEADS):
        blk = jnp.concatenate([svt[g * HEAD_DIM:(g + 1) * HEAD_DIM].astype(_BF), ones_row], axis=0)
        for c in range(tm // WINDOW):
            svt_ref[g, c] = blk[:, c * WINDOW:(c + 1) * WINDOW]

    z = flog + bf_ref[...]
    log_f = jnp.minimum(z, 0.0) - jnp.log1p(jnp.exp(-jnp.abs(z)))
    if n_pad:
        lane = lax.broadcasted_iota(jnp.int32, log_f.shape, 1)
        log_f = jnp.where(lane >= n_pad, log_f, 0.0)
    cum = _lane_cumsum(log_f) + carry_ref[...]
    carry_ref[...] = cum[:, -1:]
    fend_ref[...] = cum[:, -1:]
    nfl = cum * (-LOG2E)
    p1 = nfl.astype(_BF).astype(_F32)
    p2 = (nfl - p1).astype(_BF).astype(_F32)
    p3 = (nfl - p1 - p2).astype(_BF).astype(_F32)
    g = jnp.concatenate([p1, p2, p3, jnp.zeros((LANES - 3 * FOX_HEADS, tm), _F32)], axis=0)
    aug_ref[...] = g.T.astype(_BF)


def _qkv_call(x3, g1, wqk, wvt, wfl, ws, wsvt, bfg, fqg, fkg, sqg, skg, f0, *, tm, n_pad):
    B, S, D = x3.shape
    nj = S // tm
    row = lambda w: pl.BlockSpec((None, tm, w), lambda b, j: (b, j, 0))
    full = lambda a: pl.BlockSpec(a.shape, lambda b, j: (0,) * a.ndim)
    bsd = lambda w: jax.ShapeDtypeStruct((B, S, w), _BF)
    out_shape = [bsd(FOX_W), bsd(FOX_W), bsd(LANES), jax.ShapeDtypeStruct((B, FOX_HEADS, nj, 2 * HEAD_DIM, tm), _BF),
                 bsd(SWA_QW), bsd(SWA_KW),
                 jax.ShapeDtypeStruct((B, SWA_KV_HEADS, S // WINDOW, 2 * HEAD_DIM, WINDOW), _BF),
                 jax.ShapeDtypeStruct((B, FOX_HEADS, 1), _F32)]
    out_specs = [row(FOX_W), row(FOX_W), row(LANES),
                 pl.BlockSpec((None, FOX_HEADS, None, 2 * HEAD_DIM, tm), lambda b, j: (b, 0, j, 0, 0)),
                 row(SWA_QW), row(SWA_KW),
                 pl.BlockSpec((None, SWA_KV_HEADS, tm // WINDOW, 2 * HEAD_DIM, WINDOW), lambda b, j: (b, 0, j, 0, 0)),
                 pl.BlockSpec((None, FOX_HEADS, 1), lambda b, j: (b, 0, 0))]
    return pl.pallas_call(
        functools.partial(_qkv_kernel, n_pad=n_pad),
        grid=(B, nj),
        in_specs=[row(D)] + [full(a) for a in (g1, wqk, wvt, wfl, ws, wsvt, bfg, fqg, fkg, sqg, skg, f0)],
        out_specs=out_specs, out_shape=out_shape,
        scratch_shapes=[pltpu.VMEM((FOX_HEADS, 1), _F32)],
        compiler_params=_cparams(("arbitrary", "arbitrary")),
        name="qkv_proj",
    )(x3, g1, wqk, wvt, wfl, ws, wsvt, bfg, fqg, fkg, sqg, skg, f0)


def _fox_kernel(q_ref, k_ref, aug_ref, vt_ref, km_ref, augm_ref, vtm_ref, o_ref, m_ref, acc_ref, s_ref, mx_ref):
    g = pl.program_id(1)
    qi = pl.program_id(2)
    tq = q_ref.shape[0]
    tk = vt_ref.shape[-1]
    nt = (((1,), (1,)), ((), ()))
    lane = lax.broadcasted_iota(jnp.int32, (tq, LANES), 1)

    qaugs = []
    for pp in range(FOX_PAIRS):
        q = q_ref[:, pp * LANES:(pp + 1) * LANES]
        parts = []
        for a in range(2):
            head = 2 * (FOX_PAIRS * g + pp) + a
            keep = (lane < HEAD_DIM) if a == 0 else (lane >= HEAD_DIM)
            ones = jnp.where((lane < 3 * FOX_HEADS) & ((lane & (FOX_HEADS - 1)) == head), 1.0, 0.0).astype(_BF)
            parts.append(jnp.concatenate([jnp.where(keep, q, jnp.zeros_like(q)), ones], axis=1))
        qaugs.append(jnp.concatenate(parts, axis=0))

    def scores(pp, k, aug, mask):
        st = lax.dot_general(jnp.concatenate([k, aug], axis=1), qaugs[pp], nt, preferred_element_type=_F32)
        if mask is not None:
            st = jnp.where(mask, st, NEG_INF)
        return st, jnp.max(st, axis=0, keepdims=True)

    def tile_scores(pp, ki, mask=None):
        off = pl.multiple_of(ki * tk, tk)
        return scores(pp, k_ref[pl.ds(off, tk), pp * LANES:(pp + 1) * LANES], aug_ref[pl.ds(off, tk), :], mask)

    def process(pp, st, mx, vts, first):
        m_new = mx if first else jnp.maximum(m_ref[pp], mx)
        pt = jnp.exp2(st - m_new).astype(_BF)
        if not first:
            alpha = jnp.exp2(m_ref[pp] - m_new)
        for a in range(2):
            pv = jnp.dot(vts[a], pt[:, a * tq:(a + 1) * tq], preferred_element_type=_F32)
            acc_ref[2 * pp + a] = pv if first else alpha[:, a * tq:(a + 1) * tq] * acc_ref[2 * pp + a] + pv
        m_ref[pp] = m_new

    def stash(buf, pp, st_mx):
        s_ref[buf, pp] = st_mx[0]
        mx_ref[buf, pp] = st_mx[1]

    def vts(pp, ki):
        return vt_ref[2 * pp, ki], vt_ref[2 * pp + 1, ki]

    causal = lax.broadcasted_iota(jnp.int32, (tk, 2 * tq), 0) <= (lax.broadcasted_iota(jnp.int32, (tk, 2 * tq), 1) & (tq - 1))
    mask_m = lax.broadcasted_iota(jnp.int32, (META_BLOCK, 2 * tq), 0) >= META_PAD
    for pp in range(FOX_PAIRS):
        stash(0, pp, tile_scores(pp, qi, causal))
        st, mx = scores(pp, km_ref[:, pp * LANES:(pp + 1) * LANES], augm_ref[...], mask_m)
        process(pp, st, mx, (vtm_ref[2 * pp], vtm_ref[2 * pp + 1]), True)

    def step(j, buf):
        nxt = [tile_scores(pp, j) for pp in range(FOX_PAIRS)]
        cur = jnp.where(j == 0, qi, j - 1)
        for pp in range(FOX_PAIRS):
            process(pp, s_ref[buf, pp], mx_ref[buf, pp], vts(pp, cur), False)
        for pp in range(FOX_PAIRS):
            stash(1 - buf, pp, nxt[pp])

    def body(jj, c):
        step(2 * jj, 0)
        step(2 * jj + 1, 1)
        return c
    lax.fori_loop(0, qi // 2, body, 0)

    def finish(buf):
        last = jnp.where(qi == 0, qi, qi - 1)
        outs = []
        for pp in range(FOX_PAIRS):
            process(pp, s_ref[buf, pp], mx_ref[buf, pp], vts(pp, last), False)
            for a in range(2):
                acc = acc_ref[2 * pp + a]
                outs.append(acc[:HEAD_DIM] / acc[HEAD_DIM:HEAD_DIM + 1])
        o_ref[...] = jnp.concatenate(outs, axis=0).T.astype(_BF)

    @pl.when(qi % 2 == 1)
    def _():
        step(qi - 1, 0)
        finish(1)

    @pl.when(qi % 2 == 0)
    def _():
        finish(0)


def _fox_call(fq, fk, aug, fvt, km, augm, vtm, *, tq):
    B, S, _ = fq.shape
    nk = fvt.shape[2]
    assert fvt.shape[-1] == tq
    w = FOX_PAIRS * LANES
    qspec = pl.BlockSpec((None, tq, w), lambda b, g, i: (b, i, g))
    return pl.pallas_call(
        _fox_kernel, grid=(B, FOX_HEADS // (2 * FOX_PAIRS), S // tq),
        in_specs=[qspec,
                  pl.BlockSpec((None, S, w), lambda b, g, i: (b, 0, g)),
                  pl.BlockSpec((None, S, LANES), lambda b, g, i: (b, 0, 0)),
                  pl.BlockSpec((None, 2 * FOX_PAIRS, nk, 2 * HEAD_DIM, tq), lambda b, g, i: (b, g, 0, 0, 0)),
                  pl.BlockSpec((META_BLOCK, w), lambda b, g, i: (0, g)),
                  pl.BlockSpec((META_BLOCK, LANES), lambda b, g, i: (0, 0)),
                  pl.BlockSpec((2 * FOX_PAIRS, 2 * HEAD_DIM, META_BLOCK), lambda b, g, i: (g, 0, 0))],
        out_specs=qspec, out_shape=jax.ShapeDtypeStruct((B, S, FOX_W), _BF),
        scratch_shapes=[pltpu.VMEM((FOX_PAIRS, 1, 2 * tq), _F32), pltpu.VMEM((2 * FOX_PAIRS, 2 * HEAD_DIM, tq), _F32),
                        pltpu.VMEM((2, FOX_PAIRS, tq, 2 * tq), _F32), pltpu.VMEM((2, FOX_PAIRS, 1, 2 * tq), _F32)],
        compiler_params=_cparams(("arbitrary", "arbitrary", "arbitrary")), name="fox_attn",
    )(fq, fk, aug, fvt, km, augm, vtm)


def _swa_bias(tq):
    j = np.arange(WINDOW + tq)[:, None] - WINDOW
    i = np.arange(tq)[None, :]
    dist = i - j
    ok = (dist >= 0) & (dist < WINDOW)
    slopes = np.exp2(-8.0 * (np.arange(SWA_HEADS, dtype=np.float32) + 1.0) / SWA_HEADS).astype(np.float32)
    per_head = np.where(ok[None], -slopes[:, None, None] * dist[None].astype(np.float32), np.float32(NEG_INF))
    rep = SWA_HEADS // SWA_KV_HEADS
    return np.stack([np.concatenate(list(per_head[g * rep:(g + 1) * rep]), axis=1) for g in range(SWA_KV_HEADS)])


def _swa_kernel(q_ref, kc_ref, kp_ref, km_ref, vc_ref, vp_ref, vm_ref, bias_ref, sink_ref, o_ref):
    qi = pl.program_id(1)
    tq = q_ref.shape[0]
    rep = SWA_HEADS // SWA_KV_HEADS
    nt = (((1,), (1,)), ((), ()))
    first = qi == 0
    kcat = jnp.concatenate([jnp.where(first, km_ref[...], kp_ref[...]), kc_ref[...]], axis=0)
    row = lax.broadcasted_iota(jnp.int32, (WINDOW + tq, rep * tq), 0)
    row_ok = row >= jnp.where(first, META_PAD, 0)
    lane = lax.broadcasted_iota(jnp.int32, (tq, LANES), 1)
    outs = [None] * SWA_HEADS
    for g in range(SWA_KV_HEADS):
        keep = (lane < HEAD_DIM) if g == 0 else (lane >= HEAD_DIM)
        qg = jnp.concatenate([jnp.where(keep, q_ref[:, p * LANES:(p + 1) * LANES], jnp.zeros((tq, LANES), _BF))
                              for p in range(rep)], axis=0)
        vcat = jnp.concatenate([jnp.where(first, vm_ref[g], vp_ref[g, 0])] + [vc_ref[g, c] for c in range(tq // WINDOW)],
                               axis=1)
        st = lax.dot_general(kcat, qg, nt, preferred_element_type=_F32) + bias_ref[g]
        st = jnp.where(row_ok, st, NEG_INF)
        sink = sink_ref[g]
        m = jnp.maximum(jnp.max(st, axis=0, keepdims=True), sink)
        pt = jnp.exp(st - m).astype(_BF)
        acc = jnp.dot(vcat, pt, preferred_element_type=_F32)
        o = acc[:HEAD_DIM] / (acc[HEAD_DIM:HEAD_DIM + 1] + jnp.exp(sink - m))
        for p in range(rep):
            outs[2 * p + g] = o[:, p * tq:(p + 1) * tq]
    o_ref[...] = jnp.concatenate(outs, axis=0).T.astype(_BF)


def _swa_call(sinks, sq, sk, svt, km, vtm, *, tq):
    B, S, _ = sq.shape
    r = tq // WINDOW
    rep = SWA_HEADS // SWA_KV_HEADS
    bias = jnp.asarray(_swa_bias(tq))
    sink_rows = jnp.repeat(sinks.reshape(SWA_KV_HEADS, rep), tq, axis=1)[:, None, :]
    prev_blk = lambda i: jnp.maximum(i * r - 1, 0)
    return pl.pallas_call(
        _swa_kernel, grid=(B, S // tq),
        in_specs=[pl.BlockSpec((None, tq, SWA_QW), lambda b, i: (b, i, 0)),
                  pl.BlockSpec((None, tq, LANES), lambda b, i: (b, i, 0)),
                  pl.BlockSpec((None, WINDOW, LANES), lambda b, i: (b, prev_blk(i), 0)),
                  pl.BlockSpec((META_BLOCK, LANES), lambda b, i: (0, 0)),
                  pl.BlockSpec((None, SWA_KV_HEADS, r, 2 * HEAD_DIM, WINDOW), lambda b, i: (b, 0, i, 0, 0)),
                  pl.BlockSpec((None, SWA_KV_HEADS, 1, 2 * HEAD_DIM, WINDOW), lambda b, i: (b, 0, prev_blk(i), 0, 0)),
                  pl.BlockSpec((SWA_KV_HEADS, 2 * HEAD_DIM, META_BLOCK), lambda b, i: (0, 0, 0)),
                  pl.BlockSpec(bias.shape, lambda b, i: (0, 0, 0)),
                  pl.BlockSpec(sink_rows.shape, lambda b, i: (0, 0, 0))],
        out_specs=pl.BlockSpec((None, tq, SWA_QW), lambda b, i: (b, i, 0)),
        out_shape=jax.ShapeDtypeStruct((B, S, SWA_QW), _BF),
        compiler_params=_cparams(("arbitrary", "arbitrary")), name="swa_attn",
    )(sq, sk, sk, km, svt, svt, vtm, bias, sink_rows)


def _split3(v):
    a = v.astype(_BF)
    r = v - a.astype(_F32)
    b = r.astype(_BF)
    c = (r - b.astype(_F32)).astype(_BF)
    return a, b, c


def _mix_kernel(x_ref, oa_ref, ob_ref, g1_ref, wga_ref, wgb_ref, wfo_ref, wso_ref, wo_ref, g2_ref, wr_ref, br_ref,
                hres_ref, h2_ref, lg_ref):
    x = x_ref[...]
    h = _rms(x, g1_ref[...]).astype(_BF)
    ga = jax.nn.sigmoid(jnp.dot(h, wga_ref[...], preferred_element_type=_F32))
    gb = jax.nn.sigmoid(jnp.dot(h, wgb_ref[...], preferred_element_type=_F32))
    mix = ga * jnp.dot(oa_ref[...], wfo_ref[...], preferred_element_type=_F32) \
        + gb * jnp.dot(ob_ref[...], wso_ref[...], preferred_element_type=_F32)
    hres = x + jnp.dot(mix.astype(_BF), wo_ref[...], preferred_element_type=_F32)
    hres_ref[...] = hres
    h2 = _rms(hres, g2_ref[...])
    h2_ref[...] = h2.astype(_BF)
    wr = wr_ref[...]
    w_hi = wr.astype(_BF)
    w_lo = (wr - w_hi.astype(_F32)).astype(_BF)
    h_hi = h2.astype(_BF)
    h_lo = (h2 - h_hi.astype(_F32)).astype(_BF)
    lg = jnp.dot(h_hi, w_hi, preferred_element_type=_F32) + jnp.dot(h_hi, w_lo, preferred_element_type=_F32) \
        + jnp.dot(h_lo, w_hi, preferred_element_type=_F32)
    lg_ref[...] = lg + br_ref[...]


def _mix_call(x2, oa, ob, g1, wga, wgb, wfo, wso, wo, g2, wr, br, *, tm):
    N, D = x2.shape
    row = lambda w: pl.BlockSpec((tm, w), lambda i: (i, 0))
    full = lambda a: pl.BlockSpec(a.shape, lambda i: (0,) * a.ndim)
    return pl.pallas_call(
        _mix_kernel, grid=(N // tm,),
        in_specs=[row(D), row(FOX_W), row(SWA_QW)] + [full(a) for a in (g1, wga, wgb, wfo, wso, wo, g2, wr, br)],
        out_specs=[row(D), row(D), row(LANES)],
        out_shape=[jax.ShapeDtypeStruct((N, D), _F32), jax.ShapeDtypeStruct((N, D), _BF),
                   jax.ShapeDtypeStruct((N, LANES), _F32)],
        compiler_params=_cparams(("arbitrary",)), name="mix_proj",
    )(x2, oa, ob, g1, wga, wgb, wfo, wso, wo, g2, wr, br)


def _route_kernel(lg_ref, a_ref, at_ref, cnt_ref):
    T = lg_ref.shape[0]
    lt = lg_ref[...].T[:N_EXPERTS]
    e_iota = lax.broadcasted_iota(jnp.int32, lt.shape, 0)
    cur = lt
    vals, idxs = [], []
    for _ in range(TOP_K):
        m = jnp.max(cur, axis=0, keepdims=True)
        idx = jnp.min(jnp.where(cur == m, e_iota, N_EXPERTS), axis=0, keepdims=True)
        vals.append(m)
        idxs.append(idx)
        cur = jnp.where(e_iota == idx, -jnp.inf, cur)
    es = [jnp.exp(v - vals[0]) for v in vals]
    den = es[0] + es[1] + es[2] + es[3]
    gates = [e / den for e in es]

    onehot = jnp.zeros(lt.shape, _F32)
    for idx in idxs:
        onehot = onehot + jnp.where(e_iota == idx, 1.0, 0.0)
    r_i = lax.broadcasted_iota(jnp.int32, (T, T), 0)
    c_i = lax.broadcasted_iota(jnp.int32, (T, T), 1)
    utri = jnp.where(r_i < c_i, 1.0, 0.0).astype(_BF)
    prefix = jnp.dot(onehot.astype(_BF), utri, preferred_element_type=_F32)
    cnt = jnp.sum(onehot, axis=1, keepdims=True)
    chunks = jnp.floor((cnt + (RUN_ALIGN - 1)) * (1.0 / RUN_ALIGN))
    l_r = lax.broadcasted_iota(jnp.int32, (N_EXPERTS, N_EXPERTS), 0)
    l_c = lax.broadcasted_iota(jnp.int32, (N_EXPERTS, N_EXPERTS), 1)
    ltri = jnp.where(l_c < l_r, 1.0, 0.0).astype(_BF)
    lbase = RUN_ALIGN * jnp.dot(ltri, jnp.broadcast_to(chunks, (N_EXPERTS, LANES)).astype(_BF),
                                preferred_element_type=_F32)[:, 0:1]
    slot_all = prefix + lbase
    rows = [jnp.sum(jnp.where(e_iota == idx, slot_all, 0.0), axis=0, keepdims=True) for idx in idxs] + gates
    at = jnp.concatenate(rows, axis=0)
    at_ref[...] = at
    a_ref[...] = jnp.concatenate([at, jnp.zeros((LANES - 2 * TOP_K, T), _F32)], axis=0).T
    cnt_ref[...] = jnp.broadcast_to(cnt, (N_EXPERTS, LANES)).astype(jnp.int32)


def _route_call(lg, *, T):
    N = lg.shape[0]
    nt = N // T
    return pl.pallas_call(
        _route_kernel, grid=(nt,),
        in_specs=[pl.BlockSpec((T, LANES), lambda i: (i, 0))],
        out_specs=[pl.BlockSpec((T, LANES), lambda i: (i, 0)), pl.BlockSpec((8, T), lambda i: (0, i)),
                   pl.BlockSpec((None, N_EXPERTS, LANES), lambda i: (i, 0, 0))],
        out_shape=[jax.ShapeDtypeStruct((N, LANES), _F32), jax.ShapeDtypeStruct((8, N), _F32),
                   jax.ShapeDtypeStruct((nt, N_EXPERTS, LANES), jnp.int32)],
        compiler_params=_cparams(("arbitrary",)), name="route",
    )(lg)


def _div_pow2(v, d):
    assert d & (d - 1) == 0
    return lax.shift_right_logical(v, d.bit_length() - 1)


def _plan_kernel(cnt_ref, goff_ref, lbase_ref, nch_ref, eblk_ref, gap_ref, info_ref, *, nt, block_rows):
    def per_expert(e, base):
        def per_tile(i, run):
            n = _div_pow2(cnt_ref[i * N_EXPERTS + e] + (RUN_ALIGN - 1), RUN_ALIGN)
            goff_ref[i * N_EXPERTS + e] = run
            nch_ref[i * N_EXPERTS + e] = n
            return run + n * RUN_ALIGN
        end = lax.fori_loop(0, nt, per_tile, base)
        nblk = _div_pow2(end - base + (block_rows - 1), block_rows)
        nxt = base + nblk * block_rows
        gap_ref[2 * e] = end
        gap_ref[2 * e + 1] = _div_pow2(nxt - end, RUN_ALIGN)
        eblk_ref[2 * e] = _div_pow2(base, block_rows)
        eblk_ref[2 * e + 1] = nblk
        return nxt
    total = lax.fori_loop(0, N_EXPERTS, per_expert, 0)
    info_ref[0] = _div_pow2(total, block_rows)

    def per_tile2(i, c):
        def per_e(e, run):
            lbase_ref[i * N_EXPERTS + e] = run
            return run + nch_ref[i * N_EXPERTS + e] * RUN_ALIGN
        lax.fori_loop(0, N_EXPERTS, per_e, 0)
        return c
    lax.fori_loop(0, nt, per_tile2, 0)


def _plan_call(cnt_flat, *, nt, block_rows):
    smem = pl.BlockSpec(memory_space=pltpu.SMEM)
    i32 = lambda n: jax.ShapeDtypeStruct((n,), jnp.int32)
    return pl.pallas_call(
        functools.partial(_plan_kernel, nt=nt, block_rows=block_rows),
        in_specs=[smem], out_specs=[smem] * 6,
        out_shape=[i32(nt * N_EXPERTS), i32(nt * N_EXPERTS), i32(nt * N_EXPERTS), i32(2 * N_EXPERTS), i32(2 * N_EXPERTS),
                   i32(1)],
        name="plan",
    )(cnt_flat)


def _pack_pairs(v):
    w = v.shape[1] // 2
    return pltpu.pack_elementwise([v[:, :w], v[:, w:]], packed_dtype=_BF)


def _unpack_pairs(wd):
    lo = pltpu.unpack_elementwise(wd, index=0, packed_dtype=_BF, unpacked_dtype=_F32)
    hi = pltpu.unpack_elementwise(wd, index=1, packed_dtype=_BF, unpacked_dtype=_F32)
    return lo.astype(_BF), hi.astype(_BF)


def _dispatch_kernel(goff, lbase, nch, gap, info, at_ref, h2_ref, xs_ref, xl_ref, z_ref, sem, zsem, tsem):
    i = pl.program_id(0)
    R = xl_ref.shape[1]
    T = h2_ref.shape[0]
    block_rows = z_ref.shape[0]
    nb = xs_ref.shape[0] // block_rows

    @pl.when(i == 0)
    def _():
        z_ref[...] = jnp.zeros(z_ref.shape, z_ref.dtype)

        def gap_copy(e):
            rows = pl.multiple_of(gap[2 * e + 1] * RUN_ALIGN, RUN_ALIGN)
            return pltpu.make_async_copy(z_ref.at[pl.ds(0, rows)],
                                         xs_ref.at[pl.ds(pl.multiple_of(gap[2 * e], RUN_ALIGN), rows)], zsem)

        def fill(e, c):
            @pl.when(gap[2 * e + 1] > 0)
            def _():
                gap_copy(e).start()
            return c
        lax.fori_loop(0, N_EXPERTS, fill, 0)

        def fill_wait(e, c):
            @pl.when(gap[2 * e + 1] > 0)
            def _():
                gap_copy(e).wait()
            return c
        lax.fori_loop(0, N_EXPERTS, fill_wait, 0)

        def tail(b, c):
            pltpu.make_async_copy(z_ref, xs_ref.at[pl.ds(pl.multiple_of(b * block_rows, block_rows), block_rows)],
                                  tsem).start()
            return c
        lax.fori_loop(info[0], nb, tail, 0)

        def tail_wait(b, c):
            pltpu.make_async_copy(z_ref, xs_ref.at[pl.ds(0, block_rows)], tsem).wait()
            return c
        lax.fori_loop(info[0], nb, tail_wait, 0)

    slot_iota = lax.broadcasted_iota(jnp.int32, (R, T), 0).astype(_F32)
    pm = jnp.zeros((R, T), _F32)
    for k in range(TOP_K):
        pm = pm + jnp.where(slot_iota == at_ref[k:k + 1, :], 1.0, 0.0)
    xl = jnp.dot(pm.astype(_BF), h2_ref[...], preferred_element_type=_F32)
    slot = i & 1

    def wait_tile(t, s):
        total = lax.fori_loop(0, N_EXPERTS, lambda e, c: c + nch[t * N_EXPERTS + e], 0)

        @pl.when(total > 0)
        def _():
            rows = pl.multiple_of(total * RUN_ALIGN, RUN_ALIGN)
            pltpu.make_async_copy(xl_ref.at[s, pl.ds(0, rows)], xs_ref.at[pl.ds(0, rows)], sem.at[s]).wait()

    @pl.when(i >= 2)
    def _():
        wait_tile(i - 2, slot)
    xl_ref[slot] = _pack_pairs(xl)

    def issue(e, c):
        n = nch[i * N_EXPERTS + e]

        @pl.when(n > 0)
        def _():
            rows = pl.multiple_of(n * RUN_ALIGN, RUN_ALIGN)
            pltpu.make_async_copy(xl_ref.at[slot, pl.ds(pl.multiple_of(lbase[i * N_EXPERTS + e], RUN_ALIGN), rows)],
                                  xs_ref.at[pl.ds(pl.multiple_of(goff[i * N_EXPERTS + e], RUN_ALIGN), rows)],
                                  sem.at[slot]).start()
        return c
    lax.fori_loop(0, N_EXPERTS, issue, 0)

    @pl.when(i == pl.num_programs(0) - 1)
    def _():
        @pl.when(i >= 1)
        def _():
            wait_tile(i - 1, 1 - slot)
        wait_tile(i, slot)


def _dispatch_call(goff, lbase, nch, gap, info, at, h2, *, T, R, rows, block_rows):
    N, D = h2.shape
    grid_spec = pltpu.PrefetchScalarGridSpec(
        num_scalar_prefetch=5, grid=(N // T,),
        in_specs=[pl.BlockSpec((8, T), lambda i, *_: (0, i)), pl.BlockSpec((T, D), lambda i, *_: (i, 0))],
        out_specs=pl.BlockSpec(memory_space=pl.ANY),
        scratch_shapes=[pltpu.VMEM((2, R, D // 2), jnp.uint32), pltpu.VMEM((block_rows, D // 2), jnp.uint32),
                        pltpu.SemaphoreType.DMA((2,)), pltpu.SemaphoreType.DMA, pltpu.SemaphoreType.DMA])
    return pl.pallas_call(
        _dispatch_kernel, grid_spec=grid_spec, out_shape=jax.ShapeDtypeStruct((rows, D // 2), jnp.uint32),
        compiler_params=_cparams(("arbitrary",)), name="dispatch",
    )(goff, lbase, nch, gap, info, at, h2)


def _expert_kernel(eblk, info, xs_ref, wub_ref, bu_ref, wdb_ref, bd_ref, ys_ref, xb_ref, yb_ref, xsem, ysem):
    e = pl.program_id(0)
    block_rows = xb_ref.shape[1]
    nb = ys_ref.shape[0] // block_rows
    b0 = eblk[2 * e]
    n = eblk[2 * e + 1]
    half = wub_ref.shape[0] // 2

    def rows_of(blk):
        return pl.ds(pl.multiple_of(blk * block_rows, block_rows), block_rows)

    def x_copy(j, slot):
        return pltpu.make_async_copy(xs_ref.at[rows_of(b0 + j)], xb_ref.at[slot], xsem.at[slot])

    def y_copy(j, slot):
        return pltpu.make_async_copy(yb_ref.at[slot], ys_ref.at[rows_of(b0 + j)], ysem.at[slot])

    @pl.when((e == 0) & (n > 0))
    def _():
        x_copy(0, 0).start(priority=1)

    def body(j, c):
        slot = j & 1
        x_copy(j, slot).wait()

        @pl.when(j + 1 < n)
        def _():
            x_copy(j + 1, 1 - slot).start(priority=1)

        @pl.when(j >= 2)
        def _():
            y_copy(j - 2, slot).wait()

        x_lo, x_hi = _unpack_pairs(xb_ref[slot])
        gu = jnp.dot(x_lo, wub_ref[:half, :], preferred_element_type=_F32) \
            + jnp.dot(x_hi, wub_ref[half:, :], preferred_element_type=_F32) + bu_ref[...]
        g = jnp.minimum(gu[:, :D_FF], SWIGLU_LIMIT)
        u = jnp.clip(gu[:, D_FF:], -SWIGLU_LIMIT, SWIGLU_LIMIT)
        act = (u + 1.0) * (g * jax.nn.sigmoid(SWIGLU_ALPHA * g))
        y = jnp.dot(act.astype(_BF), wdb_ref[...], preferred_element_type=_F32) + bd_ref[...]
        yb_ref[slot] = _pack_pairs(y)
        y_copy(j, slot).start()
        return c
    lax.fori_loop(0, n, body, 0)

    @pl.when(n >= 2)
    def _():
        y_copy(n - 2, n & 1).wait()

    @pl.when(n >= 1)
    def _():
        y_copy(n - 1, (n - 1) & 1).wait()

    last = pl.num_programs(0) - 1
    e_next = jnp.minimum(e + 1, last)

    @pl.when((e < last) & (eblk[2 * e_next + 1] > 0))
    def _():
        pltpu.make_async_copy(xs_ref.at[rows_of(eblk[2 * e_next])], xb_ref.at[0], xsem.at[0]).start(priority=1)

    @pl.when(e == last)
    def _():
        yb_ref[0] = _pack_pairs(jnp.zeros((block_rows, 2 * yb_ref.shape[2]), _F32))

        def tail_copy(b):
            return pltpu.make_async_copy(yb_ref.at[0], ys_ref.at[pl.ds(pl.multiple_of(b * block_rows, block_rows), block_rows)],
                                         ysem.at[0])

        def tail(b, c):
            tail_copy(b).start()
            return c
        lax.fori_loop(info[0], nb, tail, 0)

        def tail_wait(b, c):
            tail_copy(b).wait()
            return c
        lax.fori_loop(info[0], nb, tail_wait, 0)


def _expert_call(eblk, info, xs, wu, bu, wd, bd, *, block_rows):
    rows, half = xs.shape
    D = 2 * half
    ex = lambda e, *_: (e, 0, 0)
    grid_spec = pltpu.PrefetchScalarGridSpec(
        num_scalar_prefetch=2, grid=(N_EXPERTS,),
        in_specs=[pl.BlockSpec(memory_space=pl.ANY),
                  pl.BlockSpec((None, D, 2 * D_FF), ex), pl.BlockSpec((None, 1, 2 * D_FF), ex),
                  pl.BlockSpec((None, D_FF, D), ex), pl.BlockSpec((None, 1, D), ex)],
        out_specs=pl.BlockSpec(memory_space=pl.ANY),
        scratch_shapes=[pltpu.VMEM((2, block_rows, half), jnp.uint32), pltpu.VMEM((2, block_rows, half), jnp.uint32),
                        pltpu.SemaphoreType.DMA((2,)), pltpu.SemaphoreType.DMA((2,))])
    return pl.pallas_call(
        _expert_kernel, grid_spec=grid_spec, out_shape=jax.ShapeDtypeStruct((rows, half), jnp.uint32),
        compiler_params=_cparams(("arbitrary",)), name="experts",
    )(eblk, info, xs, wu, bu, wd, bd)


def _combine_kernel(goff, lbase, nch, a_ref, hres_ref, ys_ref, o_ref, yl_ref, sem):
    i = pl.program_id(0)
    nt = pl.num_programs(0)
    R = yl_ref.shape[1]
    T = a_ref.shape[0]

    def fetch(t, slot):
        def issue(e, c):
            n = nch[t * N_EXPERTS + e]

            @pl.when(n > 0)
            def _():
                rows = pl.multiple_of(n * RUN_ALIGN, RUN_ALIGN)
                pltpu.make_async_copy(
                    ys_ref.at[pl.ds(pl.multiple_of(goff[t * N_EXPERTS + e], RUN_ALIGN), rows)],
                    yl_ref.at[slot, pl.ds(pl.multiple_of(lbase[t * N_EXPERTS + e], RUN_ALIGN), rows)], sem.at[slot]).start()
            return c
        lax.fori_loop(0, N_EXPERTS, issue, 0)

    @pl.when(i == 0)
    def _():
        yl_ref[...] = jnp.zeros(yl_ref.shape, yl_ref.dtype)
        fetch(0, 0)

    slot = i & 1

    @pl.when(i + 1 < nt)
    def _():
        fetch(i + 1, 1 - slot)

    total = lax.fori_loop(0, N_EXPERTS, lambda e, c: c + nch[i * N_EXPERTS + e], 0)

    @pl.when(total > 0)
    def _():
        rows = pl.multiple_of(total * RUN_ALIGN, RUN_ALIGN)
        pltpu.make_async_copy(ys_ref.at[pl.ds(0, rows)], yl_ref.at[slot, pl.ds(0, rows)], sem.at[slot]).wait()

    a = a_ref[...]
    slot_iota = lax.broadcasted_iota(jnp.int32, (T, R), 1).astype(_F32)
    pt = jnp.zeros((T, R), _F32)
    for k in range(TOP_K):
        pt = pt + jnp.where(slot_iota == a[:, k:k + 1], a[:, TOP_K + k:TOP_K + k + 1], 0.0)
    pt = pt.astype(_BF)
    y_lo, y_hi = _unpack_pairs(yl_ref[slot])
    half = y_lo.shape[1]
    o_ref[:, :half] = hres_ref[:, :half] + jnp.dot(pt, y_lo, preferred_element_type=_F32)
    o_ref[:, half:] = hres_ref[:, half:] + jnp.dot(pt, y_hi, preferred_element_type=_F32)


def _combine_call(goff, lbase, nch, a, hres, ys, *, T, R):
    N, D = hres.shape
    grid_spec = pltpu.PrefetchScalarGridSpec(
        num_scalar_prefetch=3, grid=(N // T,),
        in_specs=[pl.BlockSpec((T, LANES), lambda i, *_: (i, 0)), pl.BlockSpec((T, D), lambda i, *_: (i, 0)),
                  pl.BlockSpec(memory_space=pl.ANY)],
        out_specs=pl.BlockSpec((T, D), lambda i, *_: (i, 0)),
        scratch_shapes=[pltpu.VMEM((2, R, D // 2), jnp.uint32), pltpu.SemaphoreType.DMA((2,))])
    return pl.pallas_call(
        _combine_kernel, grid_spec=grid_spec, out_shape=jax.ShapeDtypeStruct((N, D), _F32),
        compiler_params=_cparams(("arbitrary",)), name="combine",
    )(goff, lbase, nch, a, hres, ys)


def _tiles(S):
    pick = lambda pref: next(t for t in pref if S % t == 0)
    tm = pick((512, 256, 128))
    return dict(tm=tm, tq_fox=tm, tq_swa=pick((256, 128)),
                tm_mix=pick((512, 256, 128)), t_route=pick((256, 128)), block_rows=128)


def kernel(x, meta_tokens, norm1_gain, w_in, b_forget, fox_q_gain, fox_k_gain, swa_q_gain, swa_k_gain, swa_sinks,
           w_fox_out, w_swa_out, w_out, norm2_gain, w_router, b_router, w_up, b_up, w_down, b_down):
    B, S, D = x.shape
    assert norm1_gain.shape[0] == 1 and S % LANES == 0
    tl = _tiles(S)
    N = B * S
    scale = HEAD_DIM ** -0.5

    w = w_in[0]
    c0 = 3 * FOX_W
    c1 = c0 + FOX_HEADS
    c2 = c1 + SWA_QW + 2 * SWA_KW
    wqk = w[:, :2 * FOX_W].astype(_BF)
    wvt = w[:, 2 * FOX_W:c0].T.astype(_BF)
    wfl = w[:, c0:c1].T.astype(_BF)
    perm_heads = np.array([h for p in range(SWA_HEADS // 2) for h in (p, p + SWA_HEADS // 2)])
    perm_cols = (perm_heads[:, None] * HEAD_DIM + np.arange(HEAD_DIM)[None, :]).reshape(-1)
    ws_all = w[:, c1:c2]
    ws = jnp.concatenate([ws_all[:, :SWA_QW][:, perm_cols], ws_all[:, SWA_QW:SWA_QW + SWA_KW]], axis=1).astype(_BF)
    wsvt = ws_all[:, SWA_QW + SWA_KW:].T.astype(_BF)
    wga = w[:, c2:c2 + D].astype(_BF)
    wgb = w[:, c2 + D:].astype(_BF)
    g1 = norm1_gain[0][None, :]
    g2 = norm2_gain[0][None, :]
    bfg = b_forget[0][:, None]
    fqg = jnp.tile(fox_q_gain[0], FOX_HEADS)[None, :] * (scale * LOG2E)
    fkg = jnp.tile(fox_k_gain[0], FOX_HEADS)[None, :]
    sqg = jnp.tile(swa_q_gain[0], SWA_HEADS)[None, :] * scale
    skg = jnp.tile(swa_k_gain[0], SWA_KV_HEADS)[None, :]
    wfo = w_fox_out[0].astype(_BF)
    wso = w_swa_out[0][perm_cols, :].astype(_BF)
    wo = w_out[0].astype(_BF)
    wr = jnp.pad(w_router[0], ((0, 0), (0, LANES - N_EXPERTS)))
    br = jnp.pad(b_router[0], (0, LANES - N_EXPERTS), constant_values=NEG_INF)[None, :]
    sinks = swa_sinks[0].astype(_F32)
    wu = w_up[0].astype(_BF)
    wd = w_down[0].astype(_BF)
    bu = b_up[0][:, None, :]
    bd = b_down[0][:, None, :]

    meta_blk = jnp.pad(meta_tokens.astype(x.dtype), ((META_PAD, 0), (0, 0)))[None]
    zero_f = jnp.zeros((FOX_HEADS, 1), _F32)
    _, fk_m, aug_m, fvt_m, _, sk_m, svt_m, fend_m = _qkv_call(meta_blk, g1, wqk, wvt, wfl, ws, wsvt, bfg, fqg, fkg, sqg,
                                                              skg, zero_f, tm=META_BLOCK, n_pad=META_PAD)
    fq, fk, aug, fvt, sq, sk, svt, _ = _qkv_call(x, g1, wqk, wvt, wfl, ws, wsvt, bfg, fqg, fkg, sqg, skg, fend_m[0],
                                                 tm=tl["tm"], n_pad=0)
    o_a = _fox_call(fq, fk, aug, fvt, fk_m[0], aug_m[0], fvt_m[0, :, 0], tq=tl["tq_fox"])
    o_b = _swa_call(sinks, sq, sk, svt, sk_m[0], svt_m[0, :, 0], tq=tl["tq_swa"])

    hres, h2, lg = _mix_call(x.reshape(N, D), o_a.reshape(N, FOX_W), o_b.reshape(N, SWA_QW),
                             g1, wga, wgb, wfo, wso, wo, g2, wr, br, tm=tl["tm_mix"])

    T = tl["t_route"]
    nt = N // T
    block_rows = tl["block_rows"]
    R = -(-(TOP_K * T + N_EXPERTS * (RUN_ALIGN - 1)) // LANES) * LANES
    max_rows = N * TOP_K + nt * N_EXPERTS * (RUN_ALIGN - 1) + N_EXPERTS * (block_rows - RUN_ALIGN)
    nb = -(-max_rows // block_rows)
    a, at, cnt = _route_call(lg, T=T)
    goff, lbase, nch, eblk, gap, info = _plan_call(cnt[:, :, 0].reshape(-1), nt=nt, block_rows=block_rows)
    xs = _dispatch_call(goff, lbase, nch, gap, info, at, h2, T=T, R=R, rows=nb * block_rows, block_rows=block_rows)
    ys = _expert_call(eblk, info, xs, wu, bu, wd, bd, block_rows=block_rows)
    out = _combine_call(goff, lbase, nch, a, hres, ys, T=T, R=R)
    return out.reshape(B, S, D)
```

```python
import functools

import jax
import jax.numpy as jnp
import numpy as np
from jax import lax
from jax.experimental import pallas as pl
from jax.experimental.pallas import tpu as pltpu

HEAD_DIM = 64
N_META = 16
FOX_HEADS = 8
SWA_HEADS = 8
SWA_KV_HEADS = 2
WINDOW = 128
N_EXPERTS = 32
TOP_K = 4
D_FF = 1024
SWIGLU_LIMIT = 7.0
SWIGLU_ALPHA = 1.702
RMS_EPS = 1e-6
NEG_INF = -1e30
LOG2E = 1.4426950408889634

LANES = 128
META_BLOCK = 128
META_PAD = META_BLOCK - N_META
FOX_PAIRS = 2
RUN_ALIGN = 8
VMEM_LIMIT = 56 * 1024 * 1024

FOX_W = FOX_HEADS * HEAD_DIM
SWA_QW = SWA_HEADS * HEAD_DIM
SWA_KW = SWA_KV_HEADS * HEAD_DIM

_BF = jnp.bfloat16
_F32 = jnp.float32


def _cparams(sem):
    return pltpu.CompilerParams(dimension_semantics=sem, vmem_limit_bytes=VMEM_LIMIT)


def _rms(t, gain):
    return t * lax.rsqrt(jnp.mean(t * t, axis=-1, keepdims=True) + RMS_EPS) * gain


def _head_rms(z, gain_row):
    lane = lax.broadcasted_iota(jnp.int32, (z.shape[0], LANES), 1)
    lo_mask = lane < HEAD_DIM
    outs = []
    for b in range(z.shape[1] // LANES):
        v = z[:, b * LANES:(b + 1) * LANES]
        v2 = v * v
        tot = jnp.sum(v2, axis=-1, keepdims=True)
        lo = jnp.sum(jnp.where(lo_mask, v2, 0.0), axis=-1, keepdims=True)
        hi = tot - lo
        r_lo = lax.rsqrt(lo * (1.0 / HEAD_DIM) + RMS_EPS)
        r_hi = lax.rsqrt(hi * (1.0 / HEAD_DIM) + RMS_EPS)
        outs.append(v * jnp.where(lo_mask, r_lo, r_hi))
    return jnp.concatenate(outs, axis=-1) * gain_row


def _lane_cumsum(x):
    n = x.shape[-1]
    lane = lax.broadcasted_iota(jnp.int32, x.shape, x.ndim - 1)
    s = 1
    while s < n:
        x = x + jnp.where(lane >= s, pltpu.roll(x, s, x.ndim - 1), 0.0)
        s *= 2
    return x


def _qkv_kernel(x_ref, g1_ref, wqk_ref, wvt_ref, wfl_ref, ws_ref, wsvt_ref, bf_ref, fqg_ref, fkg_ref, sqg_ref, skg_ref,
                f0_ref, fq_ref, fk_ref, aug_ref, fvt_ref, sq_ref, sk_ref, svt_ref, fend_ref, carry_ref, *, n_pad):
    j = pl.program_id(1)

    @pl.when(j == 0)
    def _():
        carry_ref[...] = f0_ref[...]

    x = x_ref[...]
    tm = x.shape[0]
    h = _rms(x, g1_ref[...]).astype(_BF)
    nt = (((1,), (1,)), ((), ()))
    f = jnp.dot(h, wqk_ref[...], preferred_element_type=_F32)
    s = jnp.dot(h, ws_ref[...], preferred_element_type=_F32)
    svt = lax.dot_general(wsvt_ref[...], h, nt, preferred_element_type=_F32)
    vt = lax.dot_general(wvt_ref[...], h, nt, preferred_element_type=_F32)
    flog = lax.dot_general(wfl_ref[...], h, nt, preferred_element_type=_F32)

    fq_ref[...] = _head_rms(f[:, :FOX_W], fqg_ref[...]).astype(_BF)
    fk_ref[...] = _head_rms(f[:, FOX_W:], fkg_ref[...]).astype(_BF)
    sq_ref[...] = _head_rms(s[:, :SWA_QW], sqg_ref[...]).astype(_BF)
    sk_ref[...] = _head_rms(s[:, SWA_QW:], skg_ref[...]).astype(_BF)

    ones_row = jnp.where(lax.broadcasted_iota(jnp.int32, (HEAD_DIM, tm), 0) == 0, 1.0, 0.0).astype(_BF)
    for hd in range(FOX_HEADS):
        fvt_ref[hd] = jnp.concatenate([vt[hd * HEAD_DIM:(hd + 1) * HEAD_DIM].astype(_BF), ones_row], axis=0)
    for g in range(SWA_KV_HEADS):
        blk = jnp.concatenate([svt[g * HEAD_DIM:(g + 1) * HEAD_DIM].astype(_BF), ones_row], axis=0)
        for c in range(tm // WINDOW):
            svt_ref[g, c] = blk[:, c * WINDOW:(c + 1) * WINDOW]

    z = flog + bf_ref[...]
    log_f = jnp.minimum(z, 0.0) - jnp.log1p(jnp.exp(-jnp.abs(z)))
    if n_pad:
        lane = lax.broadcasted_iota(jnp.int32, log_f.shape, 1)
        log_f = jnp.where(lane >= n_pad, log_f, 0.0)
    cum = _lane_cumsum(log_f) + carry_ref[...]
    carry_ref[...] = cum[:, -1:]
    fend_ref[...] = cum[:, -1:]
    nfl = cum * (-LOG2E)
    p1 = nfl.astype(_BF).astype(_F32)
    p2 = (nfl - p1).astype(_BF).astype(_F32)
    p3 = (nfl - p1 - p2).astype(_BF).astype(_F32)
    g = jnp.concatenate([p1, p2, p3, jnp.zeros((LANES - 3 * FOX_HEADS, tm), _F32)], axis=0)
    aug_ref[...] = g.T.astype(_BF)


def _qkv_call(x3, g1, wqk, wvt, wfl, ws, wsvt, bfg, fqg, fkg, sqg, skg, f0, *, tm, n_pad):
    B, S, D = x3.shape
    nj = S // tm
    row = lambda w: pl.BlockSpec((None, tm, w), lambda b, j: (b, j, 0))
    full = lambda a: pl.BlockSpec(a.shape, lambda b, j: (0,) * a.ndim)
    bsd = lambda w: jax.ShapeDtypeStruct((B, S, w), _BF)
    out_shape = [bsd(FOX_W), bsd(FOX_W), bsd(LANES), jax.ShapeDtypeStruct((B, FOX_HEADS, nj, 2 * HEAD_DIM, tm), _BF),
                 bsd(SWA_QW), bsd(SWA_KW),
                 jax.ShapeDtypeStruct((B, SWA_KV_HEADS, S // WINDOW, 2 * HEAD_DIM, WINDOW), _BF),
                 jax.ShapeDtypeStruct((B, FOX_HEADS, 1), _F32)]
    out_specs = [row(FOX_W), row(FOX_W), row(LANES),
                 pl.BlockSpec((None, FOX_HEADS, None, 2 * HEAD_DIM, tm), lambda b, j: (b, 0, j, 0, 0)),
                 row(SWA_QW), row(SWA_KW),
                 pl.BlockSpec((None, SWA_KV_HEADS, tm // WINDOW, 2 * HEAD_DIM, WINDOW), lambda b, j: (b, 0, j, 0, 0)),
                 pl.BlockSpec((None, FOX_HEADS, 1), lambda b, j: (b, 0, 0))]
    return pl.pallas_call(
        functools.partial(_qkv_kernel, n_pad=n_pad),
        grid=(B, nj),
        in_specs=[row(D)] + [full(a) for a in (g1, wqk, wvt, wfl, ws, wsvt, bfg, fqg, fkg, sqg, skg, f0)],
        out_specs=out_specs, out_shape=out_shape,
        scratch_shapes=[pltpu.VMEM((FOX_HEADS, 1), _F32)],
        compiler_params=_cparams(("arbitrary", "arbitrary")),
        name="qkv_proj",
    )(x3, g1, wqk, wvt, wfl, ws, wsvt, bfg, fqg, fkg, sqg, skg, f0)


def _fox_kernel(q_ref, k_ref, aug_ref, vt_ref, km_ref, augm_ref, vtm_ref, o_ref, m_ref, acc_ref, s_ref, mx_ref):
    g = pl.program_id(1)
    qi = pl.program_id(2)
    tq = q_ref.shape[0]
    tk = vt_ref.shape[-1]
    nt = (((1,), (1,)), ((), ()))
    lane = lax.broadcasted_iota(jnp.int32, (tq, LANES), 1)

    qaugs = []
    for pp in range(FOX_PAIRS):
        q = q_ref[:, pp * LANES:(pp + 1) * LANES]
        parts = []
        for a in range(2):
            head = 2 * (FOX_PAIRS * g + pp) + a
            keep = (lane < HEAD_DIM) if a == 0 else (lane >= HEAD_DIM)
            ones = jnp.where((lane < 3 * FOX_HEADS) & ((lane & (FOX_HEADS - 1)) == head), 1.0, 0.0).astype(_BF)
            parts.append(jnp.concatenate([jnp.where(keep, q, jnp.zeros_like(q)), ones], axis=1))
        qaugs.append(jnp.concatenate(parts, axis=0))

    def scores(pp, k, aug, mask):
        st = lax.dot_general(jnp.concatenate([k, aug], axis=1), qaugs[pp], nt, preferred_element_type=_F32)
        if mask is not None:
            st = jnp.where(mask, st, NEG_INF)
        return st, jnp.max(st, axis=0, keepdims=True)

    def tile_scores(pp, ki, mask=None):
        off = pl.multiple_of(ki * tk, tk)
        return scores(pp, k_ref[pl.ds(off, tk), pp * LANES:(pp + 1) * LANES], aug_ref[pl.ds(off, tk), :], mask)

    def process(pp, st, mx, vts, first):
        m_new = mx if first else jnp.maximum(m_ref[pp], mx)
        pt = jnp.exp2(st - m_new).astype(_BF)
        if not first:
            alpha = jnp.exp2(m_ref[pp] - m_new)
        for a in range(2):
            pv = jnp.dot(vts[a], pt[:, a * tq:(a + 1) * tq], preferred_element_type=_F32)
            acc_ref[2 * pp + a] = pv if first else alpha[:, a * tq:(a + 1) * tq] * acc_ref[2 * pp + a] + pv
        m_ref[pp] = m_new

    def stash(buf, pp, st_mx):
        s_ref[buf, pp] = st_mx[0]
        mx_ref[buf, pp] = st_mx[1]

    def vts(pp, ki):
        return vt_ref[2 * pp, ki], vt_ref[2 * pp + 1, ki]

    causal = lax.broadcasted_iota(jnp.int32, (tk, 2 * tq), 0) <= (lax.broadcasted_iota(jnp.int32, (tk, 2 * tq), 1) & (tq - 1))
    mask_m = lax.broadcasted_iota(jnp.int32, (META_BLOCK, 2 * tq), 0) >= META_PAD
    for pp in range(FOX_PAIRS):
        stash(0, pp, tile_scores(pp, qi, causal))
        st, mx = scores(pp, km_ref[:, pp * LANES:(pp + 1) * LANES], augm_ref[...], mask_m)
        process(pp, st, mx, (vtm_ref[2 * pp], vtm_ref[2 * pp + 1]), True)

    def step(j, buf):
        nxt = [tile_scores(pp, j) for pp in range(FOX_PAIRS)]
        cur = jnp.where(j == 0, qi, j - 1)
        for pp in range(FOX_PAIRS):
            process(pp, s_ref[buf, pp], mx_ref[buf, pp], vts(pp, cur), False)
        for pp in range(FOX_PAIRS):
            stash(1 - buf, pp, nxt[pp])

    def body(jj, c):
        step(2 * jj, 0)
        step(2 * jj + 1, 1)
        return c
    lax.fori_loop(0, qi // 2, body, 0)

    def finish(buf):
        last = jnp.where(qi == 0, qi, qi - 1)
        outs = []
        for pp in range(FOX_PAIRS):
            process(pp, s_ref[buf, pp], mx_ref[buf, pp], vts(pp, last), False)
            for a in range(2):
                acc = acc_ref[2 * pp + a]
                outs.append(acc[:HEAD_DIM] / acc[HEAD_DIM:HEAD_DIM + 1])
        o_ref[...] = jnp.concatenate(outs, axis=0).T.astype(_BF)

    @pl.when(qi % 2 == 1)
    def _():
        step(qi - 1, 0)
        finish(1)

    @pl.when(qi % 2 == 0)
    def _():
        finish(0)


def _fox_call(fq, fk, aug, fvt, km, augm, vtm, *, tq):
    B, S, _ = fq.shape
    nk = fvt.shape[2]
    assert fvt.shape[-1] == tq
    w = FOX_PAIRS * LANES
    qspec = pl.BlockSpec((None, tq, w), lambda b, g, i: (b, i, g))
    return pl.pallas_call(
        _fox_kernel, grid=(B, FOX_HEADS // (2 * FOX_PAIRS), S // tq),
        in_specs=[qspec,
                  pl.BlockSpec((None, S, w), lambda b, g, i: (b, 0, g)),
                  pl.BlockSpec((None, S, LANES), lambda b, g, i: (b, 0, 0)),
                  pl.BlockSpec((None, 2 * FOX_PAIRS, nk, 2 * HEAD_DIM, tq), lambda b, g, i: (b, g, 0, 0, 0)),
                  pl.BlockSpec((META_BLOCK, w), lambda b, g, i: (0, g)),
                  pl.BlockSpec((META_BLOCK, LANES), lambda b, g, i: (0, 0)),
                  pl.BlockSpec((2 * FOX_PAIRS, 2 * HEAD_DIM, META_BLOCK), lambda b, g, i: (g, 0, 0))],
        out_specs=qspec, out_shape=jax.ShapeDtypeStruct((B, S, FOX_W), _BF),
        scratch_shapes=[pltpu.VMEM((FOX_PAIRS, 1, 2 * tq), _F32), pltpu.VMEM((2 * FOX_PAIRS, 2 * HEAD_DIM, tq), _F32),
                        pltpu.VMEM((2, FOX_PAIRS, tq, 2 * tq), _F32), pltpu.VMEM((2, FOX_PAIRS, 1, 2 * tq), _F32)],
        compiler_params=_cparams(("arbitrary", "arbitrary", "arbitrary")), name="fox_attn",
    )(fq, fk, aug, fvt, km, augm, vtm)


def _swa_bias(tq):
    j = np.arange(WINDOW + tq)[:, None] - WINDOW
    i = np.arange(tq)[None, :]
    dist = i - j
    ok = (dist >= 0) & (dist < WINDOW)
    slopes = np.exp2(-8.0 * (np.arange(SWA_HEADS, dtype=np.float32) + 1.0) / SWA_HEADS).astype(np.float32)
    per_head = np.where(ok[None], -slopes[:, None, None] * dist[None].astype(np.float32), np.float32(NEG_INF))
    rep = SWA_HEADS // SWA_KV_HEADS
    return np.stack([np.concatenate(list(per_head[g * rep:(g + 1) * rep]), axis=1) for g in range(SWA_KV_HEADS)])


def _swa_kernel(q_ref, kc_ref, kp_ref, km_ref, vc_ref, vp_ref, vm_ref, bias_ref, sink_ref, o_ref):
    qi = pl.program_id(1)
    tq = q_ref.shape[0]
    rep = SWA_HEADS // SWA_KV_HEADS
    nt = (((1,), (1,)), ((), ()))
    first = qi == 0
    kcat = jnp.concatenate([jnp.where(first, km_ref[...], kp_ref[...]), kc_ref[...]], axis=0)
    row = lax.broadcasted_iota(jnp.int32, (WINDOW + tq, rep * tq), 0)
    row_ok = row >= jnp.where(first, META_PAD, 0)
    lane = lax.broadcasted_iota(jnp.int32, (tq, LANES), 1)
    outs = [None] * SWA_HEADS
    for g in range(SWA_KV_HEADS):
        keep = (lane < HEAD_DIM) if g == 0 else (lane >= HEAD_DIM)
        qg = jnp.concatenate([jnp.where(keep, q_ref[:, p * LANES:(p + 1) * LANES], jnp.zeros((tq, LANES), _BF))
                              for p in range(rep)], axis=0)
        vcat = jnp.concatenate([jnp.where(first, vm_ref[g], vp_ref[g, 0])] + [vc_ref[g, c] for c in range(tq // WINDOW)],
                               axis=1)
        st = lax.dot_general(kcat, qg, nt, preferred_element_type=_F32) + bias_ref[g]
        st = jnp.where(row_ok, st, NEG_INF)
        sink = sink_ref[g]
        m = jnp.maximum(jnp.max(st, axis=0, keepdims=True), sink)
        pt = jnp.exp(st - m).astype(_BF)
        acc = jnp.dot(vcat, pt, preferred_element_type=_F32)
        o = acc[:HEAD_DIM] / (acc[HEAD_DIM:HEAD_DIM + 1] + jnp.exp(sink - m))
        for p in range(rep):
            outs[2 * p + g] = o[:, p * tq:(p + 1) * tq]
    o_ref[...] = jnp.concatenate(outs, axis=0).T.astype(_BF)


def _swa_call(sinks, sq, sk, svt, km, vtm, *, tq):
    B, S, _ = sq.shape
    r = tq // WINDOW
    rep = SWA_HEADS // SWA_KV_HEADS
    bias = jnp.asarray(_swa_bias(tq))
    sink_rows = jnp.repeat(sinks.reshape(SWA_KV_HEADS, rep), tq, axis=1)[:, None, :]
    prev_blk = lambda i: jnp.maximum(i * r - 1, 0)
    return pl.pallas_call(
        _swa_kernel, grid=(B, S // tq),
        in_specs=[pl.BlockSpec((None, tq, SWA_QW), lambda b, i: (b, i, 0)),
                  pl.BlockSpec((None, tq, LANES), lambda b, i: (b, i, 0)),
                  pl.BlockSpec((None, WINDOW, LANES), lambda b, i: (b, prev_blk(i), 0)),
                  pl.BlockSpec((META_BLOCK, LANES), lambda b, i: (0, 0)),
                  pl.BlockSpec((None, SWA_KV_HEADS, r, 2 * HEAD_DIM, WINDOW), lambda b, i: (b, 0, i, 0, 0)),
                  pl.BlockSpec((None, SWA_KV_HEADS, 1, 2 * HEAD_DIM, WINDOW), lambda b, i: (b, 0, prev_blk(i), 0, 0)),
                  pl.BlockSpec((SWA_KV_HEADS, 2 * HEAD_DIM, META_BLOCK), lambda b, i: (0, 0, 0)),
                  pl.BlockSpec(bias.shape, lambda b, i: (0, 0, 0)),
                  pl.BlockSpec(sink_rows.shape, lambda b, i: (0, 0, 0))],
        out_specs=pl.BlockSpec((None, tq, SWA_QW), lambda b, i: (b, i, 0)),
        out_shape=jax.ShapeDtypeStruct((B, S, SWA_QW), _BF),
        compiler_params=_cparams(("arbitrary", "arbitrary")), name="swa_attn",
    )(sq, sk, sk, km, svt, svt, vtm, bias, sink_rows)


def _split3(v):
    a = v.astype(_BF)
    r = v - a.astype(_F32)
    b = r.astype(_BF)
    c = (r - b.astype(_F32)).astype(_BF)
    return a, b, c


def _mix_kernel(x_ref, oa_ref, ob_ref, g1_ref, wga_ref, wgb_ref, wfo_ref, wso_ref, wo_ref, g2_ref, wr_ref, br_ref,
                hres_ref, h2_ref, lg_ref):
    x = x_ref[...]
    h = _rms(x, g1_ref[...]).astype(_BF)
    ga = jax.nn.sigmoid(jnp.dot(h, wga_ref[...], preferred_element_type=_F32))
    gb = jax.nn.sigmoid(jnp.dot(h, wgb_ref[...], preferred_element_type=_F32))
    mix = ga * jnp.dot(oa_ref[...], wfo_ref[...], preferred_element_type=_F32) \
        + gb * jnp.dot(ob_ref[...], wso_ref[...], preferred_element_type=_F32)
    hres = x + jnp.dot(mix.astype(_BF), wo_ref[...], preferred_element_type=_F32)
    hres_ref[...] = hres
    h2 = _rms(hres, g2_ref[...])
    h2_ref[...] = h2.astype(_BF)
    wr = wr_ref[...]
    w_hi = wr.astype(_BF)
    w_lo = (wr - w_hi.astype(_F32)).astype(_BF)
    h_hi = h2.astype(_BF)
    h_lo = (h2 - h_hi.astype(_F32)).astype(_BF)
    lg = jnp.dot(h_hi, w_hi, preferred_element_type=_F32) + jnp.dot(h_hi, w_lo, preferred_element_type=_F32) \
        + jnp.dot(h_lo, w_hi, preferred_element_type=_F32)
    lg_ref[...] = lg + br_ref[...]


def _mix_call(x2, oa, ob, g1, wga, wgb, wfo, wso, wo, g2, wr, br, *, tm):
    N, D = x2.shape
    row = lambda w: pl.BlockSpec((tm, w), lambda i: (i, 0))
    full = lambda a: pl.BlockSpec(a.shape, lambda i: (0,) * a.ndim)
    return pl.pallas_call(
        _mix_kernel, grid=(N // tm,),
        in_specs=[row(D), row(FOX_W), row(SWA_QW)] + [full(a) for a in (g1, wga, wgb, wfo, wso, wo, g2, wr, br)],
        out_specs=[row(D), row(D), row(LANES)],
        out_shape=[jax.ShapeDtypeStruct((N, D), _F32), jax.ShapeDtypeStruct((N, D), _BF),
                   jax.ShapeDtypeStruct((N, LANES), _F32)],
        compiler_params=_cparams(("arbitrary",)), name="mix_proj",
    )(x2, oa, ob, g1, wga, wgb, wfo, wso, wo, g2, wr, br)


def _route_kernel(lg_ref, a_ref, at_ref, cnt_ref):
    T = lg_ref.shape[0]
    lt = lg_ref[...].T[:N_EXPERTS]
    e_iota = lax.broadcasted_iota(jnp.int32, lt.shape, 0)
    cur = lt
    vals, idxs = [], []
    for _ in range(TOP_K):
        m = jnp.max(cur, axis=0, keepdims=True)
        idx = jnp.min(jnp.where(cur == m, e_iota, N_EXPERTS), axis=0, keepdims=True)
        vals.append(m)
        idxs.append(idx)
        cur = jnp.where(e_iota == idx, -jnp.inf, cur)
    es = [jnp.exp(v - vals[0]) for v in vals]
    den = es[0] + es[1] + es[2] + es[3]
    gates = [e / den for e in es]

    onehot = jnp.zeros(lt.shape, _F32)
    for idx in idxs:
        onehot = onehot + jnp.where(e_iota == idx, 1.0, 0.0)
    r_i = lax.broadcasted_iota(jnp.int32, (T, T), 0)
    c_i = lax.broadcasted_iota(jnp.int32, (T, T), 1)
    utri = jnp.where(r_i < c_i, 1.0, 0.0).astype(_BF)
    prefix = jnp.dot(onehot.astype(_BF), utri, preferred_element_type=_F32)
    cnt = jnp.sum(onehot, axis=1, keepdims=True)
    chunks = jnp.floor((cnt + (RUN_ALIGN - 1)) * (1.0 / RUN_ALIGN))
    l_r = lax.broadcasted_iota(jnp.int32, (N_EXPERTS, N_EXPERTS), 0)
    l_c = lax.broadcasted_iota(jnp.int32, (N_EXPERTS, N_EXPERTS), 1)
    ltri = jnp.where(l_c < l_r, 1.0, 0.0).astype(_BF)
    lbase = RUN_ALIGN * jnp.dot(ltri, jnp.broadcast_to(chunks, (N_EXPERTS, LANES)).astype(_BF),
                                preferred_element_type=_F32)[:, 0:1]
    slot_all = prefix + lbase
    rows = [jnp.sum(jnp.where(e_iota == idx, slot_all, 0.0), axis=0, keepdims=True) for idx in idxs] + gates
    at = jnp.concatenate(rows, axis=0)
    at_ref[...] = at
    a_ref[...] = jnp.concatenate([at, jnp.zeros((LANES - 2 * TOP_K, T), _F32)], axis=0).T
    cnt_ref[...] = jnp.broadcast_to(cnt, (N_EXPERTS, LANES)).astype(jnp.int32)


def _route_call(lg, *, T):
    N = lg.shape[0]
    nt = N // T
    return pl.pallas_call(
        _route_kernel, grid=(nt,),
        in_specs=[pl.BlockSpec((T, LANES), lambda i: (i, 0))],
        out_specs=[pl.BlockSpec((T, LANES), lambda i: (i, 0)), pl.BlockSpec((8, T), lambda i: (0, i)),
                   pl.BlockSpec((None, N_EXPERTS, LANES), lambda i: (i, 0, 0))],
        out_shape=[jax.ShapeDtypeStruct((N, LANES), _F32), jax.ShapeDtypeStruct((8, N), _F32),
                   jax.ShapeDtypeStruct((nt, N_EXPERTS, LANES), jnp.int32)],
        compiler_params=_cparams(("arbitrary",)), name="route",
    )(lg)


def _div_pow2(v, d):
    assert d & (d - 1) == 0
    return lax.shift_right_logical(v, d.bit_length() - 1)


def _plan_kernel(cnt_ref, goff_ref, lbase_ref, nch_ref, eblk_ref, gap_ref, info_ref, *, nt, block_rows):
    def per_expert(e, base):
        def per_tile(i, run):
            n = _div_pow2(cnt_ref[i * N_EXPERTS + e] + (RUN_ALIGN - 1), RUN_ALIGN)
            goff_ref[i * N_EXPERTS + e] = run
            nch_ref[i * N_EXPERTS + e] = n
            return run + n * RUN_ALIGN
        end = lax.fori_loop(0, nt, per_tile, base)
        nblk = _div_pow2(end - base + (block_rows - 1), block_rows)
        nxt = base + nblk * block_rows
        gap_ref[2 * e] = end
        gap_ref[2 * e + 1] = _div_pow2(nxt - end, RUN_ALIGN)
        eblk_ref[2 * e] = _div_pow2(base, block_rows)
        eblk_ref[2 * e + 1] = nblk
        return nxt
    total = lax.fori_loop(0, N_EXPERTS, per_expert, 0)
    info_ref[0] = _div_pow2(total, block_rows)

    def per_tile2(i, c):
        def per_e(e, run):
            lbase_ref[i * N_EXPERTS + e] = run
            return run + nch_ref[i * N_EXPERTS + e] * RUN_ALIGN
        lax.fori_loop(0, N_EXPERTS, per_e, 0)
        return c
    lax.fori_loop(0, nt, per_tile2, 0)


def _plan_call(cnt_flat, *, nt, block_rows):
    smem = pl.BlockSpec(memory_space=pltpu.SMEM)
    i32 = lambda n: jax.ShapeDtypeStruct((n,), jnp.int32)
    return pl.pallas_call(
        functools.partial(_plan_kernel, nt=nt, block_rows=block_rows),
        in_specs=[smem], out_specs=[smem] * 6,
        out_shape=[i32(nt * N_EXPERTS), i32(nt * N_EXPERTS), i32(nt * N_EXPERTS), i32(2 * N_EXPERTS), i32(2 * N_EXPERTS),
                   i32(1)],
        name="plan",
    )(cnt_flat)


def _pack_pairs(v):
    w = v.shape[1] // 2
    return pltpu.pack_elementwise([v[:, :w], v[:, w:]], packed_dtype=_BF)


def _unpack_pairs(wd):
    lo = pltpu.unpack_elementwise(wd, index=0, packed_dtype=_BF, unpacked_dtype=_F32)
    hi = pltpu.unpack_elementwise(wd, index=1, packed_dtype=_BF, unpacked_dtype=_F32)
    return lo.astype(_BF), hi.astype(_BF)


def _dispatch_kernel(goff, lbase, nch, gap, info, at_ref, h2_ref, xs_ref, xl_ref, z_ref, sem, zsem, tsem):
    i = pl.program_id(0)
    R = xl_ref.shape[1]
    T = h2_ref.shape[0]
    block_rows = z_ref.shape[0]
    nb = xs_ref.shape[0] // block_rows

    @pl.when(i == 0)
    def _():
        z_ref[...] = jnp.zeros(z_ref.shape, z_ref.dtype)

        def gap_copy(e):
            rows = pl.multiple_of(gap[2 * e + 1] * RUN_ALIGN, RUN_ALIGN)
            return pltpu.make_async_copy(z_ref.at[pl.ds(0, rows)],
                                         xs_ref.at[pl.ds(pl.multiple_of(gap[2 * e], RUN_ALIGN), rows)], zsem)

        def fill(e, c):
            @pl.when(gap[2 * e + 1] > 0)
            def _():
                gap_copy(e).start()
            return c
        lax.fori_loop(0, N_EXPERTS, fill, 0)

        def fill_wait(e, c):
            @pl.when(gap[2 * e + 1] > 0)
            def _():
                gap_copy(e).wait()
            return c
        lax.fori_loop(0, N_EXPERTS, fill_wait, 0)

        def tail(b, c):
            pltpu.make_async_copy(z_ref, xs_ref.at[pl.ds(pl.multiple_of(b * block_rows, block_rows), block_rows)],
                                  tsem).start()
            return c
        lax.fori_loop(info[0], nb, tail, 0)

        def tail_wait(b, c):
            pltpu.make_async_copy(z_ref, xs_ref.at[pl.ds(0, block_rows)], tsem).wait()
            return c
        lax.fori_loop(info[0], nb, tail_wait, 0)

    slot_iota = lax.broadcasted_iota(jnp.int32, (R, T), 0).astype(_F32)
    pm = jnp.zeros((R, T), _F32)
    for k in range(TOP_K):
        pm = pm + jnp.where(slot_iota == at_ref[k:k + 1, :], 1.0, 0.0)
    xl = jnp.dot(pm.astype(_BF), h2_ref[...], preferred_element_type=_F32)
    slot = i & 1

    def wait_tile(t, s):
        total = lax.fori_loop(0, N_EXPERTS, lambda e, c: c + nch[t * N_EXPERTS + e], 0)

        @pl.when(total > 0)
        def _():
            rows = pl.multiple_of(total * RUN_ALIGN, RUN_ALIGN)
            pltpu.make_async_copy(xl_ref.at[s, pl.ds(0, rows)], xs_ref.at[pl.ds(0, rows)], sem.at[s]).wait()

    @pl.when(i >= 2)
    def _():
        wait_tile(i - 2, slot)
    xl_ref[slot] = _pack_pairs(xl)

    def issue(e, c):
        n = nch[i * N_EXPERTS + e]

        @pl.when(n > 0)
        def _():
            rows = pl.multiple_of(n * RUN_ALIGN, RUN_ALIGN)
            pltpu.make_async_copy(xl_ref.at[slot, pl.ds(pl.multiple_of(lbase[i * N_EXPERTS + e], RUN_ALIGN), rows)],
                                  xs_ref.at[pl.ds(pl.multiple_of(goff[i * N_EXPERTS + e], RUN_ALIGN), rows)],
                                  sem.at[slot]).start()
        return c
    lax.fori_loop(0, N_EXPERTS, issue, 0)

    @pl.when(i == pl.num_programs(0) - 1)
    def _():
        @pl.when(i >= 1)
        def _():
            wait_tile(i - 1, 1 - slot)
        wait_tile(i, slot)


def _dispatch_call(goff, lbase, nch, gap, info, at, h2, *, T, R, rows, block_rows):
    N, D = h2.shape
    grid_spec = pltpu.PrefetchScalarGridSpec(
        num_scalar_prefetch=5, grid=(N // T,),
        in_specs=[pl.BlockSpec((8, T), lambda i, *_: (0, i)), pl.BlockSpec((T, D), lambda i, *_: (i, 0))],
        out_specs=pl.BlockSpec(memory_space=pl.ANY),
        scratch_shapes=[pltpu.VMEM((2, R, D // 2), jnp.uint32), pltpu.VMEM((block_rows, D // 2), jnp.uint32),
                        pltpu.SemaphoreType.DMA((2,)), pltpu.SemaphoreType.DMA, pltpu.SemaphoreType.DMA])
    return pl.pallas_call(
        _dispatch_kernel, grid_spec=grid_spec, out_shape=jax.ShapeDtypeStruct((rows, D // 2), jnp.uint32),
        compiler_params=_cparams(("arbitrary",)), name="dispatch",
    )(goff, lbase, nch, gap, info, at, h2)


def _expert_kernel(eblk, info, xs_ref, wu_ref, bu_ref, wd_ref, bd_ref, ys_ref, wub_ref, wdb_ref, xb_ref, yb_ref, xsem, ysem):
    e = pl.program_id(0)
    block_rows = xb_ref.shape[1]
    nb = ys_ref.shape[0] // block_rows
    b0 = eblk[2 * e]
    n = eblk[2 * e + 1]
    half = wu_ref.shape[0] // 2
    wub_ref[...] = wu_ref[...].astype(_BF)
    wdb_ref[...] = wd_ref[...].astype(_BF)

    def rows_of(blk):
        return pl.ds(pl.multiple_of(blk * block_rows, block_rows), block_rows)

    def x_copy(j, slot):
        return pltpu.make_async_copy(xs_ref.at[rows_of(b0 + j)], xb_ref.at[slot], xsem.at[slot])

    def y_copy(j, slot):
        return pltpu.make_async_copy(yb_ref.at[slot], ys_ref.at[rows_of(b0 + j)], ysem.at[slot])

    @pl.when((e == 0) & (n > 0))
    def _():
        x_copy(0, 0).start(priority=1)

    def body(j, c):
        slot = j & 1
        x_copy(j, slot).wait()

        @pl.when(j + 1 < n)
        def _():
            x_copy(j + 1, 1 - slot).start(priority=1)

        @pl.when(j >= 2)
        def _():
            y_copy(j - 2, slot).wait()

        x_lo, x_hi = _unpack_pairs(xb_ref[slot])
        gu = jnp.dot(x_lo, wub_ref[:half, :], preferred_element_type=_F32) \
            + jnp.dot(x_hi, wub_ref[half:, :], preferred_element_type=_F32) + bu_ref[...]
        g = jnp.minimum(gu[:, :D_FF], SWIGLU_LIMIT)
        u = jnp.clip(gu[:, D_FF:], -SWIGLU_LIMIT, SWIGLU_LIMIT)
        act = (u + 1.0) * (g * jax.nn.sigmoid(SWIGLU_ALPHA * g))
        y = jnp.dot(act.astype(_BF), wdb_ref[...], preferred_element_type=_F32) + bd_ref[...]
        yb_ref[slot] = _pack_pairs(y)
        y_copy(j, slot).start()
        return c
    lax.fori_loop(0, n, body, 0)

    @pl.when(n >= 2)
    def _():
        y_copy(n - 2, n & 1).wait()

    @pl.when(n >= 1)
    def _():
        y_copy(n - 1, (n - 1) & 1).wait()

    last = pl.num_programs(0) - 1
    e_next = jnp.minimum(e + 1, last)

    @pl.when((e < last) & (eblk[2 * e_next + 1] > 0))
    def _():
        pltpu.make_async_copy(xs_ref.at[rows_of(eblk[2 * e_next])], xb_ref.at[0], xsem.at[0]).start(priority=1)

    @pl.when(e == last)
    def _():
        yb_ref[0] = _pack_pairs(jnp.zeros((block_rows, 2 * yb_ref.shape[2]), _F32))

        def tail_copy(b):
            return pltpu.make_async_copy(yb_ref.at[0], ys_ref.at[pl.ds(pl.multiple_of(b * block_rows, block_rows), block_rows)],
                                         ysem.at[0])

        def tail(b, c):
            tail_copy(b).start()
            return c
        lax.fori_loop(info[0], nb, tail, 0)

        def tail_wait(b, c):
            tail_copy(b).wait()
            return c
        lax.fori_loop(info[0], nb, tail_wait, 0)


def _expert_call(eblk, info, xs, wu, bu, wd, bd, *, block_rows):
    rows, half = xs.shape
    D = 2 * half
    ex = lambda e, *_: (e, 0, 0)
    grid_spec = pltpu.PrefetchScalarGridSpec(
        num_scalar_prefetch=2, grid=(N_EXPERTS,),
        in_specs=[pl.BlockSpec(memory_space=pl.ANY),
                  pl.BlockSpec((None, D, 2 * D_FF), ex), pl.BlockSpec((None, 1, 2 * D_FF), ex),
                  pl.BlockSpec((None, D_FF, D), ex), pl.BlockSpec((None, 1, D), ex)],
        out_specs=pl.BlockSpec(memory_space=pl.ANY),
        scratch_shapes=[pltpu.VMEM((D, 2 * D_FF), _BF), pltpu.VMEM((D_FF, D), _BF),
                        pltpu.VMEM((2, block_rows, half), jnp.uint32), pltpu.VMEM((2, block_rows, half), jnp.uint32),
                        pltpu.SemaphoreType.DMA((2,)), pltpu.SemaphoreType.DMA((2,))])
    return pl.pallas_call(
        _expert_kernel, grid_spec=grid_spec, out_shape=jax.ShapeDtypeStruct((rows, half), jnp.uint32),
        compiler_params=_cparams(("arbitrary",)), name="experts",
    )(eblk, info, xs, wu, bu, wd, bd)


def _combine_kernel(goff, lbase, nch, a_ref, hres_ref, ys_ref, o_ref, yl_ref, sem):
    i = pl.program_id(0)
    nt = pl.num_programs(0)
    R = yl_ref.shape[1]
    T = a_ref.shape[0]

    def fetch(t, slot):
        def issue(e, c):
            n = nch[t * N_EXPERTS + e]

            @pl.when(n > 0)
            def _():
                rows = pl.multiple_of(n * RUN_ALIGN, RUN_ALIGN)
                pltpu.make_async_copy(
                    ys_ref.at[pl.ds(pl.multiple_of(goff[t * N_EXPERTS + e], RUN_ALIGN), rows)],
                    yl_ref.at[slot, pl.ds(pl.multiple_of(lbase[t * N_EXPERTS + e], RUN_ALIGN), rows)], sem.at[slot]).start()
            return c
        lax.fori_loop(0, N_EXPERTS, issue, 0)

    @pl.when(i == 0)
    def _():
        yl_ref[...] = jnp.zeros(yl_ref.shape, yl_ref.dtype)
        fetch(0, 0)

    slot = i & 1

    @pl.when(i + 1 < nt)
    def _():
        fetch(i + 1, 1 - slot)

    total = lax.fori_loop(0, N_EXPERTS, lambda e, c: c + nch[i * N_EXPERTS + e], 0)

    @pl.when(total > 0)
    def _():
        rows = pl.multiple_of(total * RUN_ALIGN, RUN_ALIGN)
        pltpu.make_async_copy(ys_ref.at[pl.ds(0, rows)], yl_ref.at[slot, pl.ds(0, rows)], sem.at[slot]).wait()

    a = a_ref[...]
    slot_iota = lax.broadcasted_iota(jnp.int32, (T, R), 1).astype(_F32)
    pt = jnp.zeros((T, R), _F32)
    for k in range(TOP_K):
        pt = pt + jnp.where(slot_iota == a[:, k:k + 1], a[:, TOP_K + k:TOP_K + k + 1], 0.0)
    pt = pt.astype(_BF)
    y_lo, y_hi = _unpack_pairs(yl_ref[slot])
    half = y_lo.shape[1]
    o_ref[:, :half] = hres_ref[:, :half] + jnp.dot(pt, y_lo, preferred_element_type=_F32)
    o_ref[:, half:] = hres_ref[:, half:] + jnp.dot(pt, y_hi, preferred_element_type=_F32)


def _combine_call(goff, lbase, nch, a, hres, ys, *, T, R):
    N, D = hres.shape
    grid_spec = pltpu.PrefetchScalarGridSpec(
        num_scalar_prefetch=3, grid=(N // T,),
        in_specs=[pl.BlockSpec((T, LANES), lambda i, *_: (i, 0)), pl.BlockSpec((T, D), lambda i, *_: (i, 0)),
                  pl.BlockSpec(memory_space=pl.ANY)],
        out_specs=pl.BlockSpec((T, D), lambda i, *_: (i, 0)),
        scratch_shapes=[pltpu.VMEM((2, R, D // 2), jnp.uint32), pltpu.SemaphoreType.DMA((2,))])
    return pl.pallas_call(
        _combine_kernel, grid_spec=grid_spec, out_shape=jax.ShapeDtypeStruct((N, D), _F32),
        compiler_params=_cparams(("arbitrary",)), name="combine",
    )(goff, lbase, nch, a, hres, ys)


def _tiles(S):
    pick = lambda pref: next(t for t in pref if S % t == 0)
    tm = pick((512, 256, 128))
    return dict(tm=tm, tq_fox=tm, tq_swa=pick((256, 128)),
                tm_mix=pick((512, 256, 128)), t_route=pick((256, 128)), block_rows=256)


def kernel(x, meta_tokens, norm1_gain, w_in, b_forget, fox_q_gain, fox_k_gain, swa_q_gain, swa_k_gain, swa_sinks,
           w_fox_out, w_swa_out, w_out, norm2_gain, w_router, b_router, w_up, b_up, w_down, b_down):
    B, S, D = x.shape
    assert norm1_gain.shape[0] == 1 and S % LANES == 0
    tl = _tiles(S)
    N = B * S
    scale = HEAD_DIM ** -0.5

    w = w_in[0]
    c0 = 3 * FOX_W
    c1 = c0 + FOX_HEADS
    c2 = c1 + SWA_QW + 2 * SWA_KW
    wqk = w[:, :2 * FOX_W].astype(_BF)
    wvt = w[:, 2 * FOX_W:c0].T.astype(_BF)
    wfl = w[:, c0:c1].T.astype(_BF)
    perm_heads = np.array([h for p in range(SWA_HEADS // 2) for h in (p, p + SWA_HEADS // 2)])
    perm_cols = (perm_heads[:, None] * HEAD_DIM + np.arange(HEAD_DIM)[None, :]).reshape(-1)
    ws_all = w[:, c1:c2]
    ws = jnp.concatenate([ws_all[:, :SWA_QW][:, perm_cols], ws_all[:, SWA_QW:SWA_QW + SWA_KW]], axis=1).astype(_BF)
    wsvt = ws_all[:, SWA_QW + SWA_KW:].T.astype(_BF)
    wga = w[:, c2:c2 + D].astype(_BF)
    wgb = w[:, c2 + D:].astype(_BF)
    g1 = norm1_gain[0][None, :]
    g2 = norm2_gain[0][None, :]
    bfg = b_forget[0][:, None]
    fqg = jnp.tile(fox_q_gain[0], FOX_HEADS)[None, :] * (scale * LOG2E)
    fkg = jnp.tile(fox_k_gain[0], FOX_HEADS)[None, :]
    sqg = jnp.tile(swa_q_gain[0], SWA_HEADS)[None, :] * scale
    skg = jnp.tile(swa_k_gain[0], SWA_KV_HEADS)[None, :]
    wfo = w_fox_out[0].astype(_BF)
    wso = w_swa_out[0][perm_cols, :].astype(_BF)
    wo = w_out[0].astype(_BF)
    wr = jnp.pad(w_router[0], ((0, 0), (0, LANES - N_EXPERTS)))
    br = jnp.pad(b_router[0], (0, LANES - N_EXPERTS), constant_values=NEG_INF)[None, :]
    sinks = swa_sinks[0].astype(_F32)
    wu = w_up[0]
    wd = w_down[0]
    bu = b_up[0][:, None, :]
    bd = b_down[0][:, None, :]

    meta_blk = jnp.pad(meta_tokens.astype(x.dtype), ((META_PAD, 0), (0, 0)))[None]
    zero_f = jnp.zeros((FOX_HEADS, 1), _F32)
    _, fk_m, aug_m, fvt_m, _, sk_m, svt_m, fend_m = _qkv_call(meta_blk, g1, wqk, wvt, wfl, ws, wsvt, bfg, fqg, fkg, sqg,
                                                              skg, zero_f, tm=META_BLOCK, n_pad=META_PAD)
    fq, fk, aug, fvt, sq, sk, svt, _ = _qkv_call(x, g1, wqk, wvt, wfl, ws, wsvt, bfg, fqg, fkg, sqg, skg, fend_m[0],
                                                 tm=tl["tm"], n_pad=0)
    o_a = _fox_call(fq, fk, aug, fvt, fk_m[0], aug_m[0], fvt_m[0, :, 0], tq=tl["tq_fox"])
    o_b = _swa_call(sinks, sq, sk, svt, sk_m[0], svt_m[0, :, 0], tq=tl["tq_swa"])

    hres, h2, lg = _mix_call(x.reshape(N, D), o_a.reshape(N, FOX_W), o_b.reshape(N, SWA_QW),
                             g1, wga, wgb, wfo, wso, wo, g2, wr, br, tm=tl["tm_mix"])

    T = tl["t_route"]
    nt = N // T
    block_rows = tl["block_rows"]
    R = -(-(TOP_K * T + N_EXPERTS * (RUN_ALIGN - 1)) // LANES) * LANES
    max_rows = N * TOP_K + nt * N_EXPERTS * (RUN_ALIGN - 1) + N_EXPERTS * (block_rows - RUN_ALIGN)
    nb = -(-max_rows // block_rows)
    a, at, cnt = _route_call(lg, T=T)
    goff, lbase, nch, eblk, gap, info = _plan_call(cnt[:, :, 0].reshape(-1), nt=nt, block_rows=block_rows)
    xs = _dispatch_call(goff, lbase, nch, gap, info, at, h2, T=T, R=R, rows=nb * block_rows, block_rows=block_rows)
    ys = _expert_call(eblk, info, xs, wu, bu, wd, bd, block_rows=block_rows)
    out = _combine_call(goff, lbase, nch, a, hres, ys, T=T, R=R)
    return out.reshape(B, S, D)
```

```python
import functools

import jax
import jax.numpy as jnp
import numpy as np
from jax import lax
from jax.experimental import pallas as pl
from jax.experimental.pallas import tpu as pltpu

HEAD_DIM = 64
N_META = 16
FOX_HEADS = 8
SWA_HEADS = 8
SWA_KV_HEADS = 2
WINDOW = 128
N_EXPERTS = 32
TOP_K = 4
D_FF = 1024
SWIGLU_LIMIT = 7.0
SWIGLU_ALPHA = 1.702
RMS_EPS = 1e-6
NEG_INF = -1e30
LOG2E = 1.4426950408889634

LANES = 128
META_BLOCK = 128
META_PAD = META_BLOCK - N_META
W_CHUNKS = 8
FOX_PAIRS = 4
RUN_ALIGN = 8
VMEM_LIMIT = 56 * 1024 * 1024

FOX_W = FOX_HEADS * HEAD_DIM
SWA_QW = SWA_HEADS * HEAD_DIM
SWA_KW = SWA_KV_HEADS * HEAD_DIM

_BF = jnp.bfloat16
_F32 = jnp.float32


def _cparams(sem):
    return pltpu.CompilerParams(dimension_semantics=sem, vmem_limit_bytes=VMEM_LIMIT)


def _rms(t, gain):
    return t * lax.rsqrt(jnp.mean(t * t, axis=-1, keepdims=True) + RMS_EPS) * gain


def _head_rms(z, gain_row):
    lane = lax.broadcasted_iota(jnp.int32, (z.shape[0], LANES), 1)
    lo_mask = lane < HEAD_DIM
    outs = []
    for b in range(z.shape[1] // LANES):
        v = z[:, b * LANES:(b + 1) * LANES]
        v2 = v * v
        tot = jnp.sum(v2, axis=-1, keepdims=True)
        lo = jnp.sum(jnp.where(lo_mask, v2, 0.0), axis=-1, keepdims=True)
        hi = tot - lo
        r_lo = lax.rsqrt(lo * (1.0 / HEAD_DIM) + RMS_EPS)
        r_hi = lax.rsqrt(hi * (1.0 / HEAD_DIM) + RMS_EPS)
        outs.append(v * jnp.where(lo_mask, r_lo, r_hi))
    return jnp.concatenate(outs, axis=-1) * gain_row


def _lane_cumsum(x):
    n = x.shape[-1]
    lane = lax.broadcasted_iota(jnp.int32, x.shape, x.ndim - 1)
    s = 1
    while s < n:
        x = x + jnp.where(lane >= s, pltpu.roll(x, s, x.ndim - 1), 0.0)
        s *= 2
    return x


def _qkv_kernel(x_ref, g1_ref, wqk_ref, wvt_ref, wfl_ref, ws_ref, wsvt_ref, bf_ref, fqg_ref, fkg_ref, sqg_ref, skg_ref,
                f0_ref, fq_ref, fk_ref, aug_ref, fvt_ref, sq_ref, sk_ref, svt_ref, fend_ref, carry_ref, *, n_pad):
    j = pl.program_id(1)

    @pl.when(j == 0)
    def _():
        carry_ref[...] = f0_ref[...]

    x = x_ref[...]
    tm = x.shape[0]
    h = _rms(x, g1_ref[...]).astype(_BF)
    nt = (((1,), (1,)), ((), ()))
    f = jnp.dot(h, wqk_ref[...], preferred_element_type=_F32)
    s = jnp.dot(h, ws_ref[...], preferred_element_type=_F32)
    svt = lax.dot_general(wsvt_ref[...], h, nt, preferred_element_type=_F32)
    vt = lax.dot_general(wvt_ref[...], h, nt, preferred_element_type=_F32)
    flog = lax.dot_general(wfl_ref[...], h, nt, preferred_element_type=_F32)

    fq_ref[...] = _head_rms(f[:, :FOX_W], fqg_ref[...]).astype(_BF)
    fk_ref[...] = _head_rms(f[:, FOX_W:], fkg_ref[...]).astype(_BF)
    sq_ref[...] = _head_rms(s[:, :SWA_QW], sqg_ref[...]).astype(_BF)
    sk_ref[...] = _head_rms(s[:, SWA_QW:], skg_ref[...]).astype(_BF)

    ones_row = jnp.where(lax.broadcasted_iota(jnp.int32, (HEAD_DIM, tm), 0) == 0, 1.0, 0.0).astype(_BF)
    for hd in range(FOX_HEADS):
        fvt_ref[hd] = jnp.concatenate([vt[hd * HEAD_DIM:(hd + 1) * HEAD_DIM].astype(_BF), ones_row], axis=0)
    for g in range(SWA_KV_HEADS):
        blk = jnp.concatenate([svt[g * HEAD_DIM:(g + 1) * HEAD_DIM].astype(_BF), ones_row], axis=0)
        for c in range(tm // WINDOW):
            svt_ref[g, c] = blk[:, c * WINDOW:(c + 1) * WINDOW]

    z = flog + bf_ref[...]
    log_f = jnp.minimum(z, 0.0) - jnp.log1p(jnp.exp(-jnp.abs(z)))
    if n_pad:
        lane = lax.broadcasted_iota(jnp.int32, log_f.shape, 1)
        log_f = jnp.where(lane >= n_pad, log_f, 0.0)
    cum = _lane_cumsum(log_f) + carry_ref[...]
    carry_ref[...] = cum[:, -1:]
    fend_ref[...] = cum[:, -1:]
    nfl = cum * (-LOG2E)
    p1 = nfl.astype(_BF).astype(_F32)
    p2 = (nfl - p1).astype(_BF).astype(_F32)
    p3 = (nfl - p1 - p2).astype(_BF).astype(_F32)
    g = jnp.concatenate([p1, p2, p3, jnp.zeros((LANES - 3 * FOX_HEADS, tm), _F32)], axis=0)
    aug_ref[...] = g.T.astype(_BF)


def _qkv_call(x3, g1, wqk, wvt, wfl, ws, wsvt, bfg, fqg, fkg, sqg, skg, f0, *, tm, n_pad):
    B, S, D = x3.shape
    nj = S // tm
    row = lambda w: pl.BlockSpec((None, tm, w), lambda b, j: (b, j, 0))
    full = lambda a: pl.BlockSpec(a.shape, lambda b, j: (0,) * a.ndim)
    bsd = lambda w: jax.ShapeDtypeStruct((B, S, w), _BF)
    out_shape = [bsd(FOX_W), bsd(FOX_W), bsd(LANES), jax.ShapeDtypeStruct((B, FOX_HEADS, nj, 2 * HEAD_DIM, tm), _BF),
                 bsd(SWA_QW), bsd(SWA_KW),
                 jax.ShapeDtypeStruct((B, SWA_KV_HEADS, S // WINDOW, 2 * HEAD_DIM, WINDOW), _BF),
                 jax.ShapeDtypeStruct((B, FOX_HEADS, 1), _F32)]
    out_specs = [row(FOX_W), row(FOX_W), row(LANES),
                 pl.BlockSpec((None, FOX_HEADS, None, 2 * HEAD_DIM, tm), lambda b, j: (b, 0, j, 0, 0)),
                 row(SWA_QW), row(SWA_KW),
                 pl.BlockSpec((None, SWA_KV_HEADS, tm // WINDOW, 2 * HEAD_DIM, WINDOW), lambda b, j: (b, 0, j, 0, 0)),
                 pl.BlockSpec((None, FOX_HEADS, 1), lambda b, j: (b, 0, 0))]
    return pl.pallas_call(
        functools.partial(_qkv_kernel, n_pad=n_pad),
        grid=(B, nj),
        in_specs=[row(D)] + [full(a) for a in (g1, wqk, wvt, wfl, ws, wsvt, bfg, fqg, fkg, sqg, skg, f0)],
        out_specs=out_specs, out_shape=out_shape,
        scratch_shapes=[pltpu.VMEM((FOX_HEADS, 1), _F32)],
        compiler_params=_cparams(("arbitrary", "arbitrary")),
        name="qkv_proj",
    )(x3, g1, wqk, wvt, wfl, ws, wsvt, bfg, fqg, fkg, sqg, skg, f0)


def _fox_kernel(q_ref, k_ref, aug_ref, vt_ref, km_ref, augm_ref, vtm_ref, o_ref, m_ref, acc_ref, s_ref, mx_ref):
    g = pl.program_id(1)
    qi = pl.program_id(2)
    tq = q_ref.shape[0]
    tk = vt_ref.shape[-1]
    nt = (((1,), (1,)), ((), ()))
    lane = lax.broadcasted_iota(jnp.int32, (tq, LANES), 1)

    qaugs = []
    for pp in range(FOX_PAIRS):
        q = q_ref[:, pp * LANES:(pp + 1) * LANES]
        parts = []
        for a in range(2):
            head = 2 * (FOX_PAIRS * g + pp) + a
            keep = (lane < HEAD_DIM) if a == 0 else (lane >= HEAD_DIM)
            ones = jnp.where((lane < 3 * FOX_HEADS) & ((lane & (FOX_HEADS - 1)) == head), 1.0, 0.0).astype(_BF)
            parts.append(jnp.concatenate([jnp.where(keep, q, jnp.zeros_like(q)), ones], axis=1))
        qaugs.append(jnp.concatenate(parts, axis=0))

    def scores(pp, k, aug, mask):
        st = lax.dot_general(jnp.concatenate([k, aug], axis=1), qaugs[pp], nt, preferred_element_type=_F32)
        if mask is not None:
            st = jnp.where(mask, st, NEG_INF)
        return st, jnp.max(st, axis=0, keepdims=True)

    def tile_scores(pp, ki, mask=None):
        off = pl.multiple_of(ki * tk, tk)
        return scores(pp, k_ref[pl.ds(off, tk), pp * LANES:(pp + 1) * LANES], aug_ref[pl.ds(off, tk), :], mask)

    def process(pp, st, mx, vts, first):
        m_new = mx if first else jnp.maximum(m_ref[pp], mx)
        pt = jnp.exp2(st - m_new).astype(_BF)
        if not first:
            alpha = jnp.exp2(m_ref[pp] - m_new)
        for a in range(2):
            pv = jnp.dot(vts[a], pt[:, a * tq:(a + 1) * tq], preferred_element_type=_F32)
            acc_ref[2 * pp + a] = pv if first else alpha[:, a * tq:(a + 1) * tq] * acc_ref[2 * pp + a] + pv
        m_ref[pp] = m_new

    def stash(buf, pp, st_mx):
        s_ref[buf, pp] = st_mx[0]
        mx_ref[buf, pp] = st_mx[1]

    def vts(pp, ki):
        return vt_ref[2 * pp, ki], vt_ref[2 * pp + 1, ki]

    causal = lax.broadcasted_iota(jnp.int32, (tk, 2 * tq), 0) <= (lax.broadcasted_iota(jnp.int32, (tk, 2 * tq), 1) & (tq - 1))
    mask_m = lax.broadcasted_iota(jnp.int32, (META_BLOCK, 2 * tq), 0) >= META_PAD
    for pp in range(FOX_PAIRS):
        stash(0, pp, tile_scores(pp, qi, causal))
        st, mx = scores(pp, km_ref[:, pp * LANES:(pp + 1) * LANES], augm_ref[...], mask_m)
        process(pp, st, mx, (vtm_ref[2 * pp], vtm_ref[2 * pp + 1]), True)

    def step(j, buf):
        nxt = [tile_scores(pp, j) for pp in range(FOX_PAIRS)]
        cur = jnp.where(j == 0, qi, j - 1)
        for pp in range(FOX_PAIRS):
            process(pp, s_ref[buf, pp], mx_ref[buf, pp], vts(pp, cur), False)
        for pp in range(FOX_PAIRS):
            stash(1 - buf, pp, nxt[pp])

    def body(jj, c):
        step(2 * jj, 0)
        step(2 * jj + 1, 1)
        return c
    lax.fori_loop(0, qi // 2, body, 0)

    def finish(buf):
        last = jnp.where(qi == 0, qi, qi - 1)
        outs = []
        for pp in range(FOX_PAIRS):
            process(pp, s_ref[buf, pp], mx_ref[buf, pp], vts(pp, last), False)
            for a in range(2):
                acc = acc_ref[2 * pp + a]
                outs.append(acc[:HEAD_DIM] / acc[HEAD_DIM:HEAD_DIM + 1])
        o_ref[...] = jnp.concatenate(outs, axis=0).T.astype(_BF)

    @pl.when(qi % 2 == 1)
    def _():
        step(qi - 1, 0)
        finish(1)

    @pl.when(qi % 2 == 0)
    def _():
        finish(0)


def _fox_call(fq, fk, aug, fvt, km, augm, vtm, *, tq):
    B, S, _ = fq.shape
    nk = fvt.shape[2]
    assert fvt.shape[-1] == tq
    w = FOX_PAIRS * LANES
    qspec = pl.BlockSpec((None, tq, w), lambda b, g, i: (b, i, g))
    return pl.pallas_call(
        _fox_kernel, grid=(B, FOX_HEADS // (2 * FOX_PAIRS), S // tq),
        in_specs=[qspec,
                  pl.BlockSpec((None, S, w), lambda b, g, i: (b, 0, g)),
                  pl.BlockSpec((None, S, LANES), lambda b, g, i: (b, 0, 0)),
                  pl.BlockSpec((None, 2 * FOX_PAIRS, nk, 2 * HEAD_DIM, tq), lambda b, g, i: (b, g, 0, 0, 0)),
                  pl.BlockSpec((META_BLOCK, w), lambda b, g, i: (0, g)),
                  pl.BlockSpec((META_BLOCK, LANES), lambda b, g, i: (0, 0)),
                  pl.BlockSpec((2 * FOX_PAIRS, 2 * HEAD_DIM, META_BLOCK), lambda b, g, i: (g, 0, 0))],
        out_specs=qspec, out_shape=jax.ShapeDtypeStruct((B, S, FOX_W), _BF),
        scratch_shapes=[pltpu.VMEM((FOX_PAIRS, 1, 2 * tq), _F32), pltpu.VMEM((2 * FOX_PAIRS, 2 * HEAD_DIM, tq), _F32),
                        pltpu.VMEM((2, FOX_PAIRS, tq, 2 * tq), _F32), pltpu.VMEM((2, FOX_PAIRS, 1, 2 * tq), _F32)],
        compiler_params=_cparams(("arbitrary", "arbitrary", "arbitrary")), name="fox_attn",
    )(fq, fk, aug, fvt, km, augm, vtm)


def _swa_bias(tq):
    j = np.arange(WINDOW + tq)[:, None] - WINDOW
    i = np.arange(tq)[None, :]
    dist = i - j
    ok = (dist >= 0) & (dist < WINDOW)
    slopes = np.exp2(-8.0 * (np.arange(SWA_HEADS, dtype=np.float32) + 1.0) / SWA_HEADS).astype(np.float32)
    per_head = np.where(ok[None], -slopes[:, None, None] * dist[None].astype(np.float32), np.float32(NEG_INF))
    rep = SWA_HEADS // SWA_KV_HEADS
    return np.stack([np.concatenate(list(per_head[g * rep:(g + 1) * rep]), axis=1) for g in range(SWA_KV_HEADS)])


def _swa_kernel(q_ref, kc_ref, kp_ref, km_ref, vc_ref, vp_ref, vm_ref, bias_ref, sink_ref, o_ref):
    qi = pl.program_id(1)
    tq = q_ref.shape[0]
    rep = SWA_HEADS // SWA_KV_HEADS
    nt = (((1,), (1,)), ((), ()))
    first = qi == 0
    kcat = jnp.concatenate([jnp.where(first, km_ref[...], kp_ref[...]), kc_ref[...]], axis=0)
    row = lax.broadcasted_iota(jnp.int32, (WINDOW + tq, rep * tq), 0)
    row_ok = row >= jnp.where(first, META_PAD, 0)
    lane = lax.broadcasted_iota(jnp.int32, (tq, LANES), 1)
    outs = [None] * SWA_HEADS
    for g in range(SWA_KV_HEADS):
        keep = (lane < HEAD_DIM) if g == 0 else (lane >= HEAD_DIM)
        qg = jnp.concatenate([jnp.where(keep, q_ref[:, p * LANES:(p + 1) * LANES], jnp.zeros((tq, LANES), _BF))
                              for p in range(rep)], axis=0)
        vcat = jnp.concatenate([jnp.where(first, vm_ref[g], vp_ref[g, 0])] + [vc_ref[g, c] for c in range(tq // WINDOW)],
                               axis=1)
        st = lax.dot_general(kcat, qg, nt, preferred_element_type=_F32) + bias_ref[g]
        st = jnp.where(row_ok, st, NEG_INF)
        sink = sink_ref[g]
        m = jnp.maximum(jnp.max(st, axis=0, keepdims=True), sink)
        pt = jnp.exp(st - m).astype(_BF)
        acc = jnp.dot(vcat, pt, preferred_element_type=_F32)
        o = acc[:HEAD_DIM] / (acc[HEAD_DIM:HEAD_DIM + 1] + jnp.exp(sink - m))
        for p in range(rep):
            outs[2 * p + g] = o[:, p * tq:(p + 1) * tq]
    o_ref[...] = jnp.concatenate(outs, axis=0).T.astype(_BF)


def _swa_call(sinks, sq, sk, svt, km, vtm, *, tq):
    B, S, _ = sq.shape
    r = tq // WINDOW
    rep = SWA_HEADS // SWA_KV_HEADS
    bias = jnp.asarray(_swa_bias(tq))
    sink_rows = jnp.repeat(sinks.reshape(SWA_KV_HEADS, rep), tq, axis=1)[:, None, :]
    prev_blk = lambda i: jnp.maximum(i * r - 1, 0)
    return pl.pallas_call(
        _swa_kernel, grid=(B, S // tq),
        in_specs=[pl.BlockSpec((None, tq, SWA_QW), lambda b, i: (b, i, 0)),
                  pl.BlockSpec((None, tq, LANES), lambda b, i: (b, i, 0)),
                  pl.BlockSpec((None, WINDOW, LANES), lambda b, i: (b, prev_blk(i), 0)),
                  pl.BlockSpec((META_BLOCK, LANES), lambda b, i: (0, 0)),
                  pl.BlockSpec((None, SWA_KV_HEADS, r, 2 * HEAD_DIM, WINDOW), lambda b, i: (b, 0, i, 0, 0)),
                  pl.BlockSpec((None, SWA_KV_HEADS, 1, 2 * HEAD_DIM, WINDOW), lambda b, i: (b, 0, prev_blk(i), 0, 0)),
                  pl.BlockSpec((SWA_KV_HEADS, 2 * HEAD_DIM, META_BLOCK), lambda b, i: (0, 0, 0)),
                  pl.BlockSpec(bias.shape, lambda b, i: (0, 0, 0)),
                  pl.BlockSpec(sink_rows.shape, lambda b, i: (0, 0, 0))],
        out_specs=pl.BlockSpec((None, tq, SWA_QW), lambda b, i: (b, i, 0)),
        out_shape=jax.ShapeDtypeStruct((B, S, SWA_QW), _BF),
        compiler_params=_cparams(("arbitrary", "arbitrary")), name="swa_attn",
    )(sq, sk, sk, km, svt, svt, vtm, bias, sink_rows)


def _split3(v):
    a = v.astype(_BF)
    r = v - a.astype(_F32)
    b = r.astype(_BF)
    c = (r - b.astype(_F32)).astype(_BF)
    return a, b, c


def _mix_kernel(x_ref, oa_ref, ob_ref, g1_ref, wga_ref, wgb_ref, wfo_ref, wso_ref, wo_ref, g2_ref, wr_ref, br_ref,
                hres_ref, h2_ref, lg_ref):
    x = x_ref[...]
    h = _rms(x, g1_ref[...]).astype(_BF)
    ga = jax.nn.sigmoid(jnp.dot(h, wga_ref[...], preferred_element_type=_F32))
    gb = jax.nn.sigmoid(jnp.dot(h, wgb_ref[...], preferred_element_type=_F32))
    mix = ga * jnp.dot(oa_ref[...], wfo_ref[...], preferred_element_type=_F32) \
        + gb * jnp.dot(ob_ref[...], wso_ref[...], preferred_element_type=_F32)
    hres = x + jnp.dot(mix.astype(_BF), wo_ref[...], preferred_element_type=_F32)
    hres_ref[...] = hres
    h2 = _rms(hres, g2_ref[...])
    h2_ref[...] = h2.astype(_BF)
    wr = wr_ref[...]
    w_hi = wr.astype(_BF)
    w_lo = (wr - w_hi.astype(_F32)).astype(_BF)
    h_hi = h2.astype(_BF)
    h_lo = (h2 - h_hi.astype(_F32)).astype(_BF)
    lg = jnp.dot(h_hi, w_hi, preferred_element_type=_F32) + jnp.dot(h_hi, w_lo, preferred_element_type=_F32) \
        + jnp.dot(h_lo, w_hi, preferred_element_type=_F32)
    lg_ref[...] = lg + br_ref[...]


def _mix_call(x2, oa, ob, g1, wga, wgb, wfo, wso, wo, g2, wr, br, *, tm):
    N, D = x2.shape
    row = lambda w: pl.BlockSpec((tm, w), lambda i: (i, 0))
    full = lambda a: pl.BlockSpec(a.shape, lambda i: (0,) * a.ndim)
    return pl.pallas_call(
        _mix_kernel, grid=(N // tm,),
        in_specs=[row(D), row(FOX_W), row(SWA_QW)] + [full(a) for a in (g1, wga, wgb, wfo, wso, wo, g2, wr, br)],
        out_specs=[row(D), row(D), row(LANES)],
        out_shape=[jax.ShapeDtypeStruct((N, D), _F32), jax.ShapeDtypeStruct((N, D), _BF),
                   jax.ShapeDtypeStruct((N, LANES), _F32)],
        compiler_params=_cparams(("arbitrary",)), name="mix_proj",
    )(x2, oa, ob, g1, wga, wgb, wfo, wso, wo, g2, wr, br)


def _route_kernel(lg_ref, a_ref, at_ref, cnt_ref):
    T = lg_ref.shape[0]
    lt = lg_ref[...].T[:N_EXPERTS]
    e_iota = lax.broadcasted_iota(jnp.int32, lt.shape, 0)
    cur = lt
    vals, idxs = [], []
    for _ in range(TOP_K):
        m = jnp.max(cur, axis=0, keepdims=True)
        idx = jnp.min(jnp.where(cur == m, e_iota, N_EXPERTS), axis=0, keepdims=True)
        vals.append(m)
        idxs.append(idx)
        cur = jnp.where(e_iota == idx, -jnp.inf, cur)
    es = [jnp.exp(v - vals[0]) for v in vals]
    den = es[0] + es[1] + es[2] + es[3]
    gates = [e / den for e in es]

    onehot = jnp.zeros(lt.shape, _F32)
    for idx in idxs:
        onehot = onehot + jnp.where(e_iota == idx, 1.0, 0.0)
    r_i = lax.broadcasted_iota(jnp.int32, (T, T), 0)
    c_i = lax.broadcasted_iota(jnp.int32, (T, T), 1)
    utri = jnp.where(r_i < c_i, 1.0, 0.0).astype(_BF)
    prefix = jnp.dot(onehot.astype(_BF), utri, preferred_element_type=_F32)
    cnt = jnp.sum(onehot, axis=1, keepdims=True)
    chunks = jnp.floor((cnt + (RUN_ALIGN - 1)) * (1.0 / RUN_ALIGN))
    l_r = lax.broadcasted_iota(jnp.int32, (N_EXPERTS, N_EXPERTS), 0)
    l_c = lax.broadcasted_iota(jnp.int32, (N_EXPERTS, N_EXPERTS), 1)
    ltri = jnp.where(l_c < l_r, 1.0, 0.0).astype(_BF)
    lbase = RUN_ALIGN * jnp.dot(ltri, jnp.broadcast_to(chunks, (N_EXPERTS, LANES)).astype(_BF),
                                preferred_element_type=_F32)[:, 0:1]
    slot_all = prefix + lbase
    rows = [jnp.sum(jnp.where(e_iota == idx, slot_all, 0.0), axis=0, keepdims=True) for idx in idxs] + gates
    at = jnp.concatenate(rows, axis=0)
    at_ref[...] = at
    a_ref[...] = jnp.concatenate([at, jnp.zeros((LANES - 2 * TOP_K, T), _F32)], axis=0).T
    cnt_ref[...] = jnp.broadcast_to(cnt, (N_EXPERTS, LANES)).astype(jnp.int32)


def _route_call(lg, *, T):
    N = lg.shape[0]
    nt = N // T
    return pl.pallas_call(
        _route_kernel, grid=(nt,),
        in_specs=[pl.BlockSpec((T, LANES), lambda i: (i, 0))],
        out_specs=[pl.BlockSpec((T, LANES), lambda i: (i, 0)), pl.BlockSpec((8, T), lambda i: (0, i)),
                   pl.BlockSpec((None, N_EXPERTS, LANES), lambda i: (i, 0, 0))],
        out_shape=[jax.ShapeDtypeStruct((N, LANES), _F32), jax.ShapeDtypeStruct((8, N), _F32),
                   jax.ShapeDtypeStruct((nt, N_EXPERTS, LANES), jnp.int32)],
        compiler_params=_cparams(("arbitrary",)), name="route",
    )(lg)


def _div_pow2(v, d):
    assert d & (d - 1) == 0
    return lax.shift_right_logical(v, d.bit_length() - 1)


def _plan_kernel(cnt_ref, goff_ref, lbase_ref, nch_ref, eblk_ref, gap_ref, info_ref, *, nt, block_rows):
    def per_expert(e, base):
        def per_tile(i, run):
            n = _div_pow2(cnt_ref[i * N_EXPERTS + e] + (RUN_ALIGN - 1), RUN_ALIGN)
            goff_ref[i * N_EXPERTS + e] = run
            nch_ref[i * N_EXPERTS + e] = n
            return run + n * RUN_ALIGN
        end = lax.fori_loop(0, nt, per_tile, base)
        nblk = _div_pow2(end - base + (block_rows - 1), block_rows)
        nxt = base + nblk * block_rows
        gap_ref[2 * e] = end
        gap_ref[2 * e + 1] = _div_pow2(nxt - end, RUN_ALIGN)
        eblk_ref[2 * e] = _div_pow2(base, block_rows)
        eblk_ref[2 * e + 1] = nblk
        return nxt
    total = lax.fori_loop(0, N_EXPERTS, per_expert, 0)
    info_ref[0] = _div_pow2(total, block_rows)

    def per_tile2(i, c):
        def per_e(e, run):
            lbase_ref[i * N_EXPERTS + e] = run
            return run + nch_ref[i * N_EXPERTS + e] * RUN_ALIGN
        lax.fori_loop(0, N_EXPERTS, per_e, 0)
        return c
    lax.fori_loop(0, nt, per_tile2, 0)


def _plan_call(cnt_flat, *, nt, block_rows):
    smem = pl.BlockSpec(memory_space=pltpu.SMEM)
    i32 = lambda n: jax.ShapeDtypeStruct((n,), jnp.int32)
    return pl.pallas_call(
        functools.partial(_plan_kernel, nt=nt, block_rows=block_rows),
        in_specs=[smem], out_specs=[smem] * 6,
        out_shape=[i32(nt * N_EXPERTS), i32(nt * N_EXPERTS), i32(nt * N_EXPERTS), i32(2 * N_EXPERTS), i32(2 * N_EXPERTS),
                   i32(1)],
        name="plan",
    )(cnt_flat)


def _pack_pairs(v):
    w = v.shape[1] // 2
    return pltpu.pack_elementwise([v[:, :w], v[:, w:]], packed_dtype=_BF)


def _unpack_pairs(wd):
    lo = pltpu.unpack_elementwise(wd, index=0, packed_dtype=_BF, unpacked_dtype=_F32)
    hi = pltpu.unpack_elementwise(wd, index=1, packed_dtype=_BF, unpacked_dtype=_F32)
    return lo.astype(_BF), hi.astype(_BF)


def _dispatch_kernel(goff, lbase, nch, gap, info, at_ref, h2_ref, xs_ref, xl_ref, z_ref, sem, zsem, tsem):
    i = pl.program_id(0)
    R = xl_ref.shape[1]
    T = h2_ref.shape[0]
    block_rows = z_ref.shape[0]
    nb = xs_ref.shape[0] // block_rows

    @pl.when(i == 0)
    def _():
        z_ref[...] = jnp.zeros(z_ref.shape, z_ref.dtype)

        def gap_copy(e):
            rows = pl.multiple_of(gap[2 * e + 1] * RUN_ALIGN, RUN_ALIGN)
            return pltpu.make_async_copy(z_ref.at[pl.ds(0, rows)],
                                         xs_ref.at[pl.ds(pl.multiple_of(gap[2 * e], RUN_ALIGN), rows)], zsem)

        def fill(e, c):
            @pl.when(gap[2 * e + 1] > 0)
            def _():
                gap_copy(e).start()
            return c
        lax.fori_loop(0, N_EXPERTS, fill, 0)

        def fill_wait(e, c):
            @pl.when(gap[2 * e + 1] > 0)
            def _():
                gap_copy(e).wait()
            return c
        lax.fori_loop(0, N_EXPERTS, fill_wait, 0)

        def tail(b, c):
            pltpu.make_async_copy(z_ref, xs_ref.at[pl.ds(pl.multiple_of(b * block_rows, block_rows), block_rows)],
                                  tsem).start()
            return c
        lax.fori_loop(info[0], nb, tail, 0)

        def tail_wait(b, c):
            pltpu.make_async_copy(z_ref, xs_ref.at[pl.ds(0, block_rows)], tsem).wait()
            return c
        lax.fori_loop(info[0], nb, tail_wait, 0)

    slot_iota = lax.broadcasted_iota(jnp.int32, (R, T), 0).astype(_F32)
    pm = jnp.zeros((R, T), _F32)
    for k in range(TOP_K):
        pm = jnp.where(slot_iota == at_ref[k:k + 1, :], 1.0, pm)
    xl = jnp.dot(pm.astype(_BF), h2_ref[...], preferred_element_type=_F32)
    slot = i & 1

    def wait_tile(t, s):
        total = lax.fori_loop(0, N_EXPERTS, lambda e, c: c + nch[t * N_EXPERTS + e], 0)

        @pl.when(total > 0)
        def _():
            rows = pl.multiple_of(total * RUN_ALIGN, RUN_ALIGN)
            pltpu.make_async_copy(xl_ref.at[s, pl.ds(0, rows)], xs_ref.at[pl.ds(0, rows)], sem.at[s]).wait()

    @pl.when(i >= 2)
    def _():
        wait_tile(i - 2, slot)
    xl_ref[slot] = _pack_pairs(xl)

    def issue(e, c):
        n = nch[i * N_EXPERTS + e]

        @pl.when(n > 0)
        def _():
            rows = pl.multiple_of(n * RUN_ALIGN, RUN_ALIGN)
            pltpu.make_async_copy(xl_ref.at[slot, pl.ds(pl.multiple_of(lbase[i * N_EXPERTS + e], RUN_ALIGN), rows)],
                                  xs_ref.at[pl.ds(pl.multiple_of(goff[i * N_EXPERTS + e], RUN_ALIGN), rows)],
                                  sem.at[slot]).start()
        return c
    lax.fori_loop(0, N_EXPERTS, issue, 0)

    @pl.when(i == pl.num_programs(0) - 1)
    def _():
        @pl.when(i >= 1)
        def _():
            wait_tile(i - 1, 1 - slot)
        wait_tile(i, slot)


def _dispatch_call(goff, lbase, nch, gap, info, at, h2, *, T, R, rows, block_rows):
    N, D = h2.shape
    grid_spec = pltpu.PrefetchScalarGridSpec(
        num_scalar_prefetch=5, grid=(N // T,),
        in_specs=[pl.BlockSpec((8, T), lambda i, *_: (0, i)), pl.BlockSpec((T, D), lambda i, *_: (i, 0))],
        out_specs=pl.BlockSpec(memory_space=pl.ANY),
        scratch_shapes=[pltpu.VMEM((2, R, D // 2), jnp.uint32), pltpu.VMEM((block_rows, D // 2), jnp.uint32),
                        pltpu.SemaphoreType.DMA((2,)), pltpu.SemaphoreType.DMA, pltpu.SemaphoreType.DMA])
    return pl.pallas_call(
        _dispatch_kernel, grid_spec=grid_spec, out_shape=jax.ShapeDtypeStruct((rows, D // 2), jnp.uint32),
        compiler_params=_cparams(("arbitrary",)), name="dispatch",
    )(goff, lbase, nch, gap, info, at, h2)


def _expert_kernel(eblk, info, xs_ref, wu_hbm, bu_ref, wd_hbm, bd_ref, ys_ref,
                   wua_ref, wda_ref, wub_ref, wdb_ref, stu_ref, std_ref, xb_ref, yb_ref, xsem, ysem, usem, dsem):
    step = pl.program_id(0)
    n_exp = 2 * pl.num_programs(0)
    block_rows = xb_ref.shape[1]
    nb = ys_ref.shape[0] // block_rows
    half = wua_ref.shape[0] // 2
    cu = stu_ref.shape[1]
    cd = std_ref.shape[1]

    def rows_of(blk):
        return pl.ds(pl.multiple_of(blk * block_rows, block_rows), block_rows)

    def w_copies(en, c, s):
        return (pltpu.make_async_copy(wu_hbm.at[en, pl.ds(pl.multiple_of(c * cu, cu), cu)], stu_ref.at[s], usem.at[s]),
                pltpu.make_async_copy(wd_hbm.at[en, pl.ds(pl.multiple_of(c * cd, cd), cd)], std_ref.at[s], dsem.at[s]))

    def start_w(en, c, s):
        for cp in w_copies(en, c, s):
            cp.start()

    def wait_w(en, c, s):
        for cp in w_copies(en, c, s):
            cp.wait()

    def cast_w(dst_u, dst_d, c, s):
        dst_u[pl.ds(pl.multiple_of(c * cu, cu), cu), :] = stu_ref[s].astype(_BF)
        dst_d[pl.ds(pl.multiple_of(c * cd, cd), cd), :] = std_ref[s].astype(_BF)

    def run_expert(e, k, cur_u, cur_d, nxt_u, nxt_d):
        b0 = eblk[2 * e]
        n = eblk[2 * e + 1]
        en = jnp.minimum(e + 1, n_exp - 1)

        def stream_w(t):
            c = t & (W_CHUNKS - 1)
            s = t & 1
            wait_w(en, c, s)
            start_w(en, (t + 1) & (W_CHUNKS - 1), 1 - s)
            cast_w(nxt_u, nxt_d, c, s)

        def x_copy(j, slot):
            return pltpu.make_async_copy(xs_ref.at[rows_of(b0 + j)], xb_ref.at[slot], xsem.at[slot])

        def y_copy(j, slot):
            return pltpu.make_async_copy(yb_ref.at[slot], ys_ref.at[rows_of(b0 + j)], ysem.at[slot])

        start_w(en, 0, 0)

        @pl.when((e == 0) & (n > 0))
        def _():
            x_copy(0, 0).start(priority=1)

        def body(j, c):
            slot = j & 1
            x_copy(j, slot).wait()

            @pl.when(j + 1 < n)
            def _():
                x_copy(j + 1, 1 - slot).start(priority=1)

            @pl.when(j >= 2)
            def _():
                y_copy(j - 2, slot).wait()

            stream_w(j)
            x_lo, x_hi = _unpack_pairs(xb_ref[slot])
            gu = jnp.dot(x_lo, cur_u[:half, :], preferred_element_type=_F32) \
                + jnp.dot(x_hi, cur_u[half:, :], preferred_element_type=_F32) + bu_ref[k]
            g = jnp.minimum(gu[:, :D_FF], SWIGLU_LIMIT)
            u = jnp.clip(gu[:, D_FF:], -SWIGLU_LIMIT, SWIGLU_LIMIT)
            act = (u + 1.0) * (g * jax.nn.sigmoid(SWIGLU_ALPHA * g))
            y = jnp.dot(act.astype(_BF), cur_d[...], preferred_element_type=_F32) + bd_ref[k]
            yb_ref[slot] = _pack_pairs(y)
            y_copy(j, slot).start()
            return c
        lax.fori_loop(0, n, body, 0)

        def rest(t, c):
            stream_w(t)
            return c
        t_end = jnp.maximum(n, W_CHUNKS)
        lax.fori_loop(n, t_end, rest, 0)
        wait_w(en, t_end & (W_CHUNKS - 1), t_end & 1)

        @pl.when(n >= 2)
        def _():
            y_copy(n - 2, n & 1).wait()

        @pl.when(n >= 1)
        def _():
            y_copy(n - 1, (n - 1) & 1).wait()

        @pl.when((e < n_exp - 1) & (eblk[2 * en + 1] > 0))
        def _():
            pltpu.make_async_copy(xs_ref.at[rows_of(eblk[2 * en])], xb_ref.at[0], xsem.at[0]).start(priority=1)

    @pl.when(step == 0)
    def _():
        def first(c, carry):
            start_w(0, c, 0)
            wait_w(0, c, 0)
            cast_w(wua_ref, wda_ref, c, 0)
            return carry
        lax.fori_loop(0, W_CHUNKS, first, 0)

    run_expert(2 * step, 0, wua_ref, wda_ref, wub_ref, wdb_ref)
    run_expert(2 * step + 1, 1, wub_ref, wdb_ref, wua_ref, wda_ref)

    @pl.when(step == pl.num_programs(0) - 1)
    def _():
        yb_ref[0] = _pack_pairs(jnp.zeros((block_rows, 2 * yb_ref.shape[2]), _F32))

        def tail_copy(b):
            return pltpu.make_async_copy(yb_ref.at[0], ys_ref.at[rows_of(b)], ysem.at[0])

        def tail(b, c):
            tail_copy(b).start()
            return c
        lax.fori_loop(info[0], nb, tail, 0)

        def tail_wait(b, c):
            tail_copy(b).wait()
            return c
        lax.fori_loop(info[0], nb, tail_wait, 0)


def _expert_call(eblk, info, xs, wu, bu, wd, bd, *, block_rows):
    rows, half = xs.shape
    D = 2 * half
    assert N_EXPERTS % 2 == 0 and D % W_CHUNKS == 0 and D_FF % W_CHUNKS == 0
    pair = lambda i, *_: (i, 0, 0)
    grid_spec = pltpu.PrefetchScalarGridSpec(
        num_scalar_prefetch=2, grid=(N_EXPERTS // 2,),
        in_specs=[pl.BlockSpec(memory_space=pl.ANY),
                  pl.BlockSpec(memory_space=pl.ANY), pl.BlockSpec((2, 1, 2 * D_FF), pair),
                  pl.BlockSpec(memory_space=pl.ANY), pl.BlockSpec((2, 1, D), pair)],
        out_specs=pl.BlockSpec(memory_space=pl.ANY),
        scratch_shapes=[pltpu.VMEM((D, 2 * D_FF), _BF), pltpu.VMEM((D_FF, D), _BF),
                        pltpu.VMEM((D, 2 * D_FF), _BF), pltpu.VMEM((D_FF, D), _BF),
                        pltpu.VMEM((2, D // W_CHUNKS, 2 * D_FF), _F32), pltpu.VMEM((2, D_FF // W_CHUNKS, D), _F32),
                        pltpu.VMEM((2, block_rows, half), jnp.uint32), pltpu.VMEM((2, block_rows, half), jnp.uint32),
                        pltpu.SemaphoreType.DMA((2,)), pltpu.SemaphoreType.DMA((2,)),
                        pltpu.SemaphoreType.DMA((2,)), pltpu.SemaphoreType.DMA((2,))])
    return pl.pallas_call(
        _expert_kernel, grid_spec=grid_spec, out_shape=jax.ShapeDtypeStruct((rows, half), jnp.uint32),
        compiler_params=_cparams(("arbitrary",)), name="experts",
    )(eblk, info, xs, wu, bu, wd, bd)


def _combine_kernel(goff, lbase, nch, a_ref, hres_ref, ys_ref, o_ref, yl_ref, sem):
    i = pl.program_id(0)
    nt = pl.num_programs(0)
    R = yl_ref.shape[1]
    T = a_ref.shape[0]

    def fetch(t, slot):
        def issue(e, c):
            n = nch[t * N_EXPERTS + e]

            @pl.when(n > 0)
            def _():
                rows = pl.multiple_of(n * RUN_ALIGN, RUN_ALIGN)
                pltpu.make_async_copy(
                    ys_ref.at[pl.ds(pl.multiple_of(goff[t * N_EXPERTS + e], RUN_ALIGN), rows)],
                    yl_ref.at[slot, pl.ds(pl.multiple_of(lbase[t * N_EXPERTS + e], RUN_ALIGN), rows)], sem.at[slot]).start()
            return c
        lax.fori_loop(0, N_EXPERTS, issue, 0)

    @pl.when(i == 0)
    def _():
        yl_ref[...] = jnp.zeros(yl_ref.shape, yl_ref.dtype)
        fetch(0, 0)

    slot = i & 1

    @pl.when(i + 1 < nt)
    def _():
        fetch(i + 1, 1 - slot)

    total = lax.fori_loop(0, N_EXPERTS, lambda e, c: c + nch[i * N_EXPERTS + e], 0)

    @pl.when(total > 0)
    def _():
        rows = pl.multiple_of(total * RUN_ALIGN, RUN_ALIGN)
        pltpu.make_async_copy(ys_ref.at[pl.ds(0, rows)], yl_ref.at[slot, pl.ds(0, rows)], sem.at[slot]).wait()

    a = a_ref[...]
    slot_iota = lax.broadcasted_iota(jnp.int32, (T, R), 1).astype(_F32)
    pt = jnp.zeros((T, R), _F32)
    for k in range(TOP_K):
        pt = jnp.where(slot_iota == a[:, k:k + 1], a[:, TOP_K + k:TOP_K + k + 1], pt)
    pt = pt.astype(_BF)
    y_lo, y_hi = _unpack_pairs(yl_ref[slot])
    half = y_lo.shape[1]
    o_ref[:, :half] = hres_ref[:, :half] + jnp.dot(pt, y_lo, preferred_element_type=_F32)
    o_ref[:, half:] = hres_ref[:, half:] + jnp.dot(pt, y_hi, preferred_element_type=_F32)


def _combine_call(goff, lbase, nch, a, hres, ys, *, T, R):
    N, D = hres.shape
    grid_spec = pltpu.PrefetchScalarGridSpec(
        num_scalar_prefetch=3, grid=(N // T,),
        in_specs=[pl.BlockSpec((T, LANES), lambda i, *_: (i, 0)), pl.BlockSpec((T, D), lambda i, *_: (i, 0)),
                  pl.BlockSpec(memory_space=pl.ANY)],
        out_specs=pl.BlockSpec((T, D), lambda i, *_: (i, 0)),
        scratch_shapes=[pltpu.VMEM((2, R, D // 2), jnp.uint32), pltpu.SemaphoreType.DMA((2,))])
    return pl.pallas_call(
        _combine_kernel, grid_spec=grid_spec, out_shape=jax.ShapeDtypeStruct((N, D), _F32),
        compiler_params=_cparams(("arbitrary",)), name="combine",
    )(goff, lbase, nch, a, hres, ys)


def _tiles(S):
    pick = lambda pref: next(t for t in pref if S % t == 0)
    tm = pick((512, 256, 128))
    return dict(tm=tm, tq_fox=tm, tq_swa=pick((256, 128)),
                tm_mix=pick((512, 256, 128)), t_route=pick((256, 128)), block_rows=256)


def kernel(x, meta_tokens, norm1_gain, w_in, b_forget, fox_q_gain, fox_k_gain, swa_q_gain, swa_k_gain, swa_sinks,
           w_fox_out, w_swa_out, w_out, norm2_gain, w_router, b_router, w_up, b_up, w_down, b_down):
    B, S, D = x.shape
    assert norm1_gain.shape[0] == 1 and S % LANES == 0
    tl = _tiles(S)
    N = B * S
    scale = HEAD_DIM ** -0.5

    w = w_in[0]
    c0 = 3 * FOX_W
    c1 = c0 + FOX_HEADS
    c2 = c1 + SWA_QW + 2 * SWA_KW
    wqk = w[:, :2 * FOX_W].astype(_BF)
    wvt = w[:, 2 * FOX_W:c0].T.astype(_BF)
    wfl = w[:, c0:c1].T.astype(_BF)
    perm_heads = np.array([h for p in range(SWA_HEADS // 2) for h in (p, p + SWA_HEADS // 2)])
    perm_cols = (perm_heads[:, None] * HEAD_DIM + np.arange(HEAD_DIM)[None, :]).reshape(-1)
    ws_all = w[:, c1:c2]
    ws = jnp.concatenate([ws_all[:, :SWA_QW][:, perm_cols], ws_all[:, SWA_QW:SWA_QW + SWA_KW]], axis=1).astype(_BF)
    wsvt = ws_all[:, SWA_QW + SWA_KW:].T.astype(_BF)
    wga = w[:, c2:c2 + D].astype(_BF)
    wgb = w[:, c2 + D:].astype(_BF)
    g1 = norm1_gain[0][None, :]
    g2 = norm2_gain[0][None, :]
    bfg = b_forget[0][:, None]
    fqg = jnp.tile(fox_q_gain[0], FOX_HEADS)[None, :] * (scale * LOG2E)
    fkg = jnp.tile(fox_k_gain[0], FOX_HEADS)[None, :]
    sqg = jnp.tile(swa_q_gain[0], SWA_HEADS)[None, :] * scale
    skg = jnp.tile(swa_k_gain[0], SWA_KV_HEADS)[None, :]
    wfo = w_fox_out[0].astype(_BF)
    wso = w_swa_out[0][perm_cols, :].astype(_BF)
    wo = w_out[0].astype(_BF)
    wr = jnp.pad(w_router[0], ((0, 0), (0, LANES - N_EXPERTS)))
    br = jnp.pad(b_router[0], (0, LANES - N_EXPERTS), constant_values=NEG_INF)[None, :]
    sinks = swa_sinks[0].astype(_F32)
    wu = w_up[0]
    wd = w_down[0]
    bu = b_up[0][:, None, :]
    bd = b_down[0][:, None, :]

    meta_blk = jnp.pad(meta_tokens.astype(x.dtype), ((META_PAD, 0), (0, 0)))[None]
    zero_f = jnp.zeros((FOX_HEADS, 1), _F32)
    _, fk_m, aug_m, fvt_m, _, sk_m, svt_m, fend_m = _qkv_call(meta_blk, g1, wqk, wvt, wfl, ws, wsvt, bfg, fqg, fkg, sqg,
                                                              skg, zero_f, tm=META_BLOCK, n_pad=META_PAD)
    fq, fk, aug, fvt, sq, sk, svt, _ = _qkv_call(x, g1, wqk, wvt, wfl, ws, wsvt, bfg, fqg, fkg, sqg, skg, fend_m[0],
                                                 tm=tl["tm"], n_pad=0)
    o_a = _fox_call(fq, fk, aug, fvt, fk_m[0], aug_m[0], fvt_m[0, :, 0], tq=tl["tq_fox"])
    o_b = _swa_call(sinks, sq, sk, svt, sk_m[0], svt_m[0, :, 0], tq=tl["tq_swa"])

    hres, h2, lg = _mix_call(x.reshape(N, D), o_a.reshape(N, FOX_W), o_b.reshape(N, SWA_QW),
                             g1, wga, wgb, wfo, wso, wo, g2, wr, br, tm=tl["tm_mix"])

    T = tl["t_route"]
    nt = N // T
    block_rows = tl["block_rows"]
    R = -(-(TOP_K * T + N_EXPERTS * (RUN_ALIGN - 1)) // LANES) * LANES
    max_rows = N * TOP_K + nt * N_EXPERTS * (RUN_ALIGN - 1) + N_EXPERTS * (block_rows - RUN_ALIGN)
    nb = -(-max_rows // block_rows)
    a, at, cnt = _route_call(lg, T=T)
    goff, lbase, nch, eblk, gap, info = _plan_call(cnt[:, :, 0].reshape(-1), nt=nt, block_rows=block_rows)
    xs = _dispatch_call(goff, lbase, nch, gap, info, at, h2, T=T, R=R, rows=nb * block_rows, block_rows=block_rows)
    ys = _expert_call(eblk, info, xs, wu, bu, wd, bd, block_rows=block_rows)
    out = _combine_call(goff, lbase, nch, a, hres, ys, T=T, R=R)
    return out.reshape(B, S, D)
```

```python
import functools

import jax
import jax.numpy as jnp
import numpy as np
from jax import lax
from jax.experimental import pallas as pl
from jax.experimental.pallas import tpu as pltpu

HEAD_DIM = 64
N_META = 16
FOX_HEADS = 8
SWA_HEADS = 8
SWA_KV_HEADS = 2
WINDOW = 128
N_EXPERTS = 32
TOP_K = 4
D_FF = 1024
SWIGLU_LIMIT = 7.0
SWIGLU_ALPHA = 1.702
RMS_EPS = 1e-6
NEG_INF = -1e30
LOG2E = 1.4426950408889634

LANES = 128
META_BLOCK = 128
META_PAD = META_BLOCK - N_META
X_GROUP = 4
FOX_PAIRS = 4
RUN_ALIGN = 8
VMEM_LIMIT = 56 * 1024 * 1024

FOX_W = FOX_HEADS * HEAD_DIM
SWA_QW = SWA_HEADS * HEAD_DIM
SWA_KW = SWA_KV_HEADS * HEAD_DIM

_BF = jnp.bfloat16
_F32 = jnp.float32


def _cparams(sem):
    return pltpu.CompilerParams(dimension_semantics=sem, vmem_limit_bytes=VMEM_LIMIT)


def _rms(t, gain):
    return t * lax.rsqrt(jnp.mean(t * t, axis=-1, keepdims=True) + RMS_EPS) * gain


def _head_rms(z, gain_row):
    lane = lax.broadcasted_iota(jnp.int32, (z.shape[0], LANES), 1)
    lo_mask = lane < HEAD_DIM
    outs = []
    for b in range(z.shape[1] // LANES):
        v = z[:, b * LANES:(b + 1) * LANES]
        v2 = v * v
        tot = jnp.sum(v2, axis=-1, keepdims=True)
        lo = jnp.sum(jnp.where(lo_mask, v2, 0.0), axis=-1, keepdims=True)
        hi = tot - lo
        r_lo = lax.rsqrt(lo * (1.0 / HEAD_DIM) + RMS_EPS)
        r_hi = lax.rsqrt(hi * (1.0 / HEAD_DIM) + RMS_EPS)
        outs.append(v * jnp.where(lo_mask, r_lo, r_hi))
    return jnp.concatenate(outs, axis=-1) * gain_row


def _lane_cumsum(x):
    n = x.shape[-1]
    lane = lax.broadcasted_iota(jnp.int32, x.shape, x.ndim - 1)
    s = 1
    while s < n:
        x = x + jnp.where(lane >= s, pltpu.roll(x, s, x.ndim - 1), 0.0)
        s *= 2
    return x


def _qkv_kernel(x_ref, g1_ref, wqk_ref, wvt_ref, wfl_ref, ws_ref, wsvt_ref, bf_ref, fqg_ref, fkg_ref, sqg_ref, skg_ref,
                f0_ref, fq_ref, fk_ref, aug_ref, fvt_ref, sq_ref, sk_ref, svt_ref, fend_ref, carry_ref, *, n_pad):
    j = pl.program_id(1)

    @pl.when(j == 0)
    def _():
        carry_ref[...] = f0_ref[...]

    x = x_ref[...]
    tm = x.shape[0]
    h = _rms(x, g1_ref[...]).astype(_BF)
    nt = (((1,), (1,)), ((), ()))
    f = jnp.dot(h, wqk_ref[...], preferred_element_type=_F32)
    s = jnp.dot(h, ws_ref[...], preferred_element_type=_F32)
    svt = lax.dot_general(wsvt_ref[...], h, nt, preferred_element_type=_F32)
    vt = lax.dot_general(wvt_ref[...], h, nt, preferred_element_type=_F32)
    flog = lax.dot_general(wfl_ref[...], h, nt, preferred_element_type=_F32)

    fq_ref[...] = _head_rms(f[:, :FOX_W], fqg_ref[...]).astype(_BF)
    fk_ref[...] = _head_rms(f[:, FOX_W:], fkg_ref[...]).astype(_BF)
    sq_ref[...] = _head_rms(s[:, :SWA_QW], sqg_ref[...]).astype(_BF)
    sk_ref[...] = _head_rms(s[:, SWA_QW:], skg_ref[...]).astype(_BF)

    ones_row = jnp.where(lax.broadcasted_iota(jnp.int32, (HEAD_DIM, tm), 0) == 0, 1.0, 0.0).astype(_BF)
    for hd in range(FOX_HEADS):
        fvt_ref[hd] = jnp.concatenate([vt[hd * HEAD_DIM:(hd + 1) * HEAD_DIM].astype(_BF), ones_row], axis=0)
    for g in range(SWA_KV_HEADS):
        blk = jnp.concatenate([svt[g * HEAD_DIM:(g + 1) * HEAD_DIM].astype(_BF), ones_row], axis=0)
        for c in range(tm // WINDOW):
            svt_ref[g, c] = blk[:, c * WINDOW:(c + 1) * WINDOW]

    z = flog + bf_ref[...]
    log_f = jnp.minimum(z, 0.0) - jnp.log1p(jnp.exp(-jnp.abs(z)))
    if n_pad:
        lane = lax.broadcasted_iota(jnp.int32, log_f.shape, 1)
        log_f = jnp.where(lane >= n_pad, log_f, 0.0)
    cum = _lane_cumsum(log_f) + carry_ref[...]
    carry_ref[...] = cum[:, -1:]
    fend_ref[...] = cum[:, -1:]
    nfl = cum * (-LOG2E)
    p1 = nfl.astype(_BF).astype(_F32)
    p2 = (nfl - p1).astype(_BF).astype(_F32)
    p3 = (nfl - p1 - p2).astype(_BF).astype(_F32)
    g = jnp.concatenate([p1, p2, p3, jnp.zeros((LANES - 3 * FOX_HEADS, tm), _F32)], axis=0)
    aug_ref[...] = g.T.astype(_BF)


def _qkv_call(x3, g1, wqk, wvt, wfl, ws, wsvt, bfg, fqg, fkg, sqg, skg, f0, *, tm, n_pad):
    B, S, D = x3.shape
    nj = S // tm
    row = lambda w: pl.BlockSpec((None, tm, w), lambda b, j: (b, j, 0))
    full = lambda a: pl.BlockSpec(a.shape, lambda b, j: (0,) * a.ndim)
    bsd = lambda w: jax.ShapeDtypeStruct((B, S, w), _BF)
    out_shape = [bsd(FOX_W), bsd(FOX_W), bsd(LANES), jax.ShapeDtypeStruct((B, FOX_HEADS, nj, 2 * HEAD_DIM, tm), _BF),
                 bsd(SWA_QW), bsd(SWA_KW),
                 jax.ShapeDtypeStruct((B, SWA_KV_HEADS, S // WINDOW, 2 * HEAD_DIM, WINDOW), _BF),
                 jax.ShapeDtypeStruct((B, FOX_HEADS, 1), _F32)]
    out_specs = [row(FOX_W), row(FOX_W), row(LANES),
                 pl.BlockSpec((None, FOX_HEADS, None, 2 * HEAD_DIM, tm), lambda b, j: (b, 0, j, 0, 0)),
                 row(SWA_QW), row(SWA_KW),
                 pl.BlockSpec((None, SWA_KV_HEADS, tm // WINDOW, 2 * HEAD_DIM, WINDOW), lambda b, j: (b, 0, j, 0, 0)),
                 pl.BlockSpec((None, FOX_HEADS, 1), lambda b, j: (b, 0, 0))]
    return pl.pallas_call(
        functools.partial(_qkv_kernel, n_pad=n_pad),
        grid=(B, nj),
        in_specs=[row(D)] + [full(a) for a in (g1, wqk, wvt, wfl, ws, wsvt, bfg, fqg, fkg, sqg, skg, f0)],
        out_specs=out_specs, out_shape=out_shape,
        scratch_shapes=[pltpu.VMEM((FOX_HEADS, 1), _F32)],
        compiler_params=_cparams(("arbitrary", "arbitrary")),
        name="qkv_proj",
    )(x3, g1, wqk, wvt, wfl, ws, wsvt, bfg, fqg, fkg, sqg, skg, f0)


def _fox_kernel(q_ref, k_ref, aug_ref, vt_ref, km_ref, augm_ref, vtm_ref, o_ref, m_ref, acc_ref, s_ref, mx_ref):
    g = pl.program_id(1)
    qi = pl.program_id(2)
    tq = q_ref.shape[0]
    tk = vt_ref.shape[-1]
    nt = (((1,), (1,)), ((), ()))
    lane = lax.broadcasted_iota(jnp.int32, (tq, LANES), 1)

    qaugs = []
    for pp in range(FOX_PAIRS):
        q = q_ref[:, pp * LANES:(pp + 1) * LANES]
        parts = []
        for a in range(2):
            head = 2 * (FOX_PAIRS * g + pp) + a
            keep = (lane < HEAD_DIM) if a == 0 else (lane >= HEAD_DIM)
            ones = jnp.where((lane < 3 * FOX_HEADS) & ((lane & (FOX_HEADS - 1)) == head), 1.0, 0.0).astype(_BF)
            parts.append(jnp.concatenate([jnp.where(keep, q, jnp.zeros_like(q)), ones], axis=1))
        qaugs.append(jnp.concatenate(parts, axis=0))

    def scores(pp, k, aug, mask):
        st = lax.dot_general(jnp.concatenate([k, aug], axis=1), qaugs[pp], nt, preferred_element_type=_F32)
        if mask is not None:
            st = jnp.where(mask, st, NEG_INF)
        return st, jnp.max(st, axis=0, keepdims=True)

    def tile_scores(pp, ki, mask=None):
        off = pl.multiple_of(ki * tk, tk)
        return scores(pp, k_ref[pl.ds(off, tk), pp * LANES:(pp + 1) * LANES], aug_ref[pl.ds(off, tk), :], mask)

    def process(pp, st, mx, vts, first):
        m_new = mx if first else jnp.maximum(m_ref[pp], mx)
        pt = jnp.exp2(st - m_new).astype(_BF)
        if not first:
            alpha = jnp.exp2(m_ref[pp] - m_new)
        for a in range(2):
            pv = jnp.dot(vts[a], pt[:, a * tq:(a + 1) * tq], preferred_element_type=_F32)
            acc_ref[2 * pp + a] = pv if first else alpha[:, a * tq:(a + 1) * tq] * acc_ref[2 * pp + a] + pv
        m_ref[pp] = m_new

    def stash(buf, pp, st_mx):
        s_ref[buf, pp] = st_mx[0]
        mx_ref[buf, pp] = st_mx[1]

    def vts(pp, ki):
        return vt_ref[2 * pp, ki], vt_ref[2 * pp + 1, ki]

    causal = lax.broadcasted_iota(jnp.int32, (tk, 2 * tq), 0) <= (lax.broadcasted_iota(jnp.int32, (tk, 2 * tq), 1) & (tq - 1))
    mask_m = lax.broadcasted_iota(jnp.int32, (META_BLOCK, 2 * tq), 0) >= META_PAD
    for pp in range(FOX_PAIRS):
        stash(0, pp, tile_scores(pp, qi, causal))
        st, mx = scores(pp, km_ref[:, pp * LANES:(pp + 1) * LANES], augm_ref[...], mask_m)
        process(pp, st, mx, (vtm_ref[2 * pp], vtm_ref[2 * pp + 1]), True)

    def step(j, buf):
        nxt = [tile_scores(pp, j) for pp in range(FOX_PAIRS)]
        cur = jnp.where(j == 0, qi, j - 1)
        for pp in range(FOX_PAIRS):
            process(pp, s_ref[buf, pp], mx_ref[buf, pp], vts(pp, cur), False)
        for pp in range(FOX_PAIRS):
            stash(1 - buf, pp, nxt[pp])

    def body(jj, c):
        step(2 * jj, 0)
        step(2 * jj + 1, 1)
        return c
    lax.fori_loop(0, qi // 2, body, 0)

    def finish(buf):
        last = jnp.where(qi == 0, qi, qi - 1)
        outs = []
        for pp in range(FOX_PAIRS):
            process(pp, s_ref[buf, pp], mx_ref[buf, pp], vts(pp, last), False)
            for a in range(2):
                acc = acc_ref[2 * pp + a]
                outs.append(acc[:HEAD_DIM] / acc[HEAD_DIM:HEAD_DIM + 1])
        o_ref[...] = jnp.concatenate(outs, axis=0).T.astype(_BF)

    @pl.when(qi % 2 == 1)
    def _():
        step(qi - 1, 0)
        finish(1)

    @pl.when(qi % 2 == 0)
    def _():
        finish(0)


def _fox_call(fq, fk, aug, fvt, km, augm, vtm, *, tq):
    B, S, _ = fq.shape
    nk = fvt.shape[2]
    assert fvt.shape[-1] == tq
    w = FOX_PAIRS * LANES
    qspec = pl.BlockSpec((None, tq, w), lambda b, g, i: (b, i, g))
    return pl.pallas_call(
        _fox_kernel, grid=(B, FOX_HEADS // (2 * FOX_PAIRS), S // tq),
        in_specs=[qspec,
                  pl.BlockSpec((None, S, w), lambda b, g, i: (b, 0, g)),
                  pl.BlockSpec((None, S, LANES), lambda b, g, i: (b, 0, 0)),
                  pl.BlockSpec((None, 2 * FOX_PAIRS, nk, 2 * HEAD_DIM, tq), lambda b, g, i: (b, g, 0, 0, 0)),
                  pl.BlockSpec((META_BLOCK, w), lambda b, g, i: (0, g)),
                  pl.BlockSpec((META_BLOCK, LANES), lambda b, g, i: (0, 0)),
                  pl.BlockSpec((2 * FOX_PAIRS, 2 * HEAD_DIM, META_BLOCK), lambda b, g, i: (g, 0, 0))],
        out_specs=qspec, out_shape=jax.ShapeDtypeStruct((B, S, FOX_W), _BF),
        scratch_shapes=[pltpu.VMEM((FOX_PAIRS, 1, 2 * tq), _F32), pltpu.VMEM((2 * FOX_PAIRS, 2 * HEAD_DIM, tq), _F32),
                        pltpu.VMEM((2, FOX_PAIRS, tq, 2 * tq), _F32), pltpu.VMEM((2, FOX_PAIRS, 1, 2 * tq), _F32)],
        compiler_params=_cparams(("arbitrary", "arbitrary", "arbitrary")), name="fox_attn",
    )(fq, fk, aug, fvt, km, augm, vtm)


def _swa_bias(tq):
    j = np.arange(WINDOW + tq)[:, None] - WINDOW
    i = np.arange(tq)[None, :]
    dist = i - j
    ok = (dist >= 0) & (dist < WINDOW)
    slopes = np.exp2(-8.0 * (np.arange(SWA_HEADS, dtype=np.float32) + 1.0) / SWA_HEADS).astype(np.float32)
    per_head = np.where(ok[None], -slopes[:, None, None] * dist[None].astype(np.float32), np.float32(NEG_INF))
    rep = SWA_HEADS // SWA_KV_HEADS
    return np.stack([np.concatenate(list(per_head[g * rep:(g + 1) * rep]), axis=1) for g in range(SWA_KV_HEADS)])


def _swa_kernel(q_ref, kc_ref, kp_ref, km_ref, vc_ref, vp_ref, vm_ref, bias_ref, sink_ref, o_ref):
    qi = pl.program_id(1)
    tq = q_ref.shape[0]
    rep = SWA_HEADS // SWA_KV_HEADS
    nt = (((1,), (1,)), ((), ()))
    first = qi == 0
    kcat = jnp.concatenate([jnp.where(first, km_ref[...], kp_ref[...]), kc_ref[...]], axis=0)
    row = lax.broadcasted_iota(jnp.int32, (WINDOW + tq, rep * tq), 0)
    row_ok = row >= jnp.where(first, META_PAD, 0)
    lane = lax.broadcasted_iota(jnp.int32, (tq, LANES), 1)
    outs = [None] * SWA_HEADS
    for g in range(SWA_KV_HEADS):
        keep = (lane < HEAD_DIM) if g == 0 else (lane >= HEAD_DIM)
        qg = jnp.concatenate([jnp.where(keep, q_ref[:, p * LANES:(p + 1) * LANES], jnp.zeros((tq, LANES), _BF))
                              for p in range(rep)], axis=0)
        vcat = jnp.concatenate([jnp.where(first, vm_ref[g], vp_ref[g, 0])] + [vc_ref[g, c] for c in range(tq // WINDOW)],
                               axis=1)
        st = lax.dot_general(kcat, qg, nt, preferred_element_type=_F32) + bias_ref[g]
        st = jnp.where(row_ok, st, NEG_INF)
        sink = sink_ref[g]
        m = jnp.maximum(jnp.max(st, axis=0, keepdims=True), sink)
        pt = jnp.exp(st - m).astype(_BF)
        acc = jnp.dot(vcat, pt, preferred_element_type=_F32)
        o = acc[:HEAD_DIM] / (acc[HEAD_DIM:HEAD_DIM + 1] + jnp.exp(sink - m))
        for p in range(rep):
            outs[2 * p + g] = o[:, p * tq:(p + 1) * tq]
    o_ref[...] = jnp.concatenate(outs, axis=0).T.astype(_BF)


def _swa_call(sinks, sq, sk, svt, km, vtm, *, tq):
    B, S, _ = sq.shape
    r = tq // WINDOW
    rep = SWA_HEADS // SWA_KV_HEADS
    bias = jnp.asarray(_swa_bias(tq))
    sink_rows = jnp.repeat(sinks.reshape(SWA_KV_HEADS, rep), tq, axis=1)[:, None, :]
    prev_blk = lambda i: jnp.maximum(i * r - 1, 0)
    return pl.pallas_call(
        _swa_kernel, grid=(B, S // tq),
        in_specs=[pl.BlockSpec((None, tq, SWA_QW), lambda b, i: (b, i, 0)),
                  pl.BlockSpec((None, tq, LANES), lambda b, i: (b, i, 0)),
                  pl.BlockSpec((None, WINDOW, LANES), lambda b, i: (b, prev_blk(i), 0)),
                  pl.BlockSpec((META_BLOCK, LANES), lambda b, i: (0, 0)),
                  pl.BlockSpec((None, SWA_KV_HEADS, r, 2 * HEAD_DIM, WINDOW), lambda b, i: (b, 0, i, 0, 0)),
                  pl.BlockSpec((None, SWA_KV_HEADS, 1, 2 * HEAD_DIM, WINDOW), lambda b, i: (b, 0, prev_blk(i), 0, 0)),
                  pl.BlockSpec((SWA_KV_HEADS, 2 * HEAD_DIM, META_BLOCK), lambda b, i: (0, 0, 0)),
                  pl.BlockSpec(bias.shape, lambda b, i: (0, 0, 0)),
                  pl.BlockSpec(sink_rows.shape, lambda b, i: (0, 0, 0))],
        out_specs=pl.BlockSpec((None, tq, SWA_QW), lambda b, i: (b, i, 0)),
        out_shape=jax.ShapeDtypeStruct((B, S, SWA_QW), _BF),
        compiler_params=_cparams(("arbitrary", "arbitrary")), name="swa_attn",
    )(sq, sk, sk, km, svt, svt, vtm, bias, sink_rows)


def _split3(v):
    a = v.astype(_BF)
    r = v - a.astype(_F32)
    b = r.astype(_BF)
    c = (r - b.astype(_F32)).astype(_BF)
    return a, b, c


def _mix_kernel(x_ref, oa_ref, ob_ref, g1_ref, wga_ref, wgb_ref, wfo_ref, wso_ref, wo_ref, g2_ref, wr_ref, br_ref,
                hres_ref, h2_ref, lg_ref):
    x = x_ref[...]
    h = _rms(x, g1_ref[...]).astype(_BF)
    ga = jax.nn.sigmoid(jnp.dot(h, wga_ref[...], preferred_element_type=_F32))
    gb = jax.nn.sigmoid(jnp.dot(h, wgb_ref[...], preferred_element_type=_F32))
    mix = ga * jnp.dot(oa_ref[...], wfo_ref[...], preferred_element_type=_F32) \
        + gb * jnp.dot(ob_ref[...], wso_ref[...], preferred_element_type=_F32)
    hres = x + jnp.dot(mix.astype(_BF), wo_ref[...], preferred_element_type=_F32)
    hres_ref[...] = hres
    h2 = _rms(hres, g2_ref[...])
    h2_ref[...] = h2.astype(_BF)
    wr = wr_ref[...]
    w_hi = wr.astype(_BF)
    w_lo = (wr - w_hi.astype(_F32)).astype(_BF)
    h_hi = h2.astype(_BF)
    h_lo = (h2 - h_hi.astype(_F32)).astype(_BF)
    lg = jnp.dot(h_hi, w_hi, preferred_element_type=_F32) + jnp.dot(h_hi, w_lo, preferred_element_type=_F32) \
        + jnp.dot(h_lo, w_hi, preferred_element_type=_F32)
    lg_ref[...] = lg + br_ref[...]


def _mix_call(x2, oa, ob, g1, wga, wgb, wfo, wso, wo, g2, wr, br, *, tm):
    N, D = x2.shape
    row = lambda w: pl.BlockSpec((tm, w), lambda i: (i, 0))
    full = lambda a: pl.BlockSpec(a.shape, lambda i: (0,) * a.ndim)
    return pl.pallas_call(
        _mix_kernel, grid=(N // tm,),
        in_specs=[row(D), row(FOX_W), row(SWA_QW)] + [full(a) for a in (g1, wga, wgb, wfo, wso, wo, g2, wr, br)],
        out_specs=[row(D), row(D), row(LANES)],
        out_shape=[jax.ShapeDtypeStruct((N, D), _F32), jax.ShapeDtypeStruct((N, D), _BF),
                   jax.ShapeDtypeStruct((N, LANES), _F32)],
        compiler_params=_cparams(("arbitrary",)), name="mix_proj",
    )(x2, oa, ob, g1, wga, wgb, wfo, wso, wo, g2, wr, br)


def _route_kernel(lg_ref, a_ref, at_ref, cnt_ref):
    T = lg_ref.shape[0]
    lt = lg_ref[...].T[:N_EXPERTS]
    e_iota = lax.broadcasted_iota(jnp.int32, lt.shape, 0)
    cur = lt
    vals, idxs = [], []
    for _ in range(TOP_K):
        m = jnp.max(cur, axis=0, keepdims=True)
        idx = jnp.min(jnp.where(cur == m, e_iota, N_EXPERTS), axis=0, keepdims=True)
        vals.append(m)
        idxs.append(idx)
        cur = jnp.where(e_iota == idx, -jnp.inf, cur)
    es = [jnp.exp(v - vals[0]) for v in vals]
    den = es[0] + es[1] + es[2] + es[3]
    gates = [e / den for e in es]

    onehot = jnp.zeros(lt.shape, _F32)
    for idx in idxs:
        onehot = onehot + jnp.where(e_iota == idx, 1.0, 0.0)
    r_i = lax.broadcasted_iota(jnp.int32, (T, T), 0)
    c_i = lax.broadcasted_iota(jnp.int32, (T, T), 1)
    utri = jnp.where(r_i < c_i, 1.0, 0.0).astype(_BF)
    prefix = jnp.dot(onehot.astype(_BF), utri, preferred_element_type=_F32)
    cnt = jnp.sum(onehot, axis=1, keepdims=True)
    chunks = jnp.floor((cnt + (RUN_ALIGN - 1)) * (1.0 / RUN_ALIGN))
    l_r = lax.broadcasted_iota(jnp.int32, (N_EXPERTS, N_EXPERTS), 0)
    l_c = lax.broadcasted_iota(jnp.int32, (N_EXPERTS, N_EXPERTS), 1)
    ltri = jnp.where(l_c < l_r, 1.0, 0.0).astype(_BF)
    lbase = RUN_ALIGN * jnp.dot(ltri, jnp.broadcast_to(chunks, (N_EXPERTS, LANES)).astype(_BF),
                                preferred_element_type=_F32)[:, 0:1]
    slot_all = prefix + lbase
    rows = [jnp.sum(jnp.where(e_iota == idx, slot_all, 0.0), axis=0, keepdims=True) for idx in idxs] + gates
    at = jnp.concatenate(rows, axis=0)
    at_ref[...] = at
    a_ref[...] = jnp.concatenate([at, jnp.zeros((LANES - 2 * TOP_K, T), _F32)], axis=0).T
    cnt_ref[...] = jnp.broadcast_to(cnt, (N_EXPERTS, LANES)).astype(jnp.int32)


def _route_call(lg, *, T):
    N = lg.shape[0]
    nt = N // T
    return pl.pallas_call(
        _route_kernel, grid=(nt,),
        in_specs=[pl.BlockSpec((T, LANES), lambda i: (i, 0))],
        out_specs=[pl.BlockSpec((T, LANES), lambda i: (i, 0)), pl.BlockSpec((8, T), lambda i: (0, i)),
                   pl.BlockSpec((None, N_EXPERTS, LANES), lambda i: (i, 0, 0))],
        out_shape=[jax.ShapeDtypeStruct((N, LANES), _F32), jax.ShapeDtypeStruct((8, N), _F32),
                   jax.ShapeDtypeStruct((nt, N_EXPERTS, LANES), jnp.int32)],
        compiler_params=_cparams(("arbitrary",)), name="route",
    )(lg)


def _div_pow2(v, d):
    assert d & (d - 1) == 0
    return lax.shift_right_logical(v, d.bit_length() - 1)


def _plan_kernel(cnt_ref, goff_ref, lbase_ref, nch_ref, eblk_ref, gap_ref, info_ref, *, nt, block_rows):
    def per_expert(e, base):
        def per_tile(i, run):
            n = _div_pow2(cnt_ref[i * N_EXPERTS + e] + (RUN_ALIGN - 1), RUN_ALIGN)
            goff_ref[i * N_EXPERTS + e] = run
            nch_ref[i * N_EXPERTS + e] = n
            return run + n * RUN_ALIGN
        end = lax.fori_loop(0, nt, per_tile, base)
        nblk = _div_pow2(end - base + (block_rows - 1), block_rows)
        nxt = base + nblk * block_rows
        gap_ref[2 * e] = end
        gap_ref[2 * e + 1] = _div_pow2(nxt - end, RUN_ALIGN)
        eblk_ref[2 * e] = _div_pow2(base, block_rows)
        eblk_ref[2 * e + 1] = nblk
        return nxt
    total = lax.fori_loop(0, N_EXPERTS, per_expert, 0)
    info_ref[0] = _div_pow2(total, block_rows)

    def per_tile2(i, c):
        def per_e(e, run):
            lbase_ref[i * N_EXPERTS + e] = run
            return run + nch_ref[i * N_EXPERTS + e] * RUN_ALIGN
        lax.fori_loop(0, N_EXPERTS, per_e, 0)
        return c
    lax.fori_loop(0, nt, per_tile2, 0)


def _plan_call(cnt_flat, *, nt, block_rows):
    smem = pl.BlockSpec(memory_space=pltpu.SMEM)
    i32 = lambda n: jax.ShapeDtypeStruct((n,), jnp.int32)
    return pl.pallas_call(
        functools.partial(_plan_kernel, nt=nt, block_rows=block_rows),
        in_specs=[smem], out_specs=[smem] * 6,
        out_shape=[i32(nt * N_EXPERTS), i32(nt * N_EXPERTS), i32(nt * N_EXPERTS), i32(2 * N_EXPERTS), i32(2 * N_EXPERTS),
                   i32(1)],
        name="plan",
    )(cnt_flat)


def _pack_pairs(v):
    w = v.shape[1] // 2
    return pltpu.pack_elementwise([v[:, :w], v[:, w:]], packed_dtype=_BF)


def _unpack_pairs(wd):
    lo = pltpu.unpack_elementwise(wd, index=0, packed_dtype=_BF, unpacked_dtype=_F32)
    hi = pltpu.unpack_elementwise(wd, index=1, packed_dtype=_BF, unpacked_dtype=_F32)
    return lo.astype(_BF), hi.astype(_BF)


def _dispatch_kernel(goff, lbase, nch, gap, info, at_ref, h2_ref, xs_ref, xl_ref, z_ref, sem, zsem, tsem):
    i = pl.program_id(0)
    R = xl_ref.shape[1]
    T = h2_ref.shape[0]
    block_rows = z_ref.shape[0]
    nb = xs_ref.shape[0] // block_rows

    @pl.when(i == 0)
    def _():
        z_ref[...] = jnp.zeros(z_ref.shape, z_ref.dtype)

        def gap_copy(e):
            rows = pl.multiple_of(gap[2 * e + 1] * RUN_ALIGN, RUN_ALIGN)
            return pltpu.make_async_copy(z_ref.at[pl.ds(0, rows)],
                                         xs_ref.at[pl.ds(pl.multiple_of(gap[2 * e], RUN_ALIGN), rows)], zsem)

        def fill(e, c):
            @pl.when(gap[2 * e + 1] > 0)
            def _():
                gap_copy(e).start()
            return c
        lax.fori_loop(0, N_EXPERTS, fill, 0)

        def fill_wait(e, c):
            @pl.when(gap[2 * e + 1] > 0)
            def _():
                gap_copy(e).wait()
            return c
        lax.fori_loop(0, N_EXPERTS, fill_wait, 0)

        def tail(b, c):
            pltpu.make_async_copy(z_ref, xs_ref.at[pl.ds(pl.multiple_of(b * block_rows, block_rows), block_rows)],
                                  tsem).start()
            return c
        lax.fori_loop(info[0], nb, tail, 0)

        def tail_wait(b, c):
            pltpu.make_async_copy(z_ref, xs_ref.at[pl.ds(0, block_rows)], tsem).wait()
            return c
        lax.fori_loop(info[0], nb, tail_wait, 0)

    slot_iota = lax.broadcasted_iota(jnp.int32, (R, T), 0).astype(_F32)
    pm = jnp.zeros((R, T), _F32)
    for k in range(TOP_K):
        pm = jnp.where(slot_iota == at_ref[k:k + 1, :], 1.0, pm)
    xl = jnp.dot(pm.astype(_BF), h2_ref[...], preferred_element_type=_F32)
    slot = i & 1

    def wait_tile(t, s):
        total = lax.fori_loop(0, N_EXPERTS, lambda e, c: c + nch[t * N_EXPERTS + e], 0)

        @pl.when(total > 0)
        def _():
            rows = pl.multiple_of(total * RUN_ALIGN, RUN_ALIGN)
            pltpu.make_async_copy(xl_ref.at[s, pl.ds(0, rows)], xs_ref.at[pl.ds(0, rows)], sem.at[s]).wait()

    @pl.when(i >= 2)
    def _():
        wait_tile(i - 2, slot)
    xl_ref[slot] = _pack_pairs(xl)

    def issue(e, c):
        n = nch[i * N_EXPERTS + e]

        @pl.when(n > 0)
        def _():
            rows = pl.multiple_of(n * RUN_ALIGN, RUN_ALIGN)
            pltpu.make_async_copy(xl_ref.at[slot, pl.ds(pl.multiple_of(lbase[i * N_EXPERTS + e], RUN_ALIGN), rows)],
                                  xs_ref.at[pl.ds(pl.multiple_of(goff[i * N_EXPERTS + e], RUN_ALIGN), rows)],
                                  sem.at[slot]).start()
        return c
    lax.fori_loop(0, N_EXPERTS, issue, 0)

    @pl.when(i == pl.num_programs(0) - 1)
    def _():
        @pl.when(i >= 1)
        def _():
            wait_tile(i - 1, 1 - slot)
        wait_tile(i, slot)


def _dispatch_call(goff, lbase, nch, gap, info, at, h2, *, T, R, rows, block_rows):
    N, D = h2.shape
    grid_spec = pltpu.PrefetchScalarGridSpec(
        num_scalar_prefetch=5, grid=(N // T,),
        in_specs=[pl.BlockSpec((8, T), lambda i, *_: (0, i)), pl.BlockSpec((T, D), lambda i, *_: (i, 0))],
        out_specs=pl.BlockSpec(memory_space=pl.ANY),
        scratch_shapes=[pltpu.VMEM((2, R, D // 2), jnp.uint32), pltpu.VMEM((block_rows, D // 2), jnp.uint32),
                        pltpu.SemaphoreType.DMA((2,)), pltpu.SemaphoreType.DMA, pltpu.SemaphoreType.DMA])
    return pl.pallas_call(
        _dispatch_kernel, grid_spec=grid_spec, out_shape=jax.ShapeDtypeStruct((rows, D // 2), jnp.uint32),
        compiler_params=_cparams(("arbitrary",)), name="dispatch",
    )(goff, lbase, nch, gap, info, at, h2)


def _expert_kernel(eblk, info, xs_ref, wu_ref, bu_ref, wd_ref, bd_ref, ys_ref, wub_ref, wdb_ref, xb_ref, yb_ref, xsem, ysem):
    e = pl.program_id(0)
    block_rows = xb_ref.shape[1] // X_GROUP
    nb = ys_ref.shape[0] // block_rows
    half = wu_ref.shape[0] // 2
    wub_ref[...] = wu_ref[...].astype(_BF)
    wdb_ref[...] = wd_ref[...].astype(_BF)

    def rows(first_blk, n_blk):
        size = n_blk * block_rows if isinstance(n_blk, int) else pl.multiple_of(n_blk * block_rows, block_rows)
        return pl.ds(pl.multiple_of(first_blk * block_rows, block_rows), size)

    def copies(ex):
        b0 = eblk[2 * ex]
        n = eblk[2 * ex + 1]
        ng = _div_pow2(n + (X_GROUP - 1), X_GROUP)

        def blocks_in(q):
            return jnp.minimum(X_GROUP, n - q * X_GROUP)

        def x_copy(q, slot):
            return pltpu.make_async_copy(xs_ref.at[rows(b0 + q * X_GROUP, blocks_in(q))],
                                         xb_ref.at[slot, rows(0, blocks_in(q))], xsem.at[slot])

        def y_copy(q, slot):
            return pltpu.make_async_copy(yb_ref.at[slot, rows(0, blocks_in(q))],
                                         ys_ref.at[rows(b0 + q * X_GROUP, blocks_in(q))], ysem.at[slot])
        return n, ng, blocks_in, x_copy, y_copy

    n, ng, blocks_in, x_copy, y_copy = copies(e)

    @pl.when((e == 0) & (n > 0))
    def _():
        x_copy(0, 0).start(priority=1)

    def group(q, c):
        slot = q & 1
        x_copy(q, slot).wait()

        @pl.when(q + 1 < ng)
        def _():
            x_copy(q + 1, 1 - slot).start(priority=1)

        @pl.when(q >= 2)
        def _():
            y_copy(q - 2, slot).wait()

        def block(i, c2):
            r = rows(i, 1)
            x_lo, x_hi = _unpack_pairs(xb_ref[slot, r, :])
            gu = jnp.dot(x_lo, wub_ref[:half, :], preferred_element_type=_F32) \
                + jnp.dot(x_hi, wub_ref[half:, :], preferred_element_type=_F32) + bu_ref[...]
            g = jnp.minimum(gu[:, :D_FF], SWIGLU_LIMIT)
            u = jnp.clip(gu[:, D_FF:], -SWIGLU_LIMIT, SWIGLU_LIMIT)
            act = (u + 1.0) * (g * jax.nn.sigmoid(SWIGLU_ALPHA * g))
            y = jnp.dot(act.astype(_BF), wdb_ref[...], preferred_element_type=_F32) + bd_ref[...]
            yb_ref[slot, r, :] = _pack_pairs(y)
            return c2
        lax.fori_loop(0, blocks_in(q), block, 0)
        y_copy(q, slot).start()
        return c
    lax.fori_loop(0, ng, group, 0)

    @pl.when(ng >= 2)
    def _():
        y_copy(ng - 2, ng & 1).wait()

    @pl.when(ng >= 1)
    def _():
        y_copy(ng - 1, (ng - 1) & 1).wait()

    last = pl.num_programs(0) - 1
    n_next, _, _, x_copy_next, _ = copies(jnp.minimum(e + 1, last))

    @pl.when((e < last) & (n_next > 0))
    def _():
        x_copy_next(0, 0).start(priority=1)

    @pl.when(e == last)
    def _():
        yb_ref[0, 0:block_rows, :] = _pack_pairs(jnp.zeros((block_rows, 2 * yb_ref.shape[2]), _F32))

        def tail_copy(b):
            return pltpu.make_async_copy(yb_ref.at[0, 0:block_rows], ys_ref.at[rows(b, 1)], ysem.at[0])

        def tail(b, c):
            tail_copy(b).start()
            return c
        lax.fori_loop(info[0], nb, tail, 0)

        def tail_wait(b, c):
            tail_copy(b).wait()
            return c
        lax.fori_loop(info[0], nb, tail_wait, 0)


def _expert_call(eblk, info, xs, wu, bu, wd, bd, *, block_rows):
    rows, half = xs.shape
    D = 2 * half
    ex = lambda e, *_: (e, 0, 0)
    grid_spec = pltpu.PrefetchScalarGridSpec(
        num_scalar_prefetch=2, grid=(N_EXPERTS,),
        in_specs=[pl.BlockSpec(memory_space=pl.ANY),
                  pl.BlockSpec((None, D, 2 * D_FF), ex), pl.BlockSpec((None, 1, 2 * D_FF), ex),
                  pl.BlockSpec((None, D_FF, D), ex), pl.BlockSpec((None, 1, D), ex)],
        out_specs=pl.BlockSpec(memory_space=pl.ANY),
        scratch_shapes=[pltpu.VMEM((D, 2 * D_FF), _BF), pltpu.VMEM((D_FF, D), _BF),
                        pltpu.VMEM((2, X_GROUP * block_rows, half), jnp.uint32),
                        pltpu.VMEM((2, X_GROUP * block_rows, half), jnp.uint32),
                        pltpu.SemaphoreType.DMA((2,)), pltpu.SemaphoreType.DMA((2,))])
    return pl.pallas_call(
        _expert_kernel, grid_spec=grid_spec, out_shape=jax.ShapeDtypeStruct((rows, half), jnp.uint32),
        compiler_params=_cparams(("arbitrary",)), name="experts",
    )(eblk, info, xs, wu, bu, wd, bd)


def _combine_kernel(goff, lbase, nch, a_ref, hres_ref, ys_ref, o_ref, yl_ref, sem):
    i = pl.program_id(0)
    nt = pl.num_programs(0)
    R = yl_ref.shape[1]
    T = a_ref.shape[0]

    def fetch(t, slot):
        def issue(e, c):
            n = nch[t * N_EXPERTS + e]

            @pl.when(n > 0)
            def _():
                rows = pl.multiple_of(n * RUN_ALIGN, RUN_ALIGN)
                pltpu.make_async_copy(
                    ys_ref.at[pl.ds(pl.multiple_of(goff[t * N_EXPERTS + e], RUN_ALIGN), rows)],
                    yl_ref.at[slot, pl.ds(pl.multiple_of(lbase[t * N_EXPERTS + e], RUN_ALIGN), rows)], sem.at[slot]).start()
            return c
        lax.fori_loop(0, N_EXPERTS, issue, 0)

    @pl.when(i == 0)
    def _():
        yl_ref[...] = jnp.zeros(yl_ref.shape, yl_ref.dtype)
        fetch(0, 0)

    slot = i & 1

    @pl.when(i + 1 < nt)
    def _():
        fetch(i + 1, 1 - slot)

    total = lax.fori_loop(0, N_EXPERTS, lambda e, c: c + nch[i * N_EXPERTS + e], 0)

    @pl.when(total > 0)
    def _():
        rows = pl.multiple_of(total * RUN_ALIGN, RUN_ALIGN)
        pltpu.make_async_copy(ys_ref.at[pl.ds(0, rows)], yl_ref.at[slot, pl.ds(0, rows)], sem.at[slot]).wait()

    a = a_ref[...]
    slot_iota = lax.broadcasted_iota(jnp.int32, (T, R), 1).astype(_F32)
    pt = jnp.zeros((T, R), _F32)
    for k in range(TOP_K):
        pt = jnp.where(slot_iota == a[:, k:k + 1], a[:, TOP_K + k:TOP_K + k + 1], pt)
    pt = pt.astype(_BF)
    y_lo, y_hi = _unpack_pairs(yl_ref[slot])
    half = y_lo.shape[1]
    o_ref[:, :half] = hres_ref[:, :half] + jnp.dot(pt, y_lo, preferred_element_type=_F32)
    o_ref[:, half:] = hres_ref[:, half:] + jnp.dot(pt, y_hi, preferred_element_type=_F32)


def _combine_call(goff, lbase, nch, a, hres, ys, *, T, R):
    N, D = hres.shape
    grid_spec = pltpu.PrefetchScalarGridSpec(
        num_scalar_prefetch=3, grid=(N // T,),
        in_specs=[pl.BlockSpec((T, LANES), lambda i, *_: (i, 0)), pl.BlockSpec((T, D), lambda i, *_: (i, 0)),
                  pl.BlockSpec(memory_space=pl.ANY)],
        out_specs=pl.BlockSpec((T, D), lambda i, *_: (i, 0)),
        scratch_shapes=[pltpu.VMEM((2, R, D // 2), jnp.uint32), pltpu.SemaphoreType.DMA((2,))])
    return pl.pallas_call(
        _combine_kernel, grid_spec=grid_spec, out_shape=jax.ShapeDtypeStruct((N, D), _F32),
        compiler_params=_cparams(("arbitrary",)), name="combine",
    )(goff, lbase, nch, a, hres, ys)


def _tiles(S):
    pick = lambda pref: next(t for t in pref if S % t == 0)
    tm = pick((512, 256, 128))
    return dict(tm=tm, tq_fox=tm, tq_swa=pick((256, 128)),
                tm_mix=pick((512, 256, 128)), t_route=pick((256, 128)), block_rows=256)


def kernel(x, meta_tokens, norm1_gain, w_in, b_forget, fox_q_gain, fox_k_gain, swa_q_gain, swa_k_gain, swa_sinks,
           w_fox_out, w_swa_out, w_out, norm2_gain, w_router, b_router, w_up, b_up, w_down, b_down):
    B, S, D = x.shape
    assert norm1_gain.shape[0] == 1 and S % LANES == 0
    tl = _tiles(S)
    N = B * S
    scale = HEAD_DIM ** -0.5

    w = w_in[0]
    c0 = 3 * FOX_W
    c1 = c0 + FOX_HEADS
    c2 = c1 + SWA_QW + 2 * SWA_KW
    wqk = w[:, :2 * FOX_W].astype(_BF)
    wvt = w[:, 2 * FOX_W:c0].T.astype(_BF)
    wfl = w[:, c0:c1].T.astype(_BF)
    perm_heads = np.array([h for p in range(SWA_HEADS // 2) for h in (p, p + SWA_HEADS // 2)])
    perm_cols = (perm_heads[:, None] * HEAD_DIM + np.arange(HEAD_DIM)[None, :]).reshape(-1)
    ws_all = w[:, c1:c2]
    ws = jnp.concatenate([ws_all[:, :SWA_QW][:, perm_cols], ws_all[:, SWA_QW:SWA_QW + SWA_KW]], axis=1).astype(_BF)
    wsvt = ws_all[:, SWA_QW + SWA_KW:].T.astype(_BF)
    wga = w[:, c2:c2 + D].astype(_BF)
    wgb = w[:, c2 + D:].astype(_BF)
    g1 = norm1_gain[0][None, :]
    g2 = norm2_gain[0][None, :]
    bfg = b_forget[0][:, None]
    fqg = jnp.tile(fox_q_gain[0], FOX_HEADS)[None, :] * (scale * LOG2E)
    fkg = jnp.tile(fox_k_gain[0], FOX_HEADS)[None, :]
    sqg = jnp.tile(swa_q_gain[0], SWA_HEADS)[None, :] * scale
    skg = jnp.tile(swa_k_gain[0], SWA_KV_HEADS)[None, :]
    wfo = w_fox_out[0].astype(_BF)
    wso = w_swa_out[0][perm_cols, :].astype(_BF)
    wo = w_out[0].astype(_BF)
    wr = jnp.pad(w_router[0], ((0, 0), (0, LANES - N_EXPERTS)))
    br = jnp.pad(b_router[0], (0, LANES - N_EXPERTS), constant_values=NEG_INF)[None, :]
    sinks = swa_sinks[0].astype(_F32)
    wu = w_up[0]
    wd = w_down[0]
    bu = b_up[0][:, None, :]
    bd = b_down[0][:, None, :]

    meta_blk = jnp.pad(meta_tokens.astype(x.dtype), ((META_PAD, 0), (0, 0)))[None]
    zero_f = jnp.zeros((FOX_HEADS, 1), _F32)
    _, fk_m, aug_m, fvt_m, _, sk_m, svt_m, fend_m = _qkv_call(meta_blk, g1, wqk, wvt, wfl, ws, wsvt, bfg, fqg, fkg, sqg,
                                                              skg, zero_f, tm=META_BLOCK, n_pad=META_PAD)
    fq, fk, aug, fvt, sq, sk, svt, _ = _qkv_call(x, g1, wqk, wvt, wfl, ws, wsvt, bfg, fqg, fkg, sqg, skg, fend_m[0],
                                                 tm=tl["tm"], n_pad=0)
    o_a = _fox_call(fq, fk, aug, fvt, fk_m[0], aug_m[0], fvt_m[0, :, 0], tq=tl["tq_fox"])
    o_b = _swa_call(sinks, sq, sk, svt, sk_m[0], svt_m[0, :, 0], tq=tl["tq_swa"])

    hres, h2, lg = _mix_call(x.reshape(N, D), o_a.reshape(N, FOX_W), o_b.reshape(N, SWA_QW),
                             g1, wga, wgb, wfo, wso, wo, g2, wr, br, tm=tl["tm_mix"])

    T = tl["t_route"]
    nt = N // T
    block_rows = tl["block_rows"]
    R = -(-(TOP_K * T + N_EXPERTS * (RUN_ALIGN - 1)) // LANES) * LANES
    max_rows = N * TOP_K + nt * N_EXPERTS * (RUN_ALIGN - 1) + N_EXPERTS * (block_rows - RUN_ALIGN)
    nb = -(-max_rows // block_rows)
    a, at, cnt = _route_call(lg, T=T)
    goff, lbase, nch, eblk, gap, info = _plan_call(cnt[:, :, 0].reshape(-1), nt=nt, block_rows=block_rows)
    xs = _dispatch_call(goff, lbase, nch, gap, info, at, h2, T=T, R=R, rows=nb * block_rows, block_rows=block_rows)
    ys = _expert_call(eblk, info, xs, wu, bu, wd, bd, block_rows=block_rows)
    out = _combine_call(goff, lbase, nch, a, hres, ys, T=T, R=R)
    return out.reshape(B, S, D)
```

```python
import functools

import jax
import jax.numpy as jnp
import numpy as np
from jax import lax
from jax.experimental import pallas as pl
from jax.experimental.pallas import tpu as pltpu

HEAD_DIM = 64
N_META = 16
FOX_HEADS = 8
SWA_HEADS = 8
SWA_KV_HEADS = 2
WINDOW = 128
N_EXPERTS = 32
TOP_K = 4
D_FF = 1024
SWIGLU_LIMIT = 7.0
SWIGLU_ALPHA = 1.702
RMS_EPS = 1e-6
NEG_INF = -1e30
LOG2E = 1.4426950408889634

LANES = 128
META_BLOCK = 128
META_PAD = META_BLOCK - N_META
FOX_PAIRS = 4
RUN_ALIGN = 8
VMEM_LIMIT = 56 * 1024 * 1024

FOX_W = FOX_HEADS * HEAD_DIM
SWA_QW = SWA_HEADS * HEAD_DIM
SWA_KW = SWA_KV_HEADS * HEAD_DIM

_BF = jnp.bfloat16
_F32 = jnp.float32


def _cparams(sem):
    return pltpu.CompilerParams(dimension_semantics=sem, vmem_limit_bytes=VMEM_LIMIT)


def _rms(t, gain):
    return t * lax.rsqrt(jnp.mean(t * t, axis=-1, keepdims=True) + RMS_EPS) * gain


def _head_rms(z, gain_row):
    lane = lax.broadcasted_iota(jnp.int32, (z.shape[0], LANES), 1)
    lo_mask = lane < HEAD_DIM
    outs = []
    for b in range(z.shape[1] // LANES):
        v = z[:, b * LANES:(b + 1) * LANES]
        v2 = v * v
        tot = jnp.sum(v2, axis=-1, keepdims=True)
        lo = jnp.sum(jnp.where(lo_mask, v2, 0.0), axis=-1, keepdims=True)
        hi = tot - lo
        r_lo = lax.rsqrt(lo * (1.0 / HEAD_DIM) + RMS_EPS)
        r_hi = lax.rsqrt(hi * (1.0 / HEAD_DIM) + RMS_EPS)
        outs.append(v * jnp.where(lo_mask, r_lo, r_hi))
    return jnp.concatenate(outs, axis=-1) * gain_row


def _lane_cumsum(x):
    n = x.shape[-1]
    lane = lax.broadcasted_iota(jnp.int32, x.shape, x.ndim - 1)
    s = 1
    while s < n:
        x = x + jnp.where(lane >= s, pltpu.roll(x, s, x.ndim - 1), 0.0)
        s *= 2
    return x


def _qkv_kernel(x_ref, g1_ref, wqk_ref, wvt_ref, wfl_ref, ws_ref, wsvt_ref, bf_ref, fqg_ref, fkg_ref, sqg_ref, skg_ref,
                f0_ref, fq_ref, fk_ref, aug_ref, fvt_ref, sq_ref, sk_ref, svt_ref, fend_ref, carry_ref, *, n_pad):
    j = pl.program_id(1)

    @pl.when(j == 0)
    def _():
        carry_ref[...] = f0_ref[...]

    x = x_ref[...]
    tm = x.shape[0]
    h = _rms(x, g1_ref[...]).astype(_BF)
    nt = (((1,), (1,)), ((), ()))
    f = jnp.dot(h, wqk_ref[...], preferred_element_type=_F32)
    s = jnp.dot(h, ws_ref[...], preferred_element_type=_F32)
    svt = lax.dot_general(wsvt_ref[...], h, nt, preferred_element_type=_F32)
    vt = lax.dot_general(wvt_ref[...], h, nt, preferred_element_type=_F32)
    flog = lax.dot_general(wfl_ref[...], h, nt, preferred_element_type=_F32)

    fq_ref[...] = _head_rms(f[:, :FOX_W], fqg_ref[...]).astype(_BF)
    fk_ref[...] = _head_rms(f[:, FOX_W:], fkg_ref[...]).astype(_BF)
    sq_ref[...] = _head_rms(s[:, :SWA_QW], sqg_ref[...]).astype(_BF)
    sk_ref[...] = _head_rms(s[:, SWA_QW:], skg_ref[...]).astype(_BF)

    ones_row = jnp.where(lax.broadcasted_iota(jnp.int32, (HEAD_DIM, tm), 0) == 0, 1.0, 0.0).astype(_BF)
    for hd in range(FOX_HEADS):
        fvt_ref[hd] = jnp.concatenate([vt[hd * HEAD_DIM:(hd + 1) * HEAD_DIM].astype(_BF), ones_row], axis=0)
    for g in range(SWA_KV_HEADS):
        blk = jnp.concatenate([svt[g * HEAD_DIM:(g + 1) * HEAD_DIM].astype(_BF), ones_row], axis=0)
        for c in range(tm // WINDOW):
            svt_ref[g, c] = blk[:, c * WINDOW:(c + 1) * WINDOW]

    z = flog + bf_ref[...]
    log_f = jnp.minimum(z, 0.0) - jnp.log1p(jnp.exp(-jnp.abs(z)))
    if n_pad:
        lane = lax.broadcasted_iota(jnp.int32, log_f.shape, 1)
        log_f = jnp.where(lane >= n_pad, log_f, 0.0)
    cum = _lane_cumsum(log_f) + carry_ref[...]
    carry_ref[...] = cum[:, -1:]
    fend_ref[...] = cum[:, -1:]
    nfl = cum * (-LOG2E)
    p1 = nfl.astype(_BF).astype(_F32)
    p2 = (nfl - p1).astype(_BF).astype(_F32)
    p3 = (nfl - p1 - p2).astype(_BF).astype(_F32)
    g = jnp.concatenate([p1, p2, p3, jnp.zeros((LANES - 3 * FOX_HEADS, tm), _F32)], axis=0)
    aug_ref[...] = g.T.astype(_BF)


def _qkv_call(x3, g1, wqk, wvt, wfl, ws, wsvt, bfg, fqg, fkg, sqg, skg, f0, *, tm, n_pad):
    B, S, D = x3.shape
    nj = S // tm
    row = lambda w: pl.BlockSpec((None, tm, w), lambda b, j: (b, j, 0))
    full = lambda a: pl.BlockSpec(a.shape, lambda b, j: (0,) * a.ndim)
    bsd = lambda w: jax.ShapeDtypeStruct((B, S, w), _BF)
    out_shape = [bsd(FOX_W), bsd(FOX_W), bsd(LANES), jax.ShapeDtypeStruct((B, FOX_HEADS, nj, 2 * HEAD_DIM, tm), _BF),
                 bsd(SWA_QW), bsd(SWA_KW),
                 jax.ShapeDtypeStruct((B, SWA_KV_HEADS, S // WINDOW, 2 * HEAD_DIM, WINDOW), _BF),
                 jax.ShapeDtypeStruct((B, FOX_HEADS, 1), _F32)]
    out_specs = [row(FOX_W), row(FOX_W), row(LANES),
                 pl.BlockSpec((None, FOX_HEADS, None, 2 * HEAD_DIM, tm), lambda b, j: (b, 0, j, 0, 0)),
                 row(SWA_QW), row(SWA_KW),
                 pl.BlockSpec((None, SWA_KV_HEADS, tm // WINDOW, 2 * HEAD_DIM, WINDOW), lambda b, j: (b, 0, j, 0, 0)),
                 pl.BlockSpec((None, FOX_HEADS, 1), lambda b, j: (b, 0, 0))]
    return pl.pallas_call(
        functools.partial(_qkv_kernel, n_pad=n_pad),
        grid=(B, nj),
        in_specs=[row(D)] + [full(a) for a in (g1, wqk, wvt, wfl, ws, wsvt, bfg, fqg, fkg, sqg, skg, f0)],
        out_specs=out_specs, out_shape=out_shape,
        scratch_shapes=[pltpu.VMEM((FOX_HEADS, 1), _F32)],
        compiler_params=_cparams(("arbitrary", "arbitrary")),
        name="qkv_proj",
    )(x3, g1, wqk, wvt, wfl, ws, wsvt, bfg, fqg, fkg, sqg, skg, f0)


def _fox_kernel(q_ref, k_ref, aug_ref, vt_ref, km_ref, augm_ref, vtm_ref, o_ref, m_ref, acc_ref, s_ref, mx_ref):
    g = pl.program_id(1)
    qi = pl.program_id(2)
    tq = q_ref.shape[0]
    tk = vt_ref.shape[-1]
    nt = (((1,), (1,)), ((), ()))
    lane = lax.broadcasted_iota(jnp.int32, (tq, LANES), 1)

    qaugs = []
    for pp in range(FOX_PAIRS):
        q = q_ref[:, pp * LANES:(pp + 1) * LANES]
        parts = []
        for a in range(2):
            head = 2 * (FOX_PAIRS * g + pp) + a
            keep = (lane < HEAD_DIM) if a == 0 else (lane >= HEAD_DIM)
            ones = jnp.where((lane < 3 * FOX_HEADS) & ((lane & (FOX_HEADS - 1)) == head), 1.0, 0.0).astype(_BF)
            parts.append(jnp.concatenate([jnp.where(keep, q, jnp.zeros_like(q)), ones], axis=1))
        qaugs.append(jnp.concatenate(parts, axis=0))

    def scores(pp, k, aug, mask):
        st = lax.dot_general(jnp.concatenate([k, aug], axis=1), qaugs[pp], nt, preferred_element_type=_F32)
        if mask is not None:
            st = jnp.where(mask, st, NEG_INF)
        return st, jnp.max(st, axis=0, keepdims=True)

    def tile_scores(pp, ki, mask=None):
        off = pl.multiple_of(ki * tk, tk)
        return scores(pp, k_ref[pl.ds(off, tk), pp * LANES:(pp + 1) * LANES], aug_ref[pl.ds(off, tk), :], mask)

    def process(pp, st, mx, vts, first):
        m_new = mx if first else jnp.maximum(m_ref[pp], mx)
        pt = jnp.exp2(st - m_new).astype(_BF)
        if not first:
            alpha = jnp.exp2(m_ref[pp] - m_new)
        for a in range(2):
            pv = jnp.dot(vts[a], pt[:, a * tq:(a + 1) * tq], preferred_element_type=_F32)
            acc_ref[2 * pp + a] = pv if first else alpha[:, a * tq:(a + 1) * tq] * acc_ref[2 * pp + a] + pv
        m_ref[pp] = m_new

    def stash(buf, pp, st_mx):
        s_ref[buf, pp] = st_mx[0]
        mx_ref[buf, pp] = st_mx[1]

    def vts(pp, ki):
        return vt_ref[2 * pp, ki], vt_ref[2 * pp + 1, ki]

    causal = lax.broadcasted_iota(jnp.int32, (tk, 2 * tq), 0) <= (lax.broadcasted_iota(jnp.int32, (tk, 2 * tq), 1) & (tq - 1))
    mask_m = lax.broadcasted_iota(jnp.int32, (META_BLOCK, 2 * tq), 0) >= META_PAD
    for pp in range(FOX_PAIRS):
        stash(0, pp, tile_scores(pp, qi, causal))
        st, mx = scores(pp, km_ref[:, pp * LANES:(pp + 1) * LANES], augm_ref[...], mask_m)
        process(pp, st, mx, (vtm_ref[2 * pp], vtm_ref[2 * pp + 1]), True)

    def step(j, buf):
        nxt = [tile_scores(pp, j) for pp in range(FOX_PAIRS)]
        cur = jnp.where(j == 0, qi, j - 1)
        for pp in range(FOX_PAIRS):
            process(pp, s_ref[buf, pp], mx_ref[buf, pp], vts(pp, cur), False)
        for pp in range(FOX_PAIRS):
            stash(1 - buf, pp, nxt[pp])

    def body(jj, c):
        step(2 * jj, 0)
        step(2 * jj + 1, 1)
        return c
    lax.fori_loop(0, qi // 2, body, 0)

    def finish(buf):
        last = jnp.where(qi == 0, qi, qi - 1)
        outs = []
        for pp in range(FOX_PAIRS):
            process(pp, s_ref[buf, pp], mx_ref[buf, pp], vts(pp, last), False)
            for a in range(2):
                acc = acc_ref[2 * pp + a]
                outs.append(acc[:HEAD_DIM] / acc[HEAD_DIM:HEAD_DIM + 1])
        o_ref[...] = jnp.concatenate(outs, axis=0).T.astype(_BF)

    @pl.when(qi % 2 == 1)
    def _():
        step(qi - 1, 0)
        finish(1)

    @pl.when(qi % 2 == 0)
    def _():
        finish(0)


def _fox_call(fq, fk, aug, fvt, km, augm, vtm, *, tq):
    B, S, _ = fq.shape
    nk = fvt.shape[2]
    assert fvt.shape[-1] == tq
    w = FOX_PAIRS * LANES
    qspec = pl.BlockSpec((None, tq, w), lambda b, g, i: (b, i, g))
    return pl.pallas_call(
        _fox_kernel, grid=(B, FOX_HEADS // (2 * FOX_PAIRS), S // tq),
        in_specs=[qspec,
                  pl.BlockSpec((None, S, w), lambda b, g, i: (b, 0, g)),
                  pl.BlockSpec((None, S, LANES), lambda b, g, i: (b, 0, 0)),
                  pl.BlockSpec((None, 2 * FOX_PAIRS, nk, 2 * HEAD_DIM, tq), lambda b, g, i: (b, g, 0, 0, 0)),
                  pl.BlockSpec((META_BLOCK, w), lambda b, g, i: (0, g)),
                  pl.BlockSpec((META_BLOCK, LANES), lambda b, g, i: (0, 0)),
                  pl.BlockSpec((2 * FOX_PAIRS, 2 * HEAD_DIM, META_BLOCK), lambda b, g, i: (g, 0, 0))],
        out_specs=qspec, out_shape=jax.ShapeDtypeStruct((B, S, FOX_W), _BF),
        scratch_shapes=[pltpu.VMEM((FOX_PAIRS, 1, 2 * tq), _F32), pltpu.VMEM((2 * FOX_PAIRS, 2 * HEAD_DIM, tq), _F32),
                        pltpu.VMEM((2, FOX_PAIRS, tq, 2 * tq), _F32), pltpu.VMEM((2, FOX_PAIRS, 1, 2 * tq), _F32)],
        compiler_params=_cparams(("arbitrary", "arbitrary", "arbitrary")), name="fox_attn",
    )(fq, fk, aug, fvt, km, augm, vtm)


def _swa_bias(tq):
    j = np.arange(WINDOW + tq)[:, None] - WINDOW
    i = np.arange(tq)[None, :]
    dist = i - j
    ok = (dist >= 0) & (dist < WINDOW)
    slopes = np.exp2(-8.0 * (np.arange(SWA_HEADS, dtype=np.float32) + 1.0) / SWA_HEADS).astype(np.float32)
    per_head = np.where(ok[None], -slopes[:, None, None] * dist[None].astype(np.float32), np.float32(NEG_INF))
    rep = SWA_HEADS // SWA_KV_HEADS
    return np.stack([np.concatenate(list(per_head[g * rep:(g + 1) * rep]), axis=1) for g in range(SWA_KV_HEADS)])


def _swa_kernel(q_ref, kc_ref, kp_ref, km_ref, vc_ref, vp_ref, vm_ref, bias_ref, sink_ref, o_ref):
    qi = pl.program_id(1)
    tq = q_ref.shape[0]
    rep = SWA_HEADS // SWA_KV_HEADS
    nt = (((1,), (1,)), ((), ()))
    first = qi == 0
    kcat = jnp.concatenate([jnp.where(first, km_ref[...], kp_ref[...]), kc_ref[...]], axis=0)
    row = lax.broadcasted_iota(jnp.int32, (WINDOW + tq, rep * tq), 0)
    row_ok = row >= jnp.where(first, META_PAD, 0)
    lane = lax.broadcasted_iota(jnp.int32, (tq, LANES), 1)
    outs = [None] * SWA_HEADS
    for g in range(SWA_KV_HEADS):
        keep = (lane < HEAD_DIM) if g == 0 else (lane >= HEAD_DIM)
        qg = jnp.concatenate([jnp.where(keep, q_ref[:, p * LANES:(p + 1) * LANES], jnp.zeros((tq, LANES), _BF))
                              for p in range(rep)], axis=0)
        vcat = jnp.concatenate([jnp.where(first, vm_ref[g], vp_ref[g, 0])] + [vc_ref[g, c] for c in range(tq // WINDOW)],
                               axis=1)
        st = lax.dot_general(kcat, qg, nt, preferred_element_type=_F32) + bias_ref[g]
        st = jnp.where(row_ok, st, NEG_INF)
        sink = sink_ref[g]
        m = jnp.maximum(jnp.max(st, axis=0, keepdims=True), sink)
        pt = jnp.exp(st - m).astype(_BF)
        acc = jnp.dot(vcat, pt, preferred_element_type=_F32)
        o = acc[:HEAD_DIM] / (acc[HEAD_DIM:HEAD_DIM + 1] + jnp.exp(sink - m))
        for p in range(rep):
            outs[2 * p + g] = o[:, p * tq:(p + 1) * tq]
    o_ref[...] = jnp.concatenate(outs, axis=0).T.astype(_BF)


def _swa_call(sinks, sq, sk, svt, km, vtm, *, tq):
    B, S, _ = sq.shape
    r = tq // WINDOW
    rep = SWA_HEADS // SWA_KV_HEADS
    bias = jnp.asarray(_swa_bias(tq))
    sink_rows = jnp.repeat(sinks.reshape(SWA_KV_HEADS, rep), tq, axis=1)[:, None, :]
    prev_blk = lambda i: jnp.maximum(i * r - 1, 0)
    return pl.pallas_call(
        _swa_kernel, grid=(B, S // tq),
        in_specs=[pl.BlockSpec((None, tq, SWA_QW), lambda b, i: (b, i, 0)),
                  pl.BlockSpec((None, tq, LANES), lambda b, i: (b, i, 0)),
                  pl.BlockSpec((None, WINDOW, LANES), lambda b, i: (b, prev_blk(i), 0)),
                  pl.BlockSpec((META_BLOCK, LANES), lambda b, i: (0, 0)),
                  pl.BlockSpec((None, SWA_KV_HEADS, r, 2 * HEAD_DIM, WINDOW), lambda b, i: (b, 0, i, 0, 0)),
                  pl.BlockSpec((None, SWA_KV_HEADS, 1, 2 * HEAD_DIM, WINDOW), lambda b, i: (b, 0, prev_blk(i), 0, 0)),
                  pl.BlockSpec((SWA_KV_HEADS, 2 * HEAD_DIM, META_BLOCK), lambda b, i: (0, 0, 0)),
                  pl.BlockSpec(bias.shape, lambda b, i: (0, 0, 0)),
                  pl.BlockSpec(sink_rows.shape, lambda b, i: (0, 0, 0))],
        out_specs=pl.BlockSpec((None, tq, SWA_QW), lambda b, i: (b, i, 0)),
        out_shape=jax.ShapeDtypeStruct((B, S, SWA_QW), _BF),
        compiler_params=_cparams(("arbitrary", "arbitrary")), name="swa_attn",
    )(sq, sk, sk, km, svt, svt, vtm, bias, sink_rows)


def _mix_kernel(x_ref, oa_ref, ob_ref, g1_ref, wga_ref, wgb_ref, wfo_ref, wso_ref, wo_ref, g2_ref, wr_ref, br_ref,
                hres_ref, h2_ref, lg_ref):
    x = x_ref[...]
    h = _rms(x, g1_ref[...]).astype(_BF)
    ga = jax.nn.sigmoid(jnp.dot(h, wga_ref[...], preferred_element_type=_F32))
    gb = jax.nn.sigmoid(jnp.dot(h, wgb_ref[...], preferred_element_type=_F32))
    mix = ga * jnp.dot(oa_ref[...], wfo_ref[...], preferred_element_type=_F32) \
        + gb * jnp.dot(ob_ref[...], wso_ref[...], preferred_element_type=_F32)
    hres = x + jnp.dot(mix.astype(_BF), wo_ref[...], preferred_element_type=_F32)
    hres_ref[...] = hres
    h2 = _rms(hres, g2_ref[...])
    h2_ref[...] = h2.astype(_BF)
    wr = wr_ref[...]
    w_hi = wr.astype(_BF)
    w_lo = (wr - w_hi.astype(_F32)).astype(_BF)
    h_hi = h2.astype(_BF)
    h_lo = (h2 - h_hi.astype(_F32)).astype(_BF)
    lg = jnp.dot(h_hi, w_hi, preferred_element_type=_F32) + jnp.dot(h_hi, w_lo, preferred_element_type=_F32) \
        + jnp.dot(h_lo, w_hi, preferred_element_type=_F32)
    lg_ref[...] = lg + br_ref[...]


def _mix_call(x2, oa, ob, g1, wga, wgb, wfo, wso, wo, g2, wr, br, *, tm):
    N, D = x2.shape
    row = lambda w: pl.BlockSpec((tm, w), lambda i: (i, 0))
    full = lambda a: pl.BlockSpec(a.shape, lambda i: (0,) * a.ndim)
    return pl.pallas_call(
        _mix_kernel, grid=(N // tm,),
        in_specs=[row(D), row(FOX_W), row(SWA_QW)] + [full(a) for a in (g1, wga, wgb, wfo, wso, wo, g2, wr, br)],
        out_specs=[row(D), row(D), row(LANES)],
        out_shape=[jax.ShapeDtypeStruct((N, D), _F32), jax.ShapeDtypeStruct((N, D), _BF),
                   jax.ShapeDtypeStruct((N, LANES), _F32)],
        compiler_params=_cparams(("arbitrary",)), name="mix_proj",
    )(x2, oa, ob, g1, wga, wgb, wfo, wso, wo, g2, wr, br)


def _route_kernel(lg_ref, a_ref, at_ref, cnt_ref):
    T = lg_ref.shape[0]
    lt = lg_ref[...].T[:N_EXPERTS]
    e_iota = lax.broadcasted_iota(jnp.int32, lt.shape, 0)
    cur = lt
    vals, idxs = [], []
    for _ in range(TOP_K):
        m = jnp.max(cur, axis=0, keepdims=True)
        idx = jnp.min(jnp.where(cur == m, e_iota, N_EXPERTS), axis=0, keepdims=True)
        vals.append(m)
        idxs.append(idx)
        cur = jnp.where(e_iota == idx, -jnp.inf, cur)
    es = [jnp.exp(v - vals[0]) for v in vals]
    den = es[0] + es[1] + es[2] + es[3]
    gates = [e / den for e in es]

    onehot = jnp.zeros(lt.shape, _F32)
    for idx in idxs:
        onehot = onehot + jnp.where(e_iota == idx, 1.0, 0.0)
    r_i = lax.broadcasted_iota(jnp.int32, (T, T), 0)
    c_i = lax.broadcasted_iota(jnp.int32, (T, T), 1)
    utri = jnp.where(r_i < c_i, 1.0, 0.0).astype(_BF)
    prefix = jnp.dot(onehot.astype(_BF), utri, preferred_element_type=_F32)
    cnt = jnp.sum(onehot, axis=1, keepdims=True)
    chunks = jnp.floor((cnt + (RUN_ALIGN - 1)) * (1.0 / RUN_ALIGN))
    l_r = lax.broadcasted_iota(jnp.int32, (N_EXPERTS, N_EXPERTS), 0)
    l_c = lax.broadcasted_iota(jnp.int32, (N_EXPERTS, N_EXPERTS), 1)
    ltri = jnp.where(l_c < l_r, 1.0, 0.0).astype(_BF)
    lbase = RUN_ALIGN * jnp.dot(ltri, jnp.broadcast_to(chunks, (N_EXPERTS, LANES)).astype(_BF),
                                preferred_element_type=_F32)[:, 0:1]
    slot_all = prefix + lbase
    rows = [jnp.sum(jnp.where(e_iota == idx, slot_all, 0.0), axis=0, keepdims=True) for idx in idxs] + gates
    at = jnp.concatenate(rows, axis=0)
    at_ref[...] = at
    a_ref[...] = jnp.concatenate([at, jnp.zeros((LANES - 2 * TOP_K, T), _F32)], axis=0).T
    cnt_ref[...] = jnp.broadcast_to(cnt, (N_EXPERTS, LANES)).astype(jnp.int32)


def _route_call(lg, *, T):
    N = lg.shape[0]
    nt = N // T
    return pl.pallas_call(
        _route_kernel, grid=(nt,),
        in_specs=[pl.BlockSpec((T, LANES), lambda i: (i, 0))],
        out_specs=[pl.BlockSpec((T, LANES), lambda i: (i, 0)), pl.BlockSpec((8, T), lambda i: (0, i)),
                   pl.BlockSpec((None, N_EXPERTS, LANES), lambda i: (i, 0, 0))],
        out_shape=[jax.ShapeDtypeStruct((N, LANES), _F32), jax.ShapeDtypeStruct((8, N), _F32),
                   jax.ShapeDtypeStruct((nt, N_EXPERTS, LANES), jnp.int32)],
        compiler_params=_cparams(("arbitrary",)), name="route",
    )(lg)


def _div_pow2(v, d):
    assert d & (d - 1) == 0
    return lax.shift_right_logical(v, d.bit_length() - 1)


def _plan_kernel(cnt_ref, goff_ref, lbase_ref, nch_ref, eblk_ref, gap_ref, info_ref, *, nt, block_rows):
    def per_expert(e, base):
        def per_tile(i, run):
            n = _div_pow2(cnt_ref[i * N_EXPERTS + e] + (RUN_ALIGN - 1), RUN_ALIGN)
            goff_ref[i * N_EXPERTS + e] = run
            nch_ref[i * N_EXPERTS + e] = n
            return run + n * RUN_ALIGN
        end = lax.fori_loop(0, nt, per_tile, base)
        nblk = _div_pow2(end - base + (block_rows - 1), block_rows)
        nxt = base + nblk * block_rows
        gap_ref[2 * e] = end
        gap_ref[2 * e + 1] = _div_pow2(nxt - end, RUN_ALIGN)
        eblk_ref[2 * e] = _div_pow2(base, block_rows)
        eblk_ref[2 * e + 1] = nblk
        return nxt
    total = lax.fori_loop(0, N_EXPERTS, per_expert, 0)
    info_ref[0] = _div_pow2(total, block_rows)

    def per_tile2(i, c):
        def per_e(e, run):
            lbase_ref[i * N_EXPERTS + e] = run
            return run + nch_ref[i * N_EXPERTS + e] * RUN_ALIGN
        lax.fori_loop(0, N_EXPERTS, per_e, 0)
        return c
    lax.fori_loop(0, nt, per_tile2, 0)


def _plan_call(cnt_flat, *, nt, block_rows):
    smem = pl.BlockSpec(memory_space=pltpu.SMEM)
    i32 = lambda n: jax.ShapeDtypeStruct((n,), jnp.int32)
    return pl.pallas_call(
        functools.partial(_plan_kernel, nt=nt, block_rows=block_rows),
        in_specs=[smem], out_specs=[smem] * 6,
        out_shape=[i32(nt * N_EXPERTS), i32(nt * N_EXPERTS), i32(nt * N_EXPERTS), i32(2 * N_EXPERTS), i32(2 * N_EXPERTS),
                   i32(1)],
        name="plan",
    )(cnt_flat)


def _pack_pairs(v):
    w = v.shape[1] // 2
    return pltpu.pack_elementwise([v[:, :w], v[:, w:]], packed_dtype=_BF)


def _unpack_pairs(wd):
    lo = pltpu.unpack_elementwise(wd, index=0, packed_dtype=_BF, unpacked_dtype=_F32)
    hi = pltpu.unpack_elementwise(wd, index=1, packed_dtype=_BF, unpacked_dtype=_F32)
    return lo.astype(_BF), hi.astype(_BF)


def _dispatch_kernel(goff, lbase, nch, gap, info, at_ref, h2_ref, xs_ref, xl_ref, z_ref, sem, zsem, tsem):
    i = pl.program_id(0)
    R = xl_ref.shape[1]
    T = h2_ref.shape[0]
    block_rows = z_ref.shape[0]
    nb = xs_ref.shape[0] // block_rows

    @pl.when(i == 0)
    def _():
        z_ref[...] = jnp.zeros(z_ref.shape, z_ref.dtype)

        def gap_copy(e):
            rows = pl.multiple_of(gap[2 * e + 1] * RUN_ALIGN, RUN_ALIGN)
            return pltpu.make_async_copy(z_ref.at[pl.ds(0, rows)],
                                         xs_ref.at[pl.ds(pl.multiple_of(gap[2 * e], RUN_ALIGN), rows)], zsem)

        def fill(e, c):
            @pl.when(gap[2 * e + 1] > 0)
            def _():
                gap_copy(e).start()
            return c
        lax.fori_loop(0, N_EXPERTS, fill, 0)

        def fill_wait(e, c):
            @pl.when(gap[2 * e + 1] > 0)
            def _():
                gap_copy(e).wait()
            return c
        lax.fori_loop(0, N_EXPERTS, fill_wait, 0)

        def tail(b, c):
            pltpu.make_async_copy(z_ref, xs_ref.at[pl.ds(pl.multiple_of(b * block_rows, block_rows), block_rows)],
                                  tsem).start()
            return c
        lax.fori_loop(info[0], nb, tail, 0)

        def tail_wait(b, c):
            pltpu.make_async_copy(z_ref, xs_ref.at[pl.ds(0, block_rows)], tsem).wait()
            return c
        lax.fori_loop(info[0], nb, tail_wait, 0)

    slot_iota = lax.broadcasted_iota(jnp.int32, (R, T), 0).astype(_F32)
    pm = jnp.zeros((R, T), _F32)
    for k in range(TOP_K):
        pm = jnp.where(slot_iota == at_ref[k:k + 1, :], 1.0, pm)
    xl = jnp.dot(pm.astype(_BF), h2_ref[...], preferred_element_type=_F32)
    slot = i & 1

    def wait_tile(t, s):
        total = lax.fori_loop(0, N_EXPERTS, lambda e, c: c + nch[t * N_EXPERTS + e], 0)

        @pl.when(total > 0)
        def _():
            rows = pl.multiple_of(total * RUN_ALIGN, RUN_ALIGN)
            pltpu.make_async_copy(xl_ref.at[s, pl.ds(0, rows)], xs_ref.at[pl.ds(0, rows)], sem.at[s]).wait()

    @pl.when(i >= 2)
    def _():
        wait_tile(i - 2, slot)
    xl_ref[slot] = _pack_pairs(xl)

    def issue(e, c):
        n = nch[i * N_EXPERTS + e]

        @pl.when(n > 0)
        def _():
            rows = pl.multiple_of(n * RUN_ALIGN, RUN_ALIGN)
            pltpu.make_async_copy(xl_ref.at[slot, pl.ds(pl.multiple_of(lbase[i * N_EXPERTS + e], RUN_ALIGN), rows)],
                                  xs_ref.at[pl.ds(pl.multiple_of(goff[i * N_EXPERTS + e], RUN_ALIGN), rows)],
                                  sem.at[slot]).start()
        return c
    lax.fori_loop(0, N_EXPERTS, issue, 0)

    @pl.when(i == pl.num_programs(0) - 1)
    def _():
        @pl.when(i >= 1)
        def _():
            wait_tile(i - 1, 1 - slot)
        wait_tile(i, slot)


def _dispatch_call(goff, lbase, nch, gap, info, at, h2, *, T, R, rows, block_rows):
    N, D = h2.shape
    grid_spec = pltpu.PrefetchScalarGridSpec(
        num_scalar_prefetch=5, grid=(N // T,),
        in_specs=[pl.BlockSpec((8, T), lambda i, *_: (0, i)), pl.BlockSpec((T, D), lambda i, *_: (i, 0))],
        out_specs=pl.BlockSpec(memory_space=pl.ANY),
        scratch_shapes=[pltpu.VMEM((2, R, D // 2), jnp.uint32), pltpu.VMEM((block_rows, D // 2), jnp.uint32),
                        pltpu.SemaphoreType.DMA((2,)), pltpu.SemaphoreType.DMA, pltpu.SemaphoreType.DMA])
    return pl.pallas_call(
        _dispatch_kernel, grid_spec=grid_spec, out_shape=jax.ShapeDtypeStruct((rows, D // 2), jnp.uint32),
        compiler_params=_cparams(("arbitrary",)), name="dispatch",
    )(goff, lbase, nch, gap, info, at, h2)


def _expert_kernel(eblk, info, xs_ref, wu_ref, bu_ref, wd_ref, bd_ref, ys_ref, wub_ref, wdb_ref, xb_ref, yb_ref, xsem, ysem):
    e = pl.program_id(0)
    block_rows = xb_ref.shape[1]
    nb = ys_ref.shape[0] // block_rows
    b0 = eblk[2 * e]
    n = eblk[2 * e + 1]
    half = wu_ref.shape[0] // 2
    wub_ref[...] = wu_ref[...].astype(_BF)
    wdb_ref[...] = wd_ref[...].astype(_BF)

    def rows_of(blk):
        return pl.ds(pl.multiple_of(blk * block_rows, block_rows), block_rows)

    def x_copy(j, slot):
        return pltpu.make_async_copy(xs_ref.at[rows_of(b0 + j)], xb_ref.at[slot], xsem.at[slot])

    def y_copy(j, slot):
        return pltpu.make_async_copy(yb_ref.at[slot], ys_ref.at[rows_of(b0 + j)], ysem.at[slot])

    @pl.when((e == 0) & (n > 0))
    def _():
        x_copy(0, 0).start()

    def body(j, c):
        slot = j & 1
        x_copy(j, slot).wait()

        @pl.when(j + 1 < n)
        def _():
            x_copy(j + 1, 1 - slot).start()

        @pl.when(j >= 2)
        def _():
            y_copy(j - 2, slot).wait()

        x_lo, x_hi = _unpack_pairs(xb_ref[slot])
        gu = jnp.dot(x_lo, wub_ref[:half, :], preferred_element_type=_F32) \
            + jnp.dot(x_hi, wub_ref[half:, :], preferred_element_type=_F32) + bu_ref[...]
        g = jnp.minimum(gu[:, :D_FF], SWIGLU_LIMIT)
        u = jnp.clip(gu[:, D_FF:], -SWIGLU_LIMIT, SWIGLU_LIMIT)
        act = (u + 1.0) * (g * jax.nn.sigmoid(SWIGLU_ALPHA * g))
        y = jnp.dot(act.astype(_BF), wdb_ref[...], preferred_element_type=_F32) + bd_ref[...]
        yb_ref[slot] = _pack_pairs(y)
        y_copy(j, slot).start()
        return c
    lax.fori_loop(0, n, body, 0)

    @pl.when(n >= 2)
    def _():
        y_copy(n - 2, n & 1).wait()

    @pl.when(n >= 1)
    def _():
        y_copy(n - 1, (n - 1) & 1).wait()

    last = pl.num_programs(0) - 1
    e_next = jnp.minimum(e + 1, last)

    @pl.when((e < last) & (eblk[2 * e_next + 1] > 0))
    def _():
        pltpu.make_async_copy(xs_ref.at[rows_of(eblk[2 * e_next])], xb_ref.at[0], xsem.at[0]).start()

    @pl.when(e == last)
    def _():
        yb_ref[0] = _pack_pairs(jnp.zeros((block_rows, 2 * yb_ref.shape[2]), _F32))

        def tail_copy(b):
            return pltpu.make_async_copy(yb_ref.at[0], ys_ref.at[pl.ds(pl.multiple_of(b * block_rows, block_rows), block_rows)],
                                         ysem.at[0])

        def tail(b, c):
            tail_copy(b).start()
            return c
        lax.fori_loop(info[0], nb, tail, 0)

        def tail_wait(b, c):
            tail_copy(b).wait()
            return c
        lax.fori_loop(info[0], nb, tail_wait, 0)


def _expert_call(eblk, info, xs, wu, bu, wd, bd, *, block_rows):
    rows, half = xs.shape
    D = 2 * half
    ex = lambda e, *_: (e, 0, 0)
    grid_spec = pltpu.PrefetchScalarGridSpec(
        num_scalar_prefetch=2, grid=(N_EXPERTS,),
        in_specs=[pl.BlockSpec(memory_space=pl.ANY),
                  pl.BlockSpec((None, D, 2 * D_FF), ex), pl.BlockSpec((None, 1, 2 * D_FF), ex),
                  pl.BlockSpec((None, D_FF, D), ex), pl.BlockSpec((None, 1, D), ex)],
        out_specs=pl.BlockSpec(memory_space=pl.ANY),
        scratch_shapes=[pltpu.VMEM((D, 2 * D_FF), _BF), pltpu.VMEM((D_FF, D), _BF),
                        pltpu.VMEM((2, block_rows, half), jnp.uint32), pltpu.VMEM((2, block_rows, half), jnp.uint32),
                        pltpu.SemaphoreType.DMA((2,)), pltpu.SemaphoreType.DMA((2,))])
    return pl.pallas_call(
        _expert_kernel, grid_spec=grid_spec, out_shape=jax.ShapeDtypeStruct((rows, half), jnp.uint32),
        compiler_params=_cparams(("arbitrary",)), name="experts",
    )(eblk, info, xs, wu, bu, wd, bd)


def _combine_kernel(goff, lbase, nch, a_ref, hres_ref, ys_ref, o_ref, yl_ref, sem):
    i = pl.program_id(0)
    nt = pl.num_programs(0)
    R = yl_ref.shape[1]
    T = a_ref.shape[0]

    def fetch(t, slot):
        def issue(e, c):
            n = nch[t * N_EXPERTS + e]

            @pl.when(n > 0)
            def _():
                rows = pl.multiple_of(n * RUN_ALIGN, RUN_ALIGN)
                pltpu.make_async_copy(
                    ys_ref.at[pl.ds(pl.multiple_of(goff[t * N_EXPERTS + e], RUN_ALIGN), rows)],
                    yl_ref.at[slot, pl.ds(pl.multiple_of(lbase[t * N_EXPERTS + e], RUN_ALIGN), rows)], sem.at[slot]).start()
            return c
        lax.fori_loop(0, N_EXPERTS, issue, 0)

    @pl.when(i == 0)
    def _():
        yl_ref[...] = jnp.zeros(yl_ref.shape, yl_ref.dtype)
        fetch(0, 0)

    slot = i & 1

    @pl.when(i + 1 < nt)
    def _():
        fetch(i + 1, 1 - slot)

    total = lax.fori_loop(0, N_EXPERTS, lambda e, c: c + nch[i * N_EXPERTS + e], 0)

    @pl.when(total > 0)
    def _():
        rows = pl.multiple_of(total * RUN_ALIGN, RUN_ALIGN)
        pltpu.make_async_copy(ys_ref.at[pl.ds(0, rows)], yl_ref.at[slot, pl.ds(0, rows)], sem.at[slot]).wait()

    a = a_ref[...]
    slot_iota = lax.broadcasted_iota(jnp.int32, (T, R), 1).astype(_F32)
    pt = jnp.zeros((T, R), _F32)
    for k in range(TOP_K):
        pt = jnp.where(slot_iota == a[:, k:k + 1], a[:, TOP_K + k:TOP_K + k + 1], pt)
    pt = pt.astype(_BF)
    y_lo, y_hi = _unpack_pairs(yl_ref[slot])
    half = y_lo.shape[1]
    o_ref[:, :half] = hres_ref[:, :half] + jnp.dot(pt, y_lo, preferred_element_type=_F32)
    o_ref[:, half:] = hres_ref[:, half:] + jnp.dot(pt, y_hi, preferred_element_type=_F32)


def _combine_call(goff, lbase, nch, a, hres, ys, *, T, R):
    N, D = hres.shape
    grid_spec = pltpu.PrefetchScalarGridSpec(
        num_scalar_prefetch=3, grid=(N // T,),
        in_specs=[pl.BlockSpec((T, LANES), lambda i, *_: (i, 0)), pl.BlockSpec((T, D), lambda i, *_: (i, 0)),
                  pl.BlockSpec(memory_space=pl.ANY)],
        out_specs=pl.BlockSpec((T, D), lambda i, *_: (i, 0)),
        scratch_shapes=[pltpu.VMEM((2, R, D // 2), jnp.uint32), pltpu.SemaphoreType.DMA((2,))])
    return pl.pallas_call(
        _combine_kernel, grid_spec=grid_spec, out_shape=jax.ShapeDtypeStruct((N, D), _F32),
        compiler_params=_cparams(("arbitrary",)), name="combine",
    )(goff, lbase, nch, a, hres, ys)


def _tiles(S):
    pick = lambda pref: next(t for t in pref if S % t == 0)
    tm = pick((512, 256, 128))
    return dict(tm=tm, tq_fox=tm, tq_swa=pick((256, 128)),
                tm_mix=pick((512, 256, 128)), t_route=pick((256, 128)), block_rows=256)


def kernel(x, meta_tokens, norm1_gain, w_in, b_forget, fox_q_gain, fox_k_gain, swa_q_gain, swa_k_gain, swa_sinks,
           w_fox_out, w_swa_out, w_out, norm2_gain, w_router, b_router, w_up, b_up, w_down, b_down):
    B, S, D = x.shape
    assert norm1_gain.shape[0] == 1 and S % LANES == 0
    tl = _tiles(S)
    N = B * S
    scale = HEAD_DIM ** -0.5

    w = w_in[0]
    c0 = 3 * FOX_W
    c1 = c0 + FOX_HEADS
    c2 = c1 + SWA_QW + 2 * SWA_KW
    wqk = w[:, :2 * FOX_W].astype(_BF)
    wvt = w[:, 2 * FOX_W:c0].T.astype(_BF)
    wfl = w[:, c0:c1].T.astype(_BF)
    perm_heads = np.array([h for p in range(SWA_HEADS // 2) for h in (p, p + SWA_HEADS // 2)])
    perm_cols = (perm_heads[:, None] * HEAD_DIM + np.arange(HEAD_DIM)[None, :]).reshape(-1)
    ws_all = w[:, c1:c2]
    ws = jnp.concatenate([ws_all[:, :SWA_QW][:, perm_cols], ws_all[:, SWA_QW:SWA_QW + SWA_KW]], axis=1).astype(_BF)
    wsvt = ws_all[:, SWA_QW + SWA_KW:].T.astype(_BF)
    wga = w[:, c2:c2 + D].astype(_BF)
    wgb = w[:, c2 + D:].astype(_BF)
    g1 = norm1_gain[0][None, :]
    g2 = norm2_gain[0][None, :]
    bfg = b_forget[0][:, None]
    fqg = jnp.tile(fox_q_gain[0], FOX_HEADS)[None, :] * (scale * LOG2E)
    fkg = jnp.tile(fox_k_gain[0], FOX_HEADS)[None, :]
    sqg = jnp.tile(swa_q_gain[0], SWA_HEADS)[None, :] * scale
    skg = jnp.tile(swa_k_gain[0], SWA_KV_HEADS)[None, :]
    wfo = w_fox_out[0].astype(_BF)
    wso = w_swa_out[0][perm_cols, :].astype(_BF)
    wo = w_out[0].astype(_BF)
    wr = jnp.pad(w_router[0], ((0, 0), (0, LANES - N_EXPERTS)))
    br = jnp.pad(b_router[0], (0, LANES - N_EXPERTS), constant_values=NEG_INF)[None, :]
    sinks = swa_sinks[0].astype(_F32)
    wu = w_up[0]
    wd = w_down[0]
    bu = b_up[0][:, None, :]
    bd = b_down[0][:, None, :]

    meta_blk = jnp.pad(meta_tokens.astype(x.dtype), ((META_PAD, 0), (0, 0)))[None]
    zero_f = jnp.zeros((FOX_HEADS, 1), _F32)
    _, fk_m, aug_m, fvt_m, _, sk_m, svt_m, fend_m = _qkv_call(meta_blk, g1, wqk, wvt, wfl, ws, wsvt, bfg, fqg, fkg, sqg,
                                                              skg, zero_f, tm=META_BLOCK, n_pad=META_PAD)
    fq, fk, aug, fvt, sq, sk, svt, _ = _qkv_call(x, g1, wqk, wvt, wfl, ws, wsvt, bfg, fqg, fkg, sqg, skg, fend_m[0],
                                                 tm=tl["tm"], n_pad=0)
    o_a = _fox_call(fq, fk, aug, fvt, fk_m[0], aug_m[0], fvt_m[0, :, 0], tq=tl["tq_fox"])
    o_b = _swa_call(sinks, sq, sk, svt, sk_m[0], svt_m[0, :, 0], tq=tl["tq_swa"])

    hres, h2, lg = _mix_call(x.reshape(N, D), o_a.reshape(N, FOX_W), o_b.reshape(N, SWA_QW),
                             g1, wga, wgb, wfo, wso, wo, g2, wr, br, tm=tl["tm_mix"])

    T = tl["t_route"]
    nt = N // T
    block_rows = tl["block_rows"]
    R = -(-(TOP_K * T + N_EXPERTS * (RUN_ALIGN - 1)) // LANES) * LANES
    max_rows = N * TOP_K + nt * N_EXPERTS * (RUN_ALIGN - 1) + N_EXPERTS * (block_rows - RUN_ALIGN)
    nb = -(-max_rows // block_rows)
    a, at, cnt = _route_call(lg, T=T)
    goff, lbase, nch, eblk, gap, info = _plan_call(cnt[:, :, 0].reshape(-1), nt=nt, block_rows=block_rows)
    xs = _dispatch_call(goff, lbase, nch, gap, info, at, h2, T=T, R=R, rows=nb * block_rows, block_rows=block_rows)
    ys = _expert_call(eblk, info, xs, wu, bu, wd, bd, block_rows=block_rows)
    out = _combine_call(goff, lbase, nch, a, hres, ys, T=T, R=R)
    return out.reshape(B, S, D)
```

```python
import functools

import jax
import jax.numpy as jnp
import numpy as np
from jax import lax
from jax.experimental import pallas as pl
from jax.experimental.pallas import tpu as pltpu

HEAD_DIM = 64
N_META = 16
FOX_HEADS = 8
SWA_HEADS = 8
SWA_KV_HEADS = 2
WINDOW = 128
N_EXPERTS = 32
TOP_K = 4
D_FF = 1024
SWIGLU_LIMIT = 7.0
SWIGLU_ALPHA = 1.702
RMS_EPS = 1e-6
NEG_INF = -1e30
LOG2E = 1.4426950408889634

LANES = 128
META_BLOCK = 128
META_PAD = META_BLOCK - N_META
FOX_PAIRS = 4
RUN_ALIGN = 8
VMEM_LIMIT = 56 * 1024 * 1024

FOX_W = FOX_HEADS * HEAD_DIM
SWA_QW = SWA_HEADS * HEAD_DIM
SWA_KW = SWA_KV_HEADS * HEAD_DIM

_BF = jnp.bfloat16
_F32 = jnp.float32


def _cparams(sem):
    return pltpu.CompilerParams(dimension_semantics=sem, vmem_limit_bytes=VMEM_LIMIT)


def _rms(t, gain):
    return t * lax.rsqrt(jnp.mean(t * t, axis=-1, keepdims=True) + RMS_EPS) * gain


def _head_rms(z, gain_row):
    lane = lax.broadcasted_iota(jnp.int32, (z.shape[0], LANES), 1)
    lo_mask = lane < HEAD_DIM
    outs = []
    for b in range(z.shape[1] // LANES):
        v = z[:, b * LANES:(b + 1) * LANES]
        v2 = v * v
        tot = jnp.sum(v2, axis=-1, keepdims=True)
        lo = jnp.sum(jnp.where(lo_mask, v2, 0.0), axis=-1, keepdims=True)
        hi = tot - lo
        r_lo = lax.rsqrt(lo * (1.0 / HEAD_DIM) + RMS_EPS)
        r_hi = lax.rsqrt(hi * (1.0 / HEAD_DIM) + RMS_EPS)
        outs.append(v * jnp.where(lo_mask, r_lo, r_hi))
    return jnp.concatenate(outs, axis=-1) * gain_row


def _lane_cumsum(x):
    n = x.shape[-1]
    lane = lax.broadcasted_iota(jnp.int32, x.shape, x.ndim - 1)
    s = 1
    while s < n:
        x = x + jnp.where(lane >= s, pltpu.roll(x, s, x.ndim - 1), 0.0)
        s *= 2
    return x


def _qkv_kernel(x_ref, g1_ref, wqk_ref, wvt_ref, wfl_ref, ws_ref, wsvt_ref, bf_ref, fqg_ref, fkg_ref, sqg_ref, skg_ref,
                f0_ref, fq_ref, fk_ref, aug_ref, fvt_ref, sq_ref, sk_ref, svt_ref, fend_ref, carry_ref, *, n_pad):
    j = pl.program_id(1)

    @pl.when(j == 0)
    def _():
        carry_ref[...] = f0_ref[...]

    x = x_ref[...]
    tm = x.shape[0]
    h = _rms(x, g1_ref[...]).astype(_BF)
    nt = (((1,), (1,)), ((), ()))
    f = jnp.dot(h, wqk_ref[...], preferred_element_type=_F32)
    s = jnp.dot(h, ws_ref[...], preferred_element_type=_F32)
    svt = lax.dot_general(wsvt_ref[...], h, nt, preferred_element_type=_F32)
    vt = lax.dot_general(wvt_ref[...], h, nt, preferred_element_type=_F32)
    flog = lax.dot_general(wfl_ref[...], h, nt, preferred_element_type=_F32)

    fq_ref[...] = _head_rms(f[:, :FOX_W], fqg_ref[...]).astype(_BF)
    fk_ref[...] = _head_rms(f[:, FOX_W:], fkg_ref[...]).astype(_BF)
    sq_ref[...] = _head_rms(s[:, :SWA_QW], sqg_ref[...]).astype(_BF)
    sk_ref[...] = _head_rms(s[:, SWA_QW:], skg_ref[...]).astype(_BF)

    ones_row = jnp.where(lax.broadcasted_iota(jnp.int32, (HEAD_DIM, tm), 0) == 0, 1.0, 0.0).astype(_BF)
    for hd in range(FOX_HEADS):
        fvt_ref[hd] = jnp.concatenate([vt[hd * HEAD_DIM:(hd + 1) * HEAD_DIM].astype(_BF), ones_row], axis=0)
    for g in range(SWA_KV_HEADS):
        blk = jnp.concatenate([svt[g * HEAD_DIM:(g + 1) * HEAD_DIM].astype(_BF), ones_row], axis=0)
        for c in range(tm // WINDOW):
            svt_ref[g, c] = blk[:, c * WINDOW:(c + 1) * WINDOW]

    z = flog + bf_ref[...]
    log_f = jnp.minimum(z, 0.0) - jnp.log1p(jnp.exp(-jnp.abs(z)))
    if n_pad:
        lane = lax.broadcasted_iota(jnp.int32, log_f.shape, 1)
        log_f = jnp.where(lane >= n_pad, log_f, 0.0)
    cum = _lane_cumsum(log_f) + carry_ref[...]
    carry_ref[...] = cum[:, -1:]
    fend_ref[...] = cum[:, -1:]
    nfl = cum * (-LOG2E)
    p1 = nfl.astype(_BF).astype(_F32)
    p2 = (nfl - p1).astype(_BF).astype(_F32)
    p3 = (nfl - p1 - p2).astype(_BF).astype(_F32)
    g = jnp.concatenate([p1, p2, p3, jnp.zeros((LANES - 3 * FOX_HEADS, tm), _F32)], axis=0)
    aug_ref[...] = g.T.astype(_BF)


def _qkv_call(x3, g1, wqk, wvt, wfl, ws, wsvt, bfg, fqg, fkg, sqg, skg, f0, *, tm, n_pad):
    B, S, D = x3.shape
    nj = S // tm
    row = lambda w: pl.BlockSpec((None, tm, w), lambda b, j: (b, j, 0))
    full = lambda a: pl.BlockSpec(a.shape, lambda b, j: (0,) * a.ndim)
    bsd = lambda w: jax.ShapeDtypeStruct((B, S, w), _BF)
    out_shape = [bsd(FOX_W), bsd(FOX_W), bsd(LANES), jax.ShapeDtypeStruct((B, FOX_HEADS, nj, 2 * HEAD_DIM, tm), _BF),
                 bsd(SWA_QW), bsd(SWA_KW),
                 jax.ShapeDtypeStruct((B, SWA_KV_HEADS, S // WINDOW, 2 * HEAD_DIM, WINDOW), _BF),
                 jax.ShapeDtypeStruct((B, FOX_HEADS, 1), _F32)]
    out_specs = [row(FOX_W), row(FOX_W), row(LANES),
                 pl.BlockSpec((None, FOX_HEADS, None, 2 * HEAD_DIM, tm), lambda b, j: (b, 0, j, 0, 0)),
                 row(SWA_QW), row(SWA_KW),
                 pl.BlockSpec((None, SWA_KV_HEADS, tm // WINDOW, 2 * HEAD_DIM, WINDOW), lambda b, j: (b, 0, j, 0, 0)),
                 pl.BlockSpec((None, FOX_HEADS, 1), lambda b, j: (b, 0, 0))]
    return pl.pallas_call(
        functools.partial(_qkv_kernel, n_pad=n_pad),
        grid=(B, nj),
        in_specs=[row(D)] + [full(a) for a in (g1, wqk, wvt, wfl, ws, wsvt, bfg, fqg, fkg, sqg, skg, f0)],
        out_specs=out_specs, out_shape=out_shape,
        scratch_shapes=[pltpu.VMEM((FOX_HEADS, 1), _F32)],
        compiler_params=_cparams(("arbitrary", "arbitrary")),
        name="qkv_proj",
    )(x3, g1, wqk, wvt, wfl, ws, wsvt, bfg, fqg, fkg, sqg, skg, f0)


def _fox_kernel(q_ref, k_ref, aug_ref, vt_ref, km_ref, augm_ref, vtm_ref, o_ref, m_ref, acc_ref, s_ref, mx_ref):
    g = pl.program_id(1)
    qi = pl.program_id(2)
    tq = q_ref.shape[0]
    tk = vt_ref.shape[-1]
    nt = (((1,), (1,)), ((), ()))
    lane = lax.broadcasted_iota(jnp.int32, (tq, LANES), 1)

    qaugs = []
    for pp in range(FOX_PAIRS):
        q = q_ref[:, pp * LANES:(pp + 1) * LANES]
        parts = []
        for a in range(2):
            head = 2 * (FOX_PAIRS * g + pp) + a
            keep = (lane < HEAD_DIM) if a == 0 else (lane >= HEAD_DIM)
            ones = jnp.where((lane < 3 * FOX_HEADS) & ((lane & (FOX_HEADS - 1)) == head), 1.0, 0.0).astype(_BF)
            parts.append(jnp.concatenate([jnp.where(keep, q, jnp.zeros_like(q)), ones], axis=1))
        qaugs.append(jnp.concatenate(parts, axis=0))

    def scores(pp, k, aug, mask):
        st = lax.dot_general(jnp.concatenate([k, aug], axis=1), qaugs[pp], nt, preferred_element_type=_F32)
        if mask is not None:
            st = jnp.where(mask, st, NEG_INF)
        return st, jnp.max(st, axis=0, keepdims=True)

    def tile_scores(pp, ki, mask=None):
        off = pl.multiple_of(ki * tk, tk)
        return scores(pp, k_ref[pl.ds(off, tk), pp * LANES:(pp + 1) * LANES], aug_ref[pl.ds(off, tk), :], mask)

    def process(pp, st, mx, vts, first):
        m_new = mx if first else jnp.maximum(m_ref[pp], mx)
        pt = jnp.exp2(st - m_new).astype(_BF)
        if not first:
            alpha = jnp.exp2(m_ref[pp] - m_new)
        for a in range(2):
            pv = jnp.dot(vts[a], pt[:, a * tq:(a + 1) * tq], preferred_element_type=_F32)
            acc_ref[2 * pp + a] = pv if first else alpha[:, a * tq:(a + 1) * tq] * acc_ref[2 * pp + a] + pv
        m_ref[pp] = m_new

    def stash(buf, pp, st_mx):
        s_ref[buf, pp] = st_mx[0]
        mx_ref[buf, pp] = st_mx[1]

    def vts(pp, ki):
        return vt_ref[2 * pp, ki], vt_ref[2 * pp + 1, ki]

    causal = lax.broadcasted_iota(jnp.int32, (tk, 2 * tq), 0) <= (lax.broadcasted_iota(jnp.int32, (tk, 2 * tq), 1) & (tq - 1))
    mask_m = lax.broadcasted_iota(jnp.int32, (META_BLOCK, 2 * tq), 0) >= META_PAD
    for pp in range(FOX_PAIRS):
        stash(0, pp, tile_scores(pp, qi, causal))
        st, mx = scores(pp, km_ref[:, pp * LANES:(pp + 1) * LANES], augm_ref[...], mask_m)
        process(pp, st, mx, (vtm_ref[2 * pp], vtm_ref[2 * pp + 1]), True)

    def step(j, buf):
        nxt = [tile_scores(pp, j) for pp in range(FOX_PAIRS)]
        cur = jnp.where(j == 0, qi, j - 1)
        for pp in range(FOX_PAIRS):
            process(pp, s_ref[buf, pp], mx_ref[buf, pp], vts(pp, cur), False)
        for pp in range(FOX_PAIRS):
            stash(1 - buf, pp, nxt[pp])

    def body(jj, c):
        step(2 * jj, 0)
        step(2 * jj + 1, 1)
        return c
    lax.fori_loop(0, qi // 2, body, 0)

    def finish(buf):
        last = jnp.where(qi == 0, qi, qi - 1)
        outs = []
        for pp in range(FOX_PAIRS):
            process(pp, s_ref[buf, pp], mx_ref[buf, pp], vts(pp, last), False)
            for a in range(2):
                acc = acc_ref[2 * pp + a]
                outs.append(acc[:HEAD_DIM] / acc[HEAD_DIM:HEAD_DIM + 1])
        o_ref[...] = jnp.concatenate(outs, axis=0).T.astype(_BF)

    @pl.when(qi % 2 == 1)
    def _():
        step(qi - 1, 0)
        finish(1)

    @pl.when(qi % 2 == 0)
    def _():
        finish(0)


def _fox_call(fq, fk, aug, fvt, km, augm, vtm, *, tq):
    B, S, _ = fq.shape
    nk = fvt.shape[2]
    assert fvt.shape[-1] == tq
    w = FOX_PAIRS * LANES
    qspec = pl.BlockSpec((None, tq, w), lambda b, g, i: (b, i, g))
    return pl.pallas_call(
        _fox_kernel, grid=(B, FOX_HEADS // (2 * FOX_PAIRS), S // tq),
        in_specs=[qspec,
                  pl.BlockSpec((None, S, w), lambda b, g, i: (b, 0, g)),
                  pl.BlockSpec((None, S, LANES), lambda b, g, i: (b, 0, 0)),
                  pl.BlockSpec((None, 2 * FOX_PAIRS, nk, 2 * HEAD_DIM, tq), lambda b, g, i: (b, g, 0, 0, 0)),
                  pl.BlockSpec((META_BLOCK, w), lambda b, g, i: (0, g)),
                  pl.BlockSpec((META_BLOCK, LANES), lambda b, g, i: (0, 0)),
                  pl.BlockSpec((2 * FOX_PAIRS, 2 * HEAD_DIM, META_BLOCK), lambda b, g, i: (g, 0, 0))],
        out_specs=qspec, out_shape=jax.ShapeDtypeStruct((B, S, FOX_W), _BF),
        scratch_shapes=[pltpu.VMEM((FOX_PAIRS, 1, 2 * tq), _F32), pltpu.VMEM((2 * FOX_PAIRS, 2 * HEAD_DIM, tq), _F32),
                        pltpu.VMEM((2, FOX_PAIRS, tq, 2 * tq), _F32), pltpu.VMEM((2, FOX_PAIRS, 1, 2 * tq), _F32)],
        compiler_params=_cparams(("arbitrary", "arbitrary", "arbitrary")), name="fox_attn",
    )(fq, fk, aug, fvt, km, augm, vtm)


def _swa_bias(tq):
    j = np.arange(WINDOW + tq)[:, None] - WINDOW
    i = np.arange(tq)[None, :]
    dist = i - j
    ok = (dist >= 0) & (dist < WINDOW)
    slopes = np.exp2(-8.0 * (np.arange(SWA_HEADS, dtype=np.float32) + 1.0) / SWA_HEADS).astype(np.float32)
    per_head = np.where(ok[None], -slopes[:, None, None] * dist[None].astype(np.float32), np.float32(NEG_INF))
    rep = SWA_HEADS // SWA_KV_HEADS
    return np.stack([np.concatenate(list(per_head[g * rep:(g + 1) * rep]), axis=1) for g in range(SWA_KV_HEADS)])


def _swa_kernel(q_ref, kc_ref, kp_ref, km_ref, vc_ref, vp_ref, vm_ref, bias_ref, sink_ref, o_ref):
    qi = pl.program_id(1)
    tq = q_ref.shape[0]
    rep = SWA_HEADS // SWA_KV_HEADS
    nt = (((1,), (1,)), ((), ()))
    first = qi == 0
    kcat = jnp.concatenate([jnp.where(first, km_ref[...], kp_ref[...]), kc_ref[...]], axis=0)
    row = lax.broadcasted_iota(jnp.int32, (WINDOW + tq, rep * tq), 0)
    row_ok = row >= jnp.where(first, META_PAD, 0)
    lane = lax.broadcasted_iota(jnp.int32, (tq, LANES), 1)
    outs = [None] * SWA_HEADS
    for g in range(SWA_KV_HEADS):
        keep = (lane < HEAD_DIM) if g == 0 else (lane >= HEAD_DIM)
        qg = jnp.concatenate([jnp.where(keep, q_ref[:, p * LANES:(p + 1) * LANES], jnp.zeros((tq, LANES), _BF))
                              for p in range(rep)], axis=0)
        vcat = jnp.concatenate([jnp.where(first, vm_ref[g], vp_ref[g, 0])] + [vc_ref[g, c] for c in range(tq // WINDOW)],
                               axis=1)
        st = lax.dot_general(kcat, qg, nt, preferred_element_type=_F32) + bias_ref[g]
        st = jnp.where(row_ok, st, NEG_INF)
        sink = sink_ref[g]
        m = jnp.maximum(jnp.max(st, axis=0, keepdims=True), sink)
        pt = jnp.exp(st - m).astype(_BF)
        acc = jnp.dot(vcat, pt, preferred_element_type=_F32)
        o = acc[:HEAD_DIM] / (acc[HEAD_DIM:HEAD_DIM + 1] + jnp.exp(sink - m))
        for p in range(rep):
            outs[2 * p + g] = o[:, p * tq:(p + 1) * tq]
    o_ref[...] = jnp.concatenate(outs, axis=0).T.astype(_BF)


def _swa_call(sinks, sq, sk, svt, km, vtm, *, tq):
    B, S, _ = sq.shape
    r = tq // WINDOW
    rep = SWA_HEADS // SWA_KV_HEADS
    bias = jnp.asarray(_swa_bias(tq))
    sink_rows = jnp.repeat(sinks.reshape(SWA_KV_HEADS, rep), tq, axis=1)[:, None, :]
    prev_blk = lambda i: jnp.maximum(i * r - 1, 0)
    return pl.pallas_call(
        _swa_kernel, grid=(B, S // tq),
        in_specs=[pl.BlockSpec((None, tq, SWA_QW), lambda b, i: (b, i, 0)),
                  pl.BlockSpec((None, tq, LANES), lambda b, i: (b, i, 0)),
                  pl.BlockSpec((None, WINDOW, LANES), lambda b, i: (b, prev_blk(i), 0)),
                  pl.BlockSpec((META_BLOCK, LANES), lambda b, i: (0, 0)),
                  pl.BlockSpec((None, SWA_KV_HEADS, r, 2 * HEAD_DIM, WINDOW), lambda b, i: (b, 0, i, 0, 0)),
                  pl.BlockSpec((None, SWA_KV_HEADS, 1, 2 * HEAD_DIM, WINDOW), lambda b, i: (b, 0, prev_blk(i), 0, 0)),
                  pl.BlockSpec((SWA_KV_HEADS, 2 * HEAD_DIM, META_BLOCK), lambda b, i: (0, 0, 0)),
                  pl.BlockSpec(bias.shape, lambda b, i: (0, 0, 0)),
                  pl.BlockSpec(sink_rows.shape, lambda b, i: (0, 0, 0))],
        out_specs=pl.BlockSpec((None, tq, SWA_QW), lambda b, i: (b, i, 0)),
        out_shape=jax.ShapeDtypeStruct((B, S, SWA_QW), _BF),
        compiler_params=_cparams(("arbitrary", "arbitrary")), name="swa_attn",
    )(sq, sk, sk, km, svt, svt, vtm, bias, sink_rows)


def _mix_kernel(x_ref, oa_ref, ob_ref, g1_ref, wga_ref, wgb_ref, wfo_ref, wso_ref, wo_ref, g2_ref, wr_ref, br_ref,
                hres_ref, h2_ref, lg_ref):
    x = x_ref[...]
    h = _rms(x, g1_ref[...]).astype(_BF)
    ga = jax.nn.sigmoid(jnp.dot(h, wga_ref[...], preferred_element_type=_F32))
    gb = jax.nn.sigmoid(jnp.dot(h, wgb_ref[...], preferred_element_type=_F32))
    mix = ga * jnp.dot(oa_ref[...], wfo_ref[...], preferred_element_type=_F32) \
        + gb * jnp.dot(ob_ref[...], wso_ref[...], preferred_element_type=_F32)
    hres = x + jnp.dot(mix.astype(_BF), wo_ref[...], preferred_element_type=_F32)
    hres_ref[...] = hres
    h2 = _rms(hres, g2_ref[...])
    h2_ref[...] = h2.astype(_BF)
    wr = wr_ref[...]
    w_hi = wr.astype(_BF)
    w_lo = (wr - w_hi.astype(_F32)).astype(_BF)
    h_hi = h2.astype(_BF)
    h_lo = (h2 - h_hi.astype(_F32)).astype(_BF)
    lg = jnp.dot(h_hi, w_hi, preferred_element_type=_F32) + jnp.dot(h_hi, w_lo, preferred_element_type=_F32) \
        + jnp.dot(h_lo, w_hi, preferred_element_type=_F32)
    lg_ref[...] = lg + br_ref[...]


def _mix_call(x2, oa, ob, g1, wga, wgb, wfo, wso, wo, g2, wr, br, *, tm):
    N, D = x2.shape
    row = lambda w: pl.BlockSpec((tm, w), lambda i: (i, 0))
    full = lambda a: pl.BlockSpec(a.shape, lambda i: (0,) * a.ndim)
    return pl.pallas_call(
        _mix_kernel, grid=(N // tm,),
        in_specs=[row(D), row(FOX_W), row(SWA_QW)] + [full(a) for a in (g1, wga, wgb, wfo, wso, wo, g2, wr, br)],
        out_specs=[row(D), row(D), row(LANES)],
        out_shape=[jax.ShapeDtypeStruct((N, D), _F32), jax.ShapeDtypeStruct((N, D), _BF),
                   jax.ShapeDtypeStruct((N, LANES), _F32)],
        compiler_params=_cparams(("arbitrary",)), name="mix_proj",
    )(x2, oa, ob, g1, wga, wgb, wfo, wso, wo, g2, wr, br)


def _route_kernel(lg_ref, a_ref, at_ref, cnt_ref, lb_ref):
    T = lg_ref.shape[0]
    lt = lg_ref[...].T[:N_EXPERTS]
    e_iota = lax.broadcasted_iota(jnp.int32, lt.shape, 0)
    cur = lt
    vals, idxs = [], []
    for _ in range(TOP_K):
        m = jnp.max(cur, axis=0, keepdims=True)
        idx = jnp.min(jnp.where(cur == m, e_iota, N_EXPERTS), axis=0, keepdims=True)
        vals.append(m)
        idxs.append(idx)
        cur = jnp.where(e_iota == idx, -jnp.inf, cur)
    es = [jnp.exp(v - vals[0]) for v in vals]
    den = es[0] + es[1] + es[2] + es[3]
    gates = [e / den for e in es]

    onehot = jnp.zeros(lt.shape, _F32)
    for idx in idxs:
        onehot = onehot + jnp.where(e_iota == idx, 1.0, 0.0)
    r_i = lax.broadcasted_iota(jnp.int32, (T, T), 0)
    c_i = lax.broadcasted_iota(jnp.int32, (T, T), 1)
    utri = jnp.where(r_i < c_i, 1.0, 0.0).astype(_BF)
    prefix = jnp.dot(onehot.astype(_BF), utri, preferred_element_type=_F32)
    cnt = jnp.sum(onehot, axis=1, keepdims=True)
    chunks = jnp.floor((cnt + (RUN_ALIGN - 1)) * (1.0 / RUN_ALIGN))
    l_r = lax.broadcasted_iota(jnp.int32, (N_EXPERTS, N_EXPERTS), 0)
    l_c = lax.broadcasted_iota(jnp.int32, (N_EXPERTS, N_EXPERTS), 1)
    ltri = jnp.where(l_c < l_r, 1.0, 0.0).astype(_BF)
    lbase = RUN_ALIGN * jnp.dot(ltri, jnp.broadcast_to(chunks, (N_EXPERTS, LANES)).astype(_BF),
                                preferred_element_type=_F32)[:, 0:1]
    slot_all = prefix + lbase
    rows = [jnp.sum(jnp.where(e_iota == idx, slot_all, 0.0), axis=0, keepdims=True) for idx in idxs] + gates
    at = jnp.concatenate(rows, axis=0)
    at_ref[...] = at
    a_ref[...] = jnp.concatenate([at, jnp.zeros((LANES - 2 * TOP_K, T), _F32)], axis=0).T
    cnt_ref[...] = jnp.broadcast_to(cnt, (N_EXPERTS, LANES)).astype(jnp.int32)
    lb_ref[...] = jnp.broadcast_to(lbase, (N_EXPERTS, LANES)).astype(jnp.int32)


def _route_call(lg, *, T):
    N = lg.shape[0]
    nt = N // T
    per_tile = pl.BlockSpec((None, N_EXPERTS, LANES), lambda i: (i, 0, 0))
    return pl.pallas_call(
        _route_kernel, grid=(nt,),
        in_specs=[pl.BlockSpec((T, LANES), lambda i: (i, 0))],
        out_specs=[pl.BlockSpec((T, LANES), lambda i: (i, 0)), pl.BlockSpec((8, T), lambda i: (0, i)), per_tile, per_tile],
        out_shape=[jax.ShapeDtypeStruct((N, LANES), _F32), jax.ShapeDtypeStruct((8, N), _F32),
                   jax.ShapeDtypeStruct((nt, N_EXPERTS, LANES), jnp.int32),
                   jax.ShapeDtypeStruct((nt, N_EXPERTS, LANES), jnp.int32)],
        compiler_params=_cparams(("arbitrary",)), name="route",
    )(lg)


def _div_pow2(v, d):
    assert d & (d - 1) == 0
    return lax.shift_right_logical(v, d.bit_length() - 1)


def _plan_kernel(cnt_ref, goff_ref, nch_ref, eblk_ref, gap_ref, info_ref, *, nt, block_rows):
    def per_expert(e, base):
        def per_tile(i, run):
            n = _div_pow2(cnt_ref[i * N_EXPERTS + e] + (RUN_ALIGN - 1), RUN_ALIGN)
            goff_ref[i * N_EXPERTS + e] = run
            nch_ref[i * N_EXPERTS + e] = n
            return run + n * RUN_ALIGN
        end = lax.fori_loop(0, nt, per_tile, base)
        nblk = _div_pow2(end - base + (block_rows - 1), block_rows)
        nxt = base + nblk * block_rows
        gap_ref[2 * e] = end
        gap_ref[2 * e + 1] = _div_pow2(nxt - end, RUN_ALIGN)
        eblk_ref[2 * e] = _div_pow2(base, block_rows)
        eblk_ref[2 * e + 1] = nblk
        return nxt
    total = lax.fori_loop(0, N_EXPERTS, per_expert, 0)
    info_ref[0] = _div_pow2(total, block_rows)


def _plan_call(cnt_flat, *, nt, block_rows):
    smem = pl.BlockSpec(memory_space=pltpu.SMEM)
    i32 = lambda n: jax.ShapeDtypeStruct((n,), jnp.int32)
    return pl.pallas_call(
        functools.partial(_plan_kernel, nt=nt, block_rows=block_rows),
        in_specs=[smem], out_specs=[smem] * 5,
        out_shape=[i32(nt * N_EXPERTS), i32(nt * N_EXPERTS), i32(2 * N_EXPERTS), i32(2 * N_EXPERTS), i32(1)],
        name="plan",
    )(cnt_flat)


def _pack_pairs(v):
    w = v.shape[1] // 2
    return pltpu.pack_elementwise([v[:, :w], v[:, w:]], packed_dtype=_BF)


def _unpack_pairs(wd):
    lo = pltpu.unpack_elementwise(wd, index=0, packed_dtype=_BF, unpacked_dtype=_F32)
    hi = pltpu.unpack_elementwise(wd, index=1, packed_dtype=_BF, unpacked_dtype=_F32)
    return lo.astype(_BF), hi.astype(_BF)


def _dispatch_kernel(goff, lbase, nch, gap, info, at_ref, h2_ref, xs_ref, xl_ref, z_ref, sem, zsem, tsem):
    i = pl.program_id(0)
    R = xl_ref.shape[1]
    T = h2_ref.shape[0]
    block_rows = z_ref.shape[0]
    nb = xs_ref.shape[0] // block_rows

    @pl.when(i == 0)
    def _():
        z_ref[...] = jnp.zeros(z_ref.shape, z_ref.dtype)

        def gap_copy(e):
            rows = pl.multiple_of(gap[2 * e + 1] * RUN_ALIGN, RUN_ALIGN)
            return pltpu.make_async_copy(z_ref.at[pl.ds(0, rows)],
                                         xs_ref.at[pl.ds(pl.multiple_of(gap[2 * e], RUN_ALIGN), rows)], zsem)

        def fill(e, c):
            @pl.when(gap[2 * e + 1] > 0)
            def _():
                gap_copy(e).start()
            return c
        lax.fori_loop(0, N_EXPERTS, fill, 0)

        def fill_wait(e, c):
            @pl.when(gap[2 * e + 1] > 0)
            def _():
                gap_copy(e).wait()
            return c
        lax.fori_loop(0, N_EXPERTS, fill_wait, 0)

        def tail(b, c):
            pltpu.make_async_copy(z_ref, xs_ref.at[pl.ds(pl.multiple_of(b * block_rows, block_rows), block_rows)],
                                  tsem).start()
            return c
        lax.fori_loop(info[0], nb, tail, 0)

        def tail_wait(b, c):
            pltpu.make_async_copy(z_ref, xs_ref.at[pl.ds(0, block_rows)], tsem).wait()
            return c
        lax.fori_loop(info[0], nb, tail_wait, 0)

    slot_iota = lax.broadcasted_iota(jnp.int32, (R, T), 0).astype(_F32)
    pm = jnp.zeros((R, T), _F32)
    for k in range(TOP_K):
        pm = jnp.where(slot_iota == at_ref[k:k + 1, :], 1.0, pm)
    xl = jnp.dot(pm.astype(_BF), h2_ref[...], preferred_element_type=_F32)
    slot = i & 1

    def wait_tile(t, s):
        total = lax.fori_loop(0, N_EXPERTS, lambda e, c: c + nch[t * N_EXPERTS + e], 0)

        @pl.when(total > 0)
        def _():
            rows = pl.multiple_of(total * RUN_ALIGN, RUN_ALIGN)
            pltpu.make_async_copy(xl_ref.at[s, pl.ds(0, rows)], xs_ref.at[pl.ds(0, rows)], sem.at[s]).wait()

    @pl.when(i >= 2)
    def _():
        wait_tile(i - 2, slot)
    xl_ref[slot] = _pack_pairs(xl)

    def issue(e, c):
        n = nch[i * N_EXPERTS + e]

        @pl.when(n > 0)
        def _():
            rows = pl.multiple_of(n * RUN_ALIGN, RUN_ALIGN)
            pltpu.make_async_copy(xl_ref.at[slot, pl.ds(pl.multiple_of(lbase[i * N_EXPERTS + e], RUN_ALIGN), rows)],
                                  xs_ref.at[pl.ds(pl.multiple_of(goff[i * N_EXPERTS + e], RUN_ALIGN), rows)],
                                  sem.at[slot]).start()
        return c
    lax.fori_loop(0, N_EXPERTS, issue, 0)

    @pl.when(i == pl.num_programs(0) - 1)
    def _():
        @pl.when(i >= 1)
        def _():
            wait_tile(i - 1, 1 - slot)
        wait_tile(i, slot)


def _dispatch_call(goff, lbase, nch, gap, info, at, h2, *, T, R, rows, block_rows):
    N, D = h2.shape
    grid_spec = pltpu.PrefetchScalarGridSpec(
        num_scalar_prefetch=5, grid=(N // T,),
        in_specs=[pl.BlockSpec((8, T), lambda i, *_: (0, i)), pl.BlockSpec((T, D), lambda i, *_: (i, 0))],
        out_specs=pl.BlockSpec(memory_space=pl.ANY),
        scratch_shapes=[pltpu.VMEM((2, R, D // 2), jnp.uint32), pltpu.VMEM((block_rows, D // 2), jnp.uint32),
                        pltpu.SemaphoreType.DMA((2,)), pltpu.SemaphoreType.DMA, pltpu.SemaphoreType.DMA])
    return pl.pallas_call(
        _dispatch_kernel, grid_spec=grid_spec, out_shape=jax.ShapeDtypeStruct((rows, D // 2), jnp.uint32),
        compiler_params=_cparams(("arbitrary",)), name="dispatch",
    )(goff, lbase, nch, gap, info, at, h2)


def _expert_kernel(eblk, info, xs_ref, wu_ref, bu_ref, wd_ref, bd_ref, ys_ref, wub_ref, wdb_ref, xb_ref, yb_ref, xsem, ysem):
    e = pl.program_id(0)
    block_rows = xb_ref.shape[1]
    nb = ys_ref.shape[0] // block_rows
    b0 = eblk[2 * e]
    n = eblk[2 * e + 1]
    half = wu_ref.shape[0] // 2
    wub_ref[...] = wu_ref[...].astype(_BF)
    wdb_ref[...] = wd_ref[...].astype(_BF)

    def rows_of(blk):
        return pl.ds(pl.multiple_of(blk * block_rows, block_rows), block_rows)

    def x_copy(j, slot):
        return pltpu.make_async_copy(xs_ref.at[rows_of(b0 + j)], xb_ref.at[slot], xsem.at[slot])

    def y_copy(j, slot):
        return pltpu.make_async_copy(yb_ref.at[slot], ys_ref.at[rows_of(b0 + j)], ysem.at[slot])

    @pl.when((e == 0) & (n > 0))
    def _():
        x_copy(0, 0).start()

    def body(j, c):
        slot = j & 1
        x_copy(j, slot).wait()

        @pl.when(j + 1 < n)
        def _():
            x_copy(j + 1, 1 - slot).start()

        @pl.when(j >= 2)
        def _():
            y_copy(j - 2, slot).wait()

        x_lo, x_hi = _unpack_pairs(xb_ref[slot])
        gu = jnp.dot(x_lo, wub_ref[:half, :], preferred_element_type=_F32) \
            + jnp.dot(x_hi, wub_ref[half:, :], preferred_element_type=_F32) + bu_ref[...]
        g = jnp.minimum(gu[:, :D_FF], SWIGLU_LIMIT)
        u = jnp.clip(gu[:, D_FF:], -SWIGLU_LIMIT, SWIGLU_LIMIT)
        act = (u + 1.0) * (g * jax.nn.sigmoid(SWIGLU_ALPHA * g))
        y = jnp.dot(act.astype(_BF), wdb_ref[...], preferred_element_type=_F32) + bd_ref[...]
        yb_ref[slot] = _pack_pairs(y)
        y_copy(j, slot).start()
        return c
    lax.fori_loop(0, n, body, 0)

    @pl.when(n >= 2)
    def _():
        y_copy(n - 2, n & 1).wait()

    @pl.when(n >= 1)
    def _():
        y_copy(n - 1, (n - 1) & 1).wait()

    last = pl.num_programs(0) - 1
    e_next = jnp.minimum(e + 1, last)

    @pl.when((e < last) & (eblk[2 * e_next + 1] > 0))
    def _():
        pltpu.make_async_copy(xs_ref.at[rows_of(eblk[2 * e_next])], xb_ref.at[0], xsem.at[0]).start()

    @pl.when(e == last)
    def _():
        yb_ref[0] = _pack_pairs(jnp.zeros((block_rows, 2 * yb_ref.shape[2]), _F32))

        def tail_copy(b):
            return pltpu.make_async_copy(yb_ref.at[0], ys_ref.at[pl.ds(pl.multiple_of(b * block_rows, block_rows), block_rows)],
                                         ysem.at[0])

        def tail(b, c):
            tail_copy(b).start()
            return c
        lax.fori_loop(info[0], nb, tail, 0)

        def tail_wait(b, c):
            tail_copy(b).wait()
            return c
        lax.fori_loop(info[0], nb, tail_wait, 0)


def _expert_call(eblk, info, xs, wu, bu, wd, bd, *, block_rows):
    rows, half = xs.shape
    D = 2 * half
    ex = lambda e, *_: (e, 0, 0)
    grid_spec = pltpu.PrefetchScalarGridSpec(
        num_scalar_prefetch=2, grid=(N_EXPERTS,),
        in_specs=[pl.BlockSpec(memory_space=pl.ANY),
                  pl.BlockSpec((None, D, 2 * D_FF), ex), pl.BlockSpec((None, 1, 2 * D_FF), ex),
                  pl.BlockSpec((None, D_FF, D), ex), pl.BlockSpec((None, 1, D), ex)],
        out_specs=pl.BlockSpec(memory_space=pl.ANY),
        scratch_shapes=[pltpu.VMEM((D, 2 * D_FF), _BF), pltpu.VMEM((D_FF, D), _BF),
                        pltpu.VMEM((2, block_rows, half), jnp.uint32), pltpu.VMEM((2, block_rows, half), jnp.uint32),
                        pltpu.SemaphoreType.DMA((2,)), pltpu.SemaphoreType.DMA((2,))])
    return pl.pallas_call(
        _expert_kernel, grid_spec=grid_spec, out_shape=jax.ShapeDtypeStruct((rows, half), jnp.uint32),
        compiler_params=_cparams(("arbitrary",)), name="experts",
    )(eblk, info, xs, wu, bu, wd, bd)


def _combine_kernel(goff, lbase, nch, a_ref, hres_ref, ys_ref, o_ref, yl_ref, sem):
    i = pl.program_id(0)
    nt = pl.num_programs(0)
    R = yl_ref.shape[1]
    T = a_ref.shape[0]

    def fetch(t, slot):
        def issue(e, c):
            n = nch[t * N_EXPERTS + e]

            @pl.when(n > 0)
            def _():
                rows = pl.multiple_of(n * RUN_ALIGN, RUN_ALIGN)
                pltpu.make_async_copy(
                    ys_ref.at[pl.ds(pl.multiple_of(goff[t * N_EXPERTS + e], RUN_ALIGN), rows)],
                    yl_ref.at[slot, pl.ds(pl.multiple_of(lbase[t * N_EXPERTS + e], RUN_ALIGN), rows)], sem.at[slot]).start()
            return c
        lax.fori_loop(0, N_EXPERTS, issue, 0)

    @pl.when(i == 0)
    def _():
        yl_ref[...] = jnp.zeros(yl_ref.shape, yl_ref.dtype)
        fetch(0, 0)

    slot = i & 1

    @pl.when(i + 1 < nt)
    def _():
        fetch(i + 1, 1 - slot)

    total = lax.fori_loop(0, N_EXPERTS, lambda e, c: c + nch[i * N_EXPERTS + e], 0)

    @pl.when(total > 0)
    def _():
        rows = pl.multiple_of(total * RUN_ALIGN, RUN_ALIGN)
        pltpu.make_async_copy(ys_ref.at[pl.ds(0, rows)], yl_ref.at[slot, pl.ds(0, rows)], sem.at[slot]).wait()

    a = a_ref[...]
    slot_iota = lax.broadcasted_iota(jnp.int32, (T, R), 1).astype(_F32)
    pt = jnp.zeros((T, R), _F32)
    for k in range(TOP_K):
        pt = jnp.where(slot_iota == a[:, k:k + 1], a[:, TOP_K + k:TOP_K + k + 1], pt)
    pt = pt.astype(_BF)
    y_lo, y_hi = _unpack_pairs(yl_ref[slot])
    half = y_lo.shape[1]
    o_ref[:, :half] = hres_ref[:, :half] + jnp.dot(pt, y_lo, preferred_element_type=_F32)
    o_ref[:, half:] = hres_ref[:, half:] + jnp.dot(pt, y_hi, preferred_element_type=_F32)


def _combine_call(goff, lbase, nch, a, hres, ys, *, T, R):
    N, D = hres.shape
    grid_spec = pltpu.PrefetchScalarGridSpec(
        num_scalar_prefetch=3, grid=(N // T,),
        in_specs=[pl.BlockSpec((T, LANES), lambda i, *_: (i, 0)), pl.BlockSpec((T, D), lambda i, *_: (i, 0)),
                  pl.BlockSpec(memory_space=pl.ANY)],
        out_specs=pl.BlockSpec((T, D), lambda i, *_: (i, 0)),
        scratch_shapes=[pltpu.VMEM((2, R, D // 2), jnp.uint32), pltpu.SemaphoreType.DMA((2,))])
    return pl.pallas_call(
        _combine_kernel, grid_spec=grid_spec, out_shape=jax.ShapeDtypeStruct((N, D), _F32),
        compiler_params=_cparams(("arbitrary",)), name="combine",
    )(goff, lbase, nch, a, hres, ys)


def _tiles(S):
    pick = lambda pref: next(t for t in pref if S % t == 0)
    tm = pick((512, 256, 128))
    return dict(tm=tm, tq_fox=tm, tq_swa=pick((256, 128)),
                tm_mix=pick((512, 256, 128)), t_route=pick((256, 128)), block_rows=256)


def kernel(x, meta_tokens, norm1_gain, w_in, b_forget, fox_q_gain, fox_k_gain, swa_q_gain, swa_k_gain, swa_sinks,
           w_fox_out, w_swa_out, w_out, norm2_gain, w_router, b_router, w_up, b_up, w_down, b_down):
    B, S, D = x.shape
    assert norm1_gain.shape[0] == 1 and S % LANES == 0
    tl = _tiles(S)
    N = B * S
    scale = HEAD_DIM ** -0.5

    w = w_in[0]
    c0 = 3 * FOX_W
    c1 = c0 + FOX_HEADS
    c2 = c1 + SWA_QW + 2 * SWA_KW
    wqk = w[:, :2 * FOX_W].astype(_BF)
    wvt = w[:, 2 * FOX_W:c0].T.astype(_BF)
    wfl = w[:, c0:c1].T.astype(_BF)
    perm_heads = np.array([h for p in range(SWA_HEADS // 2) for h in (p, p + SWA_HEADS // 2)])
    perm_cols = (perm_heads[:, None] * HEAD_DIM + np.arange(HEAD_DIM)[None, :]).reshape(-1)
    ws_all = w[:, c1:c2]
    ws = jnp.concatenate([ws_all[:, :SWA_QW][:, perm_cols], ws_all[:, SWA_QW:SWA_QW + SWA_KW]], axis=1).astype(_BF)
    wsvt = ws_all[:, SWA_QW + SWA_KW:].T.astype(_BF)
    wga = w[:, c2:c2 + D].astype(_BF)
    wgb = w[:, c2 + D:].astype(_BF)
    g1 = norm1_gain[0][None, :]
    g2 = norm2_gain[0][None, :]
    bfg = b_forget[0][:, None]
    fqg = jnp.tile(fox_q_gain[0], FOX_HEADS)[None, :] * (scale * LOG2E)
    fkg = jnp.tile(fox_k_gain[0], FOX_HEADS)[None, :]
    sqg = jnp.tile(swa_q_gain[0], SWA_HEADS)[None, :] * scale
    skg = jnp.tile(swa_k_gain[0], SWA_KV_HEADS)[None, :]
    wfo = w_fox_out[0].astype(_BF)
    wso = w_swa_out[0][perm_cols, :].astype(_BF)
    wo = w_out[0].astype(_BF)
    wr = jnp.pad(w_router[0], ((0, 0), (0, LANES - N_EXPERTS)))
    br = jnp.pad(b_router[0], (0, LANES - N_EXPERTS), constant_values=NEG_INF)[None, :]
    sinks = swa_sinks[0].astype(_F32)
    wu = w_up[0]
    wd = w_down[0]
    bu = b_up[0][:, None, :]
    bd = b_down[0][:, None, :]

    meta_blk = jnp.pad(meta_tokens.astype(x.dtype), ((META_PAD, 0), (0, 0)))[None]
    zero_f = jnp.zeros((FOX_HEADS, 1), _F32)
    _, fk_m, aug_m, fvt_m, _, sk_m, svt_m, fend_m = _qkv_call(meta_blk, g1, wqk, wvt, wfl, ws, wsvt, bfg, fqg, fkg, sqg,
                                                              skg, zero_f, tm=META_BLOCK, n_pad=META_PAD)
    fq, fk, aug, fvt, sq, sk, svt, _ = _qkv_call(x, g1, wqk, wvt, wfl, ws, wsvt, bfg, fqg, fkg, sqg, skg, fend_m[0],
                                                 tm=tl["tm"], n_pad=0)
    o_a = _fox_call(fq, fk, aug, fvt, fk_m[0], aug_m[0], fvt_m[0, :, 0], tq=tl["tq_fox"])
    o_b = _swa_call(sinks, sq, sk, svt, sk_m[0], svt_m[0, :, 0], tq=tl["tq_swa"])

    hres, h2, lg = _mix_call(x.reshape(N, D), o_a.reshape(N, FOX_W), o_b.reshape(N, SWA_QW),
                             g1, wga, wgb, wfo, wso, wo, g2, wr, br, tm=tl["tm_mix"])

    T = tl["t_route"]
    nt = N // T
    block_rows = tl["block_rows"]
    R = -(-(TOP_K * T + N_EXPERTS * (RUN_ALIGN - 1)) // LANES) * LANES
    max_rows = N * TOP_K + nt * N_EXPERTS * (RUN_ALIGN - 1) + N_EXPERTS * (block_rows - RUN_ALIGN)
    nb = -(-max_rows // block_rows)
    a, at, cnt, lb = _route_call(lg, T=T)
    lbase = lb[:, :, 0].reshape(-1)
    goff, nch, eblk, gap, info = _plan_call(cnt[:, :, 0].reshape(-1), nt=nt, block_rows=block_rows)
    xs = _dispatch_call(goff, lbase, nch, gap, info, at, h2, T=T, R=R, rows=nb * block_rows, block_rows=block_rows)
    ys = _expert_call(eblk, info, xs, wu, bu, wd, bd, block_rows=block_rows)
    out = _combine_call(goff, lbase, nch, a, hres, ys, T=T, R=R)
    return out.reshape(B, S, D)
```

```python
import functools

import jax
import jax.numpy as jnp
import numpy as np
from jax import lax
from jax.experimental import pallas as pl
from jax.experimental.pallas import tpu as pltpu

HEAD_DIM = 64
N_META = 16
FOX_HEADS = 8
SWA_HEADS = 8
SWA_KV_HEADS = 2
WINDOW = 128
N_EXPERTS = 32
TOP_K = 4
D_FF = 1024
SWIGLU_LIMIT = 7.0
SWIGLU_ALPHA = 1.702
RMS_EPS = 1e-6
NEG_INF = -1e30
LOG2E = 1.4426950408889634

LANES = 128
META_BLOCK = 128
META_PAD = META_BLOCK - N_META
SWA_TILES = 2
FOX_PAIRS = 4
RUN_ALIGN = 8
VMEM_LIMIT = 56 * 1024 * 1024

FOX_W = FOX_HEADS * HEAD_DIM
SWA_QW = SWA_HEADS * HEAD_DIM
SWA_KW = SWA_KV_HEADS * HEAD_DIM

_BF = jnp.bfloat16
_F32 = jnp.float32


def _cparams(sem):
    return pltpu.CompilerParams(dimension_semantics=sem, vmem_limit_bytes=VMEM_LIMIT)


def _rms(t, gain):
    return t * lax.rsqrt(jnp.mean(t * t, axis=-1, keepdims=True) + RMS_EPS) * gain


def _head_rms(z, gain_row):
    lane = lax.broadcasted_iota(jnp.int32, (z.shape[0], LANES), 1)
    lo_mask = lane < HEAD_DIM
    outs = []
    for b in range(z.shape[1] // LANES):
        v = z[:, b * LANES:(b + 1) * LANES]
        v2 = v * v
        tot = jnp.sum(v2, axis=-1, keepdims=True)
        lo = jnp.sum(jnp.where(lo_mask, v2, 0.0), axis=-1, keepdims=True)
        hi = tot - lo
        r_lo = lax.rsqrt(lo * (1.0 / HEAD_DIM) + RMS_EPS)
        r_hi = lax.rsqrt(hi * (1.0 / HEAD_DIM) + RMS_EPS)
        outs.append(v * jnp.where(lo_mask, r_lo, r_hi))
    return jnp.concatenate(outs, axis=-1) * gain_row


def _lane_cumsum(x):
    n = x.shape[-1]
    lane = lax.broadcasted_iota(jnp.int32, x.shape, x.ndim - 1)
    s = 1
    while s < n:
        x = x + jnp.where(lane >= s, pltpu.roll(x, s, x.ndim - 1), 0.0)
        s *= 2
    return x


def _qkv_kernel(x_ref, g1_ref, wqk_ref, wvt_ref, wfl_ref, ws_ref, wsvt_ref, bf_ref, fqg_ref, fkg_ref, sqg_ref, skg_ref,
                f0_ref, fq_ref, fk_ref, aug_ref, fvt_ref, sq_ref, sk_ref, svt_ref, fend_ref, carry_ref, *, n_pad):
    j = pl.program_id(1)

    @pl.when(j == 0)
    def _():
        carry_ref[...] = f0_ref[...]

    x = x_ref[...]
    tm = x.shape[0]
    h = _rms(x, g1_ref[...]).astype(_BF)
    nt = (((1,), (1,)), ((), ()))
    f = jnp.dot(h, wqk_ref[...], preferred_element_type=_F32)
    s = jnp.dot(h, ws_ref[...], preferred_element_type=_F32)
    svt = lax.dot_general(wsvt_ref[...], h, nt, preferred_element_type=_F32)
    vt = lax.dot_general(wvt_ref[...], h, nt, preferred_element_type=_F32)
    flog = lax.dot_general(wfl_ref[...], h, nt, preferred_element_type=_F32)

    fq_ref[...] = _head_rms(f[:, :FOX_W], fqg_ref[...]).astype(_BF)
    fk_ref[...] = _head_rms(f[:, FOX_W:], fkg_ref[...]).astype(_BF)
    sq_ref[...] = _head_rms(s[:, :SWA_QW], sqg_ref[...]).astype(_BF)
    sk_ref[...] = _head_rms(s[:, SWA_QW:], skg_ref[...]).astype(_BF)

    ones_row = jnp.where(lax.broadcasted_iota(jnp.int32, (HEAD_DIM, tm), 0) == 0, 1.0, 0.0).astype(_BF)
    for hd in range(FOX_HEADS):
        fvt_ref[hd] = jnp.concatenate([vt[hd * HEAD_DIM:(hd + 1) * HEAD_DIM].astype(_BF), ones_row], axis=0)
    for g in range(SWA_KV_HEADS):
        blk = jnp.concatenate([svt[g * HEAD_DIM:(g + 1) * HEAD_DIM].astype(_BF), ones_row], axis=0)
        for c in range(tm // WINDOW):
            svt_ref[g, c] = blk[:, c * WINDOW:(c + 1) * WINDOW]

    z = flog + bf_ref[...]
    log_f = jnp.minimum(z, 0.0) - jnp.log1p(jnp.exp(-jnp.abs(z)))
    if n_pad:
        lane = lax.broadcasted_iota(jnp.int32, log_f.shape, 1)
        log_f = jnp.where(lane >= n_pad, log_f, 0.0)
    cum = _lane_cumsum(log_f) + carry_ref[...]
    carry_ref[...] = cum[:, -1:]
    fend_ref[...] = cum[:, -1:]
    nfl = cum * (-LOG2E)
    p1 = nfl.astype(_BF).astype(_F32)
    p2 = (nfl - p1).astype(_BF).astype(_F32)
    p3 = (nfl - p1 - p2).astype(_BF).astype(_F32)
    g = jnp.concatenate([p1, p2, p3, jnp.zeros((LANES - 3 * FOX_HEADS, tm), _F32)], axis=0)
    aug_ref[...] = g.T.astype(_BF)


def _qkv_call(x3, g1, wqk, wvt, wfl, ws, wsvt, bfg, fqg, fkg, sqg, skg, f0, *, tm, n_pad):
    B, S, D = x3.shape
    nj = S // tm
    row = lambda w: pl.BlockSpec((None, tm, w), lambda b, j: (b, j, 0))
    full = lambda a: pl.BlockSpec(a.shape, lambda b, j: (0,) * a.ndim)
    bsd = lambda w: jax.ShapeDtypeStruct((B, S, w), _BF)
    out_shape = [bsd(FOX_W), bsd(FOX_W), bsd(LANES), jax.ShapeDtypeStruct((B, FOX_HEADS, nj, 2 * HEAD_DIM, tm), _BF),
                 bsd(SWA_QW), bsd(SWA_KW),
                 jax.ShapeDtypeStruct((B, SWA_KV_HEADS, S // WINDOW, 2 * HEAD_DIM, WINDOW), _BF),
                 jax.ShapeDtypeStruct((B, FOX_HEADS, 1), _F32)]
    out_specs = [row(FOX_W), row(FOX_W), row(LANES),
                 pl.BlockSpec((None, FOX_HEADS, None, 2 * HEAD_DIM, tm), lambda b, j: (b, 0, j, 0, 0)),
                 row(SWA_QW), row(SWA_KW),
                 pl.BlockSpec((None, SWA_KV_HEADS, tm // WINDOW, 2 * HEAD_DIM, WINDOW), lambda b, j: (b, 0, j, 0, 0)),
                 pl.BlockSpec((None, FOX_HEADS, 1), lambda b, j: (b, 0, 0))]
    return pl.pallas_call(
        functools.partial(_qkv_kernel, n_pad=n_pad),
        grid=(B, nj),
        in_specs=[row(D)] + [full(a) for a in (g1, wqk, wvt, wfl, ws, wsvt, bfg, fqg, fkg, sqg, skg, f0)],
        out_specs=out_specs, out_shape=out_shape,
        scratch_shapes=[pltpu.VMEM((FOX_HEADS, 1), _F32)],
        compiler_params=_cparams(("arbitrary", "arbitrary")),
        name="qkv_proj",
    )(x3, g1, wqk, wvt, wfl, ws, wsvt, bfg, fqg, fkg, sqg, skg, f0)


def _fox_kernel(q_ref, k_ref, aug_ref, vt_ref, km_ref, augm_ref, vtm_ref, o_ref, m_ref, acc_ref, s_ref, mx_ref):
    g = pl.program_id(1)
    qi = pl.program_id(2)
    tq = q_ref.shape[0]
    tk = vt_ref.shape[-1]
    nt = (((1,), (1,)), ((), ()))
    lane = lax.broadcasted_iota(jnp.int32, (tq, LANES), 1)

    qaugs = []
    for pp in range(FOX_PAIRS):
        q = q_ref[:, pp * LANES:(pp + 1) * LANES]
        parts = []
        for a in range(2):
            head = 2 * (FOX_PAIRS * g + pp) + a
            keep = (lane < HEAD_DIM) if a == 0 else (lane >= HEAD_DIM)
            ones = jnp.where((lane < 3 * FOX_HEADS) & ((lane & (FOX_HEADS - 1)) == head), 1.0, 0.0).astype(_BF)
            parts.append(jnp.concatenate([jnp.where(keep, q, jnp.zeros_like(q)), ones], axis=1))
        qaugs.append(jnp.concatenate(parts, axis=0))

    def scores(pp, k, aug, mask):
        st = lax.dot_general(jnp.concatenate([k, aug], axis=1), qaugs[pp], nt, preferred_element_type=_F32)
        if mask is not None:
            st = jnp.where(mask, st, NEG_INF)
        return st, jnp.max(st, axis=0, keepdims=True)

    def tile_scores(pp, ki, mask=None):
        off = pl.multiple_of(ki * tk, tk)
        return scores(pp, k_ref[pl.ds(off, tk), pp * LANES:(pp + 1) * LANES], aug_ref[pl.ds(off, tk), :], mask)

    def process(pp, st, mx, vts, first):
        m_new = mx if first else jnp.maximum(m_ref[pp], mx)
        pt = jnp.exp2(st - m_new).astype(_BF)
        if not first:
            alpha = jnp.exp2(m_ref[pp] - m_new)
        for a in range(2):
            pv = jnp.dot(vts[a], pt[:, a * tq:(a + 1) * tq], preferred_element_type=_F32)
            acc_ref[2 * pp + a] = pv if first else alpha[:, a * tq:(a + 1) * tq] * acc_ref[2 * pp + a] + pv
        m_ref[pp] = m_new

    def stash(buf, pp, st_mx):
        s_ref[buf, pp] = st_mx[0]
        mx_ref[buf, pp] = st_mx[1]

    def vts(pp, ki):
        return vt_ref[2 * pp, ki], vt_ref[2 * pp + 1, ki]

    causal = lax.broadcasted_iota(jnp.int32, (tk, 2 * tq), 0) <= (lax.broadcasted_iota(jnp.int32, (tk, 2 * tq), 1) & (tq - 1))
    mask_m = lax.broadcasted_iota(jnp.int32, (META_BLOCK, 2 * tq), 0) >= META_PAD
    for pp in range(FOX_PAIRS):
        stash(0, pp, tile_scores(pp, qi, causal))
        st, mx = scores(pp, km_ref[:, pp * LANES:(pp + 1) * LANES], augm_ref[...], mask_m)
        process(pp, st, mx, (vtm_ref[2 * pp], vtm_ref[2 * pp + 1]), True)

    def step(j, buf):
        nxt = [tile_scores(pp, j) for pp in range(FOX_PAIRS)]
        cur = jnp.where(j == 0, qi, j - 1)
        for pp in range(FOX_PAIRS):
            process(pp, s_ref[buf, pp], mx_ref[buf, pp], vts(pp, cur), False)
        for pp in range(FOX_PAIRS):
            stash(1 - buf, pp, nxt[pp])

    def body(jj, c):
        step(2 * jj, 0)
        step(2 * jj + 1, 1)
        return c
    lax.fori_loop(0, qi // 2, body, 0)

    def finish(buf):
        last = jnp.where(qi == 0, qi, qi - 1)
        outs = []
        for pp in range(FOX_PAIRS):
            process(pp, s_ref[buf, pp], mx_ref[buf, pp], vts(pp, last), False)
            for a in range(2):
                acc = acc_ref[2 * pp + a]
                outs.append(acc[:HEAD_DIM] / acc[HEAD_DIM:HEAD_DIM + 1])
        o_ref[...] = jnp.concatenate(outs, axis=0).T.astype(_BF)

    @pl.when(qi % 2 == 1)
    def _():
        step(qi - 1, 0)
        finish(1)

    @pl.when(qi % 2 == 0)
    def _():
        finish(0)


def _fox_call(fq, fk, aug, fvt, km, augm, vtm, *, tq):
    B, S, _ = fq.shape
    nk = fvt.shape[2]
    assert fvt.shape[-1] == tq
    w = FOX_PAIRS * LANES
    qspec = pl.BlockSpec((None, tq, w), lambda b, g, i: (b, i, g))
    return pl.pallas_call(
        _fox_kernel, grid=(B, FOX_HEADS // (2 * FOX_PAIRS), S // tq),
        in_specs=[qspec,
                  pl.BlockSpec((None, S, w), lambda b, g, i: (b, 0, g)),
                  pl.BlockSpec((None, S, LANES), lambda b, g, i: (b, 0, 0)),
                  pl.BlockSpec((None, 2 * FOX_PAIRS, nk, 2 * HEAD_DIM, tq), lambda b, g, i: (b, g, 0, 0, 0)),
                  pl.BlockSpec((META_BLOCK, w), lambda b, g, i: (0, g)),
                  pl.BlockSpec((META_BLOCK, LANES), lambda b, g, i: (0, 0)),
                  pl.BlockSpec((2 * FOX_PAIRS, 2 * HEAD_DIM, META_BLOCK), lambda b, g, i: (g, 0, 0))],
        out_specs=qspec, out_shape=jax.ShapeDtypeStruct((B, S, FOX_W), _BF),
        scratch_shapes=[pltpu.VMEM((FOX_PAIRS, 1, 2 * tq), _F32), pltpu.VMEM((2 * FOX_PAIRS, 2 * HEAD_DIM, tq), _F32),
                        pltpu.VMEM((2, FOX_PAIRS, tq, 2 * tq), _F32), pltpu.VMEM((2, FOX_PAIRS, 1, 2 * tq), _F32)],
        compiler_params=_cparams(("arbitrary", "arbitrary", "arbitrary")), name="fox_attn",
    )(fq, fk, aug, fvt, km, augm, vtm)


def _swa_bias(tq):
    j = np.arange(WINDOW + tq)[:, None] - WINDOW
    i = np.arange(tq)[None, :]
    dist = i - j
    ok = (dist >= 0) & (dist < WINDOW)
    slopes = np.exp2(-8.0 * (np.arange(SWA_HEADS, dtype=np.float32) + 1.0) / SWA_HEADS).astype(np.float32)
    per_head = np.where(ok[None], -slopes[:, None, None] * dist[None].astype(np.float32), np.float32(NEG_INF))
    rep = SWA_HEADS // SWA_KV_HEADS
    return np.stack([np.concatenate(list(per_head[g * rep:(g + 1) * rep]), axis=1) for g in range(SWA_KV_HEADS)])


def _swa_kernel(q_ref, kc_ref, kp_ref, km_ref, vc_ref, vp_ref, vm_ref, bias_ref, sink_ref, o_ref):
    qi = pl.program_id(1)
    tq = q_ref.shape[0] // SWA_TILES
    r = tq // WINDOW
    rep = SWA_HEADS // SWA_KV_HEADS
    nt = (((1,), (1,)), ((), ()))
    first = qi == 0
    row = lax.broadcasted_iota(jnp.int32, (WINDOW + tq, rep * tq), 0)
    row_ok = row >= jnp.where(first, META_PAD, 0)
    lane = lax.broadcasted_iota(jnp.int32, (tq, LANES), 1)
    for sub in range(SWA_TILES):
        q0 = sub * tq
        if sub == 0:
            kcat = jnp.concatenate([jnp.where(first, km_ref[...], kp_ref[...]), kc_ref[:tq, :]], axis=0)
        else:
            kcat = kc_ref[q0 - WINDOW:q0 + tq, :]
        outs = [None] * SWA_HEADS
        for g in range(SWA_KV_HEADS):
            keep = (lane < HEAD_DIM) if g == 0 else (lane >= HEAD_DIM)
            qg = jnp.concatenate([jnp.where(keep, q_ref[q0:q0 + tq, p * LANES:(p + 1) * LANES], jnp.zeros((tq, LANES), _BF))
                                  for p in range(rep)], axis=0)
            vprev = jnp.where(first, vm_ref[g], vp_ref[g, 0]) if sub == 0 else vc_ref[g, sub * r - 1]
            vcat = jnp.concatenate([vprev] + [vc_ref[g, sub * r + c] for c in range(r)], axis=1)
            st = lax.dot_general(kcat, qg, nt, preferred_element_type=_F32) + bias_ref[g]
            if sub == 0:
                st = jnp.where(row_ok, st, NEG_INF)
            sink = sink_ref[g]
            m = jnp.maximum(jnp.max(st, axis=0, keepdims=True), sink)
            pt = jnp.exp(st - m).astype(_BF)
            acc = jnp.dot(vcat, pt, preferred_element_type=_F32)
            o = acc[:HEAD_DIM] / (acc[HEAD_DIM:HEAD_DIM + 1] + jnp.exp(sink - m))
            for p in range(rep):
                outs[2 * p + g] = o[:, p * tq:(p + 1) * tq]
        o_ref[q0:q0 + tq, :] = jnp.concatenate(outs, axis=0).T.astype(_BF)


def _swa_call(sinks, sq, sk, svt, km, vtm, *, tq):
    B, S, _ = sq.shape
    r = tq // WINDOW
    rep = SWA_HEADS // SWA_KV_HEADS
    ts = tq * SWA_TILES
    assert S % ts == 0
    bias = jnp.asarray(_swa_bias(tq))
    sink_rows = jnp.repeat(sinks.reshape(SWA_KV_HEADS, rep), tq, axis=1)[:, None, :]
    prev_blk = lambda i: jnp.maximum(i * r * SWA_TILES - 1, 0)
    return pl.pallas_call(
        _swa_kernel, grid=(B, S // ts),
        in_specs=[pl.BlockSpec((None, ts, SWA_QW), lambda b, i: (b, i, 0)),
                  pl.BlockSpec((None, ts, LANES), lambda b, i: (b, i, 0)),
                  pl.BlockSpec((None, WINDOW, LANES), lambda b, i: (b, prev_blk(i), 0)),
                  pl.BlockSpec((META_BLOCK, LANES), lambda b, i: (0, 0)),
                  pl.BlockSpec((None, SWA_KV_HEADS, r * SWA_TILES, 2 * HEAD_DIM, WINDOW), lambda b, i: (b, 0, i, 0, 0)),
                  pl.BlockSpec((None, SWA_KV_HEADS, 1, 2 * HEAD_DIM, WINDOW), lambda b, i: (b, 0, prev_blk(i), 0, 0)),
                  pl.BlockSpec((SWA_KV_HEADS, 2 * HEAD_DIM, META_BLOCK), lambda b, i: (0, 0, 0)),
                  pl.BlockSpec(bias.shape, lambda b, i: (0, 0, 0)),
                  pl.BlockSpec(sink_rows.shape, lambda b, i: (0, 0, 0))],
        out_specs=pl.BlockSpec((None, ts, SWA_QW), lambda b, i: (b, i, 0)),
        out_shape=jax.ShapeDtypeStruct((B, S, SWA_QW), _BF),
        compiler_params=_cparams(("arbitrary", "arbitrary")), name="swa_attn",
    )(sq, sk, sk, km, svt, svt, vtm, bias, sink_rows)


def _mix_kernel(x_ref, oa_ref, ob_ref, g1_ref, wga_ref, wgb_ref, wfo_ref, wso_ref, wo_ref, g2_ref, wr_ref, br_ref,
                hres_ref, h2_ref, lg_ref):
    x = x_ref[...]
    h = _rms(x, g1_ref[...]).astype(_BF)
    ga = jax.nn.sigmoid(jnp.dot(h, wga_ref[...], preferred_element_type=_F32))
    gb = jax.nn.sigmoid(jnp.dot(h, wgb_ref[...], preferred_element_type=_F32))
    mix = ga * jnp.dot(oa_ref[...], wfo_ref[...], preferred_element_type=_F32) \
        + gb * jnp.dot(ob_ref[...], wso_ref[...], preferred_element_type=_F32)
    hres = x + jnp.dot(mix.astype(_BF), wo_ref[...], preferred_element_type=_F32)
    hres_ref[...] = hres
    h2 = _rms(hres, g2_ref[...])
    h2_ref[...] = h2.astype(_BF)
    wr = wr_ref[...]
    w_hi = wr.astype(_BF)
    w_lo = (wr - w_hi.astype(_F32)).astype(_BF)
    h_hi = h2.astype(_BF)
    h_lo = (h2 - h_hi.astype(_F32)).astype(_BF)
    lg = jnp.dot(h_hi, w_hi, preferred_element_type=_F32) + jnp.dot(h_hi, w_lo, preferred_element_type=_F32) \
        + jnp.dot(h_lo, w_hi, preferred_element_type=_F32)
    lg_ref[...] = lg + br_ref[...]


def _mix_call(x2, oa, ob, g1, wga, wgb, wfo, wso, wo, g2, wr, br, *, tm):
    N, D = x2.shape
    row = lambda w: pl.BlockSpec((tm, w), lambda i: (i, 0))
    full = lambda a: pl.BlockSpec(a.shape, lambda i: (0,) * a.ndim)
    return pl.pallas_call(
        _mix_kernel, grid=(N // tm,),
        in_specs=[row(D), row(FOX_W), row(SWA_QW)] + [full(a) for a in (g1, wga, wgb, wfo, wso, wo, g2, wr, br)],
        out_specs=[row(D), row(D), row(LANES)],
        out_shape=[jax.ShapeDtypeStruct((N, D), _F32), jax.ShapeDtypeStruct((N, D), _BF),
                   jax.ShapeDtypeStruct((N, LANES), _F32)],
        compiler_params=_cparams(("arbitrary",)), name="mix_proj",
    )(x2, oa, ob, g1, wga, wgb, wfo, wso, wo, g2, wr, br)


def _route_kernel(lg_ref, a_ref, at_ref, cnt_ref, lb_ref):
    T = lg_ref.shape[0]
    lt = lg_ref[...].T[:N_EXPERTS]
    e_iota = lax.broadcasted_iota(jnp.int32, lt.shape, 0)
    cur = lt
    vals, idxs = [], []
    for _ in range(TOP_K):
        m = jnp.max(cur, axis=0, keepdims=True)
        idx = jnp.min(jnp.where(cur == m, e_iota, N_EXPERTS), axis=0, keepdims=True)
        vals.append(m)
        idxs.append(idx)
        cur = jnp.where(e_iota == idx, -jnp.inf, cur)
    es = [jnp.exp(v - vals[0]) for v in vals]
    den = es[0] + es[1] + es[2] + es[3]
    gates = [e / den for e in es]

    onehot = jnp.zeros(lt.shape, _F32)
    for idx in idxs:
        onehot = onehot + jnp.where(e_iota == idx, 1.0, 0.0)
    r_i = lax.broadcasted_iota(jnp.int32, (T, T), 0)
    c_i = lax.broadcasted_iota(jnp.int32, (T, T), 1)
    utri = jnp.where(r_i < c_i, 1.0, 0.0).astype(_BF)
    prefix = jnp.dot(onehot.astype(_BF), utri, preferred_element_type=_F32)
    cnt = jnp.sum(onehot, axis=1, keepdims=True)
    chunks = jnp.floor((cnt + (RUN_ALIGN - 1)) * (1.0 / RUN_ALIGN))
    l_r = lax.broadcasted_iota(jnp.int32, (N_EXPERTS, N_EXPERTS), 0)
    l_c = lax.broadcasted_iota(jnp.int32, (N_EXPERTS, N_EXPERTS), 1)
    ltri = jnp.where(l_c < l_r, 1.0, 0.0).astype(_BF)
    lbase = RUN_ALIGN * jnp.dot(ltri, jnp.broadcast_to(chunks, (N_EXPERTS, LANES)).astype(_BF),
                                preferred_element_type=_F32)[:, 0:1]
    slot_all = prefix + lbase
    rows = [jnp.sum(jnp.where(e_iota == idx, slot_all, 0.0), axis=0, keepdims=True) for idx in idxs] + gates
    at = jnp.concatenate(rows, axis=0)
    at_ref[...] = at
    a_ref[...] = jnp.concatenate([at, jnp.zeros((LANES - 2 * TOP_K, T), _F32)], axis=0).T
    cnt_ref[...] = jnp.broadcast_to(cnt, (N_EXPERTS, LANES)).astype(jnp.int32)
    lb_ref[...] = jnp.broadcast_to(lbase, (N_EXPERTS, LANES)).astype(jnp.int32)


def _route_call(lg, *, T):
    N = lg.shape[0]
    nt = N // T
    per_tile = pl.BlockSpec((None, N_EXPERTS, LANES), lambda i: (i, 0, 0))
    return pl.pallas_call(
        _route_kernel, grid=(nt,),
        in_specs=[pl.BlockSpec((T, LANES), lambda i: (i, 0))],
        out_specs=[pl.BlockSpec((T, LANES), lambda i: (i, 0)), pl.BlockSpec((8, T), lambda i: (0, i)), per_tile, per_tile],
        out_shape=[jax.ShapeDtypeStruct((N, LANES), _F32), jax.ShapeDtypeStruct((8, N), _F32),
                   jax.ShapeDtypeStruct((nt, N_EXPERTS, LANES), jnp.int32),
                   jax.ShapeDtypeStruct((nt, N_EXPERTS, LANES), jnp.int32)],
        compiler_params=_cparams(("arbitrary",)), name="route",
    )(lg)


def _div_pow2(v, d):
    assert d & (d - 1) == 0
    return lax.shift_right_logical(v, d.bit_length() - 1)


def _plan_kernel(cnt_ref, goff_ref, nch_ref, eblk_ref, gap_ref, info_ref, *, nt, block_rows):
    def per_expert(e, base):
        def per_tile(i, run):
            n = _div_pow2(cnt_ref[i * N_EXPERTS + e] + (RUN_ALIGN - 1), RUN_ALIGN)
            goff_ref[i * N_EXPERTS + e] = run
            nch_ref[i * N_EXPERTS + e] = n
            return run + n * RUN_ALIGN
        end = lax.fori_loop(0, nt, per_tile, base)
        nblk = _div_pow2(end - base + (block_rows - 1), block_rows)
        nxt = base + nblk * block_rows
        gap_ref[2 * e] = end
        gap_ref[2 * e + 1] = _div_pow2(nxt - end, RUN_ALIGN)
        eblk_ref[2 * e] = _div_pow2(base, block_rows)
        eblk_ref[2 * e + 1] = nblk
        return nxt
    total = lax.fori_loop(0, N_EXPERTS, per_expert, 0)
    info_ref[0] = _div_pow2(total, block_rows)


def _plan_call(cnt_flat, *, nt, block_rows):
    smem = pl.BlockSpec(memory_space=pltpu.SMEM)
    i32 = lambda n: jax.ShapeDtypeStruct((n,), jnp.int32)
    return pl.pallas_call(
        functools.partial(_plan_kernel, nt=nt, block_rows=block_rows),
        in_specs=[smem], out_specs=[smem] * 5,
        out_shape=[i32(nt * N_EXPERTS), i32(nt * N_EXPERTS), i32(2 * N_EXPERTS), i32(2 * N_EXPERTS), i32(1)],
        name="plan",
    )(cnt_flat)


def _pack_pairs(v):
    w = v.shape[1] // 2
    return pltpu.pack_elementwise([v[:, :w], v[:, w:]], packed_dtype=_BF)


def _unpack_pairs(wd):
    lo = pltpu.unpack_elementwise(wd, index=0, packed_dtype=_BF, unpacked_dtype=_F32)
    hi = pltpu.unpack_elementwise(wd, index=1, packed_dtype=_BF, unpacked_dtype=_F32)
    return lo.astype(_BF), hi.astype(_BF)


def _dispatch_kernel(goff, lbase, nch, gap, info, at_ref, h2_ref, xs_ref, xl_ref, z_ref, sem, zsem, tsem):
    i = pl.program_id(0)
    R = xl_ref.shape[1]
    T = h2_ref.shape[0]
    block_rows = z_ref.shape[0]
    nb = xs_ref.shape[0] // block_rows

    @pl.when(i == 0)
    def _():
        z_ref[...] = jnp.zeros(z_ref.shape, z_ref.dtype)

        def gap_copy(e):
            rows = pl.multiple_of(gap[2 * e + 1] * RUN_ALIGN, RUN_ALIGN)
            return pltpu.make_async_copy(z_ref.at[pl.ds(0, rows)],
                                         xs_ref.at[pl.ds(pl.multiple_of(gap[2 * e], RUN_ALIGN), rows)], zsem)

        def fill(e, c):
            @pl.when(gap[2 * e + 1] > 0)
            def _():
                gap_copy(e).start()
            return c
        lax.fori_loop(0, N_EXPERTS, fill, 0)

        def fill_wait(e, c):
            @pl.when(gap[2 * e + 1] > 0)
            def _():
                gap_copy(e).wait()
            return c
        lax.fori_loop(0, N_EXPERTS, fill_wait, 0)

        def tail(b, c):
            pltpu.make_async_copy(z_ref, xs_ref.at[pl.ds(pl.multiple_of(b * block_rows, block_rows), block_rows)],
                                  tsem).start()
            return c
        lax.fori_loop(info[0], nb, tail, 0)

        def tail_wait(b, c):
            pltpu.make_async_copy(z_ref, xs_ref.at[pl.ds(0, block_rows)], tsem).wait()
            return c
        lax.fori_loop(info[0], nb, tail_wait, 0)

    slot_iota = lax.broadcasted_iota(jnp.int32, (R, T), 0).astype(_F32)
    pm = jnp.zeros((R, T), _F32)
    for k in range(TOP_K):
        pm = jnp.where(slot_iota == at_ref[k:k + 1, :], 1.0, pm)
    xl = jnp.dot(pm.astype(_BF), h2_ref[...], preferred_element_type=_F32)
    slot = i & 1

    def wait_tile(t, s):
        total = lax.fori_loop(0, N_EXPERTS, lambda e, c: c + nch[t * N_EXPERTS + e], 0)

        @pl.when(total > 0)
        def _():
            rows = pl.multiple_of(total * RUN_ALIGN, RUN_ALIGN)
            pltpu.make_async_copy(xl_ref.at[s, pl.ds(0, rows)], xs_ref.at[pl.ds(0, rows)], sem.at[s]).wait()

    @pl.when(i >= 2)
    def _():
        wait_tile(i - 2, slot)
    xl_ref[slot] = _pack_pairs(xl)

    def issue(e, c):
        n = nch[i * N_EXPERTS + e]

        @pl.when(n > 0)
        def _():
            rows = pl.multiple_of(n * RUN_ALIGN, RUN_ALIGN)
            pltpu.make_async_copy(xl_ref.at[slot, pl.ds(pl.multiple_of(lbase[i * N_EXPERTS + e], RUN_ALIGN), rows)],
                                  xs_ref.at[pl.ds(pl.multiple_of(goff[i * N_EXPERTS + e], RUN_ALIGN), rows)],
                                  sem.at[slot]).start()
        return c
    lax.fori_loop(0, N_EXPERTS, issue, 0)

    @pl.when(i == pl.num_programs(0) - 1)
    def _():
        @pl.when(i >= 1)
        def _():
            wait_tile(i - 1, 1 - slot)
        wait_tile(i, slot)


def _dispatch_call(goff, lbase, nch, gap, info, at, h2, *, T, R, rows, block_rows):
    N, D = h2.shape
    grid_spec = pltpu.PrefetchScalarGridSpec(
        num_scalar_prefetch=5, grid=(N // T,),
        in_specs=[pl.BlockSpec((8, T), lambda i, *_: (0, i)), pl.BlockSpec((T, D), lambda i, *_: (i, 0))],
        out_specs=pl.BlockSpec(memory_space=pl.ANY),
        scratch_shapes=[pltpu.VMEM((2, R, D // 2), jnp.uint32), pltpu.VMEM((block_rows, D // 2), jnp.uint32),
                        pltpu.SemaphoreType.DMA((2,)), pltpu.SemaphoreType.DMA, pltpu.SemaphoreType.DMA])
    return pl.pallas_call(
        _dispatch_kernel, grid_spec=grid_spec, out_shape=jax.ShapeDtypeStruct((rows, D // 2), jnp.uint32),
        compiler_params=_cparams(("arbitrary",)), name="dispatch",
    )(goff, lbase, nch, gap, info, at, h2)


def _expert_kernel(eblk, info, xs_ref, wu_ref, bu_ref, wd_ref, bd_ref, ys_ref, wub_ref, wdb_ref, xb_ref, yb_ref, xsem, ysem):
    e = pl.program_id(0)
    block_rows = xb_ref.shape[1]
    nb = ys_ref.shape[0] // block_rows
    b0 = eblk[2 * e]
    n = eblk[2 * e + 1]
    half = wu_ref.shape[0] // 2
    wub_ref[...] = wu_ref[...].astype(_BF)
    wdb_ref[...] = wd_ref[...].astype(_BF)

    def rows_of(blk):
        return pl.ds(pl.multiple_of(blk * block_rows, block_rows), block_rows)

    def x_copy(j, slot):
        return pltpu.make_async_copy(xs_ref.at[rows_of(b0 + j)], xb_ref.at[slot], xsem.at[slot])

    def y_copy(j, slot):
        return pltpu.make_async_copy(yb_ref.at[slot], ys_ref.at[rows_of(b0 + j)], ysem.at[slot])

    @pl.when((e == 0) & (n > 0))
    def _():
        x_copy(0, 0).start()

    def body(j, c):
        slot = j & 1
        x_copy(j, slot).wait()

        @pl.when(j + 1 < n)
        def _():
            x_copy(j + 1, 1 - slot).start()

        @pl.when(j >= 2)
        def _():
            y_copy(j - 2, slot).wait()

        x_lo, x_hi = _unpack_pairs(xb_ref[slot])
        gu = jnp.dot(x_lo, wub_ref[:half, :], preferred_element_type=_F32) \
            + jnp.dot(x_hi, wub_ref[half:, :], preferred_element_type=_F32) + bu_ref[...]
        g = jnp.minimum(gu[:, :D_FF], SWIGLU_LIMIT)
        u = jnp.clip(gu[:, D_FF:], -SWIGLU_LIMIT, SWIGLU_LIMIT)
        act = (u + 1.0) * (g * jax.nn.sigmoid(SWIGLU_ALPHA * g))
        y = jnp.dot(act.astype(_BF), wdb_ref[...], preferred_element_type=_F32) + bd_ref[...]
        yb_ref[slot] = _pack_pairs(y)
        y_copy(j, slot).start()
        return c
    lax.fori_loop(0, n, body, 0)

    @pl.when(n >= 2)
    def _():
        y_copy(n - 2, n & 1).wait()

    @pl.when(n >= 1)
    def _():
        y_copy(n - 1, (n - 1) & 1).wait()

    last = pl.num_programs(0) - 1
    e_next = jnp.minimum(e + 1, last)

    @pl.when((e < last) & (eblk[2 * e_next + 1] > 0))
    def _():
        pltpu.make_async_copy(xs_ref.at[rows_of(eblk[2 * e_next])], xb_ref.at[0], xsem.at[0]).start()

    @pl.when(e == last)
    def _():
        yb_ref[0] = _pack_pairs(jnp.zeros((block_rows, 2 * yb_ref.shape[2]), _F32))

        def tail_copy(b):
            return pltpu.make_async_copy(yb_ref.at[0], ys_ref.at[pl.ds(pl.multiple_of(b * block_rows, block_rows), block_rows)],
                                         ysem.at[0])

        def tail(b, c):
            tail_copy(b).start()
            return c
        lax.fori_loop(info[0], nb, tail, 0)

        def tail_wait(b, c):
            tail_copy(b).wait()
            return c
        lax.fori_loop(info[0], nb, tail_wait, 0)


def _expert_call(eblk, info, xs, wu, bu, wd, bd, *, block_rows):
    rows, half = xs.shape
    D = 2 * half
    ex = lambda e, *_: (e, 0, 0)
    grid_spec = pltpu.PrefetchScalarGridSpec(
        num_scalar_prefetch=2, grid=(N_EXPERTS,),
        in_specs=[pl.BlockSpec(memory_space=pl.ANY),
                  pl.BlockSpec((None, D, 2 * D_FF), ex), pl.BlockSpec((None, 1, 2 * D_FF), ex),
                  pl.BlockSpec((None, D_FF, D), ex), pl.BlockSpec((None, 1, D), ex)],
        out_specs=pl.BlockSpec(memory_space=pl.ANY),
        scratch_shapes=[pltpu.VMEM((D, 2 * D_FF), _BF), pltpu.VMEM((D_FF, D), _BF),
                        pltpu.VMEM((2, block_rows, half), jnp.uint32), pltpu.VMEM((2, block_rows, half), jnp.uint32),
                        pltpu.SemaphoreType.DMA((2,)), pltpu.SemaphoreType.DMA((2,))])
    return pl.pallas_call(
        _expert_kernel, grid_spec=grid_spec, out_shape=jax.ShapeDtypeStruct((rows, half), jnp.uint32),
        compiler_params=_cparams(("arbitrary",)), name="experts",
    )(eblk, info, xs, wu, bu, wd, bd)


def _combine_kernel(goff, lbase, nch, a_ref, hres_ref, ys_ref, o_ref, yl_ref, sem):
    i = pl.program_id(0)
    nt = pl.num_programs(0)
    R = yl_ref.shape[1]
    T = a_ref.shape[0]

    def fetch(t, slot):
        def issue(e, c):
            n = nch[t * N_EXPERTS + e]

            @pl.when(n > 0)
            def _():
                rows = pl.multiple_of(n * RUN_ALIGN, RUN_ALIGN)
                pltpu.make_async_copy(
                    ys_ref.at[pl.ds(pl.multiple_of(goff[t * N_EXPERTS + e], RUN_ALIGN), rows)],
                    yl_ref.at[slot, pl.ds(pl.multiple_of(lbase[t * N_EXPERTS + e], RUN_ALIGN), rows)], sem.at[slot]).start()
            return c
        lax.fori_loop(0, N_EXPERTS, issue, 0)

    @pl.when(i == 0)
    def _():
        yl_ref[...] = jnp.zeros(yl_ref.shape, yl_ref.dtype)
        fetch(0, 0)

    slot = i & 1

    @pl.when(i + 1 < nt)
    def _():
        fetch(i + 1, 1 - slot)

    total = lax.fori_loop(0, N_EXPERTS, lambda e, c: c + nch[i * N_EXPERTS + e], 0)

    @pl.when(total > 0)
    def _():
        rows = pl.multiple_of(total * RUN_ALIGN, RUN_ALIGN)
        pltpu.make_async_copy(ys_ref.at[pl.ds(0, rows)], yl_ref.at[slot, pl.ds(0, rows)], sem.at[slot]).wait()

    a = a_ref[...]
    slot_iota = lax.broadcasted_iota(jnp.int32, (T, R), 1).astype(_F32)
    pt = jnp.zeros((T, R), _F32)
    for k in range(TOP_K):
        pt = jnp.where(slot_iota == a[:, k:k + 1], a[:, TOP_K + k:TOP_K + k + 1], pt)
    pt = pt.astype(_BF)
    y_lo, y_hi = _unpack_pairs(yl_ref[slot])
    half = y_lo.shape[1]
    o_ref[:, :half] = hres_ref[:, :half] + jnp.dot(pt, y_lo, preferred_element_type=_F32)
    o_ref[:, half:] = hres_ref[:, half:] + jnp.dot(pt, y_hi, preferred_element_type=_F32)


def _combine_call(goff, lbase, nch, a, hres, ys, *, T, R):
    N, D = hres.shape
    grid_spec = pltpu.PrefetchScalarGridSpec(
        num_scalar_prefetch=3, grid=(N // T,),
        in_specs=[pl.BlockSpec((T, LANES), lambda i, *_: (i, 0)), pl.BlockSpec((T, D), lambda i, *_: (i, 0)),
                  pl.BlockSpec(memory_space=pl.ANY)],
        out_specs=pl.BlockSpec((T, D), lambda i, *_: (i, 0)),
        scratch_shapes=[pltpu.VMEM((2, R, D // 2), jnp.uint32), pltpu.SemaphoreType.DMA((2,))])
    return pl.pallas_call(
        _combine_kernel, grid_spec=grid_spec, out_shape=jax.ShapeDtypeStruct((N, D), _F32),
        compiler_params=_cparams(("arbitrary",)), name="combine",
    )(goff, lbase, nch, a, hres, ys)


def _tiles(S):
    pick = lambda pref: next(t for t in pref if S % t == 0)
    tm = pick((512, 256, 128))
    return dict(tm=tm, tq_fox=tm, tq_swa=pick((256, 128)),
                tm_mix=pick((512, 256, 128)), t_route=pick((256, 128)), block_rows=256)


def kernel(x, meta_tokens, norm1_gain, w_in, b_forget, fox_q_gain, fox_k_gain, swa_q_gain, swa_k_gain, swa_sinks,
           w_fox_out, w_swa_out, w_out, norm2_gain, w_router, b_router, w_up, b_up, w_down, b_down):
    B, S, D = x.shape
    assert norm1_gain.shape[0] == 1 and S % LANES == 0
    tl = _tiles(S)
    N = B * S
    scale = HEAD_DIM ** -0.5

    w = w_in[0]
    c0 = 3 * FOX_W
    c1 = c0 + FOX_HEADS
    c2 = c1 + SWA_QW + 2 * SWA_KW
    wqk = w[:, :2 * FOX_W].astype(_BF)
    wvt = w[:, 2 * FOX_W:c0].T.astype(_BF)
    wfl = w[:, c0:c1].T.astype(_BF)
    perm_heads = np.array([h for p in range(SWA_HEADS // 2) for h in (p, p + SWA_HEADS // 2)])
    perm_cols = (perm_heads[:, None] * HEAD_DIM + np.arange(HEAD_DIM)[None, :]).reshape(-1)
    ws_all = w[:, c1:c2]
    ws = jnp.concatenate([ws_all[:, :SWA_QW][:, perm_cols], ws_all[:, SWA_QW:SWA_QW + SWA_KW]], axis=1).astype(_BF)
    wsvt = ws_all[:, SWA_QW + SWA_KW:].T.astype(_BF)
    wga = w[:, c2:c2 + D].astype(_BF)
    wgb = w[:, c2 + D:].astype(_BF)
    g1 = norm1_gain[0][None, :]
    g2 = norm2_gain[0][None, :]
    bfg = b_forget[0][:, None]
    fqg = jnp.tile(fox_q_gain[0], FOX_HEADS)[None, :] * (scale * LOG2E)
    fkg = jnp.tile(fox_k_gain[0], FOX_HEADS)[None, :]
    sqg = jnp.tile(swa_q_gain[0], SWA_HEADS)[None, :] * scale
    skg = jnp.tile(swa_k_gain[0], SWA_KV_HEADS)[None, :]
    wfo = w_fox_out[0].astype(_BF)
    wso = w_swa_out[0][perm_cols, :].astype(_BF)
    wo = w_out[0].astype(_BF)
    wr = jnp.pad(w_router[0], ((0, 0), (0, LANES - N_EXPERTS)))
    br = jnp.pad(b_router[0], (0, LANES - N_EXPERTS), constant_values=NEG_INF)[None, :]
    sinks = swa_sinks[0].astype(_F32)
    wu = w_up[0]
    wd = w_down[0]
    bu = b_up[0][:, None, :]
    bd = b_down[0][:, None, :]

    meta_blk = jnp.pad(meta_tokens.astype(x.dtype), ((META_PAD, 0), (0, 0)))[None]
    zero_f = jnp.zeros((FOX_HEADS, 1), _F32)
    _, fk_m, aug_m, fvt_m, _, sk_m, svt_m, fend_m = _qkv_call(meta_blk, g1, wqk, wvt, wfl, ws, wsvt, bfg, fqg, fkg, sqg,
                                                              skg, zero_f, tm=META_BLOCK, n_pad=META_PAD)
    fq, fk, aug, fvt, sq, sk, svt, _ = _qkv_call(x, g1, wqk, wvt, wfl, ws, wsvt, bfg, fqg, fkg, sqg, skg, fend_m[0],
                                                 tm=tl["tm"], n_pad=0)
    o_a = _fox_call(fq, fk, aug, fvt, fk_m[0], aug_m[0], fvt_m[0, :, 0], tq=tl["tq_fox"])
    o_b = _swa_call(sinks, sq, sk, svt, sk_m[0], svt_m[0, :, 0], tq=tl["tq_swa"])

    hres, h2, lg = _mix_call(x.reshape(N, D), o_a.reshape(N, FOX_W), o_b.reshape(N, SWA_QW),
                             g1, wga, wgb, wfo, wso, wo, g2, wr, br, tm=tl["tm_mix"])

    T = tl["t_route"]
    nt = N // T
    block_rows = tl["block_rows"]
    R = -(-(TOP_K * T + N_EXPERTS * (RUN_ALIGN - 1)) // LANES) * LANES
    max_rows = N * TOP_K + nt * N_EXPERTS * (RUN_ALIGN - 1) + N_EXPERTS * (block_rows - RUN_ALIGN)
    nb = -(-max_rows // block_rows)
    a, at, cnt, lb = _route_call(lg, T=T)
    lbase = lb[:, :, 0].reshape(-1)
    goff, nch, eblk, gap, info = _plan_call(cnt[:, :, 0].reshape(-1), nt=nt, block_rows=block_rows)
    xs = _dispatch_call(goff, lbase, nch, gap, info, at, h2, T=T, R=R, rows=nb * block_rows, block_rows=block_rows)
    ys = _expert_call(eblk, info, xs, wu, bu, wd, bd, block_rows=block_rows)
    out = _combine_call(goff, lbase, nch, a, hres, ys, T=T, R=R)
    return out.reshape(B, S, D)
```

```python
import functools

import jax
import jax.numpy as jnp
import numpy as np
from jax import lax
from jax.experimental import pallas as pl
from jax.experimental.pallas import tpu as pltpu

HEAD_DIM = 64
N_META = 16
FOX_HEADS = 8
SWA_HEADS = 8
SWA_KV_HEADS = 2
WINDOW = 128
N_EXPERTS = 32
TOP_K = 4
D_FF = 1024
SWIGLU_LIMIT = 7.0
SWIGLU_ALPHA = 1.702
RMS_EPS = 1e-6
NEG_INF = -1e30
LOG2E = 1.4426950408889634

LANES = 128
META_BLOCK = 128
META_PAD = META_BLOCK - N_META
FOX_PAIRS = 4
RUN_ALIGN = 8
VMEM_LIMIT = 56 * 1024 * 1024

FOX_W = FOX_HEADS * HEAD_DIM
SWA_QW = SWA_HEADS * HEAD_DIM
SWA_KW = SWA_KV_HEADS * HEAD_DIM

_BF = jnp.bfloat16
_F32 = jnp.float32


def _cparams(sem):
    return pltpu.CompilerParams(dimension_semantics=sem, vmem_limit_bytes=VMEM_LIMIT)


def _rms(t, gain):
    return t * lax.rsqrt(jnp.mean(t * t, axis=-1, keepdims=True) + RMS_EPS) * gain


def _head_rms(z, gain_row):
    lane = lax.broadcasted_iota(jnp.int32, (z.shape[0], LANES), 1)
    lo_mask = lane < HEAD_DIM
    outs = []
    for b in range(z.shape[1] // LANES):
        v = z[:, b * LANES:(b + 1) * LANES]
        v2 = v * v
        tot = jnp.sum(v2, axis=-1, keepdims=True)
        lo = jnp.sum(jnp.where(lo_mask, v2, 0.0), axis=-1, keepdims=True)
        hi = tot - lo
        r_lo = lax.rsqrt(lo * (1.0 / HEAD_DIM) + RMS_EPS)
        r_hi = lax.rsqrt(hi * (1.0 / HEAD_DIM) + RMS_EPS)
        outs.append(v * jnp.where(lo_mask, r_lo, r_hi))
    return jnp.concatenate(outs, axis=-1) * gain_row


def _lane_cumsum(x):
    n = x.shape[-1]
    lane = lax.broadcasted_iota(jnp.int32, x.shape, x.ndim - 1)
    s = 1
    while s < n:
        x = x + jnp.where(lane >= s, pltpu.roll(x, s, x.ndim - 1), 0.0)
        s *= 2
    return x


def _qkv_kernel(x_ref, g1_ref, wqk_ref, wvt_ref, wfl_ref, ws_ref, wsvt_ref, bf_ref, fqg_ref, fkg_ref, sqg_ref, skg_ref,
                f0_ref, fq_ref, fk_ref, aug_ref, fvt_ref, sq_ref, sk_ref, svt_ref, fend_ref, carry_ref, *, n_pad):
    j = pl.program_id(1)

    @pl.when(j == 0)
    def _():
        carry_ref[...] = f0_ref[...]

    x = x_ref[...]
    tm = x.shape[0]
    h = _rms(x, g1_ref[...]).astype(_BF)
    nt = (((1,), (1,)), ((), ()))
    f = jnp.dot(h, wqk_ref[...], preferred_element_type=_F32)
    s = jnp.dot(h, ws_ref[...], preferred_element_type=_F32)
    svt = lax.dot_general(wsvt_ref[...], h, nt, preferred_element_type=_F32)
    vt = lax.dot_general(wvt_ref[...], h, nt, preferred_element_type=_F32)
    flog = lax.dot_general(wfl_ref[...], h, nt, preferred_element_type=_F32)

    fq_ref[...] = _head_rms(f[:, :FOX_W], fqg_ref[...]).astype(_BF)
    fk_ref[...] = _head_rms(f[:, FOX_W:], fkg_ref[...]).astype(_BF)
    sq_ref[...] = _head_rms(s[:, :SWA_QW], sqg_ref[...]).astype(_BF)
    sk_ref[...] = _head_rms(s[:, SWA_QW:], skg_ref[...]).astype(_BF)

    ones_row = jnp.where(lax.broadcasted_iota(jnp.int32, (HEAD_DIM, tm), 0) == 0, 1.0, 0.0).astype(_BF)
    for hd in range(FOX_HEADS):
        fvt_ref[hd] = jnp.concatenate([vt[hd * HEAD_DIM:(hd + 1) * HEAD_DIM].astype(_BF), ones_row], axis=0)
    for g in range(SWA_KV_HEADS):
        blk = jnp.concatenate([svt[g * HEAD_DIM:(g + 1) * HEAD_DIM].astype(_BF), ones_row], axis=0)
        for c in range(tm // WINDOW):
            svt_ref[g, c] = blk[:, c * WINDOW:(c + 1) * WINDOW]

    z = flog + bf_ref[...]
    log_f = jnp.minimum(z, 0.0) - jnp.log1p(jnp.exp(-jnp.abs(z)))
    if n_pad:
        lane = lax.broadcasted_iota(jnp.int32, log_f.shape, 1)
        log_f = jnp.where(lane >= n_pad, log_f, 0.0)
    cum = _lane_cumsum(log_f) + carry_ref[...]
    carry_ref[...] = cum[:, -1:]
    fend_ref[...] = cum[:, -1:]
    nfl = cum * (-LOG2E)
    p1 = nfl.astype(_BF).astype(_F32)
    p2 = (nfl - p1).astype(_BF).astype(_F32)
    p3 = (nfl - p1 - p2).astype(_BF).astype(_F32)
    g = jnp.concatenate([p1, p2, p3, jnp.zeros((LANES - 3 * FOX_HEADS, tm), _F32)], axis=0)
    aug_ref[...] = g.T.astype(_BF)


def _qkv_call(x3, g1, wqk, wvt, wfl, ws, wsvt, bfg, fqg, fkg, sqg, skg, f0, *, tm, n_pad):
    B, S, D = x3.shape
    nj = S // tm
    row = lambda w: pl.BlockSpec((None, tm, w), lambda b, j: (b, j, 0))
    full = lambda a: pl.BlockSpec(a.shape, lambda b, j: (0,) * a.ndim)
    bsd = lambda w: jax.ShapeDtypeStruct((B, S, w), _BF)
    out_shape = [bsd(FOX_W), bsd(FOX_W), bsd(LANES), jax.ShapeDtypeStruct((B, FOX_HEADS, nj, 2 * HEAD_DIM, tm), _BF),
                 bsd(SWA_QW), bsd(SWA_KW),
                 jax.ShapeDtypeStruct((B, SWA_KV_HEADS, S // WINDOW, 2 * HEAD_DIM, WINDOW), _BF),
                 jax.ShapeDtypeStruct((B, FOX_HEADS, 1), _F32)]
    out_specs = [row(FOX_W), row(FOX_W), row(LANES),
                 pl.BlockSpec((None, FOX_HEADS, None, 2 * HEAD_DIM, tm), lambda b, j: (b, 0, j, 0, 0)),
                 row(SWA_QW), row(SWA_KW),
                 pl.BlockSpec((None, SWA_KV_HEADS, tm // WINDOW, 2 * HEAD_DIM, WINDOW), lambda b, j: (b, 0, j, 0, 0)),
                 pl.BlockSpec((None, FOX_HEADS, 1), lambda b, j: (b, 0, 0))]
    return pl.pallas_call(
        functools.partial(_qkv_kernel, n_pad=n_pad),
        grid=(B, nj),
        in_specs=[row(D)] + [full(a) for a in (g1, wqk, wvt, wfl, ws, wsvt, bfg, fqg, fkg, sqg, skg, f0)],
        out_specs=out_specs, out_shape=out_shape,
        scratch_shapes=[pltpu.VMEM((FOX_HEADS, 1), _F32)],
        compiler_params=_cparams(("arbitrary", "arbitrary")),
        name="qkv_proj",
    )(x3, g1, wqk, wvt, wfl, ws, wsvt, bfg, fqg, fkg, sqg, skg, f0)


def _fox_kernel(q_ref, k_ref, aug_ref, vt_ref, km_ref, augm_ref, vtm_ref, o_ref, m_ref, acc_ref, s_ref, mx_ref):
    g = pl.program_id(1)
    qi = pl.program_id(2)
    tq = q_ref.shape[0]
    tk = vt_ref.shape[-1]
    nt = (((1,), (1,)), ((), ()))
    lane = lax.broadcasted_iota(jnp.int32, (tq, LANES), 1)

    qaugs = []
    for pp in range(FOX_PAIRS):
        q = q_ref[:, pp * LANES:(pp + 1) * LANES]
        parts = []
        for a in range(2):
            head = 2 * (FOX_PAIRS * g + pp) + a
            keep = (lane < HEAD_DIM) if a == 0 else (lane >= HEAD_DIM)
            ones = jnp.where((lane < 3 * FOX_HEADS) & ((lane & (FOX_HEADS - 1)) == head), 1.0, 0.0).astype(_BF)
            parts.append(jnp.concatenate([jnp.where(keep, q, jnp.zeros_like(q)), ones], axis=1))
        qaugs.append(jnp.concatenate(parts, axis=0))

    def scores(pp, k, aug, mask):
        st = lax.dot_general(jnp.concatenate([k, aug], axis=1), qaugs[pp], nt, preferred_element_type=_F32)
        if mask is not None:
            st = jnp.where(mask, st, NEG_INF)
        return st, jnp.max(st, axis=0, keepdims=True)

    def tile_scores(pp, ki, mask=None):
        off = pl.multiple_of(ki * tk, tk)
        return scores(pp, k_ref[pl.ds(off, tk), pp * LANES:(pp + 1) * LANES], aug_ref[pl.ds(off, tk), :], mask)

    def process(pp, st, mx, vts, first):
        m_new = mx if first else jnp.maximum(m_ref[pp], mx)
        pt = jnp.exp2(st - m_new).astype(_BF)
        if not first:
            alpha = jnp.exp2(m_ref[pp] - m_new)
        for a in range(2):
            pv = jnp.dot(vts[a], pt[:, a * tq:(a + 1) * tq], preferred_element_type=_F32)
            acc_ref[2 * pp + a] = pv if first else alpha[:, a * tq:(a + 1) * tq] * acc_ref[2 * pp + a] + pv
        m_ref[pp] = m_new

    def stash(buf, pp, st_mx):
        s_ref[buf, pp] = st_mx[0]
        mx_ref[buf, pp] = st_mx[1]

    def vts(pp, ki):
        return vt_ref[2 * pp, ki], vt_ref[2 * pp + 1, ki]

    causal = lax.broadcasted_iota(jnp.int32, (tk, 2 * tq), 0) <= (lax.broadcasted_iota(jnp.int32, (tk, 2 * tq), 1) & (tq - 1))
    mask_m = lax.broadcasted_iota(jnp.int32, (META_BLOCK, 2 * tq), 0) >= META_PAD
    for pp in range(FOX_PAIRS):
        stash(0, pp, tile_scores(pp, qi, causal))
        st, mx = scores(pp, km_ref[:, pp * LANES:(pp + 1) * LANES], augm_ref[...], mask_m)
        process(pp, st, mx, (vtm_ref[2 * pp], vtm_ref[2 * pp + 1]), True)

    def step(j, buf):
        nxt = [tile_scores(pp, j) for pp in range(FOX_PAIRS)]
        cur = jnp.where(j == 0, qi, j - 1)
        for pp in range(FOX_PAIRS):
            process(pp, s_ref[buf, pp], mx_ref[buf, pp], vts(pp, cur), False)
        for pp in range(FOX_PAIRS):
            stash(1 - buf, pp, nxt[pp])

    def body(jj, c):
        step(2 * jj, 0)
        step(2 * jj + 1, 1)
        return c
    lax.fori_loop(0, qi // 2, body, 0)

    def finish(buf):
        last = jnp.where(qi == 0, qi, qi - 1)
        outs = []
        for pp in range(FOX_PAIRS):
            process(pp, s_ref[buf, pp], mx_ref[buf, pp], vts(pp, last), False)
            for a in range(2):
                acc = acc_ref[2 * pp + a]
                outs.append(acc[:HEAD_DIM] / acc[HEAD_DIM:HEAD_DIM + 1])
        o_ref[...] = jnp.concatenate(outs, axis=0).T.astype(_BF)

    @pl.when(qi % 2 == 1)
    def _():
        step(qi - 1, 0)
        finish(1)

    @pl.when(qi % 2 == 0)
    def _():
        finish(0)


def _fox_call(fq, fk, aug, fvt, km, augm, vtm, *, tq):
    B, S, _ = fq.shape
    nk = fvt.shape[2]
    assert fvt.shape[-1] == tq
    w = FOX_PAIRS * LANES
    qspec = pl.BlockSpec((None, tq, w), lambda b, g, i: (b, i, g))
    return pl.pallas_call(
        _fox_kernel, grid=(B, FOX_HEADS // (2 * FOX_PAIRS), S // tq),
        in_specs=[qspec,
                  pl.BlockSpec((None, S, w), lambda b, g, i: (b, 0, g)),
                  pl.BlockSpec((None, S, LANES), lambda b, g, i: (b, 0, 0)),
                  pl.BlockSpec((None, 2 * FOX_PAIRS, nk, 2 * HEAD_DIM, tq), lambda b, g, i: (b, g, 0, 0, 0)),
                  pl.BlockSpec((META_BLOCK, w), lambda b, g, i: (0, g)),
                  pl.BlockSpec((META_BLOCK, LANES), lambda b, g, i: (0, 0)),
                  pl.BlockSpec((2 * FOX_PAIRS, 2 * HEAD_DIM, META_BLOCK), lambda b, g, i: (g, 0, 0))],
        out_specs=qspec, out_shape=jax.ShapeDtypeStruct((B, S, FOX_W), _BF),
        scratch_shapes=[pltpu.VMEM((FOX_PAIRS, 1, 2 * tq), _F32), pltpu.VMEM((2 * FOX_PAIRS, 2 * HEAD_DIM, tq), _F32),
                        pltpu.VMEM((2, FOX_PAIRS, tq, 2 * tq), _F32), pltpu.VMEM((2, FOX_PAIRS, 1, 2 * tq), _F32)],
        compiler_params=_cparams(("arbitrary", "arbitrary", "arbitrary")), name="fox_attn",
    )(fq, fk, aug, fvt, km, augm, vtm)


def _swa_bias(tq):
    j = np.arange(WINDOW + tq)[:, None] - WINDOW
    i = np.arange(tq)[None, :]
    dist = i - j
    ok = (dist >= 0) & (dist < WINDOW)
    slopes = np.exp2(-8.0 * (np.arange(SWA_HEADS, dtype=np.float32) + 1.0) / SWA_HEADS).astype(np.float32)
    per_head = np.where(ok[None], -slopes[:, None, None] * dist[None].astype(np.float32), np.float32(NEG_INF))
    rep = SWA_HEADS // SWA_KV_HEADS
    return np.stack([np.concatenate(list(per_head[g * rep:(g + 1) * rep]), axis=1) for g in range(SWA_KV_HEADS)])


def _swa_kernel(q_ref, kc_ref, kp_ref, km_ref, vc_ref, vp_ref, vm_ref, bias_ref, sink_ref, o_ref):
    qi = pl.program_id(1)
    tq = q_ref.shape[0]
    rep = SWA_HEADS // SWA_KV_HEADS
    nt = (((1,), (1,)), ((), ()))
    first = qi == 0
    kcat = jnp.concatenate([jnp.where(first, km_ref[...], kp_ref[...]), kc_ref[...]], axis=0)
    row = lax.broadcasted_iota(jnp.int32, (WINDOW + tq, rep * tq), 0)
    row_ok = row >= jnp.where(first, META_PAD, 0)
    lane = lax.broadcasted_iota(jnp.int32, (tq, LANES), 1)
    outs = [None] * SWA_HEADS
    for g in range(SWA_KV_HEADS):
        keep = (lane < HEAD_DIM) if g == 0 else (lane >= HEAD_DIM)
        qg = jnp.concatenate([jnp.where(keep, q_ref[:, p * LANES:(p + 1) * LANES], jnp.zeros((tq, LANES), _BF))
                              for p in range(rep)], axis=0)
        vcat = jnp.concatenate([jnp.where(first, vm_ref[g], vp_ref[g, 0])] + [vc_ref[g, c] for c in range(tq // WINDOW)],
                               axis=1)
        st = lax.dot_general(kcat, qg, nt, preferred_element_type=_F32) + bias_ref[g]
        st = jnp.where(row_ok, st, NEG_INF)
        sink = sink_ref[g]
        m = jnp.maximum(jnp.max(st, axis=0, keepdims=True), sink)
        pt = jnp.exp(st - m).astype(_BF)
        acc = jnp.dot(vcat, pt, preferred_element_type=_F32)
        o = acc[:HEAD_DIM] / (acc[HEAD_DIM:HEAD_DIM + 1] + jnp.exp(sink - m))
        for p in range(rep):
            outs[2 * p + g] = o[:, p * tq:(p + 1) * tq]
    o_ref[...] = jnp.concatenate(outs, axis=0).T.astype(_BF)


def _swa_call(sinks, sq, sk, svt, km, vtm, *, tq):
    B, S, _ = sq.shape
    r = tq // WINDOW
    rep = SWA_HEADS // SWA_KV_HEADS
    bias = jnp.asarray(_swa_bias(tq))
    sink_rows = jnp.repeat(sinks.reshape(SWA_KV_HEADS, rep), tq, axis=1)[:, None, :]
    prev_blk = lambda i: jnp.maximum(i * r - 1, 0)
    return pl.pallas_call(
        _swa_kernel, grid=(B, S // tq),
        in_specs=[pl.BlockSpec((None, tq, SWA_QW), lambda b, i: (b, i, 0)),
                  pl.BlockSpec((None, tq, LANES), lambda b, i: (b, i, 0)),
                  pl.BlockSpec((None, WINDOW, LANES), lambda b, i: (b, prev_blk(i), 0)),
                  pl.BlockSpec((META_BLOCK, LANES), lambda b, i: (0, 0)),
                  pl.BlockSpec((None, SWA_KV_HEADS, r, 2 * HEAD_DIM, WINDOW), lambda b, i: (b, 0, i, 0, 0)),
                  pl.BlockSpec((None, SWA_KV_HEADS, 1, 2 * HEAD_DIM, WINDOW), lambda b, i: (b, 0, prev_blk(i), 0, 0)),
                  pl.BlockSpec((SWA_KV_HEADS, 2 * HEAD_DIM, META_BLOCK), lambda b, i: (0, 0, 0)),
                  pl.BlockSpec(bias.shape, lambda b, i: (0, 0, 0)),
                  pl.BlockSpec(sink_rows.shape, lambda b, i: (0, 0, 0))],
        out_specs=pl.BlockSpec((None, tq, SWA_QW), lambda b, i: (b, i, 0)),
        out_shape=jax.ShapeDtypeStruct((B, S, SWA_QW), _BF),
        compiler_params=_cparams(("arbitrary", "arbitrary")), name="swa_attn",
    )(sq, sk, sk, km, svt, svt, vtm, bias, sink_rows)


def _mix_kernel(x_ref, oa_ref, ob_ref, g1_ref, wga_ref, wgb_ref, wfo_ref, wso_ref, wo_ref, g2_ref, wr_ref, br_ref,
                hres_ref, h2_ref, lg_ref):
    x = x_ref[...]
    h = _rms(x, g1_ref[...]).astype(_BF)
    ga = jax.nn.sigmoid(jnp.dot(h, wga_ref[...], preferred_element_type=_F32))
    gb = jax.nn.sigmoid(jnp.dot(h, wgb_ref[...], preferred_element_type=_F32))
    mix = ga * jnp.dot(oa_ref[...], wfo_ref[...], preferred_element_type=_F32) \
        + gb * jnp.dot(ob_ref[...], wso_ref[...], preferred_element_type=_F32)
    hres = x + jnp.dot(mix.astype(_BF), wo_ref[...], preferred_element_type=_F32)
    hres_ref[...] = hres
    h2 = _rms(hres, g2_ref[...])
    h2_ref[...] = h2.astype(_BF)
    wr = wr_ref[...]
    w_hi = wr.astype(_BF)
    w_lo = (wr - w_hi.astype(_F32)).astype(_BF)
    h_hi = h2.astype(_BF)
    h_lo = (h2 - h_hi.astype(_F32)).astype(_BF)
    lg = jnp.dot(h_hi, w_hi, preferred_element_type=_F32) + jnp.dot(h_hi, w_lo, preferred_element_type=_F32) \
        + jnp.dot(h_lo, w_hi, preferred_element_type=_F32)
    lg_ref[...] = lg + br_ref[...]


def _mix_call(x2, oa, ob, g1, wga, wgb, wfo, wso, wo, g2, wr, br, *, tm):
    N, D = x2.shape
    row = lambda w: pl.BlockSpec((tm, w), lambda i: (i, 0))
    full = lambda a: pl.BlockSpec(a.shape, lambda i: (0,) * a.ndim)
    return pl.pallas_call(
        _mix_kernel, grid=(N // tm,),
        in_specs=[row(D), row(FOX_W), row(SWA_QW)] + [full(a) for a in (g1, wga, wgb, wfo, wso, wo, g2, wr, br)],
        out_specs=[row(D), row(D), row(LANES)],
        out_shape=[jax.ShapeDtypeStruct((N, D), _F32), jax.ShapeDtypeStruct((N, D), _BF),
                   jax.ShapeDtypeStruct((N, LANES), _F32)],
        compiler_params=_cparams(("arbitrary",)), name="mix_proj",
    )(x2, oa, ob, g1, wga, wgb, wfo, wso, wo, g2, wr, br)


def _route_kernel(lg_ref, a_ref, at_ref, cnt_ref, lb_ref):
    T = lg_ref.shape[0]
    lt = lg_ref[...].T[:N_EXPERTS]
    e_iota = lax.broadcasted_iota(jnp.int32, lt.shape, 0)
    cur = lt
    vals, idxs = [], []
    for _ in range(TOP_K):
        m = jnp.max(cur, axis=0, keepdims=True)
        idx = jnp.min(jnp.where(cur == m, e_iota, N_EXPERTS), axis=0, keepdims=True)
        vals.append(m)
        idxs.append(idx)
        cur = jnp.where(e_iota == idx, -jnp.inf, cur)
    es = [jnp.exp(v - vals[0]) for v in vals]
    den = es[0] + es[1] + es[2] + es[3]
    gates = [e / den for e in es]

    onehot = jnp.zeros(lt.shape, _F32)
    for idx in idxs:
        onehot = onehot + jnp.where(e_iota == idx, 1.0, 0.0)
    r_i = lax.broadcasted_iota(jnp.int32, (T, T), 0)
    c_i = lax.broadcasted_iota(jnp.int32, (T, T), 1)
    utri = jnp.where(r_i < c_i, 1.0, 0.0).astype(_BF)
    prefix = jnp.dot(onehot.astype(_BF), utri, preferred_element_type=_F32)
    cnt = jnp.sum(onehot, axis=1, keepdims=True)
    chunks = jnp.floor((cnt + (RUN_ALIGN - 1)) * (1.0 / RUN_ALIGN))
    l_r = lax.broadcasted_iota(jnp.int32, (N_EXPERTS, N_EXPERTS), 0)
    l_c = lax.broadcasted_iota(jnp.int32, (N_EXPERTS, N_EXPERTS), 1)
    ltri = jnp.where(l_c < l_r, 1.0, 0.0).astype(_BF)
    lbase = RUN_ALIGN * jnp.dot(ltri, jnp.broadcast_to(chunks, (N_EXPERTS, LANES)).astype(_BF),
                                preferred_element_type=_F32)[:, 0:1]
    slot_all = prefix + lbase
    rows = [jnp.sum(jnp.where(e_iota == idx, slot_all, 0.0), axis=0, keepdims=True) for idx in idxs] + gates
    at = jnp.concatenate(rows, axis=0)
    at_ref[...] = at
    a_ref[...] = jnp.concatenate([at, jnp.zeros((LANES - 2 * TOP_K, T), _F32)], axis=0).T
    cnt_ref[...] = jnp.broadcast_to(cnt, (N_EXPERTS, LANES)).astype(jnp.int32)
    lb_ref[...] = jnp.broadcast_to(lbase, (N_EXPERTS, LANES)).astype(jnp.int32)


def _route_call(lg, *, T):
    N = lg.shape[0]
    nt = N // T
    per_tile = pl.BlockSpec((None, N_EXPERTS, LANES), lambda i: (i, 0, 0))
    return pl.pallas_call(
        _route_kernel, grid=(nt,),
        in_specs=[pl.BlockSpec((T, LANES), lambda i: (i, 0))],
        out_specs=[pl.BlockSpec((T, LANES), lambda i: (i, 0)), pl.BlockSpec((8, T), lambda i: (0, i)), per_tile, per_tile],
        out_shape=[jax.ShapeDtypeStruct((N, LANES), _F32), jax.ShapeDtypeStruct((8, N), _F32),
                   jax.ShapeDtypeStruct((nt, N_EXPERTS, LANES), jnp.int32),
                   jax.ShapeDtypeStruct((nt, N_EXPERTS, LANES), jnp.int32)],
        compiler_params=_cparams(("arbitrary",)), name="route",
    )(lg)


def _div_pow2(v, d):
    assert d & (d - 1) == 0
    return lax.shift_right_logical(v, d.bit_length() - 1)


def _plan_kernel(cnt_ref, goff_ref, nch_ref, eblk_ref, gap_ref, info_ref, *, nt, block_rows):
    def per_expert(e, base):
        def per_tile(i, run):
            n = _div_pow2(cnt_ref[i * N_EXPERTS + e] + (RUN_ALIGN - 1), RUN_ALIGN)
            goff_ref[i * N_EXPERTS + e] = run
            nch_ref[i * N_EXPERTS + e] = n
            return run + n * RUN_ALIGN
        end = lax.fori_loop(0, nt, per_tile, base)
        nblk = _div_pow2(end - base + (block_rows - 1), block_rows)
        nxt = base + nblk * block_rows
        gap_ref[2 * e] = end
        gap_ref[2 * e + 1] = _div_pow2(nxt - end, RUN_ALIGN)
        eblk_ref[2 * e] = _div_pow2(base, block_rows)
        eblk_ref[2 * e + 1] = nblk
        return nxt
    total = lax.fori_loop(0, N_EXPERTS, per_expert, 0)
    info_ref[0] = _div_pow2(total, block_rows)


def _plan_call(cnt_flat, *, nt, block_rows):
    smem = pl.BlockSpec(memory_space=pltpu.SMEM)
    i32 = lambda n: jax.ShapeDtypeStruct((n,), jnp.int32)
    return pl.pallas_call(
        functools.partial(_plan_kernel, nt=nt, block_rows=block_rows),
        in_specs=[smem], out_specs=[smem] * 5,
        out_shape=[i32(nt * N_EXPERTS), i32(nt * N_EXPERTS), i32(2 * N_EXPERTS), i32(2 * N_EXPERTS), i32(1)],
        name="plan",
    )(cnt_flat)


def _pack_pairs(v):
    w = v.shape[1] // 2
    return pltpu.pack_elementwise([v[:, :w], v[:, w:]], packed_dtype=_BF)


def _unpack_pairs(wd):
    lo = pltpu.unpack_elementwise(wd, index=0, packed_dtype=_BF, unpacked_dtype=_F32)
    hi = pltpu.unpack_elementwise(wd, index=1, packed_dtype=_BF, unpacked_dtype=_F32)
    return lo.astype(_BF), hi.astype(_BF)


def _dispatch_kernel(goff, lbase, nch, gap, info, at_ref, h2_ref, xs_ref, xl_ref, z_ref, sem, zsem, tsem):
    i = pl.program_id(0)
    R = xl_ref.shape[1]
    T = h2_ref.shape[0]
    block_rows = z_ref.shape[0]
    nb = xs_ref.shape[0] // block_rows

    @pl.when(i == 0)
    def _():
        z_ref[...] = jnp.zeros(z_ref.shape, z_ref.dtype)

        def gap_copy(e):
            rows = pl.multiple_of(gap[2 * e + 1] * RUN_ALIGN, RUN_ALIGN)
            return pltpu.make_async_copy(z_ref.at[pl.ds(0, rows)],
                                         xs_ref.at[pl.ds(pl.multiple_of(gap[2 * e], RUN_ALIGN), rows)], zsem)

        def fill(e, c):
            @pl.when(gap[2 * e + 1] > 0)
            def _():
                gap_copy(e).start()
            return c
        lax.fori_loop(0, N_EXPERTS, fill, 0)

        def fill_wait(e, c):
            @pl.when(gap[2 * e + 1] > 0)
            def _():
                gap_copy(e).wait()
            return c
        lax.fori_loop(0, N_EXPERTS, fill_wait, 0)

        def tail(b, c):
            pltpu.make_async_copy(z_ref, xs_ref.at[pl.ds(pl.multiple_of(b * block_rows, block_rows), block_rows)],
                                  tsem).start()
            return c
        lax.fori_loop(info[0], nb, tail, 0)

        def tail_wait(b, c):
            pltpu.make_async_copy(z_ref, xs_ref.at[pl.ds(0, block_rows)], tsem).wait()
            return c
        lax.fori_loop(info[0], nb, tail_wait, 0)

    slot_iota = lax.broadcasted_iota(jnp.int32, (R, T), 0).astype(_F32)
    pm = jnp.zeros((R, T), _F32)
    for k in range(TOP_K):
        pm = jnp.where(slot_iota == at_ref[k:k + 1, :], 1.0, pm)
    xl = jnp.dot(pm.astype(_BF), h2_ref[...], preferred_element_type=_F32)
    slot = i & 1

    def wait_tile(t, s):
        total = lax.fori_loop(0, N_EXPERTS, lambda e, c: c + nch[t * N_EXPERTS + e], 0)

        @pl.when(total > 0)
        def _():
            rows = pl.multiple_of(total * RUN_ALIGN, RUN_ALIGN)
            pltpu.make_async_copy(xl_ref.at[s, pl.ds(0, rows)], xs_ref.at[pl.ds(0, rows)], sem.at[s]).wait()

    @pl.when(i >= 2)
    def _():
        wait_tile(i - 2, slot)
    xl_ref[slot] = _pack_pairs(xl)

    def issue(e, queue):
        n = nch[i * N_EXPERTS + e]

        @pl.when(n > 0)
        def _():
            rows = pl.multiple_of(n * RUN_ALIGN, RUN_ALIGN)
            pltpu.make_async_copy(xl_ref.at[slot, pl.ds(pl.multiple_of(lbase[i * N_EXPERTS + e], RUN_ALIGN), rows)],
                                  xs_ref.at[pl.ds(pl.multiple_of(goff[i * N_EXPERTS + e], RUN_ALIGN), rows)],
                                  sem.at[slot]).start(priority=queue)

    def issue_pair(e2, c):
        issue(2 * e2, 0)
        issue(2 * e2 + 1, 1)
        return c
    lax.fori_loop(0, N_EXPERTS // 2, issue_pair, 0)

    @pl.when(i == pl.num_programs(0) - 1)
    def _():
        @pl.when(i >= 1)
        def _():
            wait_tile(i - 1, 1 - slot)
        wait_tile(i, slot)


def _dispatch_call(goff, lbase, nch, gap, info, at, h2, *, T, R, rows, block_rows):
    N, D = h2.shape
    grid_spec = pltpu.PrefetchScalarGridSpec(
        num_scalar_prefetch=5, grid=(N // T,),
        in_specs=[pl.BlockSpec((8, T), lambda i, *_: (0, i)), pl.BlockSpec((T, D), lambda i, *_: (i, 0))],
        out_specs=pl.BlockSpec(memory_space=pl.ANY),
        scratch_shapes=[pltpu.VMEM((2, R, D // 2), jnp.uint32), pltpu.VMEM((block_rows, D // 2), jnp.uint32),
                        pltpu.SemaphoreType.DMA((2,)), pltpu.SemaphoreType.DMA, pltpu.SemaphoreType.DMA])
    return pl.pallas_call(
        _dispatch_kernel, grid_spec=grid_spec, out_shape=jax.ShapeDtypeStruct((rows, D // 2), jnp.uint32),
        compiler_params=_cparams(("arbitrary",)), name="dispatch",
    )(goff, lbase, nch, gap, info, at, h2)


def _expert_kernel(eblk, info, xs_ref, wu_ref, bu_ref, wd_ref, bd_ref, ys_ref, wub_ref, wdb_ref, xb_ref, yb_ref, xsem, ysem):
    e = pl.program_id(0)
    block_rows = xb_ref.shape[1]
    nb = ys_ref.shape[0] // block_rows
    b0 = eblk[2 * e]
    n = eblk[2 * e + 1]
    half = wu_ref.shape[0] // 2
    wub_ref[...] = wu_ref[...].astype(_BF)
    wdb_ref[...] = wd_ref[...].astype(_BF)

    def rows_of(blk):
        return pl.ds(pl.multiple_of(blk * block_rows, block_rows), block_rows)

    def x_copy(j, slot):
        return pltpu.make_async_copy(xs_ref.at[rows_of(b0 + j)], xb_ref.at[slot], xsem.at[slot])

    def y_copy(j, slot):
        return pltpu.make_async_copy(yb_ref.at[slot], ys_ref.at[rows_of(b0 + j)], ysem.at[slot])

    @pl.when((e == 0) & (n > 0))
    def _():
        x_copy(0, 0).start()

    def body(j, c):
        slot = j & 1
        x_copy(j, slot).wait()

        @pl.when(j + 1 < n)
        def _():
            x_copy(j + 1, 1 - slot).start()

        @pl.when(j >= 2)
        def _():
            y_copy(j - 2, slot).wait()

        x_lo, x_hi = _unpack_pairs(xb_ref[slot])
        gu = jnp.dot(x_lo, wub_ref[:half, :], preferred_element_type=_F32) \
            + jnp.dot(x_hi, wub_ref[half:, :], preferred_element_type=_F32) + bu_ref[...]
        g = jnp.minimum(gu[:, :D_FF], SWIGLU_LIMIT)
        u = jnp.clip(gu[:, D_FF:], -SWIGLU_LIMIT, SWIGLU_LIMIT)
        act = (u + 1.0) * (g * jax.nn.sigmoid(SWIGLU_ALPHA * g))
        y = jnp.dot(act.astype(_BF), wdb_ref[...], preferred_element_type=_F32) + bd_ref[...]
        yb_ref[slot] = _pack_pairs(y)
        y_copy(j, slot).start()
        return c
    lax.fori_loop(0, n, body, 0)

    @pl.when(n >= 2)
    def _():
        y_copy(n - 2, n & 1).wait()

    @pl.when(n >= 1)
    def _():
        y_copy(n - 1, (n - 1) & 1).wait()

    last = pl.num_programs(0) - 1
    e_next = jnp.minimum(e + 1, last)

    @pl.when((e < last) & (eblk[2 * e_next + 1] > 0))
    def _():
        pltpu.make_async_copy(xs_ref.at[rows_of(eblk[2 * e_next])], xb_ref.at[0], xsem.at[0]).start()

    @pl.when(e == last)
    def _():
        yb_ref[0] = _pack_pairs(jnp.zeros((block_rows, 2 * yb_ref.shape[2]), _F32))

        def tail_copy(b):
            return pltpu.make_async_copy(yb_ref.at[0], ys_ref.at[pl.ds(pl.multiple_of(b * block_rows, block_rows), block_rows)],
                                         ysem.at[0])

        def tail(b, c):
            tail_copy(b).start()
            return c
        lax.fori_loop(info[0], nb, tail, 0)

        def tail_wait(b, c):
            tail_copy(b).wait()
            return c
        lax.fori_loop(info[0], nb, tail_wait, 0)


def _expert_call(eblk, info, xs, wu, bu, wd, bd, *, block_rows):
    rows, half = xs.shape
    D = 2 * half
    ex = lambda e, *_: (e, 0, 0)
    grid_spec = pltpu.PrefetchScalarGridSpec(
        num_scalar_prefetch=2, grid=(N_EXPERTS,),
        in_specs=[pl.BlockSpec(memory_space=pl.ANY),
                  pl.BlockSpec((None, D, 2 * D_FF), ex), pl.BlockSpec((None, 1, 2 * D_FF), ex),
                  pl.BlockSpec((None, D_FF, D), ex), pl.BlockSpec((None, 1, D), ex)],
        out_specs=pl.BlockSpec(memory_space=pl.ANY),
        scratch_shapes=[pltpu.VMEM((D, 2 * D_FF), _BF), pltpu.VMEM((D_FF, D), _BF),
                        pltpu.VMEM((2, block_rows, half), jnp.uint32), pltpu.VMEM((2, block_rows, half), jnp.uint32),
                        pltpu.SemaphoreType.DMA((2,)), pltpu.SemaphoreType.DMA((2,))])
    return pl.pallas_call(
        _expert_kernel, grid_spec=grid_spec, out_shape=jax.ShapeDtypeStruct((rows, half), jnp.uint32),
        compiler_params=_cparams(("arbitrary",)), name="experts",
    )(eblk, info, xs, wu, bu, wd, bd)


def _combine_kernel(goff, lbase, nch, a_ref, hres_ref, ys_ref, o_ref, yl_ref, sem):
    i = pl.program_id(0)
    nt = pl.num_programs(0)
    R = yl_ref.shape[1]
    T = a_ref.shape[0]

    def fetch(t, slot):
        def issue(e, queue):
            n = nch[t * N_EXPERTS + e]

            @pl.when(n > 0)
            def _():
                rows = pl.multiple_of(n * RUN_ALIGN, RUN_ALIGN)
                pltpu.make_async_copy(
                    ys_ref.at[pl.ds(pl.multiple_of(goff[t * N_EXPERTS + e], RUN_ALIGN), rows)],
                    yl_ref.at[slot, pl.ds(pl.multiple_of(lbase[t * N_EXPERTS + e], RUN_ALIGN), rows)],
                    sem.at[slot]).start(priority=queue)

        def issue_pair(e2, c):
            issue(2 * e2, 0)
            issue(2 * e2 + 1, 1)
            return c
        lax.fori_loop(0, N_EXPERTS // 2, issue_pair, 0)

    @pl.when(i == 0)
    def _():
        yl_ref[...] = jnp.zeros(yl_ref.shape, yl_ref.dtype)
        fetch(0, 0)

    slot = i & 1

    @pl.when(i + 1 < nt)
    def _():
        fetch(i + 1, 1 - slot)

    total = lax.fori_loop(0, N_EXPERTS, lambda e, c: c + nch[i * N_EXPERTS + e], 0)

    @pl.when(total > 0)
    def _():
        rows = pl.multiple_of(total * RUN_ALIGN, RUN_ALIGN)
        pltpu.make_async_copy(ys_ref.at[pl.ds(0, rows)], yl_ref.at[slot, pl.ds(0, rows)], sem.at[slot]).wait()

    a = a_ref[...]
    slot_iota = lax.broadcasted_iota(jnp.int32, (T, R), 1).astype(_F32)
    pt = jnp.zeros((T, R), _F32)
    for k in range(TOP_K):
        pt = jnp.where(slot_iota == a[:, k:k + 1], a[:, TOP_K + k:TOP_K + k + 1], pt)
    pt = pt.astype(_BF)
    y_lo, y_hi = _unpack_pairs(yl_ref[slot])
    half = y_lo.shape[1]
    o_ref[:, :half] = hres_ref[:, :half] + jnp.dot(pt, y_lo, preferred_element_type=_F32)
    o_ref[:, half:] = hres_ref[:, half:] + jnp.dot(pt, y_hi, preferred_element_type=_F32)


def _combine_call(goff, lbase, nch, a, hres, ys, *, T, R):
    N, D = hres.shape
    grid_spec = pltpu.PrefetchScalarGridSpec(
        num_scalar_prefetch=3, grid=(N // T,),
        in_specs=[pl.BlockSpec((T, LANES), lambda i, *_: (i, 0)), pl.BlockSpec((T, D), lambda i, *_: (i, 0)),
                  pl.BlockSpec(memory_space=pl.ANY)],
        out_specs=pl.BlockSpec((T, D), lambda i, *_: (i, 0)),
        scratch_shapes=[pltpu.VMEM((2, R, D // 2), jnp.uint32), pltpu.SemaphoreType.DMA((2,))])
    return pl.pallas_call(
        _combine_kernel, grid_spec=grid_spec, out_shape=jax.ShapeDtypeStruct((N, D), _F32),
        compiler_params=_cparams(("arbitrary",)), name="combine",
    )(goff, lbase, nch, a, hres, ys)


def _tiles(S):
    pick = lambda pref: next(t for t in pref if S % t == 0)
    tm = pick((512, 256, 128))
    return dict(tm=tm, tq_fox=tm, tq_swa=pick((256, 128)),
                tm_mix=pick((512, 256, 128)), t_route=pick((256, 128)), block_rows=256)


def kernel(x, meta_tokens, norm1_gain, w_in, b_forget, fox_q_gain, fox_k_gain, swa_q_gain, swa_k_gain, swa_sinks,
           w_fox_out, w_swa_out, w_out, norm2_gain, w_router, b_router, w_up, b_up, w_down, b_down):
    B, S, D = x.shape
    assert norm1_gain.shape[0] == 1 and S % LANES == 0
    tl = _tiles(S)
    N = B * S
    scale = HEAD_DIM ** -0.5

    w = w_in[0]
    c0 = 3 * FOX_W
    c1 = c0 + FOX_HEADS
    c2 = c1 + SWA_QW + 2 * SWA_KW
    wqk = w[:, :2 * FOX_W].astype(_BF)
    wvt = w[:, 2 * FOX_W:c0].T.astype(_BF)
    wfl = w[:, c0:c1].T.astype(_BF)
    perm_heads = np.array([h for p in range(SWA_HEADS // 2) for h in (p, p + SWA_HEADS // 2)])
    perm_cols = (perm_heads[:, None] * HEAD_DIM + np.arange(HEAD_DIM)[None, :]).reshape(-1)
    ws_all = w[:, c1:c2]
    ws = jnp.concatenate([ws_all[:, :SWA_QW][:, perm_cols], ws_all[:, SWA_QW:SWA_QW + SWA_KW]], axis=1).astype(_BF)
    wsvt = ws_all[:, SWA_QW + SWA_KW:].T.astype(_BF)
    wga = w[:, c2:c2 + D].astype(_BF)
    wgb = w[:, c2 + D:].astype(_BF)
    g1 = norm1_gain[0][None, :]
    g2 = norm2_gain[0][None, :]
    bfg = b_forget[0][:, None]
    fqg = jnp.tile(fox_q_gain[0], FOX_HEADS)[None, :] * (scale * LOG2E)
    fkg = jnp.tile(fox_k_gain[0], FOX_HEADS)[None, :]
    sqg = jnp.tile(swa_q_gain[0], SWA_HEADS)[None, :] * scale
    skg = jnp.tile(swa_k_gain[0], SWA_KV_HEADS)[None, :]
    wfo = w_fox_out[0].astype(_BF)
    wso = w_swa_out[0][perm_cols, :].astype(_BF)
    wo = w_out[0].astype(_BF)
    wr = jnp.pad(w_router[0], ((0, 0), (0, LANES - N_EXPERTS)))
    br = jnp.pad(b_router[0], (0, LANES - N_EXPERTS), constant_values=NEG_INF)[None, :]
    sinks = swa_sinks[0].astype(_F32)
    wu = w_up[0]
    wd = w_down[0]
    bu = b_up[0][:, None, :]
    bd = b_down[0][:, None, :]

    meta_blk = jnp.pad(meta_tokens.astype(x.dtype), ((META_PAD, 0), (0, 0)))[None]
    zero_f = jnp.zeros((FOX_HEADS, 1), _F32)
    _, fk_m, aug_m, fvt_m, _, sk_m, svt_m, fend_m = _qkv_call(meta_blk, g1, wqk, wvt, wfl, ws, wsvt, bfg, fqg, fkg, sqg,
                                                              skg, zero_f, tm=META_BLOCK, n_pad=META_PAD)
    fq, fk, aug, fvt, sq, sk, svt, _ = _qkv_call(x, g1, wqk, wvt, wfl, ws, wsvt, bfg, fqg, fkg, sqg, skg, fend_m[0],
                                                 tm=tl["tm"], n_pad=0)
    o_a = _fox_call(fq, fk, aug, fvt, fk_m[0], aug_m[0], fvt_m[0, :, 0], tq=tl["tq_fox"])
    o_b = _swa_call(sinks, sq, sk, svt, sk_m[0], svt_m[0, :, 0], tq=tl["tq_swa"])

    hres, h2, lg = _mix_call(x.reshape(N, D), o_a.reshape(N, FOX_W), o_b.reshape(N, SWA_QW),
                             g1, wga, wgb, wfo, wso, wo, g2, wr, br, tm=tl["tm_mix"])

    T = tl["t_route"]
    nt = N // T
    block_rows = tl["block_rows"]
    R = -(-(TOP_K * T + N_EXPERTS * (RUN_ALIGN - 1)) // LANES) * LANES
    max_rows = N * TOP_K + nt * N_EXPERTS * (RUN_ALIGN - 1) + N_EXPERTS * (block_rows - RUN_ALIGN)
    nb = -(-max_rows // block_rows)
    a, at, cnt, lb = _route_call(lg, T=T)
    lbase = lb[:, :, 0].reshape(-1)
    goff, nch, eblk, gap, info = _plan_call(cnt[:, :, 0].reshape(-1), nt=nt, block_rows=block_rows)
    xs = _dispatch_call(goff, lbase, nch, gap, info, at, h2, T=T, R=R, rows=nb * block_rows, block_rows=block_rows)
    ys = _expert_call(eblk, info, xs, wu, bu, wd, bd, block_rows=block_rows)
    out = _combine_call(goff, lbase, nch, a, hres, ys, T=T, R=R)
    return out.reshape(B, S, D)
```
